```python
import math
import jax
import jax.numpy as jnp
from jax import lax
import numpy as np

D_MODEL = 1024
BATCH = 8
SEQ = 4096
DEPTH = 2

GRID_W = 64
CTX_LEN = 256
N_MIXERS = 2
N_MAMBA = (DEPTH + 1) // 2
N_HYENA = DEPTH // 2
DEEPNORM_ALPHA = (2 * DEPTH) ** 0.25
DEEPNORM_BETA = (8 * DEPTH) ** -0.25
LN_EPS = 1e-5
RMS_EPS = 1e-5

SSM_INNER = 2 * D_MODEL
SSM_HEAD_DIM = 64
SSM_HEADS = SSM_INNER // SSM_HEAD_DIM
SSM_GROUPS = 4
SSM_STATE = 128
SSM_CHUNK = 128
SSM_CONV = 3
SSM_CONV_DIM = SSM_INNER + 2 * SSM_GROUPS * SSM_STATE
SSM_IN_DIM = SSM_INNER + SSM_CONV_DIM + 2 * SSM_HEADS

HYENA_ORDER = 2
HYENA_SHORT = 3
HYENA_EMB = 33
HYENA_FF = 64
HYENA_INNER_MLPS = 2
HYENA_STEEP_DECAY = 0.3
HYENA_GENTLE_DECAY = 1.5
HYENA_TARGET = 1e-2
HYENA_NORM_EPS = 1e-6

PEER_HEADS = 8
PEER_NKEYS = 128
PEER_EXPERTS = PEER_NKEYS * PEER_NKEYS
PEER_DK = 256
PEER_TOPK = 16
PEER_BLOCK = 128

kernel_name = 'hybrid_ssd_hyena_peer_dit'


def layer_norm(x, g, b):
    xf = x.astype(jnp.float32)
    mu = jnp.mean(xf, axis=-1, keepdims=True)
    var = jnp.mean(jnp.square(xf - mu), axis=-1, keepdims=True)
    return ((xf - mu) * lax.rsqrt(var + LN_EPS) * g.astype(jnp.float32) + b.astype(jnp.float32)).astype(x.dtype)


def gated_group_rmsnorm(y, z, w):
    b, L, _ = y.shape
    u = (y * jax.nn.silu(z)).astype(jnp.float32).reshape(b, L, SSM_GROUPS, -1)
    u = u * lax.rsqrt(jnp.mean(jnp.square(u), axis=-1, keepdims=True) + RMS_EPS)
    return (u.reshape(b, L, SSM_INNER) * w.astype(jnp.float32)).astype(y.dtype)


def depthwise_conv_grid(u, w, bias, rows, cols):
    b, L, C = u.shape
    out = lax.conv_general_dilated(u.reshape(b, rows, cols, C), w.astype(u.dtype)[:, :, None, :],
                                   window_strides=(1, 1), padding='SAME',
                                   dimension_numbers=('NHWC', 'HWIO', 'NHWC'), feature_group_count=C)
    return out.reshape(b, L, C) + bias


def ssd_scan(xs, dt, a_neg, bm, cm, init, want_y):
    f32 = jnp.float32
    b, L, H, P = xs.shape
    G, N = bm.shape[2], bm.shape[3]
    R = H // G
    Q = SSM_CHUNK
    nc = L // Q
    xdt = (xs.astype(f32) * dt[..., None]).reshape(b, nc, Q, G, R, P)
    bc = bm.astype(f32).reshape(b, nc, Q, G, N)
    cc = cm.astype(f32).reshape(b, nc, Q, G, N)
    a = (dt * a_neg.astype(f32)).reshape(b, nc, Q, G, R)
    a_cum = jnp.cumsum(a, axis=2)
    decay_to_end = jnp.exp(a_cum[:, :, -1:] - a_cum)
    chunk_states = jnp.einsum('bclgn,bclgrp->bcgrpn', bc, xdt * decay_to_end[..., None])
    chunk_decay = jnp.exp(a_cum[:, :, -1])

    def step(state, inp):
        cs, cd = inp
        return state * cd[..., None, None] + cs, state

    final, prev = lax.scan(step, init.astype(f32).reshape(b, G, R, P, N),
                           (jnp.moveaxis(chunk_states, 1, 0), jnp.moveaxis(chunk_decay, 1, 0)))
    final = final.reshape(b, H, P, N)
    if not want_y:
        return None, final
    prev = jnp.moveaxis(prev, 0, 1)
    seg = a_cum[:, :, :, None] - a_cum[:, :, None, :]
    lower = jnp.tril(jnp.ones((Q, Q), dtype=bool))[None, None, :, :, None, None]
    decay_ls = jnp.exp(jnp.where(lower, seg, -jnp.inf))
    cb = jnp.einsum('bclgn,bcsgn->bclsg', cc, bc)
    y_diag = jnp.einsum('bclsgr,bcsgrp->bclgrp', cb[..., None] * decay_ls, xdt)
    y_off = jnp.einsum('bclgn,bcgrpn->bclgrp', cc, prev) * jnp.exp(a_cum)[..., None]
    y = (y_diag + y_off).reshape(b, L, H, P)
    return y.astype(xs.dtype), final


def mamba2_mixer(h_lat, h_ctx, rows, w_in, conv_w, conv_b, a_log, dt_bias, d_skip, norm_w, w_out, ctx_out):
    f32 = jnp.float32
    a_neg = -jnp.exp(a_log.astype(f32))
    G, N = SSM_GROUPS, SSM_STATE

    def branch(h, rows_, cols_, need_z):
        bb, LL, _ = h.shape
        proj = h @ (w_in if need_z else w_in[:, SSM_INNER:])
        z = None
        if need_z:
            z, proj = proj[..., :SSM_INNER], proj[..., SSM_INNER:]
        xbc = jax.nn.silu(depthwise_conv_grid(proj[..., :SSM_CONV_DIM], conv_w, conv_b, rows_, cols_))
        dt = jax.nn.softplus(proj[..., SSM_CONV_DIM:].astype(f32).reshape(bb, LL, 2, SSM_HEADS)
                             + dt_bias.astype(f32))
        xs = xbc[..., :SSM_INNER].reshape(bb, LL, SSM_HEADS, SSM_HEAD_DIM)
        bm = xbc[..., SSM_INNER:SSM_INNER + G * N].reshape(bb, LL, G, N)
        cm = xbc[..., SSM_INNER + G * N:].reshape(bb, LL, G, N)
        return z, xs, bm, cm, dt

    def scans(xs, bm, cm, dt, init_f, init_b, want_y):
        y_f, s_f = ssd_scan(xs, dt[:, :, 0], a_neg[0], bm, cm, init_f, want_y)
        y_b, s_b = ssd_scan(jnp.flip(xs, 1), jnp.flip(dt[:, :, 1], 1), a_neg[1],
                            jnp.flip(bm, 1), jnp.flip(cm, 1), init_b, want_y)
        y = None
        if want_y:
            y = y_f + jnp.flip(y_b, 1) + xs * d_skip[:, None]
        return y, s_f, s_b

    def finish(y, z):
        bb, LL = y.shape[:2]
        return gated_group_rmsnorm(y.reshape(bb, LL, SSM_INNER), z, norm_w) @ w_out

    bsz = h_ctx.shape[0]
    zeros = jnp.zeros((bsz, SSM_HEADS, SSM_HEAD_DIM, SSM_STATE), f32)
    zc, xc, bmc, cmc, dtc = branch(h_ctx, 1, h_ctx.shape[1], ctx_out)
    yc, s_f, s_b = scans(xc, bmc, cmc, dtc, zeros, zeros, ctx_out)
    zl, xl, bml, cml, dtl = branch(h_lat, rows, GRID_W, True)
    yl, _, _ = scans(xl, bml, cml, dtl, s_f, s_b, True)
    out_ctx = finish(yc, zc) if ctx_out else None
    return finish(yl, zl), out_ctx


def hyena_filters(L, w1, b1, w2, b2, w3, freq):
    f32 = jnp.float32
    t = jnp.linspace(0.0, 1.0, L, dtype=f32)[:, None]
    bands = (HYENA_EMB - 1) // 2
    ang = 2.0 * math.pi * jnp.arange(L, dtype=f32)[:, None] / L
    fb = jnp.linspace(1e-4, bands - 1, bands, dtype=f32)[None, :]
    z = jnp.concatenate([t, jnp.cos(fb * ang), -jnp.sin(fb * ang)], axis=-1)
    fr = freq.astype(f32)
    hid = jnp.sin(fr * (z @ w1.astype(f32) + b1.astype(f32)))
    for j in range(HYENA_INNER_MLPS):
        hid = jnp.sin(fr * (hid @ w2[j].astype(f32) + b2[j].astype(f32)))
    h = (hid @ w3.astype(f32)).reshape(L, HYENA_ORDER, 2, D_MODEL)
    max_decay = math.log(HYENA_TARGET) / HYENA_STEEP_DECAY
    min_decay = math.log(HYENA_TARGET) / HYENA_GENTLE_DECAY
    deltas = jnp.linspace(min_decay, max_decay, D_MODEL, dtype=f32)
    h = h * jnp.exp(-t * jnp.abs(deltas))[:, None, None, :]
    h_fwd, h_bwd = h[:, :, 0], h[:, :, 1]
    two_sided = jnp.concatenate([h_fwd[:1] + h_bwd[:1], h_fwd[1:],
                                 jnp.zeros((1, HYENA_ORDER, D_MODEL), f32),
                                 jnp.flip(h_bwd[1:], 0)], axis=0)
    two_sided = two_sided * lax.rsqrt(jnp.sum(jnp.square(two_sided), axis=0, keepdims=True) + HYENA_NORM_EPS)
    return jnp.fft.rfft(two_sided, n=2 * L, axis=0)


def fft_long_conv(u, hf, bias):
    L = u.shape[1]
    uf = u.astype(jnp.float32)
    y = jnp.fft.irfft(jnp.fft.rfft(uf, n=2 * L, axis=1) * hf[None], n=2 * L, axis=1)[:, :L]
    return (y + uf * bias.astype(jnp.float32)).astype(u.dtype)


def short_conv_centred(u, w, bias):
    up = jnp.pad(u, ((0, 0), (1, 1), (0, 0)))
    return up[:, :-2] * w[0] + up[:, 1:-1] * w[1] + up[:, 2:] * w[2] + bias


def hyena_mixer(h, w_in, conv_w, conv_b, w1, b1, w2, b2, w3, freq, fbias, w_out):
    L = h.shape[1]
    hf = hyena_filters(L, w1, b1, w2, b2, w3, freq)
    u = short_conv_centred(h @ w_in, conv_w, conv_b)
    x1, x2, v = jnp.split(u, 3, axis=-1)
    zz = x1 * fft_long_conv(v, hf[:, 0], fbias[0])
    zz = x2 * fft_long_conv(zz, hf[:, 1], fbias[1])
    return zz @ w_out


def peer_ffn(h, wq, k1, k2, u_tab, v_tab):
    f32 = jnp.float32
    shp = h.shape
    t = h.reshape(-1, D_MODEL)
    T = t.shape[0]
    half = PEER_DK // 2
    q = (t @ wq).astype(f32).reshape(T, PEER_HEADS, PEER_DK)
    s1 = jnp.einsum('thd,hkd->thk', q[..., :half], k1.astype(f32))
    s2 = jnp.einsum('thd,hkd->thk', q[..., half:], k2.astype(f32))
    v1, i1 = lax.top_k(s1, PEER_TOPK)
    v2, i2 = lax.top_k(s2, PEER_TOPK)
    cand_s = (v1[..., :, None] + v2[..., None, :]).reshape(T, PEER_HEADS, PEER_TOPK * PEER_TOPK)
    cand_i = (i1[..., :, None] * PEER_NKEYS + i2[..., None, :]).reshape(T, PEER_HEADS, PEER_TOPK * PEER_TOPK)
    top_s, pos = lax.top_k(cand_s, PEER_TOPK)
    idx = jnp.take_along_axis(cand_i, pos, axis=-1)
    gate = jax.nn.softmax(top_s, axis=-1)
    nb = T // PEER_BLOCK

    def block(args):
        tb, ib, gb = args
        act = jax.nn.gelu(jnp.einsum('td,thkd->thk', tb, u_tab[ib]).astype(f32), approximate=False)
        return jnp.einsum('thk,thkd->td', (gb * act).astype(tb.dtype), v_tab[ib])

    out = lax.map(block, (t.reshape(nb, PEER_BLOCK, D_MODEL),
                          idx.reshape(nb, PEER_BLOCK, PEER_HEADS, PEER_TOPK),
                          gate.reshape(nb, PEER_BLOCK, PEER_HEADS, PEER_TOPK)))
    return out.reshape(shp)


def setup_inputs(seed: int = 0) -> dict:
    key = jax.random.key(seed)
    ks = iter(jax.random.split(key, 48))
    f32 = jnp.float32
    D = D_MODEL

    def nrm(shape, scale):
        return jax.random.normal(next(ks), shape, f32) * scale

    dt0 = jnp.exp(jax.random.uniform(next(ks), (N_MAMBA, 2, SSM_HEADS), f32, math.log(1e-3), math.log(1e-1)))
    a0 = jax.random.uniform(next(ks), (N_MAMBA, 2, SSM_HEADS), f32, 1.0, 16.0)
    return {
        'x': nrm((BATCH, SEQ, D), 1.0),
        'c': nrm((BATCH, D), 1.0),
        'ctx': nrm((BATCH, CTX_LEN, D), 1.0),
        'c_ctx': nrm((D,), 1.0),
        'ada_w': nrm((DEPTH, D, 6 * D), 0.5 * D ** -0.5),
        'ada_b': nrm((DEPTH, 6 * D), 0.02),
        'ln_g': 1.0 + nrm((DEPTH, 2, D), 0.02),
        'ln_b': nrm((DEPTH, 2, D), 0.02),
        'm_w_in': nrm((N_MAMBA, D, SSM_IN_DIM), D ** -0.5),
        'm_conv_w': nrm((N_MAMBA, SSM_CONV, SSM_CONV, SSM_CONV_DIM), 1.0 / SSM_CONV),
        'm_conv_b': nrm((N_MAMBA, SSM_CONV_DIM), 0.02),
        'm_a_log': jnp.log(a0),
        'm_dt_bias': dt0 + jnp.log(-jnp.expm1(-dt0)),
        'm_d': 1.0 + nrm((N_MAMBA, SSM_HEADS), 0.02),
        'm_norm_w': 1.0 + nrm((N_MAMBA, SSM_INNER), 0.02),
        'm_w_out': nrm((N_MAMBA, SSM_INNER, D), DEEPNORM_BETA * SSM_INNER ** -0.5),
        'h_w_in': nrm((N_HYENA, D, 3 * D), D ** -0.5),
        'h_conv_w': nrm((N_HYENA, HYENA_SHORT, 3 * D), HYENA_SHORT ** -0.5),
        'h_conv_b': nrm((N_HYENA, 3 * D), 0.02),
        'h_f_w1': nrm((N_HYENA, HYENA_EMB, HYENA_FF), HYENA_EMB ** -0.5),
        'h_f_b1': nrm((N_HYENA, HYENA_FF), 0.1),
        'h_f_w2': nrm((N_HYENA, HYENA_INNER_MLPS, HYENA_FF, HYENA_FF), HYENA_FF ** -0.5),
        'h_f_b2': nrm((N_HYENA, HYENA_INNER_MLPS, HYENA_FF), 0.1),
        'h_f_w3': nrm((N_HYENA, HYENA_FF, HYENA_ORDER * 2 * D), HYENA_FF ** -0.5),
        'h_freq': 1.0 + nrm((N_HYENA, HYENA_FF), 0.05),
        'h_bias': nrm((N_HYENA, HYENA_ORDER, D), 0.2),
        'h_w_out': nrm((N_HYENA, D, D), DEEPNORM_BETA * D ** -0.5),
        'p_wq': nrm((DEPTH, D, PEER_HEADS * PEER_DK), D ** -0.5),
        'p_k1': nrm((DEPTH, PEER_HEADS, PEER_NKEYS, PEER_DK // 2), (PEER_DK // 2) ** -0.5),
        'p_k2': nrm((DEPTH, PEER_HEADS, PEER_NKEYS, PEER_DK // 2), (PEER_DK // 2) ** -0.5),
        'p_u': nrm((DEPTH, PEER_EXPERTS, D), D ** -0.5),
        'p_v': nrm((DEPTH, PEER_EXPERTS, D), DEEPNORM_BETA),
    }


def reference(x, c, ctx, c_ctx, ada_w, ada_b, ln_g, ln_b,
              m_w_in, m_conv_w, m_conv_b, m_a_log, m_dt_bias, m_d, m_norm_w, m_w_out,
              h_w_in, h_conv_w, h_conv_b, h_f_w1, h_f_b1, h_f_w2, h_f_b2, h_f_w3, h_freq, h_bias, h_w_out,
              p_wq, p_k1, p_k2, p_u, p_v):
    rows = x.shape[1] // GRID_W
    s_lat = jax.nn.silu(c)
    s_ctx = jax.nn.silu(c_ctx)
    for i in range(DEPTH):
        kind, k = i % N_MIXERS, i // N_MIXERS
        ctx_live = any(j % N_MIXERS == 0 for j in range(i + 1, DEPTH))
        ctx_read = (kind == 0) or ctx_live
        sh1, sc1, g1, sh2, sc2, g2 = jnp.split((s_lat @ ada_w[i] + ada_b[i])[:, None, :], 6, axis=-1)
        hx = x * (1 + sc1) + sh1
        if ctx_read:
            csh1, csc1, cg1, csh2, csc2, cg2 = jnp.split(s_ctx @ ada_w[i] + ada_b[i], 6, axis=-1)
            hc = ctx * (1 + csc1) + csh1
        if kind == 0:
            y_lat, y_ctx = mamba2_mixer(hx, hc, rows, m_w_in[k], m_conv_w[k], m_conv_b[k], m_a_log[k],
                                        m_dt_bias[k], m_d[k], m_norm_w[k], m_w_out[k], ctx_live)
        else:
            hyena_w = (h_w_in[k], h_conv_w[k], h_conv_b[k], h_f_w1[k], h_f_b1[k], h_f_w2[k], h_f_b2[k],
                       h_f_w3[k], h_freq[k], h_bias[k], h_w_out[k])
            y_lat = hyena_mixer(hx, *hyena_w)
            y_ctx = hyena_mixer(hc, *hyena_w) if ctx_live else None
        peer_w = (p_wq[i], p_k1[i], p_k2[i], p_u[i], p_v[i])
        x = layer_norm(DEEPNORM_ALPHA * x + g1 * y_lat, ln_g[i, 0], ln_b[i, 0])
        x = layer_norm(DEEPNORM_ALPHA * x + g2 * peer_ffn(x * (1 + sc2) + sh2, *peer_w), ln_g[i, 1], ln_b[i, 1])
        if ctx_live:
            ctx = layer_norm(DEEPNORM_ALPHA * ctx + cg1 * y_ctx, ln_g[i, 0], ln_b[i, 0])
            ctx = layer_norm(DEEPNORM_ALPHA * ctx + cg2 * peer_ffn(ctx * (1 + csc2) + csh2, *peer_w),
                             ln_g[i, 1], ln_b[i, 1])
    return x
```

```python
import functools
import math

import jax
import jax.numpy as jnp
from jax import lax
from jax.experimental import pallas as pl
from jax.experimental.pallas import tpu as pltpu

D_MODEL = 1024
DEPTH = 2
GRID_W = 64
N_MIXERS = 2
DEEPNORM_ALPHA = (2 * DEPTH) ** 0.25
LN_EPS = 1e-5
RMS_EPS = 1e-5

SSM_INNER = 2 * D_MODEL
SSM_HEAD_DIM = 64
SSM_HEADS = SSM_INNER // SSM_HEAD_DIM
SSM_GROUPS = 4
SSM_STATE = 128
SSM_CHUNK = 128
SSM_CONV = 3
SSM_CONV_DIM = SSM_INNER + 2 * SSM_GROUPS * SSM_STATE

HYENA_ORDER = 2
HYENA_EMB = 33
HYENA_INNER_MLPS = 2
HYENA_STEEP_DECAY = 0.3
HYENA_GENTLE_DECAY = 1.5
HYENA_TARGET = 1e-2
HYENA_NORM_EPS = 1e-6

PEER_HEADS = 8
PEER_NKEYS = 128
PEER_DK = 256
PEER_TOPK = 16


def _ln_kernel(x_ref, g_ref, b_ref, o_ref):
    xf = x_ref[...]
    mu = jnp.mean(xf, axis=-1, keepdims=True)
    xc = xf - mu
    var = jnp.mean(xc * xc, axis=-1, keepdims=True)
    o_ref[...] = xc * lax.rsqrt(var + LN_EPS) * g_ref[...] + b_ref[...]


def layer_norm_pallas(x, g, b):
    shp = x.shape
    t = x.reshape(-1, D_MODEL)
    tb = 512
    out = pl.pallas_call(
        _ln_kernel,
        grid=(t.shape[0] // tb,),
        in_specs=[pl.BlockSpec((tb, D_MODEL), lambda i: (i, 0)),
                  pl.BlockSpec((1, D_MODEL), lambda i: (0, 0)),
                  pl.BlockSpec((1, D_MODEL), lambda i: (0, 0))],
        out_specs=pl.BlockSpec((tb, D_MODEL), lambda i: (i, 0)),
        out_shape=jax.ShapeDtypeStruct(t.shape, jnp.float32),
        name="layer_norm",
    )(t, g.reshape(1, D_MODEL), b.reshape(1, D_MODEL))
    return out.reshape(shp)


def layer_norm(x, g, b):
    xf = x.astype(jnp.float32)
    mu = jnp.mean(xf, axis=-1, keepdims=True)
    var = jnp.mean(jnp.square(xf - mu), axis=-1, keepdims=True)
    return ((xf - mu) * lax.rsqrt(var + LN_EPS) * g.astype(jnp.float32) + b.astype(jnp.float32)).astype(x.dtype)


def gated_group_rmsnorm(y, z, w):
    b, L, _ = y.shape
    u = (y * jax.nn.silu(z)).astype(jnp.float32).reshape(b, L, SSM_GROUPS, -1)
    u = u * lax.rsqrt(jnp.mean(jnp.square(u), axis=-1, keepdims=True) + RMS_EPS)
    return (u.reshape(b, L, SSM_INNER) * w.astype(jnp.float32)).astype(y.dtype)


def depthwise_conv_grid(u, w, bias, rows, cols):
    b, L, C = u.shape
    out = lax.conv_general_dilated(u.reshape(b, rows, cols, C), w.astype(u.dtype)[:, :, None, :],
                                   window_strides=(1, 1), padding='SAME',
                                   dimension_numbers=('NHWC', 'HWIO', 'NHWC'), feature_group_count=C)
    return out.reshape(b, L, C) + bias


def ssd_scan(xs, dt, a_neg, bm, cm, init, want_y):
    f32 = jnp.float32
    b, L, H, P = xs.shape
    G, N = bm.shape[2], bm.shape[3]
    R = H // G
    Q = SSM_CHUNK
    nc = L // Q
    xdt = (xs.astype(f32) * dt[..., None]).reshape(b, nc, Q, G, R, P)
    bc = bm.astype(f32).reshape(b, nc, Q, G, N)
    cc = cm.astype(f32).reshape(b, nc, Q, G, N)
    a = (dt * a_neg.astype(f32)).reshape(b, nc, Q, G, R)
    a_cum = jnp.cumsum(a, axis=2)
    decay_to_end = jnp.exp(a_cum[:, :, -1:] - a_cum)
    chunk_states = jnp.einsum('bclgn,bclgrp->bcgrpn', bc, xdt * decay_to_end[..., None])
    chunk_decay = jnp.exp(a_cum[:, :, -1])

    def step(state, inp):
        cs, cd = inp
        return state * cd[..., None, None] + cs, state

    final, prev = lax.scan(step, init.astype(f32).reshape(b, G, R, P, N),
                           (jnp.moveaxis(chunk_states, 1, 0), jnp.moveaxis(chunk_decay, 1, 0)))
    final = final.reshape(b, H, P, N)
    if not want_y:
        return None, final
    prev = jnp.moveaxis(prev, 0, 1)
    seg = a_cum[:, :, :, None] - a_cum[:, :, None, :]
    lower = jnp.tril(jnp.ones((Q, Q), dtype=bool))[None, None, :, :, None, None]
    decay_ls = jnp.exp(jnp.where(lower, seg, -jnp.inf))
    cb = jnp.einsum('bclgn,bcsgn->bclsg', cc, bc)
    y_diag = jnp.einsum('bclsgr,bcsgrp->bclgrp', cb[..., None] * decay_ls, xdt)
    y_off = jnp.einsum('bclgn,bcgrpn->bclgrp', cc, prev) * jnp.exp(a_cum)[..., None]
    y = (y_diag + y_off).reshape(b, L, H, P)
    return y.astype(xs.dtype), final


def mamba2_mixer(h_lat, h_ctx, rows, w_in, conv_w, conv_b, a_log, dt_bias, d_skip, norm_w, w_out, ctx_out):
    f32 = jnp.float32
    a_neg = -jnp.exp(a_log.astype(f32))
    G, N = SSM_GROUPS, SSM_STATE

    def branch(h, rows_, cols_, need_z):
        bb, LL, _ = h.shape
        proj = h @ (w_in if need_z else w_in[:, SSM_INNER:])
        z = None
        if need_z:
            z, proj = proj[..., :SSM_INNER], proj[..., SSM_INNER:]
        xbc = jax.nn.silu(depthwise_conv_grid(proj[..., :SSM_CONV_DIM], conv_w, conv_b, rows_, cols_))
        dt = jax.nn.softplus(proj[..., SSM_CONV_DIM:].astype(f32).reshape(bb, LL, 2, SSM_HEADS)
                             + dt_bias.astype(f32))
        xs = xbc[..., :SSM_INNER].reshape(bb, LL, SSM_HEADS, SSM_HEAD_DIM)
        bm = xbc[..., SSM_INNER:SSM_INNER + G * N].reshape(bb, LL, G, N)
        cm = xbc[..., SSM_INNER + G * N:].reshape(bb, LL, G, N)
        return z, xs, bm, cm, dt

    def scans(xs, bm, cm, dt, init_f, init_b, want_y):
        y_f, s_f = ssd_scan(xs, dt[:, :, 0], a_neg[0], bm, cm, init_f, want_y)
        y_b, s_b = ssd_scan(jnp.flip(xs, 1), jnp.flip(dt[:, :, 1], 1), a_neg[1],
                            jnp.flip(bm, 1), jnp.flip(cm, 1), init_b, want_y)
        y = None
        if want_y:
            y = y_f + jnp.flip(y_b, 1) + xs * d_skip[:, None]
        return y, s_f, s_b

    def finish(y, z):
        bb, LL = y.shape[:2]
        return gated_group_rmsnorm(y.reshape(bb, LL, SSM_INNER), z, norm_w) @ w_out

    bsz = h_ctx.shape[0]
    zeros = jnp.zeros((bsz, SSM_HEADS, SSM_HEAD_DIM, SSM_STATE), f32)
    zc, xc, bmc, cmc, dtc = branch(h_ctx, 1, h_ctx.shape[1], ctx_out)
    yc, s_f, s_b = scans(xc, bmc, cmc, dtc, zeros, zeros, ctx_out)
    zl, xl, bml, cml, dtl = branch(h_lat, rows, GRID_W, True)
    yl, _, _ = scans(xl, bml, cml, dtl, s_f, s_b, True)
    out_ctx = finish(yc, zc) if ctx_out else None
    return finish(yl, zl), out_ctx


def hyena_filters(L, w1, b1, w2, b2, w3, freq):
    f32 = jnp.float32
    t = jnp.linspace(0.0, 1.0, L, dtype=f32)[:, None]
    bands = (HYENA_EMB - 1) // 2
    ang = 2.0 * math.pi * jnp.arange(L, dtype=f32)[:, None] / L
    fb = jnp.linspace(1e-4, bands - 1, bands, dtype=f32)[None, :]
    z = jnp.concatenate([t, jnp.cos(fb * ang), -jnp.sin(fb * ang)], axis=-1)
    fr = freq.astype(f32)
    hid = jnp.sin(fr * (z @ w1.astype(f32) + b1.astype(f32)))
    for j in range(HYENA_INNER_MLPS):
        hid = jnp.sin(fr * (hid @ w2[j].astype(f32) + b2[j].astype(f32)))
    h = (hid @ w3.astype(f32)).reshape(L, HYENA_ORDER, 2, D_MODEL)
    max_decay = math.log(HYENA_TARGET) / HYENA_STEEP_DECAY
    min_decay = math.log(HYENA_TARGET) / HYENA_GENTLE_DECAY
    deltas = jnp.linspace(min_decay, max_decay, D_MODEL, dtype=f32)
    h = h * jnp.exp(-t * jnp.abs(deltas))[:, None, None, :]
    h_fwd, h_bwd = h[:, :, 0], h[:, :, 1]
    two_sided = jnp.concatenate([h_fwd[:1] + h_bwd[:1], h_fwd[1:],
                                 jnp.zeros((1, HYENA_ORDER, D_MODEL), f32),
                                 jnp.flip(h_bwd[1:], 0)], axis=0)
    two_sided = two_sided * lax.rsqrt(jnp.sum(jnp.square(two_sided), axis=0, keepdims=True) + HYENA_NORM_EPS)
    return jnp.fft.rfft(two_sided, n=2 * L, axis=0)


def fft_long_conv(u, hf, bias):
    L = u.shape[1]
    uf = u.astype(jnp.float32)
    y = jnp.fft.irfft(jnp.fft.rfft(uf, n=2 * L, axis=1) * hf[None], n=2 * L, axis=1)[:, :L]
    return (y + uf * bias.astype(jnp.float32)).astype(u.dtype)


def short_conv_centred(u, w, bias):
    up = jnp.pad(u, ((0, 0), (1, 1), (0, 0)))
    return up[:, :-2] * w[0] + up[:, 1:-1] * w[1] + up[:, 2:] * w[2] + bias


def hyena_mixer(h, w_in, conv_w, conv_b, w1, b1, w2, b2, w3, freq, fbias, w_out):
    L = h.shape[1]
    hf = hyena_filters(L, w1, b1, w2, b2, w3, freq)
    u = short_conv_centred(h @ w_in, conv_w, conv_b)
    x1, x2, v = jnp.split(u, 3, axis=-1)
    zz = x1 * fft_long_conv(v, hf[:, 0], fbias[0])
    zz = x2 * fft_long_conv(zz, hf[:, 1], fbias[1])
    return zz @ w_out


PEER_SLOTS = PEER_HEADS * PEER_TOPK
PEER_TB = 64
VREG_SUBLANES = 8
VREG_LANES = 128
PEER_VMEM_LIMIT = 52 * 1024 * 1024
HI_MASK = -65536


def _pack_rows_bf16(tab):
    n, d = tab.shape
    bits = lax.bitcast_convert_type(tab.astype(jnp.bfloat16), jnp.uint16).astype(jnp.uint32)
    packed = (bits[1::2] << 16) | bits[0::2]
    return lax.bitcast_convert_type(packed, jnp.int32).reshape(n // 2 * (d // VREG_LANES), VREG_LANES)


def _unpack_row(tab_ref, off, shift):
    word = tab_ref[pl.ds(pl.multiple_of(off, VREG_SUBLANES), VREG_SUBLANES), :]
    return pltpu.bitcast((word << shift) & HI_MASK, jnp.float32)


def _sublane_sums(tiles, sub):
    for h in (1, 2, 4):
        keep = (sub & h) == 0
        nxt = []
        for a, b in zip(tiles[0::2], tiles[1::2]):
            nxt.append(jnp.where(keep, a, b) + pltpu.roll(jnp.where(keep, b, a), h, 0))
        tiles = nxt
    return tiles[0]


def _splat_rows(src_ref, dst_ref):
    def token(t, carry):
        row = src_ref[pl.ds(t, 1), :]
        tile = jnp.broadcast_to(row, (PEER_SLOTS, VREG_LANES)).T
        dst_ref[pl.ds(pl.multiple_of(t * PEER_SLOTS, PEER_SLOTS), PEER_SLOTS), :] = tile
        return carry
    lax.fori_loop(0, PEER_TB, token, 0, unroll=8)


def _peer_act_kernel(row_ref, shift_ref, x_ref, gate_ref, tab_ref, w_ref, part_ref, shift_s):
    sub = lax.broadcasted_iota(jnp.int32, (VREG_SUBLANES, VREG_LANES), 0)
    _splat_rows(shift_ref, shift_s)

    def token(t, carry):
        xt = x_ref[t]
        base = pl.multiple_of(t * PEER_SLOTS, PEER_SLOTS)
        tok_rows = row_ref.at[0, 0, pl.ds(base, PEER_SLOTS)]
        for g in range(PEER_SLOTS // VREG_SUBLANES):
            prods = []
            for k in range(VREG_SUBLANES):
                s = g * VREG_SUBLANES + k
                prods.append(_unpack_row(tab_ref, tok_rows[s], shift_s[pl.ds(base + s, 1), :]) * xt)
            part_ref[pl.ds(base + g * VREG_SUBLANES, VREG_SUBLANES), :] = _sublane_sums(prods, sub)
        return carry

    lax.fori_loop(0, PEER_TB, token, 0)
    ones = jnp.ones((VREG_SUBLANES, VREG_LANES), jnp.float32)
    act = lax.dot_general(ones, part_ref[...], (((1,), (1,)), ((), ())),
                          precision=lax.Precision.HIGHEST, preferred_element_type=jnp.float32)[0:1]
    gelu = 0.5 * act * (1.0 + lax.erf(act * (2.0 ** -0.5)))
    w_ref[0] = gate_ref[0] * gelu


def _peer_out_kernel(row_ref, shift_ref, w_ref, tab_ref, o_ref, shift_s, w_s):
    n_acc = 4
    _splat_rows(shift_ref, shift_s)
    _splat_rows(w_ref, w_s)

    def token(t, carry):
        base = pl.multiple_of(t * PEER_SLOTS, PEER_SLOTS)
        tok_rows = row_ref.at[0, 0, pl.ds(base, PEER_SLOTS)]
        accs = [jnp.zeros((VREG_SUBLANES, VREG_LANES), jnp.float32) for _ in range(n_acc)]
        for s in range(PEER_SLOTS):
            val = _unpack_row(tab_ref, tok_rows[s], shift_s[pl.ds(base + s, 1), :])
            accs[s % n_acc] = accs[s % n_acc] + w_s[pl.ds(base + s, 1), :] * val
        o_ref[t] = (accs[0] + accs[1]) + (accs[2] + accs[3])
        return carry

    lax.fori_loop(0, PEER_TB, token, 0)


def peer_experts(t, idx, gate, u_tab, v_tab):
    T = t.shape[0]
    nblk = T // PEER_TB
    per_blk = PEER_TB * PEER_SLOTS
    d_sub = D_MODEL // VREG_LANES
    offs = ((idx >> 1) * VREG_SUBLANES).reshape(nblk, 1, per_blk)
    shifts = jnp.where((idx & 1) == 1, 0, 16).astype(jnp.int32)
    smem_spec = pl.BlockSpec((1, 1, per_blk), lambda i: (i, 0, 0), memory_space=pltpu.SMEM)
    lane_spec = pl.BlockSpec((1, 1, per_blk), lambda i: (i, 0, 0))
    slot_spec = pl.BlockSpec((PEER_TB, PEER_SLOTS), lambda i: (i, 0))
    tok_spec = pl.BlockSpec((PEER_TB, d_sub, VREG_LANES), lambda i: (i, 0, 0))
    tab_spec = pl.BlockSpec(memory_space=pltpu.VMEM)
    params = pltpu.CompilerParams(dimension_semantics=("arbitrary",), vmem_limit_bytes=PEER_VMEM_LIMIT)
    splat_i32 = pltpu.VMEM((per_blk, VREG_LANES), jnp.int32)
    splat_f32 = pltpu.VMEM((per_blk, VREG_LANES), jnp.float32)
    w = pl.pallas_call(
        _peer_act_kernel,
        grid=(nblk,),
        in_specs=[smem_spec, slot_spec, tok_spec, lane_spec, tab_spec],
        out_specs=lane_spec,
        out_shape=jax.ShapeDtypeStruct((nblk, 1, per_blk), jnp.float32),
        scratch_shapes=[splat_f32, splat_i32],
        compiler_params=params,
        name="peer_act",
    )(offs, shifts, t.reshape(T, d_sub, VREG_LANES), gate.reshape(nblk, 1, per_blk), _pack_rows_bf16(u_tab))
    out = pl.pallas_call(
        _peer_out_kernel,
        grid=(nblk,),
        in_specs=[smem_spec, slot_spec, slot_spec, tab_spec],
        out_specs=tok_spec,
        out_shape=jax.ShapeDtypeStruct((T, d_sub, VREG_LANES), jnp.float32),
        scratch_shapes=[splat_i32, splat_f32],
        compiler_params=params,
        name="peer_out",
    )(offs, shifts, w.reshape(T, PEER_SLOTS), _pack_rows_bf16(v_tab))
    return out.reshape(T, D_MODEL)


def peer_ffn(h, wq, k1, k2, u_tab, v_tab):
    f32 = jnp.float32
    shp = h.shape
    t = h.reshape(-1, D_MODEL)
    T = t.shape[0]
    half = PEER_DK // 2
    q = (t @ wq).astype(f32).reshape(T, PEER_HEADS, PEER_DK)
    s1 = jnp.einsum('thd,hkd->thk', q[..., :half], k1.astype(f32))
    s2 = jnp.einsum('thd,hkd->thk', q[..., half:], k2.astype(f32))
    v1, i1 = lax.top_k(s1, PEER_TOPK)
    v2, i2 = lax.top_k(s2, PEER_TOPK)
    cand_s = (v1[..., :, None] + v2[..., None, :]).reshape(T, PEER_HEADS, PEER_TOPK * PEER_TOPK)
    cand_i = (i1[..., :, None] * PEER_NKEYS + i2[..., None, :]).reshape(T, PEER_HEADS, PEER_TOPK * PEER_TOPK)
    top_s, pos = lax.top_k(cand_s, PEER_TOPK)
    idx = jnp.take_along_axis(cand_i, pos, axis=-1)
    gate = jax.nn.softmax(top_s, axis=-1)
    out = peer_experts(t, idx.reshape(T, PEER_SLOTS), gate.reshape(T, PEER_SLOTS), u_tab, v_tab)
    return out.reshape(shp)


def kernel(x, c, ctx, c_ctx, ada_w, ada_b, ln_g, ln_b,
           m_w_in, m_conv_w, m_conv_b, m_a_log, m_dt_bias, m_d, m_norm_w, m_w_out,
           h_w_in, h_conv_w, h_conv_b, h_f_w1, h_f_b1, h_f_w2, h_f_b2, h_f_w3, h_freq, h_bias, h_w_out,
           p_wq, p_k1, p_k2, p_u, p_v):
    rows = x.shape[1] // GRID_W
    s_lat = jax.nn.silu(c)
    s_ctx = jax.nn.silu(c_ctx)
    for i in range(DEPTH):
        kind, k = i % N_MIXERS, i // N_MIXERS
        sh1, sc1, g1, sh2, sc2, g2 = jnp.split((s_lat @ ada_w[i] + ada_b[i])[:, None, :], 6, axis=-1)
        hx = x * (1 + sc1) + sh1
        if kind == 0:
            csh1, csc1, cg1, csh2, csc2, cg2 = jnp.split(s_ctx @ ada_w[i] + ada_b[i], 6, axis=-1)
            hc = ctx * (1 + csc1) + csh1
            y_lat, _ = mamba2_mixer(hx, hc, rows, m_w_in[k], m_conv_w[k], m_conv_b[k], m_a_log[k],
                                    m_dt_bias[k], m_d[k], m_norm_w[k], m_w_out[k], False)
        else:
            hyena_w = (h_w_in[k], h_conv_w[k], h_conv_b[k], h_f_w1[k], h_f_b1[k], h_f_w2[k], h_f_b2[k],
                       h_f_w3[k], h_freq[k], h_bias[k], h_w_out[k])
            y_lat = hyena_mixer(hx, *hyena_w)
        peer_w = (p_wq[i], p_k1[i], p_k2[i], p_u[i], p_v[i])
        x = layer_norm(DEEPNORM_ALPHA * x + g1 * y_lat, ln_g[i, 0], ln_b[i, 0])
        x = layer_norm_pallas(DEEPNORM_ALPHA * x + g2 * peer_ffn(x * (1 + sc2) + sh2, *peer_w),
                              ln_g[i, 1], ln_b[i, 1])
    return x
```

```python
import functools
import math

import jax
import jax.numpy as jnp
from jax import lax
from jax.experimental import pallas as pl
from jax.experimental.pallas import tpu as pltpu

D_MODEL = 1024
DEPTH = 2
GRID_W = 64
N_MIXERS = 2
DEEPNORM_ALPHA = (2 * DEPTH) ** 0.25
LN_EPS = 1e-5
RMS_EPS = 1e-5

SSM_INNER = 2 * D_MODEL
SSM_HEAD_DIM = 64
SSM_HEADS = SSM_INNER // SSM_HEAD_DIM
SSM_GROUPS = 4
SSM_STATE = 128
SSM_CHUNK = 128
SSM_CONV = 3
SSM_CONV_DIM = SSM_INNER + 2 * SSM_GROUPS * SSM_STATE

HYENA_ORDER = 2
HYENA_EMB = 33
HYENA_INNER_MLPS = 2
HYENA_STEEP_DECAY = 0.3
HYENA_GENTLE_DECAY = 1.5
HYENA_TARGET = 1e-2
HYENA_NORM_EPS = 1e-6

PEER_HEADS = 8
PEER_NKEYS = 128
PEER_DK = 256
PEER_TOPK = 16


def _ln_kernel(x_ref, g_ref, b_ref, o_ref):
    xf = x_ref[...]
    mu = jnp.mean(xf, axis=-1, keepdims=True)
    xc = xf - mu
    var = jnp.mean(xc * xc, axis=-1, keepdims=True)
    o_ref[...] = xc * lax.rsqrt(var + LN_EPS) * g_ref[...] + b_ref[...]


def layer_norm_pallas(x, g, b):
    shp = x.shape
    t = x.reshape(-1, D_MODEL)
    tb = 512
    out = pl.pallas_call(
        _ln_kernel,
        grid=(t.shape[0] // tb,),
        in_specs=[pl.BlockSpec((tb, D_MODEL), lambda i: (i, 0)),
                  pl.BlockSpec((1, D_MODEL), lambda i: (0, 0)),
                  pl.BlockSpec((1, D_MODEL), lambda i: (0, 0))],
        out_specs=pl.BlockSpec((tb, D_MODEL), lambda i: (i, 0)),
        out_shape=jax.ShapeDtypeStruct(t.shape, jnp.float32),
        name="layer_norm",
    )(t, g.reshape(1, D_MODEL), b.reshape(1, D_MODEL))
    return out.reshape(shp)


def layer_norm(x, g, b):
    xf = x.astype(jnp.float32)
    mu = jnp.mean(xf, axis=-1, keepdims=True)
    var = jnp.mean(jnp.square(xf - mu), axis=-1, keepdims=True)
    return ((xf - mu) * lax.rsqrt(var + LN_EPS) * g.astype(jnp.float32) + b.astype(jnp.float32)).astype(x.dtype)


def gated_group_rmsnorm(y, z, w):
    b, L, _ = y.shape
    u = (y * jax.nn.silu(z)).astype(jnp.float32).reshape(b, L, SSM_GROUPS, -1)
    u = u * lax.rsqrt(jnp.mean(jnp.square(u), axis=-1, keepdims=True) + RMS_EPS)
    return (u.reshape(b, L, SSM_INNER) * w.astype(jnp.float32)).astype(y.dtype)


def depthwise_conv_grid(u, w, bias, rows, cols):
    b, L, C = u.shape
    out = lax.conv_general_dilated(u.reshape(b, rows, cols, C), w.astype(u.dtype)[:, :, None, :],
                                   window_strides=(1, 1), padding='SAME',
                                   dimension_numbers=('NHWC', 'HWIO', 'NHWC'), feature_group_count=C)
    return out.reshape(b, L, C) + bias


def ssd_scan(xs, dt, a_neg, bm, cm, init, want_y):
    f32 = jnp.float32
    b, L, H, P = xs.shape
    G, N = bm.shape[2], bm.shape[3]
    R = H // G
    Q = SSM_CHUNK
    nc = L // Q
    xdt = (xs.astype(f32) * dt[..., None]).reshape(b, nc, Q, G, R, P)
    bc = bm.astype(f32).reshape(b, nc, Q, G, N)
    cc = cm.astype(f32).reshape(b, nc, Q, G, N)
    a = (dt * a_neg.astype(f32)).reshape(b, nc, Q, G, R)
    a_cum = jnp.cumsum(a, axis=2)
    decay_to_end = jnp.exp(a_cum[:, :, -1:] - a_cum)
    chunk_states = jnp.einsum('bclgn,bclgrp->bcgrpn', bc, xdt * decay_to_end[..., None])
    chunk_decay = jnp.exp(a_cum[:, :, -1])

    def step(state, inp):
        cs, cd = inp
        return state * cd[..., None, None] + cs, state

    final, prev = lax.scan(step, init.astype(f32).reshape(b, G, R, P, N),
                           (jnp.moveaxis(chunk_states, 1, 0), jnp.moveaxis(chunk_decay, 1, 0)))
    final = final.reshape(b, H, P, N)
    if not want_y:
        return None, final
    prev = jnp.moveaxis(prev, 0, 1)
    seg = a_cum[:, :, :, None] - a_cum[:, :, None, :]
    lower = jnp.tril(jnp.ones((Q, Q), dtype=bool))[None, None, :, :, None, None]
    decay_ls = jnp.exp(jnp.where(lower, seg, -jnp.inf))
    cb = jnp.einsum('bclgn,bcsgn->bclsg', cc, bc)
    y_diag = jnp.einsum('bclsgr,bcsgrp->bclgrp', cb[..., None] * decay_ls, xdt)
    y_off = jnp.einsum('bclgn,bcgrpn->bclgrp', cc, prev) * jnp.exp(a_cum)[..., None]
    y = (y_diag + y_off).reshape(b, L, H, P)
    return y.astype(xs.dtype), final


def mamba2_mixer(h_lat, h_ctx, rows, w_in, conv_w, conv_b, a_log, dt_bias, d_skip, norm_w, w_out, ctx_out):
    f32 = jnp.float32
    a_neg = -jnp.exp(a_log.astype(f32))
    G, N = SSM_GROUPS, SSM_STATE

    def branch(h, rows_, cols_, need_z):
        bb, LL, _ = h.shape
        proj = h @ (w_in if need_z else w_in[:, SSM_INNER:])
        z = None
        if need_z:
            z, proj = proj[..., :SSM_INNER], proj[..., SSM_INNER:]
        xbc = jax.nn.silu(depthwise_conv_grid(proj[..., :SSM_CONV_DIM], conv_w, conv_b, rows_, cols_))
        dt = jax.nn.softplus(proj[..., SSM_CONV_DIM:].astype(f32).reshape(bb, LL, 2, SSM_HEADS)
                             + dt_bias.astype(f32))
        xs = xbc[..., :SSM_INNER].reshape(bb, LL, SSM_HEADS, SSM_HEAD_DIM)
        bm = xbc[..., SSM_INNER:SSM_INNER + G * N].reshape(bb, LL, G, N)
        cm = xbc[..., SSM_INNER + G * N:].reshape(bb, LL, G, N)
        return z, xs, bm, cm, dt

    def scans(xs, bm, cm, dt, init_f, init_b, want_y):
        y_f, s_f = ssd_scan(xs, dt[:, :, 0], a_neg[0], bm, cm, init_f, want_y)
        y_b, s_b = ssd_scan(jnp.flip(xs, 1), jnp.flip(dt[:, :, 1], 1), a_neg[1],
                            jnp.flip(bm, 1), jnp.flip(cm, 1), init_b, want_y)
        y = None
        if want_y:
            y = y_f + jnp.flip(y_b, 1) + xs * d_skip[:, None]
        return y, s_f, s_b

    def finish(y, z):
        bb, LL = y.shape[:2]
        return gated_group_rmsnorm(y.reshape(bb, LL, SSM_INNER), z, norm_w) @ w_out

    bsz = h_ctx.shape[0]
    zeros = jnp.zeros((bsz, SSM_HEADS, SSM_HEAD_DIM, SSM_STATE), f32)
    zc, xc, bmc, cmc, dtc = branch(h_ctx, 1, h_ctx.shape[1], ctx_out)
    yc, s_f, s_b = scans(xc, bmc, cmc, dtc, zeros, zeros, ctx_out)
    zl, xl, bml, cml, dtl = branch(h_lat, rows, GRID_W, True)
    yl, _, _ = scans(xl, bml, cml, dtl, s_f, s_b, True)
    out_ctx = finish(yc, zc) if ctx_out else None
    return finish(yl, zl), out_ctx


def hyena_filters(L, w1, b1, w2, b2, w3, freq):
    f32 = jnp.float32
    t = jnp.linspace(0.0, 1.0, L, dtype=f32)[:, None]
    bands = (HYENA_EMB - 1) // 2
    ang = 2.0 * math.pi * jnp.arange(L, dtype=f32)[:, None] / L
    fb = jnp.linspace(1e-4, bands - 1, bands, dtype=f32)[None, :]
    z = jnp.concatenate([t, jnp.cos(fb * ang), -jnp.sin(fb * ang)], axis=-1)
    fr = freq.astype(f32)
    hid = jnp.sin(fr * (z @ w1.astype(f32) + b1.astype(f32)))
    for j in range(HYENA_INNER_MLPS):
        hid = jnp.sin(fr * (hid @ w2[j].astype(f32) + b2[j].astype(f32)))
    h = (hid @ w3.astype(f32)).reshape(L, HYENA_ORDER, 2, D_MODEL)
    max_decay = math.log(HYENA_TARGET) / HYENA_STEEP_DECAY
    min_decay = math.log(HYENA_TARGET) / HYENA_GENTLE_DECAY
    deltas = jnp.linspace(min_decay, max_decay, D_MODEL, dtype=f32)
    h = h * jnp.exp(-t * jnp.abs(deltas))[:, None, None, :]
    h_fwd, h_bwd = h[:, :, 0], h[:, :, 1]
    two_sided = jnp.concatenate([h_fwd[:1] + h_bwd[:1], h_fwd[1:],
                                 jnp.zeros((1, HYENA_ORDER, D_MODEL), f32),
                                 jnp.flip(h_bwd[1:], 0)], axis=0)
    two_sided = two_sided * lax.rsqrt(jnp.sum(jnp.square(two_sided), axis=0, keepdims=True) + HYENA_NORM_EPS)
    return jnp.fft.rfft(two_sided, n=2 * L, axis=0)


def fft_long_conv(u, hf, bias):
    L = u.shape[1]
    uf = u.astype(jnp.float32)
    y = jnp.fft.irfft(jnp.fft.rfft(uf, n=2 * L, axis=1) * hf[None], n=2 * L, axis=1)[:, :L]
    return (y + uf * bias.astype(jnp.float32)).astype(u.dtype)


def short_conv_centred(u, w, bias):
    up = jnp.pad(u, ((0, 0), (1, 1), (0, 0)))
    return up[:, :-2] * w[0] + up[:, 1:-1] * w[1] + up[:, 2:] * w[2] + bias


def hyena_mixer(h, w_in, conv_w, conv_b, w1, b1, w2, b2, w3, freq, fbias, w_out):
    L = h.shape[1]
    hf = hyena_filters(L, w1, b1, w2, b2, w3, freq)
    u = short_conv_centred(h @ w_in, conv_w, conv_b)
    x1, x2, v = jnp.split(u, 3, axis=-1)
    zz = x1 * fft_long_conv(v, hf[:, 0], fbias[0])
    zz = x2 * fft_long_conv(zz, hf[:, 1], fbias[1])
    return zz @ w_out


PEER_SLOTS = PEER_HEADS * PEER_TOPK
PEER_TB = 64
VREG_SUBLANES = 8
VREG_LANES = 128
PEER_VMEM_LIMIT = 52 * 1024 * 1024
HI_MASK = -65536


def _pack_rows_bf16(tab):
    n, d = tab.shape
    bits = lax.bitcast_convert_type(tab.astype(jnp.bfloat16), jnp.uint16).astype(jnp.uint32)
    packed = (bits[1::2] << 16) | bits[0::2]
    return lax.bitcast_convert_type(packed, jnp.int32).reshape(n // 2 * (d // VREG_LANES), VREG_LANES)


def _unpack_row(tab_ref, off, shift):
    word = tab_ref[pl.ds(pl.multiple_of(off, VREG_SUBLANES), VREG_SUBLANES), :]
    return pltpu.bitcast((word << shift) & HI_MASK, jnp.float32)


def _sublane_sums(tiles, sub):
    for h in (1, 2, 4):
        keep = (sub & h) == 0
        nxt = []
        for a, b in zip(tiles[0::2], tiles[1::2]):
            nxt.append(jnp.where(keep, a, b) + pltpu.roll(jnp.where(keep, b, a), h, 0))
        tiles = nxt
    return tiles[0]


def _splat_rows(src_ref, dst_ref):
    def token(t, carry):
        row = src_ref[pl.ds(t, 1), :]
        tile = jnp.broadcast_to(row, (PEER_SLOTS, VREG_LANES)).T
        dst_ref[pl.ds(pl.multiple_of(t * PEER_SLOTS, PEER_SLOTS), PEER_SLOTS), :] = tile
        return carry
    lax.fori_loop(0, PEER_TB, token, 0, unroll=8)


def _peer_act_kernel(row_ref, shift_ref, x_ref, gate_ref, tab_ref, w_ref, part_ref, shift_s):
    sub = lax.broadcasted_iota(jnp.int32, (VREG_SUBLANES, VREG_LANES), 0)
    _splat_rows(shift_ref, shift_s)

    def token(t, carry):
        xt = x_ref[t]
        base = pl.multiple_of(t * PEER_SLOTS, PEER_SLOTS)
        tok_rows = row_ref.at[0, 0, pl.ds(base, PEER_SLOTS)]
        for g in range(PEER_SLOTS // VREG_SUBLANES):
            prods = []
            for k in range(VREG_SUBLANES):
                s = g * VREG_SUBLANES + k
                prods.append(_unpack_row(tab_ref, tok_rows[s], shift_s[pl.ds(base + s, 1), :]) * xt)
            part_ref[pl.ds(base + g * VREG_SUBLANES, VREG_SUBLANES), :] = _sublane_sums(prods, sub)
        return carry

    lax.fori_loop(0, PEER_TB, token, 0)
    ones = jnp.ones((VREG_SUBLANES, VREG_LANES), jnp.float32)
    act = lax.dot_general(ones, part_ref[...], (((1,), (1,)), ((), ())),
                          precision=lax.Precision.HIGHEST, preferred_element_type=jnp.float32)[0:1]
    gelu = 0.5 * act * (1.0 + lax.erf(act * (2.0 ** -0.5)))
    w_ref[0] = gate_ref[0] * gelu


def _peer_out_kernel(row_ref, shift_ref, w_ref, tab_ref, o_ref, shift_s, w_s):
    n_acc = 4
    _splat_rows(shift_ref, shift_s)
    _splat_rows(w_ref, w_s)

    def token(t, carry):
        base = pl.multiple_of(t * PEER_SLOTS, PEER_SLOTS)
        tok_rows = row_ref.at[0, 0, pl.ds(base, PEER_SLOTS)]
        accs = [jnp.zeros((VREG_SUBLANES, VREG_LANES), jnp.float32) for _ in range(n_acc)]
        for s in range(PEER_SLOTS):
            val = _unpack_row(tab_ref, tok_rows[s], shift_s[pl.ds(base + s, 1), :])
            accs[s % n_acc] = accs[s % n_acc] + w_s[pl.ds(base + s, 1), :] * val
        o_ref[t] = (accs[0] + accs[1]) + (accs[2] + accs[3])
        return carry

    lax.fori_loop(0, PEER_TB, token, 0)


def peer_experts(t, offs, shifts, gate, u_tab, v_tab):
    T = t.shape[0]
    nblk = T // PEER_TB
    per_blk = PEER_TB * PEER_SLOTS
    d_sub = D_MODEL // VREG_LANES
    offs = offs.reshape(nblk, 1, per_blk)
    smem_spec = pl.BlockSpec((1, 1, per_blk), lambda i: (i, 0, 0), memory_space=pltpu.SMEM)
    lane_spec = pl.BlockSpec((1, 1, per_blk), lambda i: (i, 0, 0))
    slot_spec = pl.BlockSpec((PEER_TB, PEER_SLOTS), lambda i: (i, 0))
    tok_spec = pl.BlockSpec((PEER_TB, d_sub, VREG_LANES), lambda i: (i, 0, 0))
    tab_spec = pl.BlockSpec(memory_space=pltpu.VMEM)
    params = pltpu.CompilerParams(dimension_semantics=("arbitrary",), vmem_limit_bytes=PEER_VMEM_LIMIT)
    splat_i32 = pltpu.VMEM((per_blk, VREG_LANES), jnp.int32)
    splat_f32 = pltpu.VMEM((per_blk, VREG_LANES), jnp.float32)
    w = pl.pallas_call(
        _peer_act_kernel,
        grid=(nblk,),
        in_specs=[smem_spec, slot_spec, tok_spec, lane_spec, tab_spec],
        out_specs=lane_spec,
        out_shape=jax.ShapeDtypeStruct((nblk, 1, per_blk), jnp.float32),
        scratch_shapes=[splat_f32, splat_i32],
        compiler_params=params,
        name="peer_act",
    )(offs, shifts, t.reshape(T, d_sub, VREG_LANES), gate.reshape(nblk, 1, per_blk), _pack_rows_bf16(u_tab))
    out = pl.pallas_call(
        _peer_out_kernel,
        grid=(nblk,),
        in_specs=[smem_spec, slot_spec, slot_spec, tab_spec],
        out_specs=tok_spec,
        out_shape=jax.ShapeDtypeStruct((T, d_sub, VREG_LANES), jnp.float32),
        scratch_shapes=[splat_i32, splat_f32],
        compiler_params=params,
        name="peer_out",
    )(offs, shifts, w.reshape(T, PEER_SLOTS), _pack_rows_bf16(v_tab))
    return out.reshape(T, D_MODEL)


def _top_rows(s, k):
    n = s.shape[0]
    row = lax.broadcasted_iota(jnp.int32, s.shape, 0)
    vals, idxs = [], []
    for _ in range(k):
        m = jnp.max(s, axis=0, keepdims=True)
        i = jnp.min(jnp.where(s == m, row, n), axis=0, keepdims=True)
        vals.append(m)
        idxs.append(i)
        s = jnp.where(row == i, -jnp.inf, s)
    return jnp.concatenate(vals, axis=0), jnp.concatenate(idxs, axis=0)


def _pick_rows(table, which):
    out = jnp.zeros(which.shape, table.dtype)
    for a in range(table.shape[0]):
        out = jnp.where(which == a, table[a:a + 1, :], out)
    return out


PAIR_COUNTS = tuple(PEER_TOPK // (a + 1) for a in range(PEER_TOPK))
PAIR_STARTS = tuple(sum(PAIR_COUNTS[:a]) for a in range(PEER_TOPK))
N_PAIRS = sum(PAIR_COUNTS)
N_PAIRS_PADDED = -(-N_PAIRS // VREG_SUBLANES) * VREG_SUBLANES
SEL_TB = 512


def _peer_select_kernel(x_ref, sc_ref, sh_ref, wq_ref, k1_ref, k2_ref, off_ref, shift_ref, gate_ref):
    half = PEER_DK // 2
    h = (x_ref[...] * (1.0 + sc_ref[0]) + sh_ref[0]).astype(jnp.bfloat16)
    q = jnp.dot(h, wq_ref[...], preferred_element_type=jnp.float32).astype(jnp.bfloat16)
    contract_last = (((1,), (1,)), ((), ()))
    s1 = lax.dot_general(k1_ref[0], q[:, :half], contract_last, preferred_element_type=jnp.float32)
    s2 = lax.dot_general(k2_ref[0], q[:, half:], contract_last, preferred_element_type=jnp.float32)
    v1, i1 = _top_rows(s1, PEER_TOPK)
    v2, i2 = _top_rows(s2, PEER_TOPK)
    pairs = [v1[a:a + 1, :] + v2[:PAIR_COUNTS[a], :] for a in range(PEER_TOPK)]
    pairs.append(jnp.full((N_PAIRS_PADDED - N_PAIRS, v1.shape[1]), -jnp.inf, jnp.float32))
    top_s, pos = _top_rows(jnp.concatenate(pairs, axis=0), PEER_TOPK)
    a_sel = jnp.zeros(pos.shape, jnp.int32)
    start = jnp.zeros(pos.shape, jnp.int32)
    for a in range(1, PEER_TOPK):
        later = pos >= PAIR_STARTS[a]
        a_sel = jnp.where(later, a, a_sel)
        start = jnp.where(later, PAIR_STARTS[a], start)
    expert = _pick_rows(i1, a_sel) * PEER_NKEYS + _pick_rows(i2, pos - start)
    e = jnp.exp(top_s - top_s[0:1, :])
    gate_ref[0] = e / jnp.sum(e, axis=0, keepdims=True)
    off_ref[0] = (expert >> 1) * VREG_SUBLANES
    shift_ref[0] = jnp.where((expert & 1) == 1, 0, 16)


def peer_select(x, sc, sh, wq, k1, k2):
    T = x.shape[0]
    per_seq = T // sc.shape[0] // SEL_TB
    bf16 = jnp.bfloat16
    tok_spec = pl.BlockSpec((SEL_TB, D_MODEL), lambda i, h: (i, 0))
    mod_spec = pl.BlockSpec((1, 1, D_MODEL), lambda i, h: (i // per_seq, 0, 0))
    wq_spec = pl.BlockSpec((D_MODEL, PEER_DK), lambda i, h: (0, h))
    key_spec = pl.BlockSpec((1, PEER_NKEYS, PEER_DK // 2), lambda i, h: (h, 0, 0))
    out_spec = pl.BlockSpec((1, PEER_TOPK, SEL_TB), lambda i, h: (h, 0, i))
    shape = (PEER_HEADS, PEER_TOPK, T)
    off, shift, gate = pl.pallas_call(
        _peer_select_kernel,
        grid=(T // SEL_TB, PEER_HEADS),
        in_specs=[tok_spec, mod_spec, mod_spec, wq_spec, key_spec, key_spec],
        out_specs=[out_spec, out_spec, out_spec],
        out_shape=[jax.ShapeDtypeStruct(shape, jnp.int32), jax.ShapeDtypeStruct(shape, jnp.int32),
                   jax.ShapeDtypeStruct(shape, jnp.float32)],
        compiler_params=pltpu.CompilerParams(dimension_semantics=("arbitrary", "arbitrary")),
        name="peer_select",
    )(x, sc, sh, wq.astype(bf16), k1.astype(bf16), k2.astype(bf16))

    def to_tokens(a):
        return a.reshape(PEER_SLOTS, T).T

    return to_tokens(off), to_tokens(shift), to_tokens(gate)


def peer_ffn(x, sc, sh, wq, k1, k2, u_tab, v_tab):
    t = x.reshape(-1, D_MODEL)
    off, shift, gate = peer_select(t, sc, sh, wq, k1, k2)
    h = (x * (1 + sc) + sh).reshape(-1, D_MODEL)
    return peer_experts(h, off, shift, gate, u_tab, v_tab).reshape(x.shape)


def kernel(x, c, ctx, c_ctx, ada_w, ada_b, ln_g, ln_b,
           m_w_in, m_conv_w, m_conv_b, m_a_log, m_dt_bias, m_d, m_norm_w, m_w_out,
           h_w_in, h_conv_w, h_conv_b, h_f_w1, h_f_b1, h_f_w2, h_f_b2, h_f_w3, h_freq, h_bias, h_w_out,
           p_wq, p_k1, p_k2, p_u, p_v):
    rows = x.shape[1] // GRID_W
    s_lat = jax.nn.silu(c)
    s_ctx = jax.nn.silu(c_ctx)
    for i in range(DEPTH):
        kind, k = i % N_MIXERS, i // N_MIXERS
        sh1, sc1, g1, sh2, sc2, g2 = jnp.split((s_lat @ ada_w[i] + ada_b[i])[:, None, :], 6, axis=-1)
        hx = x * (1 + sc1) + sh1
        if kind == 0:
            csh1, csc1, cg1, csh2, csc2, cg2 = jnp.split(s_ctx @ ada_w[i] + ada_b[i], 6, axis=-1)
            hc = ctx * (1 + csc1) + csh1
            y_lat, _ = mamba2_mixer(hx, hc, rows, m_w_in[k], m_conv_w[k], m_conv_b[k], m_a_log[k],
                                    m_dt_bias[k], m_d[k], m_norm_w[k], m_w_out[k], False)
        else:
            hyena_w = (h_w_in[k], h_conv_w[k], h_conv_b[k], h_f_w1[k], h_f_b1[k], h_f_w2[k], h_f_b2[k],
                       h_f_w3[k], h_freq[k], h_bias[k], h_w_out[k])
            y_lat = hyena_mixer(hx, *hyena_w)
        peer_w = (p_wq[i], p_k1[i], p_k2[i], p_u[i], p_v[i])
        x = layer_norm(DEEPNORM_ALPHA * x + g1 * y_lat, ln_g[i, 0], ln_b[i, 0])
        x = layer_norm_pallas(DEEPNORM_ALPHA * x + g2 * peer_ffn(x, sc2, sh2, *peer_w),
                              ln_g[i, 1], ln_b[i, 1])
    return x
```

```python
import functools
import math

import jax
import jax.numpy as jnp
from jax import lax
from jax.experimental import pallas as pl
from jax.experimental.pallas import tpu as pltpu

D_MODEL = 1024
DEPTH = 2
GRID_W = 64
N_MIXERS = 2
DEEPNORM_ALPHA = (2 * DEPTH) ** 0.25
LN_EPS = 1e-5
RMS_EPS = 1e-5

SSM_INNER = 2 * D_MODEL
SSM_HEAD_DIM = 64
SSM_HEADS = SSM_INNER // SSM_HEAD_DIM
SSM_GROUPS = 4
SSM_STATE = 128
SSM_CHUNK = 128
SSM_CONV = 3
SSM_CONV_DIM = SSM_INNER + 2 * SSM_GROUPS * SSM_STATE

HYENA_ORDER = 2
HYENA_EMB = 33
HYENA_INNER_MLPS = 2
HYENA_STEEP_DECAY = 0.3
HYENA_GENTLE_DECAY = 1.5
HYENA_TARGET = 1e-2
HYENA_NORM_EPS = 1e-6

PEER_HEADS = 8
PEER_NKEYS = 128
PEER_DK = 256
PEER_TOPK = 16


def _ln_kernel(x_ref, g_ref, b_ref, o_ref):
    xf = x_ref[...]
    mu = jnp.mean(xf, axis=-1, keepdims=True)
    xc = xf - mu
    var = jnp.mean(xc * xc, axis=-1, keepdims=True)
    o_ref[...] = xc * lax.rsqrt(var + LN_EPS) * g_ref[...] + b_ref[...]


def layer_norm_pallas(x, g, b):
    shp = x.shape
    t = x.reshape(-1, D_MODEL)
    tb = 512
    out = pl.pallas_call(
        _ln_kernel,
        grid=(t.shape[0] // tb,),
        in_specs=[pl.BlockSpec((tb, D_MODEL), lambda i: (i, 0)),
                  pl.BlockSpec((1, D_MODEL), lambda i: (0, 0)),
                  pl.BlockSpec((1, D_MODEL), lambda i: (0, 0))],
        out_specs=pl.BlockSpec((tb, D_MODEL), lambda i: (i, 0)),
        out_shape=jax.ShapeDtypeStruct(t.shape, jnp.float32),
        name="layer_norm",
    )(t, g.reshape(1, D_MODEL), b.reshape(1, D_MODEL))
    return out.reshape(shp)


def layer_norm(x, g, b):
    xf = x.astype(jnp.float32)
    mu = jnp.mean(xf, axis=-1, keepdims=True)
    var = jnp.mean(jnp.square(xf - mu), axis=-1, keepdims=True)
    return ((xf - mu) * lax.rsqrt(var + LN_EPS) * g.astype(jnp.float32) + b.astype(jnp.float32)).astype(x.dtype)


def gated_group_rmsnorm(y, z, w):
    b, L, _ = y.shape
    u = (y * jax.nn.silu(z)).astype(jnp.float32).reshape(b, L, SSM_GROUPS, -1)
    u = u * lax.rsqrt(jnp.mean(jnp.square(u), axis=-1, keepdims=True) + RMS_EPS)
    return (u.reshape(b, L, SSM_INNER) * w.astype(jnp.float32)).astype(y.dtype)


def depthwise_conv_grid(u, w, bias, rows, cols):
    b, L, C = u.shape
    out = lax.conv_general_dilated(u.reshape(b, rows, cols, C), w.astype(u.dtype)[:, :, None, :],
                                   window_strides=(1, 1), padding='SAME',
                                   dimension_numbers=('NHWC', 'HWIO', 'NHWC'), feature_group_count=C)
    return out.reshape(b, L, C) + bias


def ssd_scan(xs, dt, a_neg, bm, cm, init, want_y):
    f32 = jnp.float32
    b, L, H, P = xs.shape
    G, N = bm.shape[2], bm.shape[3]
    R = H // G
    Q = SSM_CHUNK
    nc = L // Q
    xdt = (xs.astype(f32) * dt[..., None]).reshape(b, nc, Q, G, R, P)
    bc = bm.astype(f32).reshape(b, nc, Q, G, N)
    cc = cm.astype(f32).reshape(b, nc, Q, G, N)
    a = (dt * a_neg.astype(f32)).reshape(b, nc, Q, G, R)
    a_cum = jnp.cumsum(a, axis=2)
    decay_to_end = jnp.exp(a_cum[:, :, -1:] - a_cum)
    chunk_states = jnp.einsum('bclgn,bclgrp->bcgrpn', bc, xdt * decay_to_end[..., None])
    chunk_decay = jnp.exp(a_cum[:, :, -1])

    def step(state, inp):
        cs, cd = inp
        return state * cd[..., None, None] + cs, state

    final, prev = lax.scan(step, init.astype(f32).reshape(b, G, R, P, N),
                           (jnp.moveaxis(chunk_states, 1, 0), jnp.moveaxis(chunk_decay, 1, 0)))
    final = final.reshape(b, H, P, N)
    if not want_y:
        return None, final
    prev = jnp.moveaxis(prev, 0, 1)
    seg = a_cum[:, :, :, None] - a_cum[:, :, None, :]
    lower = jnp.tril(jnp.ones((Q, Q), dtype=bool))[None, None, :, :, None, None]
    decay_ls = jnp.exp(jnp.where(lower, seg, -jnp.inf))
    cb = jnp.einsum('bclgn,bcsgn->bclsg', cc, bc)
    y_diag = jnp.einsum('bclsgr,bcsgrp->bclgrp', cb[..., None] * decay_ls, xdt)
    y_off = jnp.einsum('bclgn,bcgrpn->bclgrp', cc, prev) * jnp.exp(a_cum)[..., None]
    y = (y_diag + y_off).reshape(b, L, H, P)
    return y.astype(xs.dtype), final


def mamba2_mixer(h_lat, h_ctx, rows, w_in, conv_w, conv_b, a_log, dt_bias, d_skip, norm_w, w_out, ctx_out):
    f32 = jnp.float32
    a_neg = -jnp.exp(a_log.astype(f32))
    G, N = SSM_GROUPS, SSM_STATE

    def branch(h, rows_, cols_, need_z):
        bb, LL, _ = h.shape
        proj = h @ (w_in if need_z else w_in[:, SSM_INNER:])
        z = None
        if need_z:
            z, proj = proj[..., :SSM_INNER], proj[..., SSM_INNER:]
        xbc = jax.nn.silu(depthwise_conv_grid(proj[..., :SSM_CONV_DIM], conv_w, conv_b, rows_, cols_))
        dt = jax.nn.softplus(proj[..., SSM_CONV_DIM:].astype(f32).reshape(bb, LL, 2, SSM_HEADS)
                             + dt_bias.astype(f32))
        xs = xbc[..., :SSM_INNER].reshape(bb, LL, SSM_HEADS, SSM_HEAD_DIM)
        bm = xbc[..., SSM_INNER:SSM_INNER + G * N].reshape(bb, LL, G, N)
        cm = xbc[..., SSM_INNER + G * N:].reshape(bb, LL, G, N)
        return z, xs, bm, cm, dt

    def scans(xs, bm, cm, dt, init_f, init_b, want_y):
        y_f, s_f = ssd_scan(xs, dt[:, :, 0], a_neg[0], bm, cm, init_f, want_y)
        y_b, s_b = ssd_scan(jnp.flip(xs, 1), jnp.flip(dt[:, :, 1], 1), a_neg[1],
                            jnp.flip(bm, 1), jnp.flip(cm, 1), init_b, want_y)
        y = None
        if want_y:
            y = y_f + jnp.flip(y_b, 1) + xs * d_skip[:, None]
        return y, s_f, s_b

    def finish(y, z):
        bb, LL = y.shape[:2]
        return gated_group_rmsnorm(y.reshape(bb, LL, SSM_INNER), z, norm_w) @ w_out

    bsz = h_ctx.shape[0]
    zeros = jnp.zeros((bsz, SSM_HEADS, SSM_HEAD_DIM, SSM_STATE), f32)
    zc, xc, bmc, cmc, dtc = branch(h_ctx, 1, h_ctx.shape[1], ctx_out)
    yc, s_f, s_b = scans(xc, bmc, cmc, dtc, zeros, zeros, ctx_out)
    zl, xl, bml, cml, dtl = branch(h_lat, rows, GRID_W, True)
    yl, _, _ = scans(xl, bml, cml, dtl, s_f, s_b, True)
    out_ctx = finish(yc, zc) if ctx_out else None
    return finish(yl, zl), out_ctx


def hyena_filters(L, w1, b1, w2, b2, w3, freq):
    f32 = jnp.float32
    t = jnp.linspace(0.0, 1.0, L, dtype=f32)[:, None]
    bands = (HYENA_EMB - 1) // 2
    ang = 2.0 * math.pi * jnp.arange(L, dtype=f32)[:, None] / L
    fb = jnp.linspace(1e-4, bands - 1, bands, dtype=f32)[None, :]
    z = jnp.concatenate([t, jnp.cos(fb * ang), -jnp.sin(fb * ang)], axis=-1)
    fr = freq.astype(f32)
    hid = jnp.sin(fr * (z @ w1.astype(f32) + b1.astype(f32)))
    for j in range(HYENA_INNER_MLPS):
        hid = jnp.sin(fr * (hid @ w2[j].astype(f32) + b2[j].astype(f32)))
    h = (hid @ w3.astype(f32)).reshape(L, HYENA_ORDER, 2, D_MODEL)
    max_decay = math.log(HYENA_TARGET) / HYENA_STEEP_DECAY
    min_decay = math.log(HYENA_TARGET) / HYENA_GENTLE_DECAY
    deltas = jnp.linspace(min_decay, max_decay, D_MODEL, dtype=f32)
    h = h * jnp.exp(-t * jnp.abs(deltas))[:, None, None, :]
    h_fwd, h_bwd = h[:, :, 0], h[:, :, 1]
    two_sided = jnp.concatenate([h_fwd[:1] + h_bwd[:1], h_fwd[1:],
                                 jnp.zeros((1, HYENA_ORDER, D_MODEL), f32),
                                 jnp.flip(h_bwd[1:], 0)], axis=0)
    two_sided = two_sided * lax.rsqrt(jnp.sum(jnp.square(two_sided), axis=0, keepdims=True) + HYENA_NORM_EPS)
    return jnp.fft.fft(two_sided, axis=0)


VREG_SUBLANES = 8
VREG_LANES = 128
FFT_N1 = 128
FFT_N2 = 64
FFT_N = FFT_N1 * FFT_N2
CONV_L = FFT_N // 2
CONV_T1 = CONV_L // FFT_N2
FFT_VMEM_LIMIT = 56 * 1024 * 1024
FFT_UNROLL = 4


def _dft_tables():
    def cos_sin(phase, n):
        ang = (2.0 * math.pi / n) * (phase % n).astype(jnp.float32)
        return jnp.cos(ang), jnp.sin(ang)
    t2 = jnp.arange(FFT_N2, dtype=jnp.int32)[:, None, None]
    k1 = jnp.arange(FFT_N1, dtype=jnp.int32)[None, :, None]
    t1 = jnp.arange(CONV_T1, dtype=jnp.int32)[None, None, :]
    c, s = cos_sin(FFT_N2 * t1 * k1 + t2 * k1, FFT_N)
    fwd = jnp.concatenate([c, -s], axis=1)
    inv = jnp.concatenate([c, s], axis=1).reshape(FFT_N2, 2, FFT_N1, CONV_T1)
    inv = inv.transpose(0, 1, 3, 2).reshape(FFT_N2, 2 * CONV_T1, FFT_N1) / FFT_N
    k2 = jnp.arange(FFT_N2, dtype=jnp.int32)
    c2, s2 = cos_sin(k2[:, None] * k2[None, :], FFT_N2)
    mid = jnp.concatenate([c2, -s2], axis=0)
    return fwd, mid, inv


def _split_bf16(w):
    hi = w.astype(jnp.bfloat16)
    return hi, (w - hi.astype(jnp.float32)).astype(jnp.bfloat16)


def _mm3(w_hi, w_lo, x):
    x_hi, x_lo = _split_bf16(x)
    f32 = jnp.float32
    return (jnp.dot(w_hi, x_hi, preferred_element_type=f32) + jnp.dot(w_lo, x_hi, preferred_element_type=f32)
            + jnp.dot(w_hi, x_lo, preferred_element_type=f32))


def _long_conv_kernel(x_ref, g_ref, hr_ref, hi_ref, b_ref, wfh_ref, wfl_ref, wmh_ref, wml_ref, wih_ref, wil_ref,
                      o_ref, sr_ref, si_ref):
    def rows_t2(t2):
        return pl.ds(t2, CONV_T1, stride=FFT_N2)

    def stage_a(t2, carry):
        wh, wl = wfh_ref[t2], wfl_ref[t2]
        p = _mm3(wh, wl, x_ref.at[0][rows_t2(t2), :])
        q = _mm3(wh, wl, x_ref.at[1][rows_t2(t2), :])
        dst = pl.ds(pl.multiple_of(t2 * FFT_N1, FFT_N1), FFT_N1)
        sr_ref[dst, :] = p[:FFT_N1] - q[FFT_N1:]
        si_ref[dst, :] = p[FFT_N1:] + q[:FFT_N1]
        return carry

    lax.fori_loop(0, FFT_N2, stage_a, 0, unroll=FFT_UNROLL)
    wmh, wml = wmh_ref[...], wml_ref[...]

    def per_k1(k1, carry):
        sel = pl.ds(k1, FFT_N2, stride=FFT_N1)
        p = _mm3(wmh, wml, sr_ref[sel, :])
        q = _mm3(wmh, wml, si_ref[sel, :])
        xr = p[:FFT_N2] - q[FFT_N2:]
        xi = p[FFT_N2:] + q[:FFT_N2]
        hsel = pl.ds(pl.multiple_of(k1 * FFT_N2, FFT_N2), FFT_N2)
        hr = hr_ref[hsel, :]
        hi = hi_ref[hsel, :]
        p = _mm3(wmh, wml, xr * hr - xi * hi)
        q = _mm3(wmh, wml, xr * hi + xi * hr)
        sr_ref[sel, :] = p[:FFT_N2] + q[FFT_N2:]
        si_ref[sel, :] = q[:FFT_N2] - p[FFT_N2:]
        return carry

    lax.fori_loop(0, FFT_N1, per_k1, 0, unroll=FFT_UNROLL)
    bias = b_ref[...]

    def stage_c(t2, carry):
        wh, wl = wih_ref[t2], wil_ref[t2]
        src = pl.ds(pl.multiple_of(t2 * FFT_N1, FFT_N1), FFT_N1)
        p = _mm3(wh, wl, sr_ref[src, :])
        q = _mm3(wh, wl, si_ref[src, :])
        rows = rows_t2(t2)
        o_ref.at[0][rows, :] = g_ref.at[0][rows, :] * (p[:CONV_T1] - q[CONV_T1:] + x_ref.at[0][rows, :] * bias)
        o_ref.at[1][rows, :] = g_ref.at[1][rows, :] * (q[:CONV_T1] + p[CONV_T1:] + x_ref.at[1][rows, :] * bias)
        return carry

    lax.fori_loop(0, FFT_N2, stage_c, 0, unroll=FFT_UNROLL)


def _spectrum_layout(hf_full):
    c = hf_full.shape[1]
    return hf_full.reshape(FFT_N2, FFT_N1, c).transpose(1, 0, 2).reshape(FFT_N, c)


def long_conv_gated(u, u_col, gate, gate_col, hf, bias, tables):
    B, L, _ = u.shape
    C = hf.shape[1]
    assert L == CONV_L and B % 2 == 0 and C % VREG_LANES == 0
    assert u_col % VREG_LANES == 0 and gate_col % VREG_LANES == 0
    h_re, h_im = _spectrum_layout(jnp.real(hf)), _spectrum_layout(jnp.imag(hf))
    consts = [part for t in tables for part in _split_bf16(t)]
    sig_spec = pl.BlockSpec((2, L, VREG_LANES), lambda c, b: (b, 0, c))
    u_spec = pl.BlockSpec((2, L, VREG_LANES), lambda c, b: (b, 0, c + u_col // VREG_LANES))
    gate_spec = pl.BlockSpec((2, L, VREG_LANES), lambda c, b: (b, 0, c + gate_col // VREG_LANES))
    once = pl.Buffered(1)
    spec_h = pl.BlockSpec((FFT_N, VREG_LANES), lambda c, b: (0, c), pipeline_mode=once)
    const_specs = [pl.BlockSpec(t.shape, functools.partial(lambda nd, c, b: (0,) * nd, t.ndim), pipeline_mode=once)
                   for t in consts]
    return pl.pallas_call(
        _long_conv_kernel,
        grid=(C // VREG_LANES, B // 2),
        in_specs=[u_spec, gate_spec, spec_h, spec_h, pl.BlockSpec((1, VREG_LANES), lambda c, b: (0, c))] + const_specs,
        out_specs=sig_spec,
        out_shape=jax.ShapeDtypeStruct((B, L, C), jnp.float32),
        scratch_shapes=[pltpu.VMEM((FFT_N, VREG_LANES), jnp.float32), pltpu.VMEM((FFT_N, VREG_LANES), jnp.float32)],
        compiler_params=pltpu.CompilerParams(dimension_semantics=("arbitrary", "arbitrary"),
                                             vmem_limit_bytes=FFT_VMEM_LIMIT),
        name="long_conv",
    )(u, gate, h_re, h_im, bias.reshape(1, C), *consts)


def short_conv_centred(u, w, bias):
    up = jnp.pad(u, ((0, 0), (1, 1), (0, 0)))
    return up[:, :-2] * w[0] + up[:, 1:-1] * w[1] + up[:, 2:] * w[2] + bias


def hyena_mixer(h, w_in, conv_w, conv_b, w1, b1, w2, b2, w3, freq, fbias, w_out):
    L = h.shape[1]
    hf = hyena_filters(L, w1, b1, w2, b2, w3, freq)
    u = short_conv_centred(h @ w_in, conv_w, conv_b)
    tables = _dft_tables()
    zz = long_conv_gated(u, 2 * D_MODEL, u, 0, hf[:, 0], fbias[0], tables)
    zz = long_conv_gated(zz, 0, u, D_MODEL, hf[:, 1], fbias[1], tables)
    return zz @ w_out


PEER_SLOTS = PEER_HEADS * PEER_TOPK
PEER_TB = 64
PEER_VMEM_LIMIT = 52 * 1024 * 1024
HI_MASK = -65536


def _pack_rows_bf16(tab):
    n, d = tab.shape
    bits = lax.bitcast_convert_type(tab.astype(jnp.bfloat16), jnp.uint16).astype(jnp.uint32)
    packed = (bits[1::2] << 16) | bits[0::2]
    return lax.bitcast_convert_type(packed, jnp.int32).reshape(n // 2 * (d // VREG_LANES), VREG_LANES)


def _unpack_row(tab_ref, off, shift):
    word = tab_ref[pl.ds(pl.multiple_of(off, VREG_SUBLANES), VREG_SUBLANES), :]
    return pltpu.bitcast((word << shift) & HI_MASK, jnp.float32)


def _sublane_sums(tiles, sub):
    for h in (1, 2, 4):
        keep = (sub & h) == 0
        nxt = []
        for a, b in zip(tiles[0::2], tiles[1::2]):
            nxt.append(jnp.where(keep, a, b) + pltpu.roll(jnp.where(keep, b, a), h, 0))
        tiles = nxt
    return tiles[0]


def _splat_tile(row):
    return jnp.broadcast_to(row, (PEER_SLOTS, VREG_LANES)).T


def _splat_small_ints_tile(row, eye, ones):
    diag = jnp.where(eye, row.astype(jnp.float32), 0.0).astype(jnp.bfloat16)
    return jnp.dot(diag, ones, preferred_element_type=jnp.float32).astype(jnp.int32)


def _splat_consts():
    eye = (lax.broadcasted_iota(jnp.int32, (PEER_SLOTS, PEER_SLOTS), 0)
           == lax.broadcasted_iota(jnp.int32, (PEER_SLOTS, PEER_SLOTS), 1))
    return eye, jnp.ones((PEER_SLOTS, VREG_LANES), jnp.bfloat16)


def _splat_block(shift_ref, shift_s, w_ref=None, w_s=None):
    eye, ones = _splat_consts()

    def token(t, carry):
        dst = pl.ds(pl.multiple_of(t * PEER_SLOTS, PEER_SLOTS), PEER_SLOTS)
        shift_s[dst, :] = _splat_small_ints_tile(shift_ref[pl.ds(t, 1), :], eye, ones)
        if w_ref is not None:
            w_s[dst, :] = _splat_tile(w_ref[pl.ds(t, 1), :])
        return carry
    lax.fori_loop(0, PEER_TB, token, 0, unroll=8)


def _peer_act_kernel(row_ref, shift_ref, x_ref, gate_ref, tab_ref, w_ref, part_ref, shift_s):
    sub = lax.broadcasted_iota(jnp.int32, (VREG_SUBLANES, VREG_LANES), 0)
    _splat_block(shift_ref, shift_s)

    def token(t, carry):
        xt = x_ref[t]
        base = pl.multiple_of(t * PEER_SLOTS, PEER_SLOTS)
        tok_rows = row_ref.at[0, 0, pl.ds(base, PEER_SLOTS)]
        for g in range(PEER_SLOTS // VREG_SUBLANES):
            prods = []
            for k in range(VREG_SUBLANES):
                s = g * VREG_SUBLANES + k
                prods.append(_unpack_row(tab_ref, tok_rows[s], shift_s[pl.ds(base + s, 1), :]) * xt)
            part_ref[pl.ds(base + g * VREG_SUBLANES, VREG_SUBLANES), :] = _sublane_sums(prods, sub)
        return carry

    lax.fori_loop(0, PEER_TB, token, 0)
    part = part_ref[...]
    part_hi = part.astype(jnp.bfloat16)
    part_lo = (part - part_hi.astype(jnp.float32)).astype(jnp.bfloat16)
    contract_lanes = (((1,), (1,)), ((), ()))
    ones = jnp.ones((VREG_SUBLANES, VREG_LANES), jnp.bfloat16)
    act = (lax.dot_general(ones, part_hi, contract_lanes, preferred_element_type=jnp.float32)
           + lax.dot_general(ones, part_lo, contract_lanes, preferred_element_type=jnp.float32))[0:1]
    gelu = 0.5 * act * (1.0 + lax.erf(act * (2.0 ** -0.5)))
    w_ref[0] = gate_ref[0] * gelu


def _peer_out_kernel(row_ref, shift_ref, w_ref, tab_ref, o_ref, shift_s, w_s):
    n_acc = 4
    _splat_block(shift_ref, shift_s, w_ref, w_s)

    def token(t, carry):
        base = pl.multiple_of(t * PEER_SLOTS, PEER_SLOTS)
        tok_rows = row_ref.at[0, 0, pl.ds(base, PEER_SLOTS)]
        accs = [jnp.zeros((VREG_SUBLANES, VREG_LANES), jnp.float32) for _ in range(n_acc)]
        for s in range(PEER_SLOTS):
            val = _unpack_row(tab_ref, tok_rows[s], shift_s[pl.ds(base + s, 1), :])
            accs[s % n_acc] = accs[s % n_acc] + w_s[pl.ds(base + s, 1), :] * val
        o_ref[t] = (accs[0] + accs[1]) + (accs[2] + accs[3])
        return carry

    lax.fori_loop(0, PEER_TB, token, 0)


def peer_experts(t, offs, shifts, gate, u_tab, v_tab):
    T = t.shape[0]
    nblk = T // PEER_TB
    per_blk = PEER_TB * PEER_SLOTS
    d_sub = D_MODEL // VREG_LANES
    offs = offs.reshape(nblk, 1, per_blk)
    smem_spec = pl.BlockSpec((1, 1, per_blk), lambda i: (i, 0, 0), memory_space=pltpu.SMEM)
    lane_spec = pl.BlockSpec((1, 1, per_blk), lambda i: (i, 0, 0))
    slot_spec = pl.BlockSpec((PEER_TB, PEER_SLOTS), lambda i: (i, 0))
    tok_spec = pl.BlockSpec((PEER_TB, d_sub, VREG_LANES), lambda i: (i, 0, 0))
    tab_spec = pl.BlockSpec(memory_space=pltpu.VMEM)
    params = pltpu.CompilerParams(dimension_semantics=("arbitrary",), vmem_limit_bytes=PEER_VMEM_LIMIT)
    splat_i32 = pltpu.VMEM((per_blk, VREG_LANES), jnp.int32)
    splat_f32 = pltpu.VMEM((per_blk, VREG_LANES), jnp.float32)
    w = pl.pallas_call(
        _peer_act_kernel,
        grid=(nblk,),
        in_specs=[smem_spec, slot_spec, tok_spec, lane_spec, tab_spec],
        out_specs=lane_spec,
        out_shape=jax.ShapeDtypeStruct((nblk, 1, per_blk), jnp.float32),
        scratch_shapes=[splat_f32, splat_i32],
        compiler_params=params,
        name="peer_act",
    )(offs, shifts, t.reshape(T, d_sub, VREG_LANES), gate.reshape(nblk, 1, per_blk), _pack_rows_bf16(u_tab))
    out = pl.pallas_call(
        _peer_out_kernel,
        grid=(nblk,),
        in_specs=[smem_spec, slot_spec, slot_spec, tab_spec],
        out_specs=tok_spec,
        out_shape=jax.ShapeDtypeStruct((T, d_sub, VREG_LANES), jnp.float32),
        scratch_shapes=[splat_i32, splat_f32],
        compiler_params=params,
        name="peer_out",
    )(offs, shifts, w.reshape(T, PEER_SLOTS), _pack_rows_bf16(v_tab))
    return out.reshape(T, D_MODEL)


def _top_rows(s, k):
    n = s.shape[0]
    row = lax.broadcasted_iota(jnp.int32, s.shape, 0)
    vals, idxs = [], []
    for _ in range(k):
        m = jnp.max(s, axis=0, keepdims=True)
        i = jnp.min(jnp.where(s == m, row, n), axis=0, keepdims=True)
        vals.append(m)
        idxs.append(i)
        s = jnp.where(row == i, -jnp.inf, s)
    return jnp.concatenate(vals, axis=0), jnp.concatenate(idxs, axis=0)


def _pick_rows(table, which):
    out = jnp.zeros(which.shape, table.dtype)
    for a in range(table.shape[0]):
        out = jnp.where(which == a, table[a:a + 1, :], out)
    return out


PAIR_COUNTS = tuple(PEER_TOPK // (a + 1) for a in range(PEER_TOPK))
PAIR_STARTS = tuple(sum(PAIR_COUNTS[:a]) for a in range(PEER_TOPK))
N_PAIRS = sum(PAIR_COUNTS)
N_PAIRS_PADDED = -(-N_PAIRS // VREG_SUBLANES) * VREG_SUBLANES
SEL_TB = 512


def _peer_select_kernel(x_ref, sc_ref, sh_ref, wq_ref, k1_ref, k2_ref, off_ref, shift_ref, gate_ref):
    half = PEER_DK // 2
    h = (x_ref[...] * (1.0 + sc_ref[0]) + sh_ref[0]).astype(jnp.bfloat16)
    q = jnp.dot(h, wq_ref[...], preferred_element_type=jnp.float32).astype(jnp.bfloat16)
    contract_last = (((1,), (1,)), ((), ()))
    s1 = lax.dot_general(k1_ref[0], q[:, :half], contract_last, preferred_element_type=jnp.float32)
    s2 = lax.dot_general(k2_ref[0], q[:, half:], contract_last, preferred_element_type=jnp.float32)
    v1, i1 = _top_rows(s1, PEER_TOPK)
    v2, i2 = _top_rows(s2, PEER_TOPK)
    pairs = [v1[a:a + 1, :] + v2[:PAIR_COUNTS[a], :] for a in range(PEER_TOPK)]
    pairs.append(jnp.full((N_PAIRS_PADDED - N_PAIRS, v1.shape[1]), -jnp.inf, jnp.float32))
    top_s, pos = _top_rows(jnp.concatenate(pairs, axis=0), PEER_TOPK)
    a_sel = jnp.zeros(pos.shape, jnp.int32)
    start = jnp.zeros(pos.shape, jnp.int32)
    for a in range(1, PEER_TOPK):
        later = pos >= PAIR_STARTS[a]
        a_sel = jnp.where(later, a, a_sel)
        start = jnp.where(later, PAIR_STARTS[a], start)
    expert = _pick_rows(i1, a_sel) * PEER_NKEYS + _pick_rows(i2, pos - start)
    e = jnp.exp(top_s - top_s[0:1, :])
    gate_ref[0] = e / jnp.sum(e, axis=0, keepdims=True)
    off_ref[0] = (expert >> 1) * VREG_SUBLANES
    shift_ref[0] = jnp.where((expert & 1) == 1, 0, 16)


def peer_select(x, sc, sh, wq, k1, k2):
    T = x.shape[0]
    per_seq = T // sc.shape[0] // SEL_TB
    bf16 = jnp.bfloat16
    tok_spec = pl.BlockSpec((SEL_TB, D_MODEL), lambda i, h: (i, 0))
    mod_spec = pl.BlockSpec((1, 1, D_MODEL), lambda i, h: (i // per_seq, 0, 0))
    wq_spec = pl.BlockSpec((D_MODEL, PEER_DK), lambda i, h: (0, h))
    key_spec = pl.BlockSpec((1, PEER_NKEYS, PEER_DK // 2), lambda i, h: (h, 0, 0))
    out_spec = pl.BlockSpec((1, PEER_TOPK, SEL_TB), lambda i, h: (h, 0, i))
    shape = (PEER_HEADS, PEER_TOPK, T)
    off, shift, gate = pl.pallas_call(
        _peer_select_kernel,
        grid=(T // SEL_TB, PEER_HEADS),
        in_specs=[tok_spec, mod_spec, mod_spec, wq_spec, key_spec, key_spec],
        out_specs=[out_spec, out_spec, out_spec],
        out_shape=[jax.ShapeDtypeStruct(shape, jnp.int32), jax.ShapeDtypeStruct(shape, jnp.int32),
                   jax.ShapeDtypeStruct(shape, jnp.float32)],
        compiler_params=pltpu.CompilerParams(dimension_semantics=("arbitrary", "arbitrary")),
        name="peer_select",
    )(x, sc, sh, wq.astype(bf16), k1.astype(bf16), k2.astype(bf16))

    def to_tokens(a):
        return a.reshape(PEER_SLOTS, T).T

    return to_tokens(off), to_tokens(shift), to_tokens(gate)


def peer_ffn(x, sc, sh, wq, k1, k2, u_tab, v_tab):
    t = x.reshape(-1, D_MODEL)
    off, shift, gate = peer_select(t, sc, sh, wq, k1, k2)
    h = (x * (1 + sc) + sh).reshape(-1, D_MODEL)
    return peer_experts(h, off, shift, gate, u_tab, v_tab).reshape(x.shape)


def kernel(x, c, ctx, c_ctx, ada_w, ada_b, ln_g, ln_b,
           m_w_in, m_conv_w, m_conv_b, m_a_log, m_dt_bias, m_d, m_norm_w, m_w_out,
           h_w_in, h_conv_w, h_conv_b, h_f_w1, h_f_b1, h_f_w2, h_f_b2, h_f_w3, h_freq, h_bias, h_w_out,
           p_wq, p_k1, p_k2, p_u, p_v):
    rows = x.shape[1] // GRID_W
    s_lat = jax.nn.silu(c)
    s_ctx = jax.nn.silu(c_ctx)
    for i in range(DEPTH):
        kind, k = i % N_MIXERS, i // N_MIXERS
        sh1, sc1, g1, sh2, sc2, g2 = jnp.split((s_lat @ ada_w[i] + ada_b[i])[:, None, :], 6, axis=-1)
        hx = x * (1 + sc1) + sh1
        if kind == 0:
            csh1, csc1, cg1, csh2, csc2, cg2 = jnp.split(s_ctx @ ada_w[i] + ada_b[i], 6, axis=-1)
            hc = ctx * (1 + csc1) + csh1
            y_lat, _ = mamba2_mixer(hx, hc, rows, m_w_in[k], m_conv_w[k], m_conv_b[k], m_a_log[k],
                                    m_dt_bias[k], m_d[k], m_norm_w[k], m_w_out[k], False)
        else:
            hyena_w = (h_w_in[k], h_conv_w[k], h_conv_b[k], h_f_w1[k], h_f_b1[k], h_f_w2[k], h_f_b2[k],
                       h_f_w3[k], h_freq[k], h_bias[k], h_w_out[k])
            y_lat = hyena_mixer(hx, *hyena_w)
        peer_w = (p_wq[i], p_k1[i], p_k2[i], p_u[i], p_v[i])
        x = layer_norm(DEEPNORM_ALPHA * x + g1 * y_lat, ln_g[i, 0], ln_b[i, 0])
        x = layer_norm_pallas(DEEPNORM_ALPHA * x + g2 * peer_ffn(x, sc2, sh2, *peer_w),
                              ln_g[i, 1], ln_b[i, 1])
    return x
```

```python
import functools
import math

import jax
import jax.numpy as jnp
from jax import lax
from jax.experimental import pallas as pl
from jax.experimental.pallas import tpu as pltpu

D_MODEL = 1024
DEPTH = 2
GRID_W = 64
N_MIXERS = 2
DEEPNORM_ALPHA = (2 * DEPTH) ** 0.25
LN_EPS = 1e-5
RMS_EPS = 1e-5

SSM_INNER = 2 * D_MODEL
SSM_HEAD_DIM = 64
SSM_HEADS = SSM_INNER // SSM_HEAD_DIM
SSM_GROUPS = 4
SSM_STATE = 128
SSM_CHUNK = 128
SSM_CONV = 3
SSM_CONV_DIM = SSM_INNER + 2 * SSM_GROUPS * SSM_STATE
SSM_GROUP_HEADS = SSM_HEADS // SSM_GROUPS

HYENA_ORDER = 2
HYENA_EMB = 33
HYENA_INNER_MLPS = 2
HYENA_STEEP_DECAY = 0.3
HYENA_GENTLE_DECAY = 1.5
HYENA_TARGET = 1e-2
HYENA_NORM_EPS = 1e-6

PEER_HEADS = 8
PEER_NKEYS = 128
PEER_DK = 256
PEER_TOPK = 16

VREG_SUBLANES = 8
VREG_LANES = 128
DENSE_VMEM_LIMIT = 48 * 1024 * 1024


def _mod_matmul_kernel(x_ref, sc_ref, sh_ref, w_ref, o_ref):
    h = (x_ref[...] * (1.0 + sc_ref[0]) + sh_ref[0]).astype(jnp.bfloat16)
    o_ref[...] = jnp.dot(h, w_ref[...], preferred_element_type=jnp.float32)


def mod_matmul(x, sc, sh, w, tb, tn):
    T, D = x.shape
    N = w.shape[1]
    assert T % tb == 0 and N % tn == 0 and (T // sc.shape[0]) % tb == 0
    per_mod = T // sc.shape[0] // tb
    mod_spec = pl.BlockSpec((1, 1, D), lambda i, j: (i // per_mod, 0, 0))
    return pl.pallas_call(
        _mod_matmul_kernel,
        grid=(T // tb, N // tn),
        in_specs=[pl.BlockSpec((tb, D), lambda i, j: (i, 0)), mod_spec, mod_spec,
                  pl.BlockSpec((D, tn), lambda i, j: (0, j))],
        out_specs=pl.BlockSpec((tb, tn), lambda i, j: (i, j)),
        out_shape=jax.ShapeDtypeStruct((T, N), jnp.float32),
        compiler_params=pltpu.CompilerParams(dimension_semantics=("arbitrary", "arbitrary"),
                                             vmem_limit_bytes=DENSE_VMEM_LIMIT),
        name="mod_matmul",
    )(x, sc, sh, w.astype(jnp.bfloat16))


def _layer_norm_rows(v, g, b):
    mu = jnp.mean(v, axis=-1, keepdims=True)
    vc = v - mu
    var = jnp.mean(vc * vc, axis=-1, keepdims=True)
    return vc * lax.rsqrt(var + LN_EPS) * g + b


def _proj_ln_kernel(a_ref, w_ref, x_ref, gate_ref, g_ref, b_ref, o_ref):
    y = jnp.dot(a_ref[...].astype(jnp.bfloat16), w_ref[...], preferred_element_type=jnp.float32)
    o_ref[...] = _layer_norm_rows(DEEPNORM_ALPHA * x_ref[...] + gate_ref[0] * y, g_ref[...], b_ref[...])


def _row_specs(tb, per_mod):
    tok = pl.BlockSpec((tb, D_MODEL), lambda i: (i, 0))
    mod = pl.BlockSpec((1, 1, D_MODEL), lambda i: (i // per_mod, 0, 0))
    vec = pl.BlockSpec((1, D_MODEL), lambda i: (0, 0))
    return tok, mod, vec


def proj_ln(a, w, x, gate, ln_g, ln_b, tb=512):
    T, K = a.shape
    per_mod = T // gate.shape[0] // tb
    tok, mod, vec = _row_specs(tb, per_mod)
    return pl.pallas_call(
        _proj_ln_kernel,
        grid=(T // tb,),
        in_specs=[pl.BlockSpec((tb, K), lambda i: (i, 0)), pl.BlockSpec((K, D_MODEL), lambda i: (0, 0)),
                  tok, mod, vec, vec],
        out_specs=tok,
        out_shape=jax.ShapeDtypeStruct((T, D_MODEL), jnp.float32),
        compiler_params=pltpu.CompilerParams(dimension_semantics=("arbitrary",), vmem_limit_bytes=DENSE_VMEM_LIMIT),
        name="proj_ln",
    )(a, w.astype(jnp.bfloat16), x, gate, ln_g.reshape(1, D_MODEL), ln_b.reshape(1, D_MODEL))


def _residual_ln_kernel(y_ref, x_ref, gate_ref, g_ref, b_ref, o_ref):
    o_ref[...] = _layer_norm_rows(DEEPNORM_ALPHA * x_ref[...] + gate_ref[0] * y_ref[...], g_ref[...], b_ref[...])


def residual_ln(y, x, gate, ln_g, ln_b, tb=512):
    T = x.shape[0]
    tok, mod, vec = _row_specs(tb, T // gate.shape[0] // tb)
    return pl.pallas_call(
        _residual_ln_kernel,
        grid=(T // tb,),
        in_specs=[tok, tok, mod, vec, vec],
        out_specs=tok,
        out_shape=jax.ShapeDtypeStruct((T, D_MODEL), jnp.float32),
        compiler_params=pltpu.CompilerParams(dimension_semantics=("arbitrary",)),
        name="residual_ln",
    )(y, x, gate, ln_g.reshape(1, D_MODEL), ln_b.reshape(1, D_MODEL))


def _mamba_out_kernel(y_ref, xs_ref, z_ref, dsk_ref, nw_ref, w_ref, x_ref, gate_ref, g_ref, b_ref, o_ref):
    z = z_ref[...]
    u = (y_ref[0] + y_ref[1] + xs_ref[...] * dsk_ref[...]) * (z * jax.nn.sigmoid(z))
    gw = SSM_INNER // SSM_GROUPS
    parts = []
    for g in range(SSM_GROUPS):
        ug = u[:, g * gw:(g + 1) * gw]
        parts.append(ug * lax.rsqrt(jnp.mean(ug * ug, axis=-1, keepdims=True) + RMS_EPS))
    a = (jnp.concatenate(parts, axis=1) * nw_ref[...]).astype(jnp.bfloat16)
    y = jnp.dot(a, w_ref[...], preferred_element_type=jnp.float32)
    o_ref[...] = _layer_norm_rows(DEEPNORM_ALPHA * x_ref[...] + gate_ref[0] * y, g_ref[...], b_ref[...])


def mamba_out(y2, xbc, proj, d_skip, norm_w, w_out, x, gate, ln_g, ln_b, tb=256):
    T = x.shape[0]
    tok, mod, vec = _row_specs(tb, T // gate.shape[0] // tb)
    inner = pl.BlockSpec((tb, SSM_INNER), lambda i: (i, 0))
    ivec = pl.BlockSpec((1, SSM_INNER), lambda i: (0, 0))
    dsk = jnp.repeat(d_skip, SSM_INNER // d_skip.shape[0]).reshape(1, SSM_INNER)
    return pl.pallas_call(
        _mamba_out_kernel,
        grid=(T // tb,),
        in_specs=[pl.BlockSpec((2, tb, SSM_INNER), lambda i: (0, i, 0)), inner, inner, ivec, ivec,
                  pl.BlockSpec((SSM_INNER, D_MODEL), lambda i: (0, 0)), tok, mod, vec, vec],
        out_specs=tok,
        out_shape=jax.ShapeDtypeStruct((T, D_MODEL), jnp.float32),
        compiler_params=pltpu.CompilerParams(dimension_semantics=("arbitrary",), vmem_limit_bytes=DENSE_VMEM_LIMIT),
        name="mamba_out",
    )(y2, xbc, proj, dsk, norm_w.reshape(1, SSM_INNER), w_out.astype(jnp.bfloat16), x, gate,
      ln_g.reshape(1, D_MODEL), ln_b.reshape(1, D_MODEL))


def depthwise_conv_grid(u, w, bias, rows, cols):
    b, L, C = u.shape
    out = lax.conv_general_dilated(u.reshape(b, rows, cols, C), w.astype(u.dtype)[:, :, None, :],
                                   window_strides=(1, 1), padding='SAME',
                                   dimension_numbers=('NHWC', 'HWIO', 'NHWC'), feature_group_count=C)
    return out.reshape(b, L, C) + bias


def _split3_bf16(a):
    f32, bf16 = jnp.float32, jnp.bfloat16
    p0 = a.astype(bf16)
    r = a - p0.astype(f32)
    p1 = r.astype(bf16)
    p2 = (r - p1.astype(f32)).astype(bf16)
    return p0, p1, p2


def _ssd_kernel(x_ref, b_ref, c_ref, dt_ref, an_ref, init_ref, *rest, want_y):
    if want_y:
        y_ref, fin_ref, state_ref = rest
    else:
        fin_ref, state_ref = rest
    f32, bf16 = jnp.float32, jnp.bfloat16
    Q, P, R = SSM_CHUNK, SSM_HEAD_DIM, SSM_GROUP_HEADS
    fwd = pl.program_id(1) == 0
    c = pl.program_id(3)

    @pl.when(c == 0)
    def _():
        state_ref[...] = init_ref[0, 0]

    dtg = dt_ref[0, 0, 0]
    ag = dtg * an_ref[0, 0]
    li = lax.broadcasted_iota(jnp.int32, (Q, Q), 0)
    si = lax.broadcasted_iota(jnp.int32, (Q, Q), 1)
    tri = jnp.where(fwd, li - si, si - li) >= 0
    tri_bf = jnp.where(tri, 1.0, 0.0).astype(bf16)
    pieces = _split3_bf16(ag)
    a_cum = sum(jnp.dot(tri_bf, p, preferred_element_type=f32) for p in pieces)
    a_cum_t = sum(lax.dot_general(p, tri_bf, (((0,), (1,)), ((), ())), preferred_element_type=f32)
                  for p in pieces)
    total = jnp.sum(ag, axis=0, keepdims=True)
    bm = b_ref[0].astype(bf16)
    cm = c_ref[0].astype(bf16)
    x = x_ref[0]
    if want_y:
        cb = lax.dot_general(cm, bm, (((1,), (1,)), ((), ())), preferred_element_type=f32)
    ys = []
    for r in range(R):
        acol = a_cum[:, r:r + 1]
        xdt = x[:, r * P:(r + 1) * P] * dtg[:, r:r + 1]
        state = state_ref[r]
        if want_y:
            decay = jnp.where(tri, jnp.exp(acol - a_cum_t[r:r + 1, :]), 0.0)
            y = jnp.dot((cb * decay).astype(bf16), xdt.astype(bf16), preferred_element_type=f32)
            y_off = lax.dot_general(cm, state.astype(bf16), (((1,), (1,)), ((), ())), preferred_element_type=f32)
            ys.append(y + y_off * jnp.exp(acol))
        tot = total[:, r:r + 1]
        xw = (xdt * jnp.exp(tot - acol)).astype(bf16)
        upd = lax.dot_general(xw, bm, (((0,), (0,)), ((), ())), preferred_element_type=f32)
        state_ref[r] = state * jnp.exp(tot) + upd
    if want_y:
        y_ref[0, 0] = jnp.concatenate(ys, axis=1)
    fin_ref[0, 0] = state_ref[...]


def ssd_scan_both(xbc, dt, a_neg, init, want_y):
    B, L, _ = xbc.shape
    Q, G, R, P, N = SSM_CHUNK, SSM_GROUPS, SSM_GROUP_HEADS, SSM_HEAD_DIM, SSM_STATE
    nc = L // Q
    dt_g = dt.reshape(B, L, 2, G, R).transpose(0, 2, 3, 1, 4)
    an_g = a_neg.reshape(2, G, 1, R)

    def chunk(d, c):
        return c + d * (nc - 1 - 2 * c)

    x_spec = pl.BlockSpec((1, Q, R * P), lambda b, d, g, c: (b, chunk(d, c), g))
    b_spec = pl.BlockSpec((1, Q, N), lambda b, d, g, c: (b, chunk(d, c), SSM_INNER // N + g))
    c_spec = pl.BlockSpec((1, Q, N), lambda b, d, g, c: (b, chunk(d, c), SSM_INNER // N + G + g))
    dt_spec = pl.BlockSpec((1, 1, 1, Q, R), lambda b, d, g, c: (b, d, g, chunk(d, c), 0))
    an_spec = pl.BlockSpec((1, 1, 1, R), lambda b, d, g, c: (d, g, 0, 0))
    st_spec = pl.BlockSpec((1, 1, R, P, N), lambda b, d, g, c: (b, d, g, 0, 0))
    st_shape = jax.ShapeDtypeStruct((B, 2, SSM_HEADS, P, N), jnp.float32)
    out_specs, out_shape = [st_spec], [st_shape]
    if want_y:
        out_specs = [pl.BlockSpec((1, 1, Q, R * P), lambda b, d, g, c: (d, b, chunk(d, c), g))] + out_specs
        out_shape = [jax.ShapeDtypeStruct((2, B, L, SSM_INNER), jnp.float32)] + out_shape
    outs = pl.pallas_call(
        functools.partial(_ssd_kernel, want_y=want_y),
        grid=(B, 2, G, nc),
        in_specs=[x_spec, b_spec, c_spec, dt_spec, an_spec, st_spec],
        out_specs=out_specs,
        out_shape=out_shape,
        scratch_shapes=[pltpu.VMEM((R, P, N), jnp.float32)],
        compiler_params=pltpu.CompilerParams(dimension_semantics=("arbitrary",) * 4),
        name="ssd_scan_y" if want_y else "ssd_scan_state",
    )(xbc, xbc, xbc, dt_g, an_g, init)
    if want_y:
        return outs[0], outs[1]
    return None, outs[0]


def mamba_layer(xt, ctx_t, mods, cmods, n_batch, rows, w_in, conv_w, conv_b, a_log, dt_bias, d_skip, norm_w, w_out,
                ln_g, ln_b):
    f32 = jnp.float32
    sh1, sc1, g1 = mods
    csh1, csc1 = cmods
    a_neg = -jnp.exp(a_log.astype(f32))
    w_xbc = w_in[:, SSM_INNER:SSM_INNER + SSM_CONV_DIM]
    w_dt = w_in[:, SSM_INNER + SSM_CONV_DIM:]

    def conv_dt(pre, dt_raw, rows_, cols_):
        xbc = jax.nn.silu(depthwise_conv_grid(pre.reshape(n_batch, rows_ * cols_, SSM_CONV_DIM), conv_w, conv_b,
                                              rows_, cols_))
        dt = jax.nn.softplus(dt_raw.reshape(n_batch, rows_ * cols_, 2, SSM_HEADS) + dt_bias.astype(f32))
        return xbc, dt

    lc = ctx_t.shape[0] // n_batch
    xbc_c, dt_c = conv_dt(mod_matmul(ctx_t, csc1, csh1, w_xbc, 512, 1024),
                          mod_matmul(ctx_t, csc1, csh1, w_dt, 512, 2 * SSM_HEADS), 1, lc)
    zeros = jnp.zeros((n_batch, 2, SSM_HEADS, SSM_HEAD_DIM, SSM_STATE), f32)
    _, states = ssd_scan_both(xbc_c, dt_c, a_neg, zeros, False)
    proj = mod_matmul(xt, sc1, sh1, w_in[:, :SSM_INNER + SSM_CONV_DIM], 1024, 1024)
    dt_raw = mod_matmul(xt, sc1, sh1, w_dt, 1024, 2 * SSM_HEADS)
    xbc, dt = conv_dt(proj[:, SSM_INNER:], dt_raw, rows, GRID_W)
    y2, _ = ssd_scan_both(xbc, dt, a_neg, states, True)
    T = xt.shape[0]
    return mamba_out(y2.reshape(2, T, SSM_INNER), xbc.reshape(T, SSM_CONV_DIM), proj, d_skip, norm_w, w_out,
                     xt, g1, ln_g, ln_b)


def hyena_filters(L, w1, b1, w2, b2, w3, freq):
    f32 = jnp.float32
    t = jnp.linspace(0.0, 1.0, L, dtype=f32)[:, None]
    bands = (HYENA_EMB - 1) // 2
    ang = 2.0 * math.pi * jnp.arange(L, dtype=f32)[:, None] / L
    fb = jnp.linspace(1e-4, bands - 1, bands, dtype=f32)[None, :]
    z = jnp.concatenate([t, jnp.cos(fb * ang), -jnp.sin(fb * ang)], axis=-1)
    fr = freq.astype(f32)
    hid = jnp.sin(fr * (z @ w1.astype(f32) + b1.astype(f32)))
    for j in range(HYENA_INNER_MLPS):
        hid = jnp.sin(fr * (hid @ w2[j].astype(f32) + b2[j].astype(f32)))
    h = (hid @ w3.astype(f32)).reshape(L, HYENA_ORDER, 2, D_MODEL)
    max_decay = math.log(HYENA_TARGET) / HYENA_STEEP_DECAY
    min_decay = math.log(HYENA_TARGET) / HYENA_GENTLE_DECAY
    deltas = jnp.linspace(min_decay, max_decay, D_MODEL, dtype=f32)
    h = h * jnp.exp(-t * jnp.abs(deltas))[:, None, None, :]
    h_fwd, h_bwd = h[:, :, 0], h[:, :, 1]
    two_sided = jnp.concatenate([h_fwd[:1] + h_bwd[:1], h_fwd[1:],
                                 jnp.zeros((1, HYENA_ORDER, D_MODEL), f32),
                                 jnp.flip(h_bwd[1:], 0)], axis=0)
    two_sided = two_sided * lax.rsqrt(jnp.sum(jnp.square(two_sided), axis=0, keepdims=True) + HYENA_NORM_EPS)
    return jnp.fft.fft(two_sided, axis=0)


FFT_N1 = 128
FFT_N2 = 64
FFT_N = FFT_N1 * FFT_N2
CONV_L = FFT_N // 2
CONV_T1 = CONV_L // FFT_N2
FFT_VMEM_LIMIT = 56 * 1024 * 1024
FFT_UNROLL = 4


def _dft_tables():
    def cos_sin(phase, n):
        ang = (2.0 * math.pi / n) * (phase % n).astype(jnp.float32)
        return jnp.cos(ang), jnp.sin(ang)
    t2 = jnp.arange(FFT_N2, dtype=jnp.int32)[:, None, None]
    k1 = jnp.arange(FFT_N1, dtype=jnp.int32)[None, :, None]
    t1 = jnp.arange(CONV_T1, dtype=jnp.int32)[None, None, :]
    c, s = cos_sin(FFT_N2 * t1 * k1 + t2 * k1, FFT_N)
    fwd = jnp.concatenate([c, -s], axis=1)
    inv = jnp.concatenate([c, s], axis=1).reshape(FFT_N2, 2, FFT_N1, CONV_T1)
    inv = inv.transpose(0, 1, 3, 2).reshape(FFT_N2, 2 * CONV_T1, FFT_N1) / FFT_N
    k2 = jnp.arange(FFT_N2, dtype=jnp.int32)
    c2, s2 = cos_sin(k2[:, None] * k2[None, :], FFT_N2)
    mid = jnp.concatenate([c2, -s2], axis=0)
    return fwd, mid, inv


def _split_bf16(w):
    hi = w.astype(jnp.bfloat16)
    return hi, (w - hi.astype(jnp.float32)).astype(jnp.bfloat16)


def _mm3(w_hi, w_lo, x):
    x_hi, x_lo = _split_bf16(x)
    f32 = jnp.float32
    return (jnp.dot(w_hi, x_hi, preferred_element_type=f32) + jnp.dot(w_lo, x_hi, preferred_element_type=f32)
            + jnp.dot(w_hi, x_lo, preferred_element_type=f32))


def _long_conv_kernel(x_ref, g_ref, hr_ref, hi_ref, b_ref, wfh_ref, wfl_ref, wmh_ref, wml_ref, wih_ref, wil_ref,
                      o_ref, sr_ref, si_ref):
    def rows_t2(t2):
        return pl.ds(t2, CONV_T1, stride=FFT_N2)

    def stage_a(t2, carry):
        wh, wl = wfh_ref[t2], wfl_ref[t2]
        p = _mm3(wh, wl, x_ref.at[0][rows_t2(t2), :])
        q = _mm3(wh, wl, x_ref.at[1][rows_t2(t2), :])
        dst = pl.ds(pl.multiple_of(t2 * FFT_N1, FFT_N1), FFT_N1)
        sr_ref[dst, :] = p[:FFT_N1] - q[FFT_N1:]
        si_ref[dst, :] = p[FFT_N1:] + q[:FFT_N1]
        return carry

    lax.fori_loop(0, FFT_N2, stage_a, 0, unroll=FFT_UNROLL)
    wmh, wml = wmh_ref[...], wml_ref[...]

    def per_k1(k1, carry):
        sel = pl.ds(k1, FFT_N2, stride=FFT_N1)
        p = _mm3(wmh, wml, sr_ref[sel, :])
        q = _mm3(wmh, wml, si_ref[sel, :])
        xr = p[:FFT_N2] - q[FFT_N2:]
        xi = p[FFT_N2:] + q[:FFT_N2]
        hsel = pl.ds(pl.multiple_of(k1 * FFT_N2, FFT_N2), FFT_N2)
        hr = hr_ref[hsel, :]
        hi = hi_ref[hsel, :]
        p = _mm3(wmh, wml, xr * hr - xi * hi)
        q = _mm3(wmh, wml, xr * hi + xi * hr)
        sr_ref[sel, :] = p[:FFT_N2] + q[FFT_N2:]
        si_ref[sel, :] = q[:FFT_N2] - p[FFT_N2:]
        return carry

    lax.fori_loop(0, FFT_N1, per_k1, 0, unroll=FFT_UNROLL)
    bias = b_ref[...]

    def stage_c(t2, carry):
        wh, wl = wih_ref[t2], wil_ref[t2]
        src = pl.ds(pl.multiple_of(t2 * FFT_N1, FFT_N1), FFT_N1)
        p = _mm3(wh, wl, sr_ref[src, :])
        q = _mm3(wh, wl, si_ref[src, :])
        rows = rows_t2(t2)
        o_ref.at[0][rows, :] = g_ref.at[0][rows, :] * (p[:CONV_T1] - q[CONV_T1:] + x_ref.at[0][rows, :] * bias)
        o_ref.at[1][rows, :] = g_ref.at[1][rows, :] * (q[:CONV_T1] + p[CONV_T1:] + x_ref.at[1][rows, :] * bias)
        return carry

    lax.fori_loop(0, FFT_N2, stage_c, 0, unroll=FFT_UNROLL)


def _spectrum_layout(hf_full):
    c = hf_full.shape[1]
    return hf_full.reshape(FFT_N2, FFT_N1, c).transpose(1, 0, 2).reshape(FFT_N, c)


def long_conv_gated(u, u_col, gate, gate_col, hf, bias, tables):
    B, L, _ = u.shape
    C = hf.shape[1]
    assert L == CONV_L and B % 2 == 0 and C % VREG_LANES == 0
    assert u_col % VREG_LANES == 0 and gate_col % VREG_LANES == 0
    h_re, h_im = _spectrum_layout(jnp.real(hf)), _spectrum_layout(jnp.imag(hf))
    consts = [part for t in tables for part in _split_bf16(t)]
    sig_spec = pl.BlockSpec((2, L, VREG_LANES), lambda c, b: (b, 0, c))
    u_spec = pl.BlockSpec((2, L, VREG_LANES), lambda c, b: (b, 0, c + u_col // VREG_LANES))
    gate_spec = pl.BlockSpec((2, L, VREG_LANES), lambda c, b: (b, 0, c + gate_col // VREG_LANES))
    once = pl.Buffered(1)
    spec_h = pl.BlockSpec((FFT_N, VREG_LANES), lambda c, b: (0, c), pipeline_mode=once)
    const_specs = [pl.BlockSpec(t.shape, functools.partial(lambda nd, c, b: (0,) * nd, t.ndim), pipeline_mode=once)
                   for t in consts]
    return pl.pallas_call(
        _long_conv_kernel,
        grid=(C // VREG_LANES, B // 2),
        in_specs=[u_spec, gate_spec, spec_h, spec_h, pl.BlockSpec((1, VREG_LANES), lambda c, b: (0, c))] + const_specs,
        out_specs=sig_spec,
        out_shape=jax.ShapeDtypeStruct((B, L, C), jnp.float32),
        scratch_shapes=[pltpu.VMEM((FFT_N, VREG_LANES), jnp.float32), pltpu.VMEM((FFT_N, VREG_LANES), jnp.float32)],
        compiler_params=pltpu.CompilerParams(dimension_semantics=("arbitrary", "arbitrary"),
                                             vmem_limit_bytes=FFT_VMEM_LIMIT),
        name="long_conv",
    )(u, gate, h_re, h_im, bias.reshape(1, C), *consts)


def short_conv_centred(u, w, bias):
    up = jnp.pad(u, ((0, 0), (1, 1), (0, 0)))
    return up[:, :-2] * w[0] + up[:, 1:-1] * w[1] + up[:, 2:] * w[2] + bias


def hyena_layer(xt, mods, n_batch, w_in, conv_w, conv_b, w1, b1, w2, b2, w3, freq, fbias, w_out, ln_g, ln_b):
    sh1, sc1, g1 = mods
    T = xt.shape[0]
    L = T // n_batch
    hf = hyena_filters(L, w1, b1, w2, b2, w3, freq)
    u = short_conv_centred(mod_matmul(xt, sc1, sh1, w_in, 1024, 1024).reshape(n_batch, L, 3 * D_MODEL), conv_w, conv_b)
    tables = _dft_tables()
    zz = long_conv_gated(u, 2 * D_MODEL, u, 0, hf[:, 0], fbias[0], tables)
    zz = long_conv_gated(zz, 0, u, D_MODEL, hf[:, 1], fbias[1], tables)
    return proj_ln(zz.reshape(T, D_MODEL), w_out, xt, g1, ln_g, ln_b)


PEER_SLOTS = PEER_HEADS * PEER_TOPK
PEER_TB = 64
PEER_VMEM_LIMIT = 52 * 1024 * 1024
HI_MASK = -65536


def _pack_rows_bf16(tab):
    n, d = tab.shape
    bits = lax.bitcast_convert_type(tab.astype(jnp.bfloat16), jnp.uint16).astype(jnp.uint32)
    packed = (bits[1::2] << 16) | bits[0::2]
    return lax.bitcast_convert_type(packed, jnp.int32).reshape(n // 2 * (d // VREG_LANES), VREG_LANES)


def _unpack_row(tab_ref, off, shift):
    word = tab_ref[pl.ds(pl.multiple_of(off, VREG_SUBLANES), VREG_SUBLANES), :]
    return pltpu.bitcast((word << shift) & HI_MASK, jnp.float32)


def _sublane_sums(tiles, sub):
    for h in (1, 2, 4):
        keep = (sub & h) == 0
        nxt = []
        for a, b in zip(tiles[0::2], tiles[1::2]):
            nxt.append(jnp.where(keep, a, b) + pltpu.roll(jnp.where(keep, b, a), h, 0))
        tiles = nxt
    return tiles[0]


def _splat_tile(row):
    return jnp.broadcast_to(row, (PEER_SLOTS, VREG_LANES)).T


def _splat_small_ints_tile(row, eye, ones):
    diag = jnp.where(eye, row.astype(jnp.float32), 0.0).astype(jnp.bfloat16)
    return jnp.dot(diag, ones, preferred_element_type=jnp.float32).astype(jnp.int32)


def _splat_consts():
    eye = (lax.broadcasted_iota(jnp.int32, (PEER_SLOTS, PEER_SLOTS), 0)
           == lax.broadcasted_iota(jnp.int32, (PEER_SLOTS, PEER_SLOTS), 1))
    return eye, jnp.ones((PEER_SLOTS, VREG_LANES), jnp.bfloat16)


def _splat_block(shift_ref, shift_s, w_ref=None, w_s=None):
    eye, ones = _splat_consts()

    def token(t, carry):
        dst = pl.ds(pl.multiple_of(t * PEER_SLOTS, PEER_SLOTS), PEER_SLOTS)
        shift_s[dst, :] = _splat_small_ints_tile(shift_ref[pl.ds(t, 1), :], eye, ones)
        if w_ref is not None:
            w_s[dst, :] = _splat_tile(w_ref[pl.ds(t, 1), :])
        return carry
    lax.fori_loop(0, PEER_TB, token, 0, unroll=8)


def _peer_act_kernel(row_ref, shift_ref, x_ref, gate_ref, tab_ref, w_ref, part_ref, shift_s):
    sub = lax.broadcasted_iota(jnp.int32, (VREG_SUBLANES, VREG_LANES), 0)
    _splat_block(shift_ref, shift_s)

    def token(t, carry):
        xt = x_ref[t]
        base = pl.multiple_of(t * PEER_SLOTS, PEER_SLOTS)
        tok_rows = row_ref.at[0, 0, pl.ds(base, PEER_SLOTS)]
        for g in range(PEER_SLOTS // VREG_SUBLANES):
            prods = []
            for k in range(VREG_SUBLANES):
                s = g * VREG_SUBLANES + k
                prods.append(_unpack_row(tab_ref, tok_rows[s], shift_s[pl.ds(base + s, 1), :]) * xt)
            part_ref[pl.ds(base + g * VREG_SUBLANES, VREG_SUBLANES), :] = _sublane_sums(prods, sub)
        return carry

    lax.fori_loop(0, PEER_TB, token, 0)
    part = part_ref[...]
    part_hi = part.astype(jnp.bfloat16)
    part_lo = (part - part_hi.astype(jnp.float32)).astype(jnp.bfloat16)
    contract_lanes = (((1,), (1,)), ((), ()))
    ones = jnp.ones((VREG_SUBLANES, VREG_LANES), jnp.bfloat16)
    act = (lax.dot_general(ones, part_hi, contract_lanes, preferred_element_type=jnp.float32)
           + lax.dot_general(ones, part_lo, contract_lanes, preferred_element_type=jnp.float32))[0:1]
    gelu = 0.5 * act * (1.0 + lax.erf(act * (2.0 ** -0.5)))
    w_ref[0] = gate_ref[0] * gelu


def _peer_out_kernel(row_ref, shift_ref, w_ref, tab_ref, o_ref, shift_s, w_s):
    n_acc = 4
    _splat_block(shift_ref, shift_s, w_ref, w_s)

    def token(t, carry):
        base = pl.multiple_of(t * PEER_SLOTS, PEER_SLOTS)
        tok_rows = row_ref.at[0, 0, pl.ds(base, PEER_SLOTS)]
        accs = [jnp.zeros((VREG_SUBLANES, VREG_LANES), jnp.float32) for _ in range(n_acc)]
        for s in range(PEER_SLOTS):
            val = _unpack_row(tab_ref, tok_rows[s], shift_s[pl.ds(base + s, 1), :])
            accs[s % n_acc] = accs[s % n_acc] + w_s[pl.ds(base + s, 1), :] * val
        o_ref[t] = (accs[0] + accs[1]) + (accs[2] + accs[3])
        return carry

    lax.fori_loop(0, PEER_TB, token, 0)


def peer_experts(t, offs, shifts, gate, u_tab, v_tab):
    T = t.shape[0]
    nblk = T // PEER_TB
    per_blk = PEER_TB * PEER_SLOTS
    d_sub = D_MODEL // VREG_LANES
    offs = offs.reshape(nblk, 1, per_blk)
    smem_spec = pl.BlockSpec((1, 1, per_blk), lambda i: (i, 0, 0), memory_space=pltpu.SMEM)
    lane_spec = pl.BlockSpec((1, 1, per_blk), lambda i: (i, 0, 0))
    slot_spec = pl.BlockSpec((PEER_TB, PEER_SLOTS), lambda i: (i, 0))
    tok_spec = pl.BlockSpec((PEER_TB, d_sub, VREG_LANES), lambda i: (i, 0, 0))
    tab_spec = pl.BlockSpec(memory_space=pltpu.VMEM)
    params = pltpu.CompilerParams(dimension_semantics=("arbitrary",), vmem_limit_bytes=PEER_VMEM_LIMIT)
    splat_i32 = pltpu.VMEM((per_blk, VREG_LANES), jnp.int32)
    splat_f32 = pltpu.VMEM((per_blk, VREG_LANES), jnp.float32)
    w = pl.pallas_call(
        _peer_act_kernel,
        grid=(nblk,),
        in_specs=[smem_spec, slot_spec, tok_spec, lane_spec, tab_spec],
        out_specs=lane_spec,
        out_shape=jax.ShapeDtypeStruct((nblk, 1, per_blk), jnp.float32),
        scratch_shapes=[splat_f32, splat_i32],
        compiler_params=params,
        name="peer_act",
    )(offs, shifts, t.reshape(T, d_sub, VREG_LANES), gate.reshape(nblk, 1, per_blk), _pack_rows_bf16(u_tab))
    out = pl.pallas_call(
        _peer_out_kernel,
        grid=(nblk,),
        in_specs=[smem_spec, slot_spec, slot_spec, tab_spec],
        out_specs=tok_spec,
        out_shape=jax.ShapeDtypeStruct((T, d_sub, VREG_LANES), jnp.float32),
        scratch_shapes=[splat_i32, splat_f32],
        compiler_params=params,
        name="peer_out",
    )(offs, shifts, w.reshape(T, PEER_SLOTS), _pack_rows_bf16(v_tab))
    return out.reshape(T, D_MODEL)


def _top_rows(s, k):
    n = s.shape[0]
    row = lax.broadcasted_iota(jnp.int32, s.shape, 0)
    vals, idxs = [], []
    for _ in range(k):
        m = jnp.max(s, axis=0, keepdims=True)
        i = jnp.min(jnp.where(s == m, row, n), axis=0, keepdims=True)
        vals.append(m)
        idxs.append(i)
        s = jnp.where(row == i, -jnp.inf, s)
    return jnp.concatenate(vals, axis=0), jnp.concatenate(idxs, axis=0)


def _pick_rows(table, which):
    out = jnp.zeros(which.shape, table.dtype)
    for a in range(table.shape[0]):
        out = jnp.where(which == a, table[a:a + 1, :], out)
    return out


PAIR_COUNTS = tuple(PEER_TOPK // (a + 1) for a in range(PEER_TOPK))
PAIR_STARTS = tuple(sum(PAIR_COUNTS[:a]) for a in range(PEER_TOPK))
N_PAIRS = sum(PAIR_COUNTS)
N_PAIRS_PADDED = -(-N_PAIRS // VREG_SUBLANES) * VREG_SUBLANES
SEL_TB = 512


def _peer_select_kernel(x_ref, sc_ref, sh_ref, wq_ref, k1_ref, k2_ref, off_ref, shift_ref, gate_ref):
    half = PEER_DK // 2
    h = (x_ref[...] * (1.0 + sc_ref[0]) + sh_ref[0]).astype(jnp.bfloat16)
    q = jnp.dot(h, wq_ref[...], preferred_element_type=jnp.float32).astype(jnp.bfloat16)
    contract_last = (((1,), (1,)), ((), ()))
    s1 = lax.dot_general(k1_ref[0], q[:, :half], contract_last, preferred_element_type=jnp.float32)
    s2 = lax.dot_general(k2_ref[0], q[:, half:], contract_last, preferred_element_type=jnp.float32)
    v1, i1 = _top_rows(s1, PEER_TOPK)
    v2, i2 = _top_rows(s2, PEER_TOPK)
    pairs = [v1[a:a + 1, :] + v2[:PAIR_COUNTS[a], :] for a in range(PEER_TOPK)]
    pairs.append(jnp.full((N_PAIRS_PADDED - N_PAIRS, v1.shape[1]), -jnp.inf, jnp.float32))
    top_s, pos = _top_rows(jnp.concatenate(pairs, axis=0), PEER_TOPK)
    a_sel = jnp.zeros(pos.shape, jnp.int32)
    start = jnp.zeros(pos.shape, jnp.int32)
    for a in range(1, PEER_TOPK):
        later = pos >= PAIR_STARTS[a]
        a_sel = jnp.where(later, a, a_sel)
        start = jnp.where(later, PAIR_STARTS[a], start)
    expert = _pick_rows(i1, a_sel) * PEER_NKEYS + _pick_rows(i2, pos - start)
    e = jnp.exp(top_s - top_s[0:1, :])
    gate_ref[0] = e / jnp.sum(e, axis=0, keepdims=True)
    off_ref[0] = (expert >> 1) * VREG_SUBLANES
    shift_ref[0] = jnp.where((expert & 1) == 1, 0, 16)


def peer_select(x, sc, sh, wq, k1, k2):
    T = x.shape[0]
    per_seq = T // sc.shape[0] // SEL_TB
    bf16 = jnp.bfloat16
    tok_spec = pl.BlockSpec((SEL_TB, D_MODEL), lambda i, h: (i, 0))
    mod_spec = pl.BlockSpec((1, 1, D_MODEL), lambda i, h: (i // per_seq, 0, 0))
    wq_spec = pl.BlockSpec((D_MODEL, PEER_DK), lambda i, h: (0, h))
    key_spec = pl.BlockSpec((1, PEER_NKEYS, PEER_DK // 2), lambda i, h: (h, 0, 0))
    out_spec = pl.BlockSpec((1, PEER_TOPK, SEL_TB), lambda i, h: (h, 0, i))
    shape = (PEER_HEADS, PEER_TOPK, T)
    off, shift, gate = pl.pallas_call(
        _peer_select_kernel,
        grid=(T // SEL_TB, PEER_HEADS),
        in_specs=[tok_spec, mod_spec, mod_spec, wq_spec, key_spec, key_spec],
        out_specs=[out_spec, out_spec, out_spec],
        out_shape=[jax.ShapeDtypeStruct(shape, jnp.int32), jax.ShapeDtypeStruct(shape, jnp.int32),
                   jax.ShapeDtypeStruct(shape, jnp.float32)],
        compiler_params=pltpu.CompilerParams(dimension_semantics=("arbitrary", "arbitrary")),
        name="peer_select",
    )(x, sc, sh, wq.astype(bf16), k1.astype(bf16), k2.astype(bf16))

    def to_tokens(a):
        return a.reshape(PEER_SLOTS, T).T

    return to_tokens(off), to_tokens(shift), to_tokens(gate)


def peer_layer(xt, mods, wq, k1, k2, u_tab, v_tab, ln_g, ln_b):
    sh2, sc2, g2 = mods
    n_batch = sc2.shape[0]
    off, shift, gate = peer_select(xt, sc2, sh2, wq, k1, k2)
    h = (xt.reshape(n_batch, -1, D_MODEL) * (1 + sc2) + sh2).reshape(-1, D_MODEL)
    y = peer_experts(h, off, shift, gate, u_tab, v_tab)
    return residual_ln(y, xt, g2, ln_g, ln_b)


def kernel(x, c, ctx, c_ctx, ada_w, ada_b, ln_g, ln_b,
           m_w_in, m_conv_w, m_conv_b, m_a_log, m_dt_bias, m_d, m_norm_w, m_w_out,
           h_w_in, h_conv_w, h_conv_b, h_f_w1, h_f_b1, h_f_w2, h_f_b2, h_f_w3, h_freq, h_bias, h_w_out,
           p_wq, p_k1, p_k2, p_u, p_v):
    n_batch, seq, _ = x.shape
    rows = seq // GRID_W
    s_lat = jax.nn.silu(c)
    s_ctx = jax.nn.silu(c_ctx)
    xt = x.reshape(n_batch * seq, D_MODEL)
    ctx_t = ctx.reshape(-1, D_MODEL)
    for i in range(DEPTH):
        kind, k = i % N_MIXERS, i // N_MIXERS
        sh1, sc1, g1, sh2, sc2, g2 = jnp.split((s_lat @ ada_w[i] + ada_b[i])[:, None, :], 6, axis=-1)
        if kind == 0:
            csh1, csc1 = jnp.split((s_ctx @ ada_w[i] + ada_b[i])[None, None, :], 6, axis=-1)[:2]
            xt = mamba_layer(xt, ctx_t, (sh1, sc1, g1), (csh1, csc1), n_batch, rows, m_w_in[k], m_conv_w[k],
                             m_conv_b[k], m_a_log[k], m_dt_bias[k], m_d[k], m_norm_w[k], m_w_out[k],
                             ln_g[i, 0], ln_b[i, 0])
        else:
            xt = hyena_layer(xt, (sh1, sc1, g1), n_batch, h_w_in[k], h_conv_w[k], h_conv_b[k], h_f_w1[k], h_f_b1[k],
                             h_f_w2[k], h_f_b2[k], h_f_w3[k], h_freq[k], h_bias[k], h_w_out[k], ln_g[i, 0], ln_b[i, 0])
        xt = peer_layer(xt, (sh2, sc2, g2), p_wq[i], p_k1[i], p_k2[i], p_u[i], p_v[i], ln_g[i, 1], ln_b[i, 1])
    return xt.reshape(x.shape)
```

```python
import functools
import math

import jax
import jax.numpy as jnp
from jax import lax
from jax.experimental import pallas as pl
from jax.experimental.pallas import tpu as pltpu

D_MODEL = 1024
DEPTH = 2
GRID_W = 64
N_MIXERS = 2
DEEPNORM_ALPHA = (2 * DEPTH) ** 0.25
LN_EPS = 1e-5
RMS_EPS = 1e-5

SSM_INNER = 2 * D_MODEL
SSM_HEAD_DIM = 64
SSM_HEADS = SSM_INNER // SSM_HEAD_DIM
SSM_GROUPS = 4
SSM_STATE = 128
SSM_CHUNK = 128
SSM_CONV = 3
SSM_CONV_DIM = SSM_INNER + 2 * SSM_GROUPS * SSM_STATE
SSM_GROUP_HEADS = SSM_HEADS // SSM_GROUPS

HYENA_ORDER = 2
HYENA_EMB = 33
HYENA_INNER_MLPS = 2
HYENA_STEEP_DECAY = 0.3
HYENA_GENTLE_DECAY = 1.5
HYENA_TARGET = 1e-2
HYENA_NORM_EPS = 1e-6

PEER_HEADS = 8
PEER_NKEYS = 128
PEER_DK = 256
PEER_TOPK = 16

VREG_SUBLANES = 8
VREG_LANES = 128
DENSE_VMEM_LIMIT = 48 * 1024 * 1024


def _mod_matmul_kernel(x_ref, sc_ref, sh_ref, w_ref, o_ref):
    h = (x_ref[...] * (1.0 + sc_ref[0]) + sh_ref[0]).astype(jnp.bfloat16)
    o_ref[...] = jnp.dot(h, w_ref[...], preferred_element_type=jnp.float32)


def mod_matmul(x, sc, sh, w, tb, tn):
    T, D = x.shape
    N = w.shape[1]
    assert T % tb == 0 and N % tn == 0 and (T // sc.shape[0]) % tb == 0
    per_mod = T // sc.shape[0] // tb
    mod_spec = pl.BlockSpec((1, 1, D), lambda i, j: (i // per_mod, 0, 0))
    return pl.pallas_call(
        _mod_matmul_kernel,
        grid=(T // tb, N // tn),
        in_specs=[pl.BlockSpec((tb, D), lambda i, j: (i, 0)), mod_spec, mod_spec,
                  pl.BlockSpec((D, tn), lambda i, j: (0, j))],
        out_specs=pl.BlockSpec((tb, tn), lambda i, j: (i, j)),
        out_shape=jax.ShapeDtypeStruct((T, N), jnp.float32),
        compiler_params=pltpu.CompilerParams(dimension_semantics=("arbitrary", "arbitrary"),
                                             vmem_limit_bytes=DENSE_VMEM_LIMIT),
        name="mod_matmul",
    )(x, sc, sh, w.astype(jnp.bfloat16))


def _layer_norm_rows(v, g, b):
    mu = jnp.mean(v, axis=-1, keepdims=True)
    vc = v - mu
    var = jnp.mean(vc * vc, axis=-1, keepdims=True)
    return vc * lax.rsqrt(var + LN_EPS) * g + b


def _proj_ln_kernel(a_ref, w_ref, x_ref, gate_ref, g_ref, b_ref, o_ref):
    y = jnp.dot(a_ref[...].astype(jnp.bfloat16), w_ref[...], preferred_element_type=jnp.float32)
    o_ref[...] = _layer_norm_rows(DEEPNORM_ALPHA * x_ref[...] + gate_ref[0] * y, g_ref[...], b_ref[...])


def _row_specs(tb, per_mod):
    tok = pl.BlockSpec((tb, D_MODEL), lambda i: (i, 0))
    mod = pl.BlockSpec((1, 1, D_MODEL), lambda i: (i // per_mod, 0, 0))
    vec = pl.BlockSpec((1, D_MODEL), lambda i: (0, 0))
    return tok, mod, vec


def proj_ln(a, w, x, gate, ln_g, ln_b, tb=512):
    T, K = a.shape
    per_mod = T // gate.shape[0] // tb
    tok, mod, vec = _row_specs(tb, per_mod)
    return pl.pallas_call(
        _proj_ln_kernel,
        grid=(T // tb,),
        in_specs=[pl.BlockSpec((tb, K), lambda i: (i, 0)), pl.BlockSpec((K, D_MODEL), lambda i: (0, 0)),
                  tok, mod, vec, vec],
        out_specs=tok,
        out_shape=jax.ShapeDtypeStruct((T, D_MODEL), jnp.float32),
        compiler_params=pltpu.CompilerParams(dimension_semantics=("arbitrary",), vmem_limit_bytes=DENSE_VMEM_LIMIT),
        name="proj_ln",
    )(a, w.astype(jnp.bfloat16), x, gate, ln_g.reshape(1, D_MODEL), ln_b.reshape(1, D_MODEL))


def _residual_ln_kernel(y_ref, x_ref, gate_ref, g_ref, b_ref, o_ref):
    o_ref[...] = _layer_norm_rows(DEEPNORM_ALPHA * x_ref[...] + gate_ref[0] * y_ref[...], g_ref[...], b_ref[...])


def residual_ln(y, x, gate, ln_g, ln_b, tb=512):
    T = x.shape[0]
    tok, mod, vec = _row_specs(tb, T // gate.shape[0] // tb)
    return pl.pallas_call(
        _residual_ln_kernel,
        grid=(T // tb,),
        in_specs=[tok, tok, mod, vec, vec],
        out_specs=tok,
        out_shape=jax.ShapeDtypeStruct((T, D_MODEL), jnp.float32),
        compiler_params=pltpu.CompilerParams(dimension_semantics=("arbitrary",)),
        name="residual_ln",
    )(y, x, gate, ln_g.reshape(1, D_MODEL), ln_b.reshape(1, D_MODEL))


def _mamba_out_kernel(y_ref, xs_ref, z_ref, dsk_ref, nw_ref, w_ref, x_ref, gate_ref, g_ref, b_ref, o_ref):
    z = z_ref[...]
    u = (y_ref[0] + y_ref[1] + xs_ref[...] * dsk_ref[...]) * (z * jax.nn.sigmoid(z))
    gw = SSM_INNER // SSM_GROUPS
    parts = []
    for g in range(SSM_GROUPS):
        ug = u[:, g * gw:(g + 1) * gw]
        parts.append(ug * lax.rsqrt(jnp.mean(ug * ug, axis=-1, keepdims=True) + RMS_EPS))
    a = (jnp.concatenate(parts, axis=1) * nw_ref[...]).astype(jnp.bfloat16)
    y = jnp.dot(a, w_ref[...], preferred_element_type=jnp.float32)
    o_ref[...] = _layer_norm_rows(DEEPNORM_ALPHA * x_ref[...] + gate_ref[0] * y, g_ref[...], b_ref[...])


def mamba_out(y2, xbc, z, d_skip, norm_w, w_out, x, gate, ln_g, ln_b, tb=256):
    T = x.shape[0]
    tok, mod, vec = _row_specs(tb, T // gate.shape[0] // tb)
    inner = pl.BlockSpec((tb, SSM_INNER), lambda i: (i, 0))
    ivec = pl.BlockSpec((1, SSM_INNER), lambda i: (0, 0))
    dsk = jnp.repeat(d_skip, SSM_INNER // d_skip.shape[0]).reshape(1, SSM_INNER)
    return pl.pallas_call(
        _mamba_out_kernel,
        grid=(T // tb,),
        in_specs=[pl.BlockSpec((2, tb, SSM_INNER), lambda i: (0, i, 0)), inner, inner, ivec, ivec,
                  pl.BlockSpec((SSM_INNER, D_MODEL), lambda i: (0, 0)), tok, mod, vec, vec],
        out_specs=tok,
        out_shape=jax.ShapeDtypeStruct((T, D_MODEL), jnp.float32),
        compiler_params=pltpu.CompilerParams(dimension_semantics=("arbitrary",), vmem_limit_bytes=DENSE_VMEM_LIMIT),
        name="mamba_out",
    )(y2, xbc, z, dsk, norm_w.reshape(1, SSM_INNER), w_out.astype(jnp.bfloat16), x, gate,
      ln_g.reshape(1, D_MODEL), ln_b.reshape(1, D_MODEL))


CONV_VMEM_LIMIT = 56 * 1024 * 1024


def _shift_rows(a, k):
    n = a.shape[0]
    return a if k == 0 else pltpu.roll(a, (-k) % n, 0)


def _mod_matmul_conv_kernel(x_ref, sc_ref, sh_ref, w_ref, cw_ref, cb_ref, o_ref, *, rows, cols, silu):
    h = (x_ref[...] * (1.0 + sc_ref[0]) + sh_ref[0]).astype(jnp.bfloat16)
    p = jnp.dot(h, w_ref[...], preferred_element_type=jnp.float32)
    L = p.shape[0]
    t = lax.broadcasted_iota(jnp.int32, p.shape, 0)
    col = t & (cols - 1)
    n_dr = cw_ref.shape[0]
    out = jnp.zeros_like(p)
    for j in range(3):
        dc = j - 1
        acc = None
        for i in range(n_dr):
            dr = i - n_dr // 2
            term = _shift_rows(p, dr * cols)
            if dr < 0:
                term = jnp.where(t >= -dr * cols, term, 0.0)
            elif dr > 0:
                term = jnp.where(t < L - dr * cols, term, 0.0)
            term = term * cw_ref[i, j:j + 1, :]
            acc = term if acc is None else acc + term
        acc = _shift_rows(acc, dc)
        if dc < 0:
            acc = jnp.where(col >= -dc, acc, 0.0)
        elif dc > 0:
            acc = jnp.where(col < cols - dc, acc, 0.0)
        out = out + acc
    out = out + cb_ref[...]
    o_ref[...] = out * jax.nn.sigmoid(out) if silu else out


def mod_matmul_conv(x, sc, sh, w, conv_w, conv_b, n_seq, rows, cols, silu, tn=256):
    T, D = x.shape
    N = w.shape[1]
    L = rows * cols
    assert T == n_seq * L and N % tn == 0 and cols & (cols - 1) == 0 and conv_w.shape[0] in (1, 3)
    per_mod = n_seq // sc.shape[0]
    mod_spec = pl.BlockSpec((1, 1, D), lambda b, j: (b // per_mod, 0, 0))
    return pl.pallas_call(
        functools.partial(_mod_matmul_conv_kernel, rows=rows, cols=cols, silu=silu),
        grid=(n_seq, N // tn),
        in_specs=[pl.BlockSpec((L, D), lambda b, j: (b, 0), pipeline_mode=pl.Buffered(1)), mod_spec, mod_spec,
                  pl.BlockSpec((D, tn), lambda b, j: (0, j)),
                  pl.BlockSpec((conv_w.shape[0], 3, tn), lambda b, j: (0, 0, j)),
                  pl.BlockSpec((1, tn), lambda b, j: (0, j))],
        out_specs=pl.BlockSpec((L, tn), lambda b, j: (b, j)),
        out_shape=jax.ShapeDtypeStruct((T, N), jnp.float32),
        compiler_params=pltpu.CompilerParams(dimension_semantics=("arbitrary", "arbitrary"),
                                             vmem_limit_bytes=CONV_VMEM_LIMIT),
        name="mod_matmul_conv",
    )(x, sc, sh, w.astype(jnp.bfloat16), conv_w, conv_b.reshape(1, N))


def _split3_bf16(a):
    f32, bf16 = jnp.float32, jnp.bfloat16
    p0 = a.astype(bf16)
    r = a - p0.astype(f32)
    p1 = r.astype(bf16)
    p2 = (r - p1.astype(f32)).astype(bf16)
    return p0, p1, p2


def _ssd_kernel(x_ref, b_ref, c_ref, dt_ref, an_ref, init_ref, *rest, want_y):
    if want_y:
        y_ref, fin_ref, state_ref = rest
    else:
        fin_ref, state_ref = rest
    f32, bf16 = jnp.float32, jnp.bfloat16
    Q, P, R = SSM_CHUNK, SSM_HEAD_DIM, SSM_GROUP_HEADS
    fwd = pl.program_id(1) == 0
    c = pl.program_id(3)

    @pl.when(c == 0)
    def _():
        state_ref[...] = init_ref[0, 0]

    dtg = dt_ref[0, 0, 0]
    ag = dtg * an_ref[0, 0]
    li = lax.broadcasted_iota(jnp.int32, (Q, Q), 0)
    si = lax.broadcasted_iota(jnp.int32, (Q, Q), 1)
    tri = jnp.where(fwd, li - si, si - li) >= 0
    tri_bf = jnp.where(tri, 1.0, 0.0).astype(bf16)
    pieces = _split3_bf16(ag)
    a_cum = sum(jnp.dot(tri_bf, p, preferred_element_type=f32) for p in pieces)
    a_cum_t = sum(lax.dot_general(p, tri_bf, (((0,), (1,)), ((), ())), preferred_element_type=f32)
                  for p in pieces)
    total = jnp.sum(ag, axis=0, keepdims=True)
    bm = b_ref[0].astype(bf16)
    cm = c_ref[0].astype(bf16)
    x = x_ref[0]
    if want_y:
        cb = lax.dot_general(cm, bm, (((1,), (1,)), ((), ())), preferred_element_type=f32)
    ys = []
    for r in range(R):
        acol = a_cum[:, r:r + 1]
        xdt = x[:, r * P:(r + 1) * P] * dtg[:, r:r + 1]
        state = state_ref[r]
        if want_y:
            decay = jnp.where(tri, jnp.exp(acol - a_cum_t[r:r + 1, :]), 0.0)
            y = jnp.dot((cb * decay).astype(bf16), xdt.astype(bf16), preferred_element_type=f32)
            y_off = lax.dot_general(cm, state.astype(bf16), (((1,), (1,)), ((), ())), preferred_element_type=f32)
            ys.append(y + y_off * jnp.exp(acol))
        tot = total[:, r:r + 1]
        xw = (xdt * jnp.exp(tot - acol)).astype(bf16)
        upd = lax.dot_general(xw, bm, (((0,), (0,)), ((), ())), preferred_element_type=f32)
        state_ref[r] = state * jnp.exp(tot) + upd
    if want_y:
        y_ref[0, 0] = jnp.concatenate(ys, axis=1)
    fin_ref[0, 0] = state_ref[...]


def ssd_scan_both(xbc, dt, a_neg, init, want_y):
    B, L, _ = xbc.shape
    Q, G, R, P, N = SSM_CHUNK, SSM_GROUPS, SSM_GROUP_HEADS, SSM_HEAD_DIM, SSM_STATE
    nc = L // Q
    dt_g = dt.reshape(B, L, 2, G, R).transpose(0, 2, 3, 1, 4)
    an_g = a_neg.reshape(2, G, 1, R)

    def chunk(d, c):
        return c + d * (nc - 1 - 2 * c)

    x_spec = pl.BlockSpec((1, Q, R * P), lambda b, d, g, c: (b, chunk(d, c), g))
    b_spec = pl.BlockSpec((1, Q, N), lambda b, d, g, c: (b, chunk(d, c), SSM_INNER // N + g))
    c_spec = pl.BlockSpec((1, Q, N), lambda b, d, g, c: (b, chunk(d, c), SSM_INNER // N + G + g))
    dt_spec = pl.BlockSpec((1, 1, 1, Q, R), lambda b, d, g, c: (b, d, g, chunk(d, c), 0))
    an_spec = pl.BlockSpec((1, 1, 1, R), lambda b, d, g, c: (d, g, 0, 0))
    st_spec = pl.BlockSpec((1, 1, R, P, N), lambda b, d, g, c: (b, d, g, 0, 0))
    st_shape = jax.ShapeDtypeStruct((B, 2, SSM_HEADS, P, N), jnp.float32)
    out_specs, out_shape = [st_spec], [st_shape]
    if want_y:
        out_specs = [pl.BlockSpec((1, 1, Q, R * P), lambda b, d, g, c: (d, b, chunk(d, c), g))] + out_specs
        out_shape = [jax.ShapeDtypeStruct((2, B, L, SSM_INNER), jnp.float32)] + out_shape
    outs = pl.pallas_call(
        functools.partial(_ssd_kernel, want_y=want_y),
        grid=(B, 2, G, nc),
        in_specs=[x_spec, b_spec, c_spec, dt_spec, an_spec, st_spec],
        out_specs=out_specs,
        out_shape=out_shape,
        scratch_shapes=[pltpu.VMEM((R, P, N), jnp.float32)],
        compiler_params=pltpu.CompilerParams(dimension_semantics=("arbitrary",) * 4),
        name="ssd_scan_y" if want_y else "ssd_scan_state",
    )(xbc, xbc, xbc, dt_g, an_g, init)
    if want_y:
        return outs[0], outs[1]
    return None, outs[0]


def mamba_layer(xt, ctx_t, mods, cmods, n_batch, rows, w_in, conv_w, conv_b, a_log, dt_bias, d_skip, norm_w, w_out,
                ln_g, ln_b):
    f32 = jnp.float32
    sh1, sc1, g1 = mods
    csh1, csc1 = cmods
    a_neg = -jnp.exp(a_log.astype(f32))
    w_xbc = w_in[:, SSM_INNER:SSM_INNER + SSM_CONV_DIM]
    w_dt = w_in[:, SSM_INNER + SSM_CONV_DIM:]

    def softplus_dt(dt_raw, length):
        return jax.nn.softplus(dt_raw.reshape(n_batch, length, 2, SSM_HEADS) + dt_bias.astype(f32))

    lc = ctx_t.shape[0] // n_batch
    xbc_c = mod_matmul_conv(ctx_t, csc1, csh1, w_xbc, conv_w[1:2], conv_b, n_batch, 1, lc, True)
    dt_c = softplus_dt(mod_matmul(ctx_t, csc1, csh1, w_dt, 512, 2 * SSM_HEADS), lc)
    zeros = jnp.zeros((n_batch, 2, SSM_HEADS, SSM_HEAD_DIM, SSM_STATE), f32)
    _, states = ssd_scan_both(xbc_c.reshape(n_batch, lc, SSM_CONV_DIM), dt_c, a_neg, zeros, False)
    T = xt.shape[0]
    L = T // n_batch
    z = mod_matmul(xt, sc1, sh1, w_in[:, :SSM_INNER], 1024, 1024)
    xbc = mod_matmul_conv(xt, sc1, sh1, w_xbc, conv_w, conv_b, n_batch, rows, GRID_W, True)
    dt = softplus_dt(mod_matmul(xt, sc1, sh1, w_dt, 1024, 2 * SSM_HEADS), L)
    y2, _ = ssd_scan_both(xbc.reshape(n_batch, L, SSM_CONV_DIM), dt, a_neg, states, True)
    return mamba_out(y2.reshape(2, T, SSM_INNER), xbc, z, d_skip, norm_w, w_out, xt, g1, ln_g, ln_b)


def hyena_filters(L, w1, b1, w2, b2, w3, freq):
    f32 = jnp.float32
    t = jnp.linspace(0.0, 1.0, L, dtype=f32)[:, None]
    bands = (HYENA_EMB - 1) // 2
    ang = 2.0 * math.pi * jnp.arange(L, dtype=f32)[:, None] / L
    fb = jnp.linspace(1e-4, bands - 1, bands, dtype=f32)[None, :]
    z = jnp.concatenate([t, jnp.cos(fb * ang), -jnp.sin(fb * ang)], axis=-1)
    fr = freq.astype(f32)
    hid = jnp.sin(fr * (z @ w1.astype(f32) + b1.astype(f32)))
    for j in range(HYENA_INNER_MLPS):
        hid = jnp.sin(fr * (hid @ w2[j].astype(f32) + b2[j].astype(f32)))
    h = (hid @ w3.astype(f32)).reshape(L, HYENA_ORDER, 2, D_MODEL)
    max_decay = math.log(HYENA_TARGET) / HYENA_STEEP_DECAY
    min_decay = math.log(HYENA_TARGET) / HYENA_GENTLE_DECAY
    deltas = jnp.linspace(min_decay, max_decay, D_MODEL, dtype=f32)
    h = h * jnp.exp(-t * jnp.abs(deltas))[:, None, None, :]
    h_fwd, h_bwd = h[:, :, 0], h[:, :, 1]
    two_sided = jnp.concatenate([h_fwd[:1] + h_bwd[:1], h_fwd[1:],
                                 jnp.zeros((1, HYENA_ORDER, D_MODEL), f32),
                                 jnp.flip(h_bwd[1:], 0)], axis=0)
    two_sided = two_sided * lax.rsqrt(jnp.sum(jnp.square(two_sided), axis=0, keepdims=True) + HYENA_NORM_EPS)
    return jnp.fft.fft(two_sided, axis=0)


FFT_N1 = 128
FFT_N2 = 64
FFT_N = FFT_N1 * FFT_N2
CONV_L = FFT_N // 2
CONV_T1 = CONV_L // FFT_N2
FFT_VMEM_LIMIT = 56 * 1024 * 1024
FFT_UNROLL = 4


def _dft_tables():
    def cos_sin(phase, n):
        ang = (2.0 * math.pi / n) * (phase % n).astype(jnp.float32)
        return jnp.cos(ang), jnp.sin(ang)
    t2 = jnp.arange(FFT_N2, dtype=jnp.int32)[:, None, None]
    k1 = jnp.arange(FFT_N1, dtype=jnp.int32)[None, :, None]
    t1 = jnp.arange(CONV_T1, dtype=jnp.int32)[None, None, :]
    c, s = cos_sin(FFT_N2 * t1 * k1 + t2 * k1, FFT_N)
    fwd = jnp.concatenate([c, -s], axis=1)
    inv = jnp.concatenate([c, s], axis=1).reshape(FFT_N2, 2, FFT_N1, CONV_T1)
    inv = inv.transpose(0, 1, 3, 2).reshape(FFT_N2, 2 * CONV_T1, FFT_N1) / FFT_N
    k2 = jnp.arange(FFT_N2, dtype=jnp.int32)
    c2, s2 = cos_sin(k2[:, None] * k2[None, :], FFT_N2)
    mid = jnp.concatenate([c2, -s2], axis=0)
    return fwd, mid, inv


def _split_bf16(w):
    hi = w.astype(jnp.bfloat16)
    return hi, (w - hi.astype(jnp.float32)).astype(jnp.bfloat16)


def _mm3(w_hi, w_lo, x):
    x_hi, x_lo = _split_bf16(x)
    f32 = jnp.float32
    return (jnp.dot(w_hi, x_hi, preferred_element_type=f32) + jnp.dot(w_lo, x_hi, preferred_element_type=f32)
            + jnp.dot(w_hi, x_lo, preferred_element_type=f32))


def _long_conv_kernel(x_ref, g_ref, hr_ref, hi_ref, b_ref, wfh_ref, wfl_ref, wmh_ref, wml_ref, wih_ref, wil_ref,
                      o_ref, sr_ref, si_ref):
    def rows_t2(t2):
        return pl.ds(t2, CONV_T1, stride=FFT_N2)

    def stage_a(t2, carry):
        wh, wl = wfh_ref[t2], wfl_ref[t2]
        p = _mm3(wh, wl, x_ref.at[0][rows_t2(t2), :])
        q = _mm3(wh, wl, x_ref.at[1][rows_t2(t2), :])
        dst = pl.ds(pl.multiple_of(t2 * FFT_N1, FFT_N1), FFT_N1)
        sr_ref[dst, :] = p[:FFT_N1] - q[FFT_N1:]
        si_ref[dst, :] = p[FFT_N1:] + q[:FFT_N1]
        return carry

    lax.fori_loop(0, FFT_N2, stage_a, 0, unroll=FFT_UNROLL)
    wmh, wml = wmh_ref[...], wml_ref[...]

    def per_k1(k1, carry):
        sel = pl.ds(k1, FFT_N2, stride=FFT_N1)
        p = _mm3(wmh, wml, sr_ref[sel, :])
        q = _mm3(wmh, wml, si_ref[sel, :])
        xr = p[:FFT_N2] - q[FFT_N2:]
        xi = p[FFT_N2:] + q[:FFT_N2]
        hsel = pl.ds(pl.multiple_of(k1 * FFT_N2, FFT_N2), FFT_N2)
        hr = hr_ref[hsel, :]
        hi = hi_ref[hsel, :]
        p = _mm3(wmh, wml, xr * hr - xi * hi)
        q = _mm3(wmh, wml, xr * hi + xi * hr)
        sr_ref[sel, :] = p[:FFT_N2] + q[FFT_N2:]
        si_ref[sel, :] = q[:FFT_N2] - p[FFT_N2:]
        return carry

    lax.fori_loop(0, FFT_N1, per_k1, 0, unroll=FFT_UNROLL)
    bias = b_ref[...]

    def stage_c(t2, carry):
        wh, wl = wih_ref[t2], wil_ref[t2]
        src = pl.ds(pl.multiple_of(t2 * FFT_N1, FFT_N1), FFT_N1)
        p = _mm3(wh, wl, sr_ref[src, :])
        q = _mm3(wh, wl, si_ref[src, :])
        rows = rows_t2(t2)
        o_ref.at[0][rows, :] = g_ref.at[0][rows, :] * (p[:CONV_T1] - q[CONV_T1:] + x_ref.at[0][rows, :] * bias)
        o_ref.at[1][rows, :] = g_ref.at[1][rows, :] * (q[:CONV_T1] + p[CONV_T1:] + x_ref.at[1][rows, :] * bias)
        return carry

    lax.fori_loop(0, FFT_N2, stage_c, 0, unroll=FFT_UNROLL)


def _spectrum_layout(hf_full):
    c = hf_full.shape[1]
    return hf_full.reshape(FFT_N2, FFT_N1, c).transpose(1, 0, 2).reshape(FFT_N, c)


def long_conv_gated(u, u_col, gate, gate_col, hf, bias, tables):
    B, L, _ = u.shape
    C = hf.shape[1]
    assert L == CONV_L and B % 2 == 0 and C % VREG_LANES == 0
    assert u_col % VREG_LANES == 0 and gate_col % VREG_LANES == 0
    h_re, h_im = _spectrum_layout(jnp.real(hf)), _spectrum_layout(jnp.imag(hf))
    consts = [part for t in tables for part in _split_bf16(t)]
    sig_spec = pl.BlockSpec((2, L, VREG_LANES), lambda c, b: (b, 0, c))
    u_spec = pl.BlockSpec((2, L, VREG_LANES), lambda c, b: (b, 0, c + u_col // VREG_LANES))
    gate_spec = pl.BlockSpec((2, L, VREG_LANES), lambda c, b: (b, 0, c + gate_col // VREG_LANES))
    once = pl.Buffered(1)
    spec_h = pl.BlockSpec((FFT_N, VREG_LANES), lambda c, b: (0, c), pipeline_mode=once)
    const_specs = [pl.BlockSpec(t.shape, functools.partial(lambda nd, c, b: (0,) * nd, t.ndim), pipeline_mode=once)
                   for t in consts]
    return pl.pallas_call(
        _long_conv_kernel,
        grid=(C // VREG_LANES, B // 2),
        in_specs=[u_spec, gate_spec, spec_h, spec_h, pl.BlockSpec((1, VREG_LANES), lambda c, b: (0, c))] + const_specs,
        out_specs=sig_spec,
        out_shape=jax.ShapeDtypeStruct((B, L, C), jnp.float32),
        scratch_shapes=[pltpu.VMEM((FFT_N, VREG_LANES), jnp.float32), pltpu.VMEM((FFT_N, VREG_LANES), jnp.float32)],
        compiler_params=pltpu.CompilerParams(dimension_semantics=("arbitrary", "arbitrary"),
                                             vmem_limit_bytes=FFT_VMEM_LIMIT),
        name="long_conv",
    )(u, gate, h_re, h_im, bias.reshape(1, C), *consts)


def hyena_layer(xt, mods, n_batch, w_in, conv_w, conv_b, w1, b1, w2, b2, w3, freq, fbias, w_out, ln_g, ln_b):
    sh1, sc1, g1 = mods
    T = xt.shape[0]
    L = T // n_batch
    hf = hyena_filters(L, w1, b1, w2, b2, w3, freq)
    u = mod_matmul_conv(xt, sc1, sh1, w_in, conv_w[None], conv_b, n_batch, 1, L, False).reshape(n_batch, L, 3 * D_MODEL)
    tables = _dft_tables()
    zz = long_conv_gated(u, 2 * D_MODEL, u, 0, hf[:, 0], fbias[0], tables)
    zz = long_conv_gated(zz, 0, u, D_MODEL, hf[:, 1], fbias[1], tables)
    return proj_ln(zz.reshape(T, D_MODEL), w_out, xt, g1, ln_g, ln_b)


D_TILES = D_MODEL // VREG_LANES
PEER_SLOTS = PEER_HEADS * PEER_TOPK
PEER_TB = 64
PEER_VMEM_LIMIT = 52 * 1024 * 1024
HI_MASK = -65536


def _pack_rows_bf16(tab):
    n, d = tab.shape
    bits = lax.bitcast_convert_type(tab.astype(jnp.bfloat16), jnp.uint16).astype(jnp.uint32)
    packed = (bits[1::2] << 16) | bits[0::2]
    return lax.bitcast_convert_type(packed, jnp.int32).reshape(n // 2 * (d // VREG_LANES), VREG_LANES)


def _unpack_row(tab_ref, off, shift):
    word = tab_ref[pl.ds(pl.multiple_of(off, VREG_SUBLANES), VREG_SUBLANES), :]
    return pltpu.bitcast((word << shift) & HI_MASK, jnp.float32)


def _sublane_sums(tiles, sub):
    for h in (1, 2, 4):
        keep = (sub & h) == 0
        nxt = []
        for a, b in zip(tiles[0::2], tiles[1::2]):
            nxt.append(jnp.where(keep, a, b) + pltpu.roll(jnp.where(keep, b, a), h, 0))
        tiles = nxt
    return tiles[0]


def _splat_tile(row):
    return jnp.broadcast_to(row, (PEER_SLOTS, VREG_LANES)).T


def _splat_small_ints_tile(row, eye, ones):
    diag = jnp.where(eye, row.astype(jnp.float32), 0.0).astype(jnp.bfloat16)
    return jnp.dot(diag, ones, preferred_element_type=jnp.float32).astype(jnp.int32)


def _splat_consts():
    eye = (lax.broadcasted_iota(jnp.int32, (PEER_SLOTS, PEER_SLOTS), 0)
           == lax.broadcasted_iota(jnp.int32, (PEER_SLOTS, PEER_SLOTS), 1))
    return eye, jnp.ones((PEER_SLOTS, VREG_LANES), jnp.bfloat16)


def _splat_block(shift_ref, shift_s, w_ref=None, w_s=None):
    eye, ones = _splat_consts()

    def token(t, carry):
        dst = pl.ds(pl.multiple_of(t * PEER_SLOTS, PEER_SLOTS), PEER_SLOTS)
        shift_s[dst, :] = _splat_small_ints_tile(shift_ref[pl.ds(t, 1), :], eye, ones)
        if w_ref is not None:
            w_s[dst, :] = _splat_tile(w_ref[pl.ds(t, 1), :])
        return carry
    lax.fori_loop(0, PEER_TB, token, 0, unroll=8)


def _tile_rows(k):
    return pl.ds(k, PEER_TB, stride=D_TILES)


def _peer_act_kernel(row_ref, shift_ref, x_ref, sc_ref, sh_ref, gate_ref, tab_ref, w_ref, part_ref, shift_s, xt_ref):
    sub = lax.broadcasted_iota(jnp.int32, (VREG_SUBLANES, VREG_LANES), 0)
    _splat_block(shift_ref, shift_s)
    h = x_ref[...] * (1.0 + sc_ref[0]) + sh_ref[0]
    for k in range(D_TILES):
        xt_ref[_tile_rows(k), :] = h[:, k * VREG_LANES:(k + 1) * VREG_LANES]

    def token(t, carry):
        xt = xt_ref[pl.ds(pl.multiple_of(t * D_TILES, D_TILES), D_TILES), :]
        base = pl.multiple_of(t * PEER_SLOTS, PEER_SLOTS)
        tok_rows = row_ref.at[0, 0, pl.ds(base, PEER_SLOTS)]
        for g in range(PEER_SLOTS // VREG_SUBLANES):
            prods = []
            for k in range(VREG_SUBLANES):
                s = g * VREG_SUBLANES + k
                prods.append(_unpack_row(tab_ref, tok_rows[s], shift_s[pl.ds(base + s, 1), :]) * xt)
            part_ref[pl.ds(base + g * VREG_SUBLANES, VREG_SUBLANES), :] = _sublane_sums(prods, sub)
        return carry

    lax.fori_loop(0, PEER_TB, token, 0)
    part = part_ref[...]
    part_hi = part.astype(jnp.bfloat16)
    part_lo = (part - part_hi.astype(jnp.float32)).astype(jnp.bfloat16)
    contract_lanes = (((1,), (1,)), ((), ()))
    ones = jnp.ones((VREG_SUBLANES, VREG_LANES), jnp.bfloat16)
    act = (lax.dot_general(ones, part_hi, contract_lanes, preferred_element_type=jnp.float32)
           + lax.dot_general(ones, part_lo, contract_lanes, preferred_element_type=jnp.float32))[0:1]
    gelu = 0.5 * act * (1.0 + lax.erf(act * (2.0 ** -0.5)))
    w_ref[0] = gate_ref[0] * gelu


def _peer_out_kernel(row_ref, shift_ref, w_ref, tab_ref, x_ref, g_ref, lng_ref, lnb_ref, o_ref,
                     shift_s, w_s, yt_ref):
    n_acc = 2
    _splat_block(shift_ref, shift_s, w_ref, w_s)

    def pair(i, carry):
        bases = [pl.multiple_of((2 * i + k) * PEER_SLOTS, PEER_SLOTS) for k in range(2)]
        rows = [row_ref.at[0, 0, pl.ds(b_, PEER_SLOTS)] for b_ in bases]
        accs = [[jnp.zeros((VREG_SUBLANES, VREG_LANES), jnp.float32) for _ in range(n_acc)] for _ in range(2)]
        for s in range(PEER_SLOTS):
            for k in range(2):
                val = _unpack_row(tab_ref, rows[k][s], shift_s[pl.ds(bases[k] + s, 1), :])
                accs[k][s % n_acc] = accs[k][s % n_acc] + w_s[pl.ds(bases[k] + s, 1), :] * val
        for k in range(2):
            yt_ref[pl.ds(pl.multiple_of((2 * i + k) * D_TILES, D_TILES), D_TILES), :] = accs[k][0] + accs[k][1]
        return carry

    lax.fori_loop(0, PEER_TB // 2, pair, 0)
    y = jnp.concatenate([yt_ref[_tile_rows(k), :] for k in range(D_TILES)], axis=1)
    o_ref[...] = _layer_norm_rows(DEEPNORM_ALPHA * x_ref[...] + g_ref[0] * y, lng_ref[...], lnb_ref[...])


def peer_experts_ln(x, mods, offs, shifts, gate, u_tab, v_tab, ln_g, ln_b):
    sh, sc, g = mods
    T = x.shape[0]
    nblk = T // PEER_TB
    per_blk = PEER_TB * PEER_SLOTS
    per_mod = T // sc.shape[0] // PEER_TB
    offs = offs.reshape(nblk, 1, per_blk)
    smem_spec = pl.BlockSpec((1, 1, per_blk), lambda i: (i, 0, 0), memory_space=pltpu.SMEM)
    lane_spec = pl.BlockSpec((1, 1, per_blk), lambda i: (i, 0, 0))
    slot_spec = pl.BlockSpec((PEER_TB, PEER_SLOTS), lambda i: (i, 0))
    tok_spec = pl.BlockSpec((PEER_TB, D_MODEL), lambda i: (i, 0))
    mod_spec = pl.BlockSpec((1, 1, D_MODEL), lambda i: (i // per_mod, 0, 0))
    vec_spec = pl.BlockSpec((1, D_MODEL), lambda i: (0, 0))
    tab_spec = pl.BlockSpec(memory_space=pltpu.VMEM)
    params = pltpu.CompilerParams(dimension_semantics=("arbitrary",), vmem_limit_bytes=PEER_VMEM_LIMIT)
    splat_i32 = pltpu.VMEM((per_blk, VREG_LANES), jnp.int32)
    splat_f32 = pltpu.VMEM((per_blk, VREG_LANES), jnp.float32)
    tok_tiles = pltpu.VMEM((PEER_TB * D_TILES, VREG_LANES), jnp.float32)
    w = pl.pallas_call(
        _peer_act_kernel,
        grid=(nblk,),
        in_specs=[smem_spec, slot_spec, tok_spec, mod_spec, mod_spec, lane_spec, tab_spec],
        out_specs=lane_spec,
        out_shape=jax.ShapeDtypeStruct((nblk, 1, per_blk), jnp.float32),
        scratch_shapes=[splat_f32, splat_i32, tok_tiles],
        compiler_params=params,
        name="peer_act",
    )(offs, shifts, x, sc, sh, gate.reshape(nblk, 1, per_blk), _pack_rows_bf16(u_tab))
    return pl.pallas_call(
        _peer_out_kernel,
        grid=(nblk,),
        in_specs=[smem_spec, slot_spec, slot_spec, tab_spec, tok_spec, mod_spec, vec_spec, vec_spec],
        out_specs=tok_spec,
        out_shape=jax.ShapeDtypeStruct((T, D_MODEL), jnp.float32),
        scratch_shapes=[splat_i32, splat_f32, tok_tiles],
        compiler_params=params,
        name="peer_out",
    )(offs, shifts, w.reshape(T, PEER_SLOTS), _pack_rows_bf16(v_tab), x, g,
      ln_g.reshape(1, D_MODEL), ln_b.reshape(1, D_MODEL))


def _top_rows(s, k):
    n = s.shape[0]
    row = lax.broadcasted_iota(jnp.int32, s.shape, 0)
    vals, idxs = [], []
    for _ in range(k):
        m = jnp.max(s, axis=0, keepdims=True)
        i = jnp.min(jnp.where(s == m, row, n), axis=0, keepdims=True)
        vals.append(m)
        idxs.append(i)
        s = jnp.where(row == i, -jnp.inf, s)
    return jnp.concatenate(vals, axis=0), jnp.concatenate(idxs, axis=0)


def _pick_rows(table, which):
    out = jnp.zeros(which.shape, table.dtype)
    for a in range(table.shape[0]):
        out = jnp.where(which == a, table[a:a + 1, :], out)
    return out


PAIR_COUNTS = tuple(PEER_TOPK // (a + 1) for a in range(PEER_TOPK))
PAIR_STARTS = tuple(sum(PAIR_COUNTS[:a]) for a in range(PEER_TOPK))
N_PAIRS = sum(PAIR_COUNTS)
N_PAIRS_PADDED = -(-N_PAIRS // VREG_SUBLANES) * VREG_SUBLANES
SEL_TB = 512


def _peer_select_kernel(x_ref, sc_ref, sh_ref, wq_ref, k1_ref, k2_ref, off_ref, shift_ref, gate_ref,
                        off_s, shift_s, gate_s):
    half = PEER_DK // 2
    h = (x_ref[...] * (1.0 + sc_ref[0]) + sh_ref[0]).astype(jnp.bfloat16)
    q = jnp.dot(h, wq_ref[...], preferred_element_type=jnp.float32).astype(jnp.bfloat16)
    contract_last = (((1,), (1,)), ((), ()))
    s1 = lax.dot_general(k1_ref[0], q[:, :half], contract_last, preferred_element_type=jnp.float32)
    s2 = lax.dot_general(k2_ref[0], q[:, half:], contract_last, preferred_element_type=jnp.float32)
    v1, i1 = _top_rows(s1, PEER_TOPK)
    v2, i2 = _top_rows(s2, PEER_TOPK)
    pairs = [v1[a:a + 1, :] + v2[:PAIR_COUNTS[a], :] for a in range(PEER_TOPK)]
    pairs.append(jnp.full((N_PAIRS_PADDED - N_PAIRS, v1.shape[1]), -jnp.inf, jnp.float32))
    top_s, pos = _top_rows(jnp.concatenate(pairs, axis=0), PEER_TOPK)
    a_sel = jnp.zeros(pos.shape, jnp.int32)
    start = jnp.zeros(pos.shape, jnp.int32)
    for a in range(1, PEER_TOPK):
        later = pos >= PAIR_STARTS[a]
        a_sel = jnp.where(later, a, a_sel)
        start = jnp.where(later, PAIR_STARTS[a], start)
    expert = _pick_rows(i1, a_sel) * PEER_NKEYS + _pick_rows(i2, pos - start)
    e = jnp.exp(top_s - top_s[0:1, :])
    head = pl.program_id(1)
    rows = pl.ds(pl.multiple_of(head * PEER_TOPK, PEER_TOPK), PEER_TOPK)
    gate_s[rows, :] = e / jnp.sum(e, axis=0, keepdims=True)
    off_s[rows, :] = (expert >> 1) * VREG_SUBLANES
    shift_s[rows, :] = jnp.where((expert & 1) == 1, 0, 16)

    @pl.when(head == PEER_HEADS - 1)
    def _():
        gate_ref[...] = gate_s[...].T
        off_ref[...] = off_s[...].astype(jnp.float32).T.astype(jnp.int32)
        shift_ref[...] = shift_s[...].astype(jnp.float32).T.astype(jnp.int32)


def peer_select(x, sc, sh, wq, k1, k2):
    T = x.shape[0]
    per_seq = T // sc.shape[0] // SEL_TB
    bf16 = jnp.bfloat16
    tok_spec = pl.BlockSpec((SEL_TB, D_MODEL), lambda i, h: (i, 0))
    mod_spec = pl.BlockSpec((1, 1, D_MODEL), lambda i, h: (i // per_seq, 0, 0))
    wq_spec = pl.BlockSpec((D_MODEL, PEER_DK), lambda i, h: (0, h))
    key_spec = pl.BlockSpec((1, PEER_NKEYS, PEER_DK // 2), lambda i, h: (h, 0, 0))
    out_spec = pl.BlockSpec((SEL_TB, PEER_SLOTS), lambda i, h: (i, 0))
    return pl.pallas_call(
        _peer_select_kernel,
        grid=(T // SEL_TB, PEER_HEADS),
        in_specs=[tok_spec, mod_spec, mod_spec, wq_spec, key_spec, key_spec],
        out_specs=[out_spec, out_spec, out_spec],
        out_shape=[jax.ShapeDtypeStruct((T, PEER_SLOTS), jnp.int32), jax.ShapeDtypeStruct((T, PEER_SLOTS), jnp.int32),
                   jax.ShapeDtypeStruct((T, PEER_SLOTS), jnp.float32)],
        scratch_shapes=[pltpu.VMEM((PEER_SLOTS, SEL_TB), jnp.int32), pltpu.VMEM((PEER_SLOTS, SEL_TB), jnp.int32),
                        pltpu.VMEM((PEER_SLOTS, SEL_TB), jnp.float32)],
        compiler_params=pltpu.CompilerParams(dimension_semantics=("arbitrary", "arbitrary")),
        name="peer_select",
    )(x, sc, sh, wq.astype(bf16), k1.astype(bf16), k2.astype(bf16))


def peer_layer(xt, mods, wq, k1, k2, u_tab, v_tab, ln_g, ln_b):
    sh2, sc2, _ = mods
    off, shift, gate = peer_select(xt, sc2, sh2, wq, k1, k2)
    return peer_experts_ln(xt, mods, off, shift, gate, u_tab, v_tab, ln_g, ln_b)


def kernel(x, c, ctx, c_ctx, ada_w, ada_b, ln_g, ln_b,
           m_w_in, m_conv_w, m_conv_b, m_a_log, m_dt_bias, m_d, m_norm_w, m_w_out,
           h_w_in, h_conv_w, h_conv_b, h_f_w1, h_f_b1, h_f_w2, h_f_b2, h_f_w3, h_freq, h_bias, h_w_out,
           p_wq, p_k1, p_k2, p_u, p_v):
    n_batch, seq, _ = x.shape
    rows = seq // GRID_W
    s_lat = jax.nn.silu(c)
    s_ctx = jax.nn.silu(c_ctx)
    xt = x.reshape(n_batch * seq, D_MODEL)
    ctx_t = ctx.reshape(-1, D_MODEL)
    for i in range(DEPTH):
        kind, k = i % N_MIXERS, i // N_MIXERS
        sh1, sc1, g1, sh2, sc2, g2 = jnp.split((s_lat @ ada_w[i] + ada_b[i])[:, None, :], 6, axis=-1)
        if kind == 0:
            csh1, csc1 = jnp.split((s_ctx @ ada_w[i] + ada_b[i])[None, None, :], 6, axis=-1)[:2]
            xt = mamba_layer(xt, ctx_t, (sh1, sc1, g1), (csh1, csc1), n_batch, rows, m_w_in[k], m_conv_w[k],
                             m_conv_b[k], m_a_log[k], m_dt_bias[k], m_d[k], m_norm_w[k], m_w_out[k],
                             ln_g[i, 0], ln_b[i, 0])
        else:
            xt = hyena_layer(xt, (sh1, sc1, g1), n_batch, h_w_in[k], h_conv_w[k], h_conv_b[k], h_f_w1[k], h_f_b1[k],
                             h_f_w2[k], h_f_b2[k], h_f_w3[k], h_freq[k], h_bias[k], h_w_out[k], ln_g[i, 0], ln_b[i, 0])
        xt = peer_layer(xt, (sh2, sc2, g2), p_wq[i], p_k1[i], p_k2[i], p_u[i], p_v[i], ln_g[i, 1], ln_b[i, 1])
    return xt.reshape(x.shape)
```

```python
import functools
import math

import jax
import jax.numpy as jnp
from jax import lax
from jax.experimental import pallas as pl
from jax.experimental.pallas import tpu as pltpu

D_MODEL = 1024
DEPTH = 2
GRID_W = 64
N_MIXERS = 2
DEEPNORM_ALPHA = (2 * DEPTH) ** 0.25
LN_EPS = 1e-5
RMS_EPS = 1e-5

SSM_INNER = 2 * D_MODEL
SSM_HEAD_DIM = 64
SSM_HEADS = SSM_INNER // SSM_HEAD_DIM
SSM_GROUPS = 4
SSM_STATE = 128
SSM_CHUNK = 128
SSM_CONV = 3
SSM_CONV_DIM = SSM_INNER + 2 * SSM_GROUPS * SSM_STATE
SSM_GROUP_HEADS = SSM_HEADS // SSM_GROUPS

HYENA_ORDER = 2
HYENA_EMB = 33
HYENA_INNER_MLPS = 2
HYENA_STEEP_DECAY = 0.3
HYENA_GENTLE_DECAY = 1.5
HYENA_TARGET = 1e-2
HYENA_NORM_EPS = 1e-6

PEER_HEADS = 8
PEER_NKEYS = 128
PEER_DK = 256
PEER_TOPK = 16

VREG_SUBLANES = 8
VREG_LANES = 128
DENSE_VMEM_LIMIT = 48 * 1024 * 1024


def _mod_matmul_kernel(x_ref, sc_ref, sh_ref, w_ref, o_ref):
    h = (x_ref[...] * (1.0 + sc_ref[0]) + sh_ref[0]).astype(jnp.bfloat16)
    o_ref[...] = jnp.dot(h, w_ref[...], preferred_element_type=jnp.float32)


def mod_matmul(x, sc, sh, w, tb, tn):
    T, D = x.shape
    N = w.shape[1]
    assert T % tb == 0 and N % tn == 0 and (T // sc.shape[0]) % tb == 0
    per_mod = T // sc.shape[0] // tb
    mod_spec = pl.BlockSpec((1, 1, D), lambda i, j: (i // per_mod, 0, 0))
    return pl.pallas_call(
        _mod_matmul_kernel,
        grid=(T // tb, N // tn),
        in_specs=[pl.BlockSpec((tb, D), lambda i, j: (i, 0)), mod_spec, mod_spec,
                  pl.BlockSpec((D, tn), lambda i, j: (0, j))],
        out_specs=pl.BlockSpec((tb, tn), lambda i, j: (i, j)),
        out_shape=jax.ShapeDtypeStruct((T, N), jnp.float32),
        compiler_params=pltpu.CompilerParams(dimension_semantics=("arbitrary", "arbitrary"),
                                             vmem_limit_bytes=DENSE_VMEM_LIMIT),
        name="mod_matmul",
    )(x, sc, sh, w.astype(jnp.bfloat16))


def _layer_norm_rows(v, g, b):
    mu = jnp.mean(v, axis=-1, keepdims=True)
    vc = v - mu
    var = jnp.mean(vc * vc, axis=-1, keepdims=True)
    return vc * lax.rsqrt(var + LN_EPS) * g + b


def _proj_ln_kernel(a_ref, w_ref, x_ref, gate_ref, g_ref, b_ref, o_ref):
    y = jnp.dot(a_ref[...].astype(jnp.bfloat16), w_ref[...], preferred_element_type=jnp.float32)
    o_ref[...] = _layer_norm_rows(DEEPNORM_ALPHA * x_ref[...] + gate_ref[0] * y, g_ref[...], b_ref[...])


def _row_specs(tb, per_mod):
    tok = pl.BlockSpec((tb, D_MODEL), lambda i: (i, 0))
    mod = pl.BlockSpec((1, 1, D_MODEL), lambda i: (i // per_mod, 0, 0))
    vec = pl.BlockSpec((1, D_MODEL), lambda i: (0, 0))
    return tok, mod, vec


def proj_ln(a, w, x, gate, ln_g, ln_b, tb=512):
    T, K = a.shape
    per_mod = T // gate.shape[0] // tb
    tok, mod, vec = _row_specs(tb, per_mod)
    return pl.pallas_call(
        _proj_ln_kernel,
        grid=(T // tb,),
        in_specs=[pl.BlockSpec((tb, K), lambda i: (i, 0)), pl.BlockSpec((K, D_MODEL), lambda i: (0, 0)),
                  tok, mod, vec, vec],
        out_specs=tok,
        out_shape=jax.ShapeDtypeStruct((T, D_MODEL), jnp.float32),
        compiler_params=pltpu.CompilerParams(dimension_semantics=("arbitrary",), vmem_limit_bytes=DENSE_VMEM_LIMIT),
        name="proj_ln",
    )(a, w.astype(jnp.bfloat16), x, gate, ln_g.reshape(1, D_MODEL), ln_b.reshape(1, D_MODEL))


def _residual_ln_kernel(y_ref, x_ref, gate_ref, g_ref, b_ref, o_ref):
    o_ref[...] = _layer_norm_rows(DEEPNORM_ALPHA * x_ref[...] + gate_ref[0] * y_ref[...], g_ref[...], b_ref[...])


def residual_ln(y, x, gate, ln_g, ln_b, tb=512):
    T = x.shape[0]
    tok, mod, vec = _row_specs(tb, T // gate.shape[0] // tb)
    return pl.pallas_call(
        _residual_ln_kernel,
        grid=(T // tb,),
        in_specs=[tok, tok, mod, vec, vec],
        out_specs=tok,
        out_shape=jax.ShapeDtypeStruct((T, D_MODEL), jnp.float32),
        compiler_params=pltpu.CompilerParams(dimension_semantics=("arbitrary",)),
        name="residual_ln",
    )(y, x, gate, ln_g.reshape(1, D_MODEL), ln_b.reshape(1, D_MODEL))


def _mamba_out_kernel(y_ref, xs_ref, z_ref, dsk_ref, nw_ref, w_ref, x_ref, gate_ref, g_ref, b_ref, o_ref):
    z = z_ref[...]
    u = (y_ref[0] + y_ref[1] + xs_ref[...] * dsk_ref[...]) * (z * jax.nn.sigmoid(z))
    gw = SSM_INNER // SSM_GROUPS
    parts = []
    for g in range(SSM_GROUPS):
        ug = u[:, g * gw:(g + 1) * gw]
        parts.append(ug * lax.rsqrt(jnp.mean(ug * ug, axis=-1, keepdims=True) + RMS_EPS))
    a = (jnp.concatenate(parts, axis=1) * nw_ref[...]).astype(jnp.bfloat16)
    y = jnp.dot(a, w_ref[...], preferred_element_type=jnp.float32)
    o_ref[...] = _layer_norm_rows(DEEPNORM_ALPHA * x_ref[...] + gate_ref[0] * y, g_ref[...], b_ref[...])


def mamba_out(y2, xbc, z, d_skip, norm_w, w_out, x, gate, ln_g, ln_b, tb=256):
    T = x.shape[0]
    tok, mod, vec = _row_specs(tb, T // gate.shape[0] // tb)
    inner = pl.BlockSpec((tb, SSM_INNER), lambda i: (i, 0))
    ivec = pl.BlockSpec((1, SSM_INNER), lambda i: (0, 0))
    dsk = jnp.repeat(d_skip, SSM_INNER // d_skip.shape[0]).reshape(1, SSM_INNER)
    return pl.pallas_call(
        _mamba_out_kernel,
        grid=(T // tb,),
        in_specs=[pl.BlockSpec((2, tb, SSM_INNER), lambda i: (0, i, 0)), inner, inner, ivec, ivec,
                  pl.BlockSpec((SSM_INNER, D_MODEL), lambda i: (0, 0)), tok, mod, vec, vec],
        out_specs=tok,
        out_shape=jax.ShapeDtypeStruct((T, D_MODEL), jnp.float32),
        compiler_params=pltpu.CompilerParams(dimension_semantics=("arbitrary",), vmem_limit_bytes=DENSE_VMEM_LIMIT),
        name="mamba_out",
    )(y2, xbc, z, dsk, norm_w.reshape(1, SSM_INNER), w_out.astype(jnp.bfloat16), x, gate,
      ln_g.reshape(1, D_MODEL), ln_b.reshape(1, D_MODEL))


CONV_VMEM_LIMIT = 56 * 1024 * 1024


def _shift_rows(a, k):
    n = a.shape[0]
    return a if k == 0 else pltpu.roll(a, (-k) % n, 0)


def _mod_matmul_conv_kernel(x_ref, sc_ref, sh_ref, w_ref, cw_ref, cb_ref, o_ref, *, rows, cols, silu):
    h = (x_ref[...] * (1.0 + sc_ref[0]) + sh_ref[0]).astype(jnp.bfloat16)
    p = jnp.dot(h, w_ref[...], preferred_element_type=jnp.float32)
    L = p.shape[0]
    t = lax.broadcasted_iota(jnp.int32, p.shape, 0)
    col = t & (cols - 1)
    n_dr = cw_ref.shape[0]
    out = jnp.zeros_like(p)
    for j in range(3):
        dc = j - 1
        acc = None
        for i in range(n_dr):
            dr = i - n_dr // 2
            term = _shift_rows(p, dr * cols)
            if dr < 0:
                term = jnp.where(t >= -dr * cols, term, 0.0)
            elif dr > 0:
                term = jnp.where(t < L - dr * cols, term, 0.0)
            term = term * cw_ref[i, j:j + 1, :]
            acc = term if acc is None else acc + term
        acc = _shift_rows(acc, dc)
        if dc < 0:
            acc = jnp.where(col >= -dc, acc, 0.0)
        elif dc > 0:
            acc = jnp.where(col < cols - dc, acc, 0.0)
        out = out + acc
    out = out + cb_ref[...]
    o_ref[...] = out * jax.nn.sigmoid(out) if silu else out


def mod_matmul_conv(x, sc, sh, w, conv_w, conv_b, n_seq, rows, cols, silu, tn=256):
    T, D = x.shape
    N = w.shape[1]
    L = rows * cols
    assert T == n_seq * L and N % tn == 0 and cols & (cols - 1) == 0 and conv_w.shape[0] in (1, 3)
    per_mod = n_seq // sc.shape[0]
    mod_spec = pl.BlockSpec((1, 1, D), lambda b, j: (b // per_mod, 0, 0))
    return pl.pallas_call(
        functools.partial(_mod_matmul_conv_kernel, rows=rows, cols=cols, silu=silu),
        grid=(n_seq, N // tn),
        in_specs=[pl.BlockSpec((L, D), lambda b, j: (b, 0), pipeline_mode=pl.Buffered(1)), mod_spec, mod_spec,
                  pl.BlockSpec((D, tn), lambda b, j: (0, j)),
                  pl.BlockSpec((conv_w.shape[0], 3, tn), lambda b, j: (0, 0, j)),
                  pl.BlockSpec((1, tn), lambda b, j: (0, j))],
        out_specs=pl.BlockSpec((L, tn), lambda b, j: (b, j)),
        out_shape=jax.ShapeDtypeStruct((T, N), jnp.float32),
        compiler_params=pltpu.CompilerParams(dimension_semantics=("arbitrary", "arbitrary"),
                                             vmem_limit_bytes=CONV_VMEM_LIMIT),
        name="mod_matmul_conv",
    )(x, sc, sh, w.astype(jnp.bfloat16), conv_w, conv_b.reshape(1, N))


def _split3_bf16(a):
    f32, bf16 = jnp.float32, jnp.bfloat16
    p0 = a.astype(bf16)
    r = a - p0.astype(f32)
    p1 = r.astype(bf16)
    p2 = (r - p1.astype(f32)).astype(bf16)
    return p0, p1, p2


def _ssd_kernel(x_ref, b_ref, c_ref, dt_ref, an_ref, init_ref, *rest, want_y):
    if want_y:
        y_ref, fin_ref, state_ref = rest
    else:
        fin_ref, state_ref = rest
    f32, bf16 = jnp.float32, jnp.bfloat16
    Q, P, R = SSM_CHUNK, SSM_HEAD_DIM, SSM_GROUP_HEADS
    fwd = pl.program_id(1) == 0
    c = pl.program_id(3)

    @pl.when(c == 0)
    def _():
        state_ref[...] = init_ref[0, 0]

    dtg = dt_ref[0, 0, 0]
    ag = dtg * an_ref[0, 0]
    li = lax.broadcasted_iota(jnp.int32, (Q, Q), 0)
    si = lax.broadcasted_iota(jnp.int32, (Q, Q), 1)
    tri = jnp.where(fwd, li - si, si - li) >= 0
    tri_bf = jnp.where(tri, 1.0, 0.0).astype(bf16)
    pieces = _split3_bf16(ag)
    a_cum = sum(jnp.dot(tri_bf, p, preferred_element_type=f32) for p in pieces)
    a_cum_t = sum(lax.dot_general(p, tri_bf, (((0,), (1,)), ((), ())), preferred_element_type=f32)
                  for p in pieces)
    total = jnp.sum(ag, axis=0, keepdims=True)
    bm = b_ref[0].astype(bf16)
    cm = c_ref[0].astype(bf16)
    x = x_ref[0]
    if want_y:
        cb = lax.dot_general(cm, bm, (((1,), (1,)), ((), ())), preferred_element_type=f32)
    ys = []
    for r in range(R):
        acol = a_cum[:, r:r + 1]
        xdt = x[:, r * P:(r + 1) * P] * dtg[:, r:r + 1]
        state = state_ref[r]
        if want_y:
            decay = jnp.where(tri, jnp.exp(acol - a_cum_t[r:r + 1, :]), 0.0)
            y = jnp.dot((cb * decay).astype(bf16), xdt.astype(bf16), preferred_element_type=f32)
            y_off = lax.dot_general(cm, state.astype(bf16), (((1,), (1,)), ((), ())), preferred_element_type=f32)
            ys.append(y + y_off * jnp.exp(acol))
        tot = total[:, r:r + 1]
        xw = (xdt * jnp.exp(tot - acol)).astype(bf16)
        upd = lax.dot_general(xw, bm, (((0,), (0,)), ((), ())), preferred_element_type=f32)
        state_ref[r] = state * jnp.exp(tot) + upd
    if want_y:
        y_ref[0, 0] = jnp.concatenate(ys, axis=1)
    fin_ref[0, 0] = state_ref[...]


def ssd_scan_both(xbc, dt, a_neg, init, want_y):
    B, L, _ = xbc.shape
    Q, G, R, P, N = SSM_CHUNK, SSM_GROUPS, SSM_GROUP_HEADS, SSM_HEAD_DIM, SSM_STATE
    nc = L // Q
    dt_g = dt.reshape(B, L, 2, G, R).transpose(0, 2, 3, 1, 4)
    an_g = a_neg.reshape(2, G, 1, R)

    def chunk(d, c):
        return c + d * (nc - 1 - 2 * c)

    x_spec = pl.BlockSpec((1, Q, R * P), lambda b, d, g, c: (b, chunk(d, c), g))
    b_spec = pl.BlockSpec((1, Q, N), lambda b, d, g, c: (b, chunk(d, c), SSM_INNER // N + g))
    c_spec = pl.BlockSpec((1, Q, N), lambda b, d, g, c: (b, chunk(d, c), SSM_INNER // N + G + g))
    dt_spec = pl.BlockSpec((1, 1, 1, Q, R), lambda b, d, g, c: (b, d, g, chunk(d, c), 0))
    an_spec = pl.BlockSpec((1, 1, 1, R), lambda b, d, g, c: (d, g, 0, 0))
    st_spec = pl.BlockSpec((1, 1, R, P, N), lambda b, d, g, c: (b, d, g, 0, 0))
    st_shape = jax.ShapeDtypeStruct((B, 2, SSM_HEADS, P, N), jnp.float32)
    out_specs, out_shape = [st_spec], [st_shape]
    if want_y:
        out_specs = [pl.BlockSpec((1, 1, Q, R * P), lambda b, d, g, c: (d, b, chunk(d, c), g))] + out_specs
        out_shape = [jax.ShapeDtypeStruct((2, B, L, SSM_INNER), jnp.float32)] + out_shape
    outs = pl.pallas_call(
        functools.partial(_ssd_kernel, want_y=want_y),
        grid=(B, 2, G, nc),
        in_specs=[x_spec, b_spec, c_spec, dt_spec, an_spec, st_spec],
        out_specs=out_specs,
        out_shape=out_shape,
        scratch_shapes=[pltpu.VMEM((R, P, N), jnp.float32)],
        compiler_params=pltpu.CompilerParams(dimension_semantics=("arbitrary",) * 4),
        name="ssd_scan_y" if want_y else "ssd_scan_state",
    )(xbc, xbc, xbc, dt_g, an_g, init)
    if want_y:
        return outs[0], outs[1]
    return None, outs[0]


def mamba_layer(xt, ctx_t, mods, cmods, n_batch, rows, w_in, conv_w, conv_b, a_log, dt_bias, d_skip, norm_w, w_out,
                ln_g, ln_b):
    f32 = jnp.float32
    sh1, sc1, g1 = mods
    csh1, csc1 = cmods
    a_neg = -jnp.exp(a_log.astype(f32))
    w_xbc = w_in[:, SSM_INNER:SSM_INNER + SSM_CONV_DIM]
    w_dt = w_in[:, SSM_INNER + SSM_CONV_DIM:]

    def softplus_dt(dt_raw, length):
        return jax.nn.softplus(dt_raw.reshape(n_batch, length, 2, SSM_HEADS) + dt_bias.astype(f32))

    lc = ctx_t.shape[0] // n_batch
    xbc_c = mod_matmul_conv(ctx_t, csc1, csh1, w_xbc, conv_w[1:2], conv_b, n_batch, 1, lc, True)
    dt_c = softplus_dt(mod_matmul(ctx_t, csc1, csh1, w_dt, 512, 2 * SSM_HEADS), lc)
    zeros = jnp.zeros((n_batch, 2, SSM_HEADS, SSM_HEAD_DIM, SSM_STATE), f32)
    _, states = ssd_scan_both(xbc_c.reshape(n_batch, lc, SSM_CONV_DIM), dt_c, a_neg, zeros, False)
    T = xt.shape[0]
    L = T // n_batch
    z = mod_matmul(xt, sc1, sh1, w_in[:, :SSM_INNER], 1024, 1024)
    xbc = mod_matmul_conv(xt, sc1, sh1, w_xbc, conv_w, conv_b, n_batch, rows, GRID_W, True)
    dt = softplus_dt(mod_matmul(xt, sc1, sh1, w_dt, 1024, 2 * SSM_HEADS), L)
    y2, _ = ssd_scan_both(xbc.reshape(n_batch, L, SSM_CONV_DIM), dt, a_neg, states, True)
    return mamba_out(y2.reshape(2, T, SSM_INNER), xbc, z, d_skip, norm_w, w_out, xt, g1, ln_g, ln_b)


def hyena_filters(L, w1, b1, w2, b2, w3, freq):
    f32 = jnp.float32
    t = jnp.linspace(0.0, 1.0, L, dtype=f32)[:, None]
    bands = (HYENA_EMB - 1) // 2
    ang = 2.0 * math.pi * jnp.arange(L, dtype=f32)[:, None] / L
    fb = jnp.linspace(1e-4, bands - 1, bands, dtype=f32)[None, :]
    z = jnp.concatenate([t, jnp.cos(fb * ang), -jnp.sin(fb * ang)], axis=-1)
    fr = freq.astype(f32)
    hid = jnp.sin(fr * (z @ w1.astype(f32) + b1.astype(f32)))
    for j in range(HYENA_INNER_MLPS):
        hid = jnp.sin(fr * (hid @ w2[j].astype(f32) + b2[j].astype(f32)))
    h = (hid @ w3.astype(f32)).reshape(L, HYENA_ORDER, 2, D_MODEL)
    max_decay = math.log(HYENA_TARGET) / HYENA_STEEP_DECAY
    min_decay = math.log(HYENA_TARGET) / HYENA_GENTLE_DECAY
    deltas = jnp.linspace(min_decay, max_decay, D_MODEL, dtype=f32)
    h = h * jnp.exp(-t * jnp.abs(deltas))[:, None, None, :]
    h_fwd, h_bwd = h[:, :, 0], h[:, :, 1]
    two_sided = jnp.concatenate([h_fwd[:1] + h_bwd[:1], h_fwd[1:],
                                 jnp.zeros((1, HYENA_ORDER, D_MODEL), f32),
                                 jnp.flip(h_bwd[1:], 0)], axis=0)
    two_sided = two_sided * lax.rsqrt(jnp.sum(jnp.square(two_sided), axis=0, keepdims=True) + HYENA_NORM_EPS)
    return jnp.fft.fft(two_sided, axis=0)


FFT_N1 = 128
FFT_N2 = 64
FFT_N = FFT_N1 * FFT_N2
CONV_L = FFT_N // 2
CONV_T1 = CONV_L // FFT_N2
FFT_VMEM_LIMIT = 56 * 1024 * 1024
FFT_UNROLL = 4


def _dft_tables():
    def cos_sin(phase, n):
        ang = (2.0 * math.pi / n) * (phase % n).astype(jnp.float32)
        return jnp.cos(ang), jnp.sin(ang)
    t2 = jnp.arange(FFT_N2, dtype=jnp.int32)[:, None, None]
    k1 = jnp.arange(FFT_N1, dtype=jnp.int32)[None, :, None]
    t1 = jnp.arange(CONV_T1, dtype=jnp.int32)[None, None, :]
    c, s = cos_sin(FFT_N2 * t1 * k1 + t2 * k1, FFT_N)
    fwd = jnp.concatenate([c, -s], axis=1)
    inv = jnp.concatenate([c, s], axis=1).reshape(FFT_N2, 2, FFT_N1, CONV_T1)
    inv = inv.transpose(0, 1, 3, 2).reshape(FFT_N2, 2 * CONV_T1, FFT_N1) / FFT_N
    k2 = jnp.arange(FFT_N2, dtype=jnp.int32)
    c2, s2 = cos_sin(k2[:, None] * k2[None, :], FFT_N2)
    mid = jnp.concatenate([c2, -s2], axis=0)
    return fwd, mid, inv


def _mm(w, x):
    return jnp.dot(w, x.astype(jnp.bfloat16), preferred_element_type=jnp.float32)


def _long_conv_kernel(x_ref, g_ref, hr_ref, hi_ref, b_ref, wf_ref, wm_ref, wi_ref, o_ref, sr_ref, si_ref):
    def rows_t2(t2):
        return pl.ds(t2, CONV_T1, stride=FFT_N2)

    def stage_a(t2, carry):
        w = wf_ref[t2]
        p = _mm(w, x_ref.at[0][rows_t2(t2), :])
        q = _mm(w, x_ref.at[1][rows_t2(t2), :])
        dst = pl.ds(pl.multiple_of(t2 * FFT_N1, FFT_N1), FFT_N1)
        sr_ref[dst, :] = p[:FFT_N1] - q[FFT_N1:]
        si_ref[dst, :] = p[FFT_N1:] + q[:FFT_N1]
        return carry

    lax.fori_loop(0, FFT_N2, stage_a, 0, unroll=FFT_UNROLL)
    wm = wm_ref[...]

    def per_k1(k1, carry):
        sel = pl.ds(k1, FFT_N2, stride=FFT_N1)
        p = _mm(wm, sr_ref[sel, :])
        q = _mm(wm, si_ref[sel, :])
        xr = p[:FFT_N2] - q[FFT_N2:]
        xi = p[FFT_N2:] + q[:FFT_N2]
        hsel = pl.ds(pl.multiple_of(k1 * FFT_N2, FFT_N2), FFT_N2)
        hr = hr_ref[hsel, :]
        hi = hi_ref[hsel, :]
        p = _mm(wm, xr * hr - xi * hi)
        q = _mm(wm, xr * hi + xi * hr)
        sr_ref[sel, :] = p[:FFT_N2] + q[FFT_N2:]
        si_ref[sel, :] = q[:FFT_N2] - p[FFT_N2:]
        return carry

    lax.fori_loop(0, FFT_N1, per_k1, 0, unroll=FFT_UNROLL)
    bias = b_ref[...]

    def stage_c(t2, carry):
        w = wi_ref[t2]
        src = pl.ds(pl.multiple_of(t2 * FFT_N1, FFT_N1), FFT_N1)
        p = _mm(w, sr_ref[src, :])
        q = _mm(w, si_ref[src, :])
        rows = rows_t2(t2)
        o_ref.at[0][rows, :] = g_ref.at[0][rows, :] * (p[:CONV_T1] - q[CONV_T1:] + x_ref.at[0][rows, :] * bias)
        o_ref.at[1][rows, :] = g_ref.at[1][rows, :] * (q[:CONV_T1] + p[CONV_T1:] + x_ref.at[1][rows, :] * bias)
        return carry

    lax.fori_loop(0, FFT_N2, stage_c, 0, unroll=FFT_UNROLL)


def _spectrum_layout(hf_full):
    c = hf_full.shape[1]
    return hf_full.reshape(FFT_N2, FFT_N1, c).transpose(1, 0, 2).reshape(FFT_N, c)


def long_conv_gated(u, u_col, gate, gate_col, hf, bias, tables):
    B, L, _ = u.shape
    C = hf.shape[1]
    assert L == CONV_L and B % 2 == 0 and C % VREG_LANES == 0
    assert u_col % VREG_LANES == 0 and gate_col % VREG_LANES == 0
    h_re, h_im = _spectrum_layout(jnp.real(hf)), _spectrum_layout(jnp.imag(hf))
    consts = [t.astype(jnp.bfloat16) for t in tables]
    sig_spec = pl.BlockSpec((2, L, VREG_LANES), lambda c, b: (b, 0, c))
    u_spec = pl.BlockSpec((2, L, VREG_LANES), lambda c, b: (b, 0, c + u_col // VREG_LANES))
    gate_spec = pl.BlockSpec((2, L, VREG_LANES), lambda c, b: (b, 0, c + gate_col // VREG_LANES))
    once = pl.Buffered(1)
    spec_h = pl.BlockSpec((FFT_N, VREG_LANES), lambda c, b: (0, c), pipeline_mode=once)
    const_specs = [pl.BlockSpec(t.shape, functools.partial(lambda nd, c, b: (0,) * nd, t.ndim), pipeline_mode=once)
                   for t in consts]
    return pl.pallas_call(
        _long_conv_kernel,
        grid=(C // VREG_LANES, B // 2),
        in_specs=[u_spec, gate_spec, spec_h, spec_h, pl.BlockSpec((1, VREG_LANES), lambda c, b: (0, c))] + const_specs,
        out_specs=sig_spec,
        out_shape=jax.ShapeDtypeStruct((B, L, C), jnp.float32),
        scratch_shapes=[pltpu.VMEM((FFT_N, VREG_LANES), jnp.float32), pltpu.VMEM((FFT_N, VREG_LANES), jnp.float32)],
        compiler_params=pltpu.CompilerParams(dimension_semantics=("arbitrary", "arbitrary"),
                                             vmem_limit_bytes=FFT_VMEM_LIMIT),
        name="long_conv",
    )(u, gate, h_re, h_im, bias.reshape(1, C), *consts)


def hyena_layer(xt, mods, n_batch, w_in, conv_w, conv_b, w1, b1, w2, b2, w3, freq, fbias, w_out, ln_g, ln_b):
    sh1, sc1, g1 = mods
    T = xt.shape[0]
    L = T // n_batch
    hf = hyena_filters(L, w1, b1, w2, b2, w3, freq)
    u = mod_matmul_conv(xt, sc1, sh1, w_in, conv_w[None], conv_b, n_batch, 1, L, False).reshape(n_batch, L, 3 * D_MODEL)
    tables = _dft_tables()
    zz = long_conv_gated(u, 2 * D_MODEL, u, 0, hf[:, 0], fbias[0], tables)
    zz = long_conv_gated(zz, 0, u, D_MODEL, hf[:, 1], fbias[1], tables)
    return proj_ln(zz.reshape(T, D_MODEL), w_out, xt, g1, ln_g, ln_b)


D_TILES = D_MODEL // VREG_LANES
PEER_SLOTS = PEER_HEADS * PEER_TOPK
PEER_TB = 64
PEER_VMEM_LIMIT = 52 * 1024 * 1024
HI_MASK = -65536


def _pack_rows_bf16(tab):
    n, d = tab.shape
    bits = lax.bitcast_convert_type(tab.astype(jnp.bfloat16), jnp.uint16).astype(jnp.uint32)
    packed = (bits[n // 2:] << 16) | bits[:n // 2]
    return lax.bitcast_convert_type(packed, jnp.int32).reshape(n // 2 * (d // VREG_LANES), VREG_LANES)


def _unpack_row(tab_ref, off, shift):
    word = tab_ref[pl.ds(pl.multiple_of(off, VREG_SUBLANES), VREG_SUBLANES), :]
    return pltpu.bitcast((word << shift) & HI_MASK, jnp.float32)


def _sublane_sums(tiles, sub):
    for h in (1, 2, 4):
        keep = (sub & h) == 0
        nxt = []
        for a, b in zip(tiles[0::2], tiles[1::2]):
            nxt.append(jnp.where(keep, a, b) + pltpu.roll(jnp.where(keep, b, a), h, 0))
        tiles = nxt
    return tiles[0]


def _splat_tile(row):
    return jnp.broadcast_to(row, (PEER_SLOTS, VREG_LANES)).T


def _splat_small_ints_tile(row, eye, ones):
    diag = jnp.where(eye, row.astype(jnp.float32), 0.0).astype(jnp.bfloat16)
    return jnp.dot(diag, ones, preferred_element_type=jnp.float32).astype(jnp.int32)


def _splat_consts():
    eye = (lax.broadcasted_iota(jnp.int32, (PEER_SLOTS, PEER_SLOTS), 0)
           == lax.broadcasted_iota(jnp.int32, (PEER_SLOTS, PEER_SLOTS), 1))
    return eye, jnp.ones((PEER_SLOTS, VREG_LANES), jnp.bfloat16)


def _splat_block(shift_ref, shift_s, w_ref=None, w_s=None):
    eye, ones = _splat_consts()

    def token(t, carry):
        dst = pl.ds(pl.multiple_of(t * PEER_SLOTS, PEER_SLOTS), PEER_SLOTS)
        shift_s[dst, :] = _splat_small_ints_tile(shift_ref[pl.ds(t, 1), :], eye, ones)
        if w_ref is not None:
            w_s[dst, :] = _splat_tile(w_ref[pl.ds(t, 1), :])
        return carry
    lax.fori_loop(0, PEER_TB, token, 0, unroll=8)


def _tile_rows(k):
    return pl.ds(k, PEER_TB, stride=D_TILES)


def _peer_act_kernel(row_ref, shift_ref, x_ref, sc_ref, sh_ref, gate_ref, tab_ref, w_ref, part_ref, shift_s, xt_ref):
    sub = lax.broadcasted_iota(jnp.int32, (VREG_SUBLANES, VREG_LANES), 0)
    _splat_block(shift_ref, shift_s)
    h = x_ref[...] * (1.0 + sc_ref[0]) + sh_ref[0]
    for k in range(D_TILES):
        xt_ref[_tile_rows(k), :] = h[:, k * VREG_LANES:(k + 1) * VREG_LANES]

    def token(t, carry):
        xt = xt_ref[pl.ds(pl.multiple_of(t * D_TILES, D_TILES), D_TILES), :]
        base = pl.multiple_of(t * PEER_SLOTS, PEER_SLOTS)
        tok_rows = row_ref.at[0, 0, pl.ds(base, PEER_SLOTS)]
        for g in range(PEER_SLOTS // VREG_SUBLANES):
            prods = []
            for k in range(VREG_SUBLANES):
                s = g * VREG_SUBLANES + k
                prods.append(_unpack_row(tab_ref, tok_rows[s], shift_s[pl.ds(base + s, 1), :]) * xt)
            part_ref[pl.ds(base + g * VREG_SUBLANES, VREG_SUBLANES), :] = _sublane_sums(prods, sub)
        return carry

    lax.fori_loop(0, PEER_TB, token, 0)
    part = part_ref[...]
    part_hi = part.astype(jnp.bfloat16)
    part_lo = (part - part_hi.astype(jnp.float32)).astype(jnp.bfloat16)
    contract_lanes = (((1,), (1,)), ((), ()))
    ones = jnp.ones((VREG_SUBLANES, VREG_LANES), jnp.bfloat16)
    act = (lax.dot_general(ones, part_hi, contract_lanes, preferred_element_type=jnp.float32)
           + lax.dot_general(ones, part_lo, contract_lanes, preferred_element_type=jnp.float32))[0:1]
    gelu = 0.5 * act * (1.0 + lax.erf(act * (2.0 ** -0.5)))
    w_ref[0] = gate_ref[0] * gelu


def _peer_out_kernel(row_ref, shift_ref, w_ref, tab_ref, x_ref, g_ref, lng_ref, lnb_ref, o_ref,
                     shift_s, w_s, yt_ref):
    n_acc = 2
    _splat_block(shift_ref, shift_s, w_ref, w_s)

    def pair(i, carry):
        bases = [pl.multiple_of((2 * i + k) * PEER_SLOTS, PEER_SLOTS) for k in range(2)]
        rows = [row_ref.at[0, 0, pl.ds(b_, PEER_SLOTS)] for b_ in bases]
        accs = [[jnp.zeros((VREG_SUBLANES, VREG_LANES), jnp.float32) for _ in range(n_acc)] for _ in range(2)]
        for s in range(PEER_SLOTS):
            for k in range(2):
                val = _unpack_row(tab_ref, rows[k][s], shift_s[pl.ds(bases[k] + s, 1), :])
                accs[k][s % n_acc] = accs[k][s % n_acc] + w_s[pl.ds(bases[k] + s, 1), :] * val
        for k in range(2):
            yt_ref[pl.ds(pl.multiple_of((2 * i + k) * D_TILES, D_TILES), D_TILES), :] = accs[k][0] + accs[k][1]
        return carry

    lax.fori_loop(0, PEER_TB // 2, pair, 0)
    y = jnp.concatenate([yt_ref[_tile_rows(k), :] for k in range(D_TILES)], axis=1)
    o_ref[...] = _layer_norm_rows(DEEPNORM_ALPHA * x_ref[...] + g_ref[0] * y, lng_ref[...], lnb_ref[...])


def peer_experts_ln(x, mods, offs, shifts, gate, u_tab, v_tab, ln_g, ln_b):
    sh, sc, g = mods
    T = x.shape[0]
    nblk = T // PEER_TB
    per_blk = PEER_TB * PEER_SLOTS
    per_mod = T // sc.shape[0] // PEER_TB
    offs = offs.reshape(nblk, 1, per_blk)
    smem_spec = pl.BlockSpec((1, 1, per_blk), lambda i: (i, 0, 0), memory_space=pltpu.SMEM)
    lane_spec = pl.BlockSpec((1, 1, per_blk), lambda i: (i, 0, 0))
    slot_spec = pl.BlockSpec((PEER_TB, PEER_SLOTS), lambda i: (i, 0))
    tok_spec = pl.BlockSpec((PEER_TB, D_MODEL), lambda i: (i, 0))
    mod_spec = pl.BlockSpec((1, 1, D_MODEL), lambda i: (i // per_mod, 0, 0))
    vec_spec = pl.BlockSpec((1, D_MODEL), lambda i: (0, 0))
    tab_spec = pl.BlockSpec(memory_space=pltpu.VMEM)
    params = pltpu.CompilerParams(dimension_semantics=("arbitrary",), vmem_limit_bytes=PEER_VMEM_LIMIT)
    splat_i32 = pltpu.VMEM((per_blk, VREG_LANES), jnp.int32)
    splat_f32 = pltpu.VMEM((per_blk, VREG_LANES), jnp.float32)
    tok_tiles = pltpu.VMEM((PEER_TB * D_TILES, VREG_LANES), jnp.float32)
    w = pl.pallas_call(
        _peer_act_kernel,
        grid=(nblk,),
        in_specs=[smem_spec, slot_spec, tok_spec, mod_spec, mod_spec, lane_spec, tab_spec],
        out_specs=lane_spec,
        out_shape=jax.ShapeDtypeStruct((nblk, 1, per_blk), jnp.float32),
        scratch_shapes=[splat_f32, splat_i32, tok_tiles],
        compiler_params=params,
        name="peer_act",
    )(offs, shifts, x, sc, sh, gate.reshape(nblk, 1, per_blk), _pack_rows_bf16(u_tab))
    return pl.pallas_call(
        _peer_out_kernel,
        grid=(nblk,),
        in_specs=[smem_spec, slot_spec, slot_spec, tab_spec, tok_spec, mod_spec, vec_spec, vec_spec],
        out_specs=tok_spec,
        out_shape=jax.ShapeDtypeStruct((T, D_MODEL), jnp.float32),
        scratch_shapes=[splat_i32, splat_f32, tok_tiles],
        compiler_params=params,
        name="peer_out",
    )(offs, shifts, w.reshape(T, PEER_SLOTS), _pack_rows_bf16(v_tab), x, g,
      ln_g.reshape(1, D_MODEL), ln_b.reshape(1, D_MODEL))


def _top_rows(s, k):
    n = s.shape[0]
    row = lax.broadcasted_iota(jnp.int32, s.shape, 0).astype(jnp.float32)
    vals, idxs = [], []
    for _ in range(k):
        m = jnp.max(s, axis=0, keepdims=True)
        i = jnp.min(jnp.where(s == m, row, float(n)), axis=0, keepdims=True)
        vals.append(m)
        idxs.append(i)
        s = jnp.where(row == i, -jnp.inf, s)
    return jnp.concatenate(vals, axis=0), jnp.concatenate(idxs, axis=0).astype(jnp.int32)


def _pick_rows(table, which):
    out = jnp.zeros(which.shape, table.dtype)
    for a in range(table.shape[0]):
        out = jnp.where(which == a, table[a:a + 1, :], out)
    return out


PAIR_COUNTS = tuple(PEER_TOPK // (a + 1) for a in range(PEER_TOPK))
PAIR_STARTS = tuple(sum(PAIR_COUNTS[:a]) for a in range(PEER_TOPK))
N_PAIRS = sum(PAIR_COUNTS)
N_PAIRS_PADDED = -(-N_PAIRS // VREG_SUBLANES) * VREG_SUBLANES
SEL_TB = 512
PEER_PACKED_ROWS = PEER_NKEYS * PEER_NKEYS // 2


def _peer_select_kernel(x_ref, sc_ref, sh_ref, wq_ref, k1_ref, k2_ref, off_ref, shift_ref, gate_ref,
                        off_s, shift_s, gate_s):
    half = PEER_DK // 2
    h = (x_ref[...] * (1.0 + sc_ref[0]) + sh_ref[0]).astype(jnp.bfloat16)
    q = jnp.dot(h, wq_ref[...], preferred_element_type=jnp.float32).astype(jnp.bfloat16)
    contract_last = (((1,), (1,)), ((), ()))
    s1 = lax.dot_general(k1_ref[0], q[:, :half], contract_last, preferred_element_type=jnp.float32)
    s2 = lax.dot_general(k2_ref[0], q[:, half:], contract_last, preferred_element_type=jnp.float32)
    v1, i1 = _top_rows(s1, PEER_TOPK)
    v2, i2 = _top_rows(s2, PEER_TOPK)
    pairs = [v1[a:a + 1, :] + v2[:PAIR_COUNTS[a], :] for a in range(PEER_TOPK)]
    pairs.append(jnp.full((N_PAIRS_PADDED - N_PAIRS, v1.shape[1]), -jnp.inf, jnp.float32))
    top_s, pos = _top_rows(jnp.concatenate(pairs, axis=0), PEER_TOPK)
    a_sel = jnp.zeros(pos.shape, jnp.int32)
    start = jnp.zeros(pos.shape, jnp.int32)
    for a in range(1, PEER_TOPK):
        later = pos >= PAIR_STARTS[a]
        a_sel = jnp.where(later, a, a_sel)
        start = jnp.where(later, PAIR_STARTS[a], start)
    expert = _pick_rows(i1, a_sel) * PEER_NKEYS + _pick_rows(i2, pos - start)
    e = jnp.exp(top_s - top_s[0:1, :])
    head = pl.program_id(1)
    rows = pl.ds(pl.multiple_of(head * PEER_TOPK, PEER_TOPK), PEER_TOPK)
    gate_s[rows, :] = e / jnp.sum(e, axis=0, keepdims=True)
    off_s[rows, :] = (expert & (PEER_PACKED_ROWS - 1)) * VREG_SUBLANES
    shift_s[rows, :] = jnp.where(expert >= PEER_PACKED_ROWS, 0, 16)

    @pl.when(head == PEER_HEADS - 1)
    def _():
        gate_ref[...] = gate_s[...].T
        off_ref[...] = off_s[...].astype(jnp.float32).T.astype(jnp.int32)
        shift_ref[...] = shift_s[...].astype(jnp.float32).T.astype(jnp.int32)


def peer_select(x, sc, sh, wq, k1, k2):
    T = x.shape[0]
    per_seq = T // sc.shape[0] // SEL_TB
    bf16 = jnp.bfloat16
    tok_spec = pl.BlockSpec((SEL_TB, D_MODEL), lambda i, h: (i, 0))
    mod_spec = pl.BlockSpec((1, 1, D_MODEL), lambda i, h: (i // per_seq, 0, 0))
    wq_spec = pl.BlockSpec((D_MODEL, PEER_DK), lambda i, h: (0, h))
    key_spec = pl.BlockSpec((1, PEER_NKEYS, PEER_DK // 2), lambda i, h: (h, 0, 0))
    out_spec = pl.BlockSpec((SEL_TB, PEER_SLOTS), lambda i, h: (i, 0))
    return pl.pallas_call(
        _peer_select_kernel,
        grid=(T // SEL_TB, PEER_HEADS),
        in_specs=[tok_spec, mod_spec, mod_spec, wq_spec, key_spec, key_spec],
        out_specs=[out_spec, out_spec, out_spec],
        out_shape=[jax.ShapeDtypeStruct((T, PEER_SLOTS), jnp.int32), jax.ShapeDtypeStruct((T, PEER_SLOTS), jnp.int32),
                   jax.ShapeDtypeStruct((T, PEER_SLOTS), jnp.float32)],
        scratch_shapes=[pltpu.VMEM((PEER_SLOTS, SEL_TB), jnp.int32), pltpu.VMEM((PEER_SLOTS, SEL_TB), jnp.int32),
                        pltpu.VMEM((PEER_SLOTS, SEL_TB), jnp.float32)],
        compiler_params=pltpu.CompilerParams(dimension_semantics=("arbitrary", "arbitrary")),
        name="peer_select",
    )(x, sc, sh, wq.astype(bf16), k1.astype(bf16), k2.astype(bf16))


def peer_layer(xt, mods, wq, k1, k2, u_tab, v_tab, ln_g, ln_b):
    sh2, sc2, _ = mods
    off, shift, gate = peer_select(xt, sc2, sh2, wq, k1, k2)
    return peer_experts_ln(xt, mods, off, shift, gate, u_tab, v_tab, ln_g, ln_b)


def kernel(x, c, ctx, c_ctx, ada_w, ada_b, ln_g, ln_b,
           m_w_in, m_conv_w, m_conv_b, m_a_log, m_dt_bias, m_d, m_norm_w, m_w_out,
           h_w_in, h_conv_w, h_conv_b, h_f_w1, h_f_b1, h_f_w2, h_f_b2, h_f_w3, h_freq, h_bias, h_w_out,
           p_wq, p_k1, p_k2, p_u, p_v):
    n_batch, seq, _ = x.shape
    rows = seq // GRID_W
    s_lat = jax.nn.silu(c)
    s_ctx = jax.nn.silu(c_ctx)
    xt = x.reshape(n_batch * seq, D_MODEL)
    ctx_t = ctx.reshape(-1, D_MODEL)
    for i in range(DEPTH):
        kind, k = i % N_MIXERS, i // N_MIXERS
        sh1, sc1, g1, sh2, sc2, g2 = jnp.split((s_lat @ ada_w[i] + ada_b[i])[:, None, :], 6, axis=-1)
        if kind == 0:
            csh1, csc1 = jnp.split((s_ctx @ ada_w[i] + ada_b[i])[None, None, :], 6, axis=-1)[:2]
            xt = mamba_layer(xt, ctx_t, (sh1, sc1, g1), (csh1, csc1), n_batch, rows, m_w_in[k], m_conv_w[k],
                             m_conv_b[k], m_a_log[k], m_dt_bias[k], m_d[k], m_norm_w[k], m_w_out[k],
                             ln_g[i, 0], ln_b[i, 0])
        else:
            xt = hyena_layer(xt, (sh1, sc1, g1), n_batch, h_w_in[k], h_conv_w[k], h_conv_b[k], h_f_w1[k], h_f_b1[k],
                             h_f_w2[k], h_f_b2[k], h_f_w3[k], h_freq[k], h_bias[k], h_w_out[k], ln_g[i, 0], ln_b[i, 0])
        xt = peer_layer(xt, (sh2, sc2, g2), p_wq[i], p_k1[i], p_k2[i], p_u[i], p_v[i], ln_g[i, 1], ln_b[i, 1])
    return xt.reshape(x.shape)
```

```python
import functools
import math

import jax
import jax.numpy as jnp
from jax import lax
from jax.experimental import pallas as pl
from jax.experimental.pallas import tpu as pltpu

D_MODEL = 1024
DEPTH = 2
GRID_W = 64
N_MIXERS = 2
DEEPNORM_ALPHA = (2 * DEPTH) ** 0.25
LN_EPS = 1e-5
RMS_EPS = 1e-5

SSM_INNER = 2 * D_MODEL
SSM_HEAD_DIM = 64
SSM_HEADS = SSM_INNER // SSM_HEAD_DIM
SSM_GROUPS = 4
SSM_STATE = 128
SSM_CHUNK = 128
SSM_CONV = 3
SSM_CONV_DIM = SSM_INNER + 2 * SSM_GROUPS * SSM_STATE
SSM_GROUP_HEADS = SSM_HEADS // SSM_GROUPS
SSM_GROUP_DIM = SSM_GROUP_HEADS * SSM_HEAD_DIM

HYENA_ORDER = 2
HYENA_EMB = 33
HYENA_INNER_MLPS = 2
HYENA_STEEP_DECAY = 0.3
HYENA_GENTLE_DECAY = 1.5
HYENA_TARGET = 1e-2
HYENA_NORM_EPS = 1e-6

PEER_HEADS = 8
PEER_NKEYS = 128
PEER_DK = 256
PEER_TOPK = 16

VREG_SUBLANES = 8
VREG_LANES = 128
DENSE_VMEM_LIMIT = 48 * 1024 * 1024


def _mod_matmul_kernel(x_ref, sc_ref, sh_ref, w_ref, o_ref):
    h = (x_ref[...] * (1.0 + sc_ref[0]) + sh_ref[0]).astype(jnp.bfloat16)
    o_ref[...] = jnp.dot(h, w_ref[...], preferred_element_type=jnp.float32)


def mod_matmul(x, sc, sh, w, tb, tn):
    T, D = x.shape
    N = w.shape[1]
    assert T % tb == 0 and N % tn == 0 and (T // sc.shape[0]) % tb == 0
    per_mod = T // sc.shape[0] // tb
    mod_spec = pl.BlockSpec((1, 1, D), lambda i, j: (i // per_mod, 0, 0))
    return pl.pallas_call(
        _mod_matmul_kernel,
        grid=(T // tb, N // tn),
        in_specs=[pl.BlockSpec((tb, D), lambda i, j: (i, 0)), mod_spec, mod_spec,
                  pl.BlockSpec((D, tn), lambda i, j: (0, j))],
        out_specs=pl.BlockSpec((tb, tn), lambda i, j: (i, j)),
        out_shape=jax.ShapeDtypeStruct((T, N), jnp.float32),
        compiler_params=pltpu.CompilerParams(dimension_semantics=("arbitrary", "arbitrary"),
                                             vmem_limit_bytes=DENSE_VMEM_LIMIT),
        name="mod_matmul",
    )(x, sc, sh, w.astype(jnp.bfloat16))


def _layer_norm_rows(v, g, b):
    mu = jnp.mean(v, axis=-1, keepdims=True)
    vc = v - mu
    var = jnp.mean(vc * vc, axis=-1, keepdims=True)
    return vc * lax.rsqrt(var + LN_EPS) * g + b


def _proj_ln_kernel(a_ref, w_ref, x_ref, gate_ref, g_ref, b_ref, o_ref):
    y = jnp.dot(a_ref[...].astype(jnp.bfloat16), w_ref[...], preferred_element_type=jnp.float32)
    o_ref[...] = _layer_norm_rows(DEEPNORM_ALPHA * x_ref[...] + gate_ref[0] * y, g_ref[...], b_ref[...])


def _row_specs(tb, per_mod):
    tok = pl.BlockSpec((tb, D_MODEL), lambda i: (i, 0))
    mod = pl.BlockSpec((1, 1, D_MODEL), lambda i: (i // per_mod, 0, 0))
    vec = pl.BlockSpec((1, D_MODEL), lambda i: (0, 0))
    return tok, mod, vec


def proj_ln(a, w, x, gate, ln_g, ln_b, tb=512):
    T, K = a.shape
    per_mod = T // gate.shape[0] // tb
    tok, mod, vec = _row_specs(tb, per_mod)
    return pl.pallas_call(
        _proj_ln_kernel,
        grid=(T // tb,),
        in_specs=[pl.BlockSpec((tb, K), lambda i: (i, 0)), pl.BlockSpec((K, D_MODEL), lambda i: (0, 0)),
                  tok, mod, vec, vec],
        out_specs=tok,
        out_shape=jax.ShapeDtypeStruct((T, D_MODEL), jnp.float32),
        compiler_params=pltpu.CompilerParams(dimension_semantics=("arbitrary",), vmem_limit_bytes=DENSE_VMEM_LIMIT),
        name="proj_ln",
    )(a, w.astype(jnp.bfloat16), x, gate, ln_g.reshape(1, D_MODEL), ln_b.reshape(1, D_MODEL))


def _residual_ln_kernel(y_ref, x_ref, gate_ref, g_ref, b_ref, o_ref):
    o_ref[...] = _layer_norm_rows(DEEPNORM_ALPHA * x_ref[...] + gate_ref[0] * y_ref[...], g_ref[...], b_ref[...])


def residual_ln(y, x, gate, ln_g, ln_b, tb=512):
    T = x.shape[0]
    tok, mod, vec = _row_specs(tb, T // gate.shape[0] // tb)
    return pl.pallas_call(
        _residual_ln_kernel,
        grid=(T // tb,),
        in_specs=[tok, tok, mod, vec, vec],
        out_specs=tok,
        out_shape=jax.ShapeDtypeStruct((T, D_MODEL), jnp.float32),
        compiler_params=pltpu.CompilerParams(dimension_semantics=("arbitrary",)),
        name="residual_ln",
    )(y, x, gate, ln_g.reshape(1, D_MODEL), ln_b.reshape(1, D_MODEL))


def _mamba_out_kernel(y_ref, xs_ref, z_ref, dsk_ref, nw_ref, w_ref, x_ref, gate_ref, g_ref, b_ref, o_ref):
    z = z_ref[...]
    u = (y_ref[0] + y_ref[1] + xs_ref[...] * dsk_ref[...]) * (z * jax.nn.sigmoid(z))
    gw = SSM_INNER // SSM_GROUPS
    parts = []
    for g in range(SSM_GROUPS):
        ug = u[:, g * gw:(g + 1) * gw]
        parts.append(ug * lax.rsqrt(jnp.mean(ug * ug, axis=-1, keepdims=True) + RMS_EPS))
    a = (jnp.concatenate(parts, axis=1) * nw_ref[...]).astype(jnp.bfloat16)
    y = jnp.dot(a, w_ref[...], preferred_element_type=jnp.float32)
    o_ref[...] = _layer_norm_rows(DEEPNORM_ALPHA * x_ref[...] + gate_ref[0] * y, g_ref[...], b_ref[...])


def mamba_out(y2, xbc, z, d_skip, norm_w, w_out, x, gate, ln_g, ln_b, tb=256):
    T = x.shape[0]
    tok, mod, vec = _row_specs(tb, T // gate.shape[0] // tb)
    inner = pl.BlockSpec((tb, SSM_INNER), lambda i: (i, 0))
    ivec = pl.BlockSpec((1, SSM_INNER), lambda i: (0, 0))
    dsk = jnp.repeat(d_skip, SSM_INNER // d_skip.shape[0]).reshape(1, SSM_INNER)
    return pl.pallas_call(
        _mamba_out_kernel,
        grid=(T // tb,),
        in_specs=[pl.BlockSpec((2, tb, SSM_INNER), lambda i: (0, i, 0)), inner, inner, ivec, ivec,
                  pl.BlockSpec((SSM_INNER, D_MODEL), lambda i: (0, 0)), tok, mod, vec, vec],
        out_specs=tok,
        out_shape=jax.ShapeDtypeStruct((T, D_MODEL), jnp.float32),
        compiler_params=pltpu.CompilerParams(dimension_semantics=("arbitrary",), vmem_limit_bytes=DENSE_VMEM_LIMIT),
        name="mamba_out",
    )(y2, xbc, z, dsk, norm_w.reshape(1, SSM_INNER), w_out.astype(jnp.bfloat16), x, gate,
      ln_g.reshape(1, D_MODEL), ln_b.reshape(1, D_MODEL))


CONV_VMEM_LIMIT = 56 * 1024 * 1024


def _shift_rows(a, k):
    n = a.shape[0]
    return a if k == 0 else pltpu.roll(a, (-k) % n, 0)


def _mod_matmul_conv_kernel(x_ref, sc_ref, sh_ref, w_ref, cw_ref, cb_ref, o_ref, *, rows, cols, silu):
    h = (x_ref[...] * (1.0 + sc_ref[0]) + sh_ref[0]).astype(jnp.bfloat16)
    p = jnp.dot(h, w_ref[...], preferred_element_type=jnp.float32)
    L = p.shape[0]
    t = lax.broadcasted_iota(jnp.int32, p.shape, 0)
    col = t & (cols - 1)
    n_dr = cw_ref.shape[0]
    out = jnp.zeros_like(p)
    for j in range(3):
        dc = j - 1
        acc = None
        for i in range(n_dr):
            dr = i - n_dr // 2
            term = _shift_rows(p, dr * cols)
            if dr < 0:
                term = jnp.where(t >= -dr * cols, term, 0.0)
            elif dr > 0:
                term = jnp.where(t < L - dr * cols, term, 0.0)
            term = term * cw_ref[i, j:j + 1, :]
            acc = term if acc is None else acc + term
        acc = _shift_rows(acc, dc)
        if dc < 0:
            acc = jnp.where(col >= -dc, acc, 0.0)
        elif dc > 0:
            acc = jnp.where(col < cols - dc, acc, 0.0)
        out = out + acc
    out = out + cb_ref[...]
    o_ref[...] = out * jax.nn.sigmoid(out) if silu else out


def mod_matmul_conv(x, sc, sh, w, conv_w, conv_b, n_seq, rows, cols, silu, tn=256):
    T, D = x.shape
    N = w.shape[1]
    L = rows * cols
    assert T == n_seq * L and N % tn == 0 and cols & (cols - 1) == 0 and conv_w.shape[0] in (1, 3)
    per_mod = n_seq // sc.shape[0]
    mod_spec = pl.BlockSpec((1, 1, D), lambda b, j: (b // per_mod, 0, 0))
    return pl.pallas_call(
        functools.partial(_mod_matmul_conv_kernel, rows=rows, cols=cols, silu=silu),
        grid=(n_seq, N // tn),
        in_specs=[pl.BlockSpec((L, D), lambda b, j: (b, 0), pipeline_mode=pl.Buffered(1)), mod_spec, mod_spec,
                  pl.BlockSpec((D, tn), lambda b, j: (0, j)),
                  pl.BlockSpec((conv_w.shape[0], 3, tn), lambda b, j: (0, 0, j)),
                  pl.BlockSpec((1, tn), lambda b, j: (0, j))],
        out_specs=pl.BlockSpec((L, tn), lambda b, j: (b, j)),
        out_shape=jax.ShapeDtypeStruct((T, N), jnp.float32),
        compiler_params=pltpu.CompilerParams(dimension_semantics=("arbitrary", "arbitrary"),
                                             vmem_limit_bytes=CONV_VMEM_LIMIT),
        name="mod_matmul_conv",
    )(x, sc, sh, w.astype(jnp.bfloat16), conv_w, conv_b.reshape(1, N))


def _split3_bf16(a):
    f32, bf16 = jnp.float32, jnp.bfloat16
    p0 = a.astype(bf16)
    r = a - p0.astype(f32)
    p1 = r.astype(bf16)
    p2 = (r - p1.astype(f32)).astype(bf16)
    return p0, p1, p2


def _ssd_kernel(x_ref, b_ref, c_ref, dt_ref, an_ref, init_ref, *rest, want_y):
    if want_y:
        y_ref, fin_ref, state_ref = rest
    else:
        fin_ref, state_ref = rest
    f32, bf16 = jnp.float32, jnp.bfloat16
    Q, P, H, G, N = SSM_CHUNK, SSM_HEAD_DIM, SSM_HEADS, SSM_GROUPS, SSM_STATE
    R, GD = SSM_GROUP_HEADS, SSM_GROUP_DIM
    fwd = pl.program_id(1) == 0
    c = pl.program_id(2)

    @pl.when(c == 0)
    def _():
        state_ref[...] = init_ref[0, 0].reshape(SSM_INNER, N)

    contract_rows = (((0,), (0,)), ((), ()))
    contract_lanes = (((1,), (1,)), ((), ()))
    dt = dt_ref[0, 0]
    a = dt * an_ref[0]
    li = lax.broadcasted_iota(jnp.int32, (Q, Q), 0)
    si = lax.broadcasted_iota(jnp.int32, (Q, Q), 1)
    tri = jnp.where(fwd, li - si, si - li) >= 0
    tri_bf = jnp.where(tri, 1.0, 0.0).astype(bf16)
    pieces = _split3_bf16(a)
    a_cum = sum(jnp.dot(tri_bf, p, preferred_element_type=f32) for p in pieces)
    a_cum_t = sum(lax.dot_general(p, tri_bf, (((0,), (1,)), ((), ())), preferred_element_type=f32)
                  for p in pieces)
    total = jnp.sum(a, axis=0, keepdims=True)
    spread = (lax.broadcasted_iota(jnp.int32, (H, SSM_INNER), 1) // P
              == lax.broadcasted_iota(jnp.int32, (H, SSM_INNER), 0)).astype(bf16)

    def per_lane(v):
        return sum(jnp.dot(p, spread, preferred_element_type=f32) for p in _split3_bf16(v))

    xdt = x_ref[0] * per_lane(dt)
    xw = (xdt * per_lane(jnp.exp(total - a_cum))).astype(bf16)
    tot_col = jnp.where(fwd, a_cum_t[:, Q - 1:Q], a_cum_t[:, 0:1])
    keep = jnp.broadcast_to(jnp.exp(tot_col), (H, N))
    keep_rows = sum(lax.dot_general(spread, p, contract_rows, preferred_element_type=f32)
                    for p in _split3_bf16(keep))
    if want_y:
        xdt_bf = xdt.astype(bf16)
        out_scale = per_lane(jnp.exp(a_cum))
        lane = lax.broadcasted_iota(jnp.int32, (Q, 2 * P), 1)
    ys = []
    for g in range(G):
        cols = slice(g * GD, (g + 1) * GD)
        bm = b_ref[0, :, g * N:(g + 1) * N].astype(bf16)
        cm = c_ref[0, :, g * N:(g + 1) * N].astype(bf16)
        state = state_ref[cols, :]
        if want_y:
            cb = lax.dot_general(cm, bm, contract_lanes, preferred_element_type=f32)
            for j in range(R // 2):
                ms = []
                for h in (g * R + 2 * j, g * R + 2 * j + 1):
                    decay = jnp.where(tri, jnp.exp(a_cum[:, h:h + 1] - a_cum_t[h:h + 1, :]), 0.0)
                    ms.append((cb * decay).astype(bf16))
                xp = xdt_bf[:, g * GD + 2 * j * P:g * GD + 2 * (j + 1) * P]
                zero = jnp.zeros_like(xp)
                rhs = jnp.concatenate([jnp.where(lane < P, xp, zero), jnp.where(lane >= P, xp, zero)], axis=0)
                ys.append(jnp.dot(jnp.concatenate(ms, axis=1), rhs, preferred_element_type=f32))
            y_off = lax.dot_general(cm, state.astype(bf16), contract_lanes, preferred_element_type=f32)
            y_ref[0, 0, :, cols] = jnp.concatenate(ys[-(R // 2):], axis=1) + y_off * out_scale[:, cols]
        upd = lax.dot_general(xw[:, cols], bm, contract_rows, preferred_element_type=f32)
        state_ref[cols, :] = state * keep_rows[cols, :] + upd
    fin_ref[0, 0] = state_ref[...].reshape(H, P, N)


def ssd_scan_both(xbc, dt, a_neg, init, want_y):
    B, L, _ = xbc.shape
    Q, G, H, P, N = SSM_CHUNK, SSM_GROUPS, SSM_HEADS, SSM_HEAD_DIM, SSM_STATE
    nc = L // Q
    dt_d = dt.transpose(0, 2, 1, 3)
    an = a_neg.reshape(2, 1, H)
    bc_cols = G * N

    def chunk(d, c):
        return c + d * (nc - 1 - 2 * c)

    x_spec = pl.BlockSpec((1, Q, SSM_INNER), lambda b, d, c: (b, chunk(d, c), 0))
    b_spec = pl.BlockSpec((1, Q, bc_cols), lambda b, d, c: (b, chunk(d, c), SSM_INNER // bc_cols))
    c_spec = pl.BlockSpec((1, Q, bc_cols), lambda b, d, c: (b, chunk(d, c), SSM_INNER // bc_cols + 1))
    dt_spec = pl.BlockSpec((1, 1, Q, H), lambda b, d, c: (b, d, chunk(d, c), 0))
    an_spec = pl.BlockSpec((1, 1, H), lambda b, d, c: (d, 0, 0))
    st_spec = pl.BlockSpec((1, 1, H, P, N), lambda b, d, c: (b, d, 0, 0, 0))
    st_shape = jax.ShapeDtypeStruct((B, 2, H, P, N), jnp.float32)
    out_specs, out_shape = [st_spec], [st_shape]
    if want_y:
        out_specs = [pl.BlockSpec((1, 1, Q, SSM_INNER), lambda b, d, c: (d, b, chunk(d, c), 0))] + out_specs
        out_shape = [jax.ShapeDtypeStruct((2, B, L, SSM_INNER), jnp.float32)] + out_shape
    outs = pl.pallas_call(
        functools.partial(_ssd_kernel, want_y=want_y),
        grid=(B, 2, nc),
        in_specs=[x_spec, b_spec, c_spec, dt_spec, an_spec, st_spec],
        out_specs=out_specs,
        out_shape=out_shape,
        scratch_shapes=[pltpu.VMEM((SSM_INNER, N), jnp.float32)],
        compiler_params=pltpu.CompilerParams(dimension_semantics=("arbitrary",) * 3),
        name="ssd_scan_y" if want_y else "ssd_scan_state",
    )(xbc, xbc, xbc, dt_d, an, init)
    if want_y:
        return outs[0], outs[1]
    return None, outs[0]


def mamba_layer(xt, ctx_t, mods, cmods, n_batch, rows, w_in, conv_w, conv_b, a_log, dt_bias, d_skip, norm_w, w_out,
                ln_g, ln_b):
    f32 = jnp.float32
    sh1, sc1, g1 = mods
    csh1, csc1 = cmods
    a_neg = -jnp.exp(a_log.astype(f32))
    w_xbc = w_in[:, SSM_INNER:SSM_INNER + SSM_CONV_DIM]
    w_dt = w_in[:, SSM_INNER + SSM_CONV_DIM:]

    def softplus_dt(dt_raw, length):
        return jax.nn.softplus(dt_raw.reshape(n_batch, length, 2, SSM_HEADS) + dt_bias.astype(f32))

    lc = ctx_t.shape[0] // n_batch
    xbc_c = mod_matmul_conv(ctx_t, csc1, csh1, w_xbc, conv_w[1:2], conv_b, n_batch, 1, lc, True)
    dt_c = softplus_dt(mod_matmul(ctx_t, csc1, csh1, w_dt, 512, 2 * SSM_HEADS), lc)
    zeros = jnp.zeros((n_batch, 2, SSM_HEADS, SSM_HEAD_DIM, SSM_STATE), f32)
    _, states = ssd_scan_both(xbc_c.reshape(n_batch, lc, SSM_CONV_DIM), dt_c, a_neg, zeros, False)
    T = xt.shape[0]
    L = T // n_batch
    z = mod_matmul(xt, sc1, sh1, w_in[:, :SSM_INNER], 1024, 1024)
    xbc = mod_matmul_conv(xt, sc1, sh1, w_xbc, conv_w, conv_b, n_batch, rows, GRID_W, True)
    dt = softplus_dt(mod_matmul(xt, sc1, sh1, w_dt, 1024, 2 * SSM_HEADS), L)
    y2, _ = ssd_scan_both(xbc.reshape(n_batch, L, SSM_CONV_DIM), dt, a_neg, states, True)
    return mamba_out(y2.reshape(2, T, SSM_INNER), xbc, z, d_skip, norm_w, w_out, xt, g1, ln_g, ln_b)


def hyena_filters(L, w1, b1, w2, b2, w3, freq):
    f32 = jnp.float32
    t = jnp.linspace(0.0, 1.0, L, dtype=f32)[:, None]
    bands = (HYENA_EMB - 1) // 2
    ang = 2.0 * math.pi * jnp.arange(L, dtype=f32)[:, None] / L
    fb = jnp.linspace(1e-4, bands - 1, bands, dtype=f32)[None, :]
    z = jnp.concatenate([t, jnp.cos(fb * ang), -jnp.sin(fb * ang)], axis=-1)
    fr = freq.astype(f32)
    hid = jnp.sin(fr * (z @ w1.astype(f32) + b1.astype(f32)))
    for j in range(HYENA_INNER_MLPS):
        hid = jnp.sin(fr * (hid @ w2[j].astype(f32) + b2[j].astype(f32)))
    h = (hid @ w3.astype(f32)).reshape(L, HYENA_ORDER, 2, D_MODEL)
    max_decay = math.log(HYENA_TARGET) / HYENA_STEEP_DECAY
    min_decay = math.log(HYENA_TARGET) / HYENA_GENTLE_DECAY
    deltas = jnp.linspace(min_decay, max_decay, D_MODEL, dtype=f32)
    h = h * jnp.exp(-t * jnp.abs(deltas))[:, None, None, :]
    h_fwd, h_bwd = h[:, :, 0], h[:, :, 1]
    taps_pos = h_fwd.at[0].add(h_bwd[0]).reshape(L, HYENA_ORDER * D_MODEL)
    taps_neg = h_bwd.at[0].set(0.0).reshape(L, HYENA_ORDER * D_MODEL)
    return taps_pos, taps_neg


FFT_N1 = 128
FFT_N2 = 64
FFT_N = FFT_N1 * FFT_N2
CONV_L = FFT_N // 2
CONV_T1 = CONV_L // FFT_N2
FFT_VMEM_LIMIT = 56 * 1024 * 1024
FFT_UNROLL = 4


def _dft_tables():
    def cos_sin(phase, n):
        ang = (2.0 * math.pi / n) * (phase % n).astype(jnp.float32)
        return jnp.cos(ang), jnp.sin(ang)
    t2 = jnp.arange(FFT_N2, dtype=jnp.int32)[:, None, None]
    k1 = jnp.arange(FFT_N1, dtype=jnp.int32)[None, :, None]
    t1 = jnp.arange(CONV_T1, dtype=jnp.int32)[None, None, :]
    c, s = cos_sin(FFT_N2 * t1 * k1 + t2 * k1, FFT_N)
    fwd = jnp.concatenate([c, -s], axis=1)
    inv = jnp.concatenate([c, s], axis=1).reshape(FFT_N2, 2, FFT_N1, CONV_T1)
    inv = inv.transpose(0, 1, 3, 2).reshape(FFT_N2, 2 * CONV_T1, FFT_N1) / FFT_N
    k2 = jnp.arange(FFT_N2, dtype=jnp.int32)
    c2, s2 = cos_sin(k2[:, None] * k2[None, :], FFT_N2)
    mid = jnp.concatenate([c2, -s2], axis=0)
    return fwd, mid, inv


def _mm(w, x):
    return jnp.dot(w, x.astype(jnp.bfloat16), preferred_element_type=jnp.float32)


def _long_conv_kernel(x_ref, g_ref, hr_ref, hi_ref, b_ref, wf_ref, wm_ref, wi_ref, o_ref, sr_ref, si_ref):
    def rows_t2(t2):
        return pl.ds(t2, CONV_T1, stride=FFT_N2)

    def stage_a(t2, carry):
        w = wf_ref[t2]
        p = _mm(w, x_ref.at[0][rows_t2(t2), :])
        q = _mm(w, x_ref.at[1][rows_t2(t2), :])
        dst = pl.ds(pl.multiple_of(t2 * FFT_N1, FFT_N1), FFT_N1)
        sr_ref[dst, :] = p[:FFT_N1] - q[FFT_N1:]
        si_ref[dst, :] = p[FFT_N1:] + q[:FFT_N1]
        return carry

    lax.fori_loop(0, FFT_N2, stage_a, 0, unroll=FFT_UNROLL)
    wm = wm_ref[...]

    def per_k1(k1, carry):
        sel = pl.ds(k1, FFT_N2, stride=FFT_N1)
        p = _mm(wm, sr_ref[sel, :])
        q = _mm(wm, si_ref[sel, :])
        xr = p[:FFT_N2] - q[FFT_N2:]
        xi = p[FFT_N2:] + q[:FFT_N2]
        hsel = pl.ds(pl.multiple_of(k1 * FFT_N2, FFT_N2), FFT_N2)
        hr = hr_ref[hsel, :]
        hi = hi_ref[hsel, :]
        p = _mm(wm, xr * hr - xi * hi)
        q = _mm(wm, xr * hi + xi * hr)
        sr_ref[sel, :] = p[:FFT_N2] + q[FFT_N2:]
        si_ref[sel, :] = q[:FFT_N2] - p[FFT_N2:]
        return carry

    lax.fori_loop(0, FFT_N1, per_k1, 0, unroll=FFT_UNROLL)
    bias = b_ref[...]

    def stage_c(t2, carry):
        w = wi_ref[t2]
        src = pl.ds(pl.multiple_of(t2 * FFT_N1, FFT_N1), FFT_N1)
        p = _mm(w, sr_ref[src, :])
        q = _mm(w, si_ref[src, :])
        rows = rows_t2(t2)
        o_ref.at[0][rows, :] = g_ref.at[0][rows, :] * (p[:CONV_T1] - q[CONV_T1:] + x_ref.at[0][rows, :] * bias)
        o_ref.at[1][rows, :] = g_ref.at[1][rows, :] * (q[:CONV_T1] + p[CONV_T1:] + x_ref.at[1][rows, :] * bias)
        return carry

    lax.fori_loop(0, FFT_N2, stage_c, 0, unroll=FFT_UNROLL)


def _split_bf16(w):
    hi = w.astype(jnp.bfloat16)
    return hi, (w - hi.astype(jnp.float32)).astype(jnp.bfloat16)


def _mm3(w_hi, w_lo, x):
    x_hi, x_lo = _split_bf16(x)
    f32 = jnp.float32
    return (jnp.dot(w_hi, x_hi, preferred_element_type=f32) + jnp.dot(w_lo, x_hi, preferred_element_type=f32)
            + jnp.dot(w_hi, x_lo, preferred_element_type=f32))


def _filter_spectrum_kernel(a_ref, b_ref, wfh_ref, wfl_ref, wmh_ref, wml_ref, hr_ref, hi_ref,
                            ar_ref, ai_ref, br_ref, bi_ref):
    def rows_t2(t2):
        return pl.ds(t2, CONV_T1, stride=FFT_N2)

    def stage_a(t2, carry):
        wh, wl = wfh_ref[t2], wfl_ref[t2]
        p = _mm3(wh, wl, a_ref[rows_t2(t2), :])
        q = _mm3(wh, wl, b_ref[rows_t2(t2), :])
        dst = pl.ds(pl.multiple_of(t2 * FFT_N1, FFT_N1), FFT_N1)
        ar_ref[dst, :] = p[:FFT_N1]
        ai_ref[dst, :] = p[FFT_N1:]
        br_ref[dst, :] = q[:FFT_N1]
        bi_ref[dst, :] = q[FFT_N1:]
        return carry

    lax.fori_loop(0, FFT_N2, stage_a, 0, unroll=FFT_UNROLL)
    wmh, wml = wmh_ref[...], wml_ref[...]
    a = a_ref[...]
    b = b_ref[...]
    energy = jnp.sum(a * a, axis=0, keepdims=True) + jnp.sum(b * b, axis=0, keepdims=True)
    scale = lax.rsqrt(energy + HYENA_NORM_EPS)

    def per_k1(k1, carry):
        sel = pl.ds(k1, FFT_N2, stride=FFT_N1)
        pa, qa = _mm3(wmh, wml, ar_ref[sel, :]), _mm3(wmh, wml, ai_ref[sel, :])
        pb, qb = _mm3(wmh, wml, br_ref[sel, :]), _mm3(wmh, wml, bi_ref[sel, :])
        dst = pl.ds(pl.multiple_of(k1 * FFT_N2, FFT_N2), FFT_N2)
        a_re, a_im = pa[:FFT_N2] - qa[FFT_N2:], pa[FFT_N2:] + qa[:FFT_N2]
        b_re, b_im = pb[:FFT_N2] - qb[FFT_N2:], pb[FFT_N2:] + qb[:FFT_N2]
        hr_ref[dst, :] = (a_re + b_re) * scale
        hi_ref[dst, :] = (a_im - b_im) * scale
        return carry

    lax.fori_loop(0, FFT_N1, per_k1, 0, unroll=FFT_UNROLL)


def filter_spectrum(a, b, tables):
    L, C = a.shape
    wf, wm, _ = tables
    sig = pl.BlockSpec((L, VREG_LANES), lambda c: (0, c))
    out = pl.BlockSpec((FFT_N, VREG_LANES), lambda c: (0, c))
    once = pl.Buffered(1)
    plane = pltpu.VMEM((FFT_N, VREG_LANES), jnp.float32)
    return pl.pallas_call(
        _filter_spectrum_kernel,
        grid=(C // VREG_LANES,),
        in_specs=[sig, sig] + [pl.BlockSpec(wf.shape, lambda c: (0, 0, 0), pipeline_mode=once)] * 2
        + [pl.BlockSpec(wm.shape, lambda c: (0, 0), pipeline_mode=once)] * 2,
        out_specs=[out, out],
        out_shape=[jax.ShapeDtypeStruct((FFT_N, C), jnp.float32)] * 2,
        scratch_shapes=[plane] * 4,
        compiler_params=pltpu.CompilerParams(dimension_semantics=("arbitrary",), vmem_limit_bytes=FFT_VMEM_LIMIT),
        name="filter_spectrum",
    )(a, b, *_split_bf16(wf), *_split_bf16(wm))


def long_conv_gated(u, u_col, gate, gate_col, h_re, h_im, h_col, bias, tables):
    B, L, _ = u.shape
    C = bias.shape[0]
    assert L == CONV_L and B % 2 == 0 and C % VREG_LANES == 0
    assert u_col % VREG_LANES == 0 and gate_col % VREG_LANES == 0 and h_col % VREG_LANES == 0
    consts = [t.astype(jnp.bfloat16) for t in tables]
    sig_spec = pl.BlockSpec((2, L, VREG_LANES), lambda c, b: (b, 0, c))
    u_spec = pl.BlockSpec((2, L, VREG_LANES), lambda c, b: (b, 0, c + u_col // VREG_LANES))
    gate_spec = pl.BlockSpec((2, L, VREG_LANES), lambda c, b: (b, 0, c + gate_col // VREG_LANES))
    once = pl.Buffered(1)
    spec_h = pl.BlockSpec((FFT_N, VREG_LANES), lambda c, b: (0, c + h_col // VREG_LANES), pipeline_mode=once)
    const_specs = [pl.BlockSpec(t.shape, functools.partial(lambda nd, c, b: (0,) * nd, t.ndim), pipeline_mode=once)
                   for t in consts]
    return pl.pallas_call(
        _long_conv_kernel,
        grid=(C // VREG_LANES, B // 2),
        in_specs=[u_spec, gate_spec, spec_h, spec_h, pl.BlockSpec((1, VREG_LANES), lambda c, b: (0, c))] + const_specs,
        out_specs=sig_spec,
        out_shape=jax.ShapeDtypeStruct((B, L, C), jnp.float32),
        scratch_shapes=[pltpu.VMEM((FFT_N, VREG_LANES), jnp.float32), pltpu.VMEM((FFT_N, VREG_LANES), jnp.float32)],
        compiler_params=pltpu.CompilerParams(dimension_semantics=("arbitrary", "arbitrary"),
                                             vmem_limit_bytes=FFT_VMEM_LIMIT),
        name="long_conv",
    )(u, gate, h_re, h_im, bias.reshape(1, C), *consts)


def hyena_layer(xt, mods, n_batch, w_in, conv_w, conv_b, w1, b1, w2, b2, w3, freq, fbias, w_out, ln_g, ln_b):
    sh1, sc1, g1 = mods
    T = xt.shape[0]
    L = T // n_batch
    tables = _dft_tables()
    h_re, h_im = filter_spectrum(*hyena_filters(L, w1, b1, w2, b2, w3, freq), tables)
    u = mod_matmul_conv(xt, sc1, sh1, w_in, conv_w[None], conv_b, n_batch, 1, L, False).reshape(n_batch, L, 3 * D_MODEL)
    zz = long_conv_gated(u, 2 * D_MODEL, u, 0, h_re, h_im, 0, fbias[0], tables)
    zz = long_conv_gated(zz, 0, u, D_MODEL, h_re, h_im, D_MODEL, fbias[1], tables)
    return proj_ln(zz.reshape(T, D_MODEL), w_out, xt, g1, ln_g, ln_b)


D_TILES = D_MODEL // VREG_LANES
PEER_SLOTS = PEER_HEADS * PEER_TOPK
PEER_TB = 64
PEER_VMEM_LIMIT = 52 * 1024 * 1024
HI_MASK = -65536


def _pack_rows_bf16(tab):
    n, d = tab.shape
    bits = lax.bitcast_convert_type(tab.astype(jnp.bfloat16), jnp.uint16).astype(jnp.uint32)
    packed = (bits[n // 2:] << 16) | bits[:n // 2]
    return lax.bitcast_convert_type(packed, jnp.int32).reshape(n // 2 * (d // VREG_LANES), VREG_LANES)


def _unpack_row(tab_ref, off, shift):
    word = tab_ref[pl.ds(pl.multiple_of(off, VREG_SUBLANES), VREG_SUBLANES), :]
    return pltpu.bitcast((word << shift) & HI_MASK, jnp.float32)


def _sublane_sums(tiles, sub):
    for h in (1, 2, 4):
        keep = (sub & h) == 0
        nxt = []
        for a, b in zip(tiles[0::2], tiles[1::2]):
            nxt.append(jnp.where(keep, a, b) + pltpu.roll(jnp.where(keep, b, a), h, 0))
        tiles = nxt
    return tiles[0]


def _splat_tile(row):
    return jnp.broadcast_to(row, (PEER_SLOTS, VREG_LANES)).T


def _splat_small_ints_tile(row, eye, ones):
    diag = jnp.where(eye, row.astype(jnp.float32), 0.0).astype(jnp.bfloat16)
    return jnp.dot(diag, ones, preferred_element_type=jnp.float32).astype(jnp.int32)


def _splat_consts():
    eye = (lax.broadcasted_iota(jnp.int32, (PEER_SLOTS, PEER_SLOTS), 0)
           == lax.broadcasted_iota(jnp.int32, (PEER_SLOTS, PEER_SLOTS), 1))
    return eye, jnp.ones((PEER_SLOTS, VREG_LANES), jnp.bfloat16)


def _splat_block(shift_ref, shift_s, w_ref=None, w_s=None):
    eye, ones = _splat_consts()

    def token(t, carry):
        dst = pl.ds(pl.multiple_of(t * PEER_SLOTS, PEER_SLOTS), PEER_SLOTS)
        shift_s[dst, :] = _splat_small_ints_tile(shift_ref[pl.ds(t, 1), :], eye, ones)
        if w_ref is not None:
            w_s[dst, :] = _splat_tile(w_ref[pl.ds(t, 1), :])
        return carry
    lax.fori_loop(0, PEER_TB, token, 0, unroll=8)


def _tile_rows(k):
    return pl.ds(k, PEER_TB, stride=D_TILES)


def _peer_act_kernel(row_ref, shift_ref, x_ref, sc_ref, sh_ref, gate_ref, tab_ref, w_ref, part_ref, shift_s, xt_ref):
    sub = lax.broadcasted_iota(jnp.int32, (VREG_SUBLANES, VREG_LANES), 0)
    _splat_block(shift_ref, shift_s)
    h = x_ref[...] * (1.0 + sc_ref[0]) + sh_ref[0]
    for k in range(D_TILES):
        xt_ref[_tile_rows(k), :] = h[:, k * VREG_LANES:(k + 1) * VREG_LANES]

    def token(t, carry):
        xt = xt_ref[pl.ds(pl.multiple_of(t * D_TILES, D_TILES), D_TILES), :]
        base = pl.multiple_of(t * PEER_SLOTS, PEER_SLOTS)
        tok_rows = row_ref.at[0, 0, pl.ds(base, PEER_SLOTS)]
        for g in range(PEER_SLOTS // VREG_SUBLANES):
            prods = []
            for k in range(VREG_SUBLANES):
                s = g * VREG_SUBLANES + k
                prods.append(_unpack_row(tab_ref, tok_rows[s], shift_s[pl.ds(base + s, 1), :]) * xt)
            part_ref[pl.ds(base + g * VREG_SUBLANES, VREG_SUBLANES), :] = _sublane_sums(prods, sub)
        return carry

    lax.fori_loop(0, PEER_TB, token, 0)
    part = part_ref[...]
    part_hi = part.astype(jnp.bfloat16)
    part_lo = (part - part_hi.astype(jnp.float32)).astype(jnp.bfloat16)
    contract_lanes = (((1,), (1,)), ((), ()))
    ones = jnp.ones((VREG_SUBLANES, VREG_LANES), jnp.bfloat16)
    act = (lax.dot_general(ones, part_hi, contract_lanes, preferred_element_type=jnp.float32)
           + lax.dot_general(ones, part_lo, contract_lanes, preferred_element_type=jnp.float32))[0:1]
    gelu = 0.5 * act * (1.0 + lax.erf(act * (2.0 ** -0.5)))
    w_ref[0] = gate_ref[0] * gelu


def _peer_out_kernel(row_ref, shift_ref, w_ref, tab_ref, x_ref, g_ref, lng_ref, lnb_ref, o_ref,
                     shift_s, w_s, yt_ref):
    n_acc = 2
    _splat_block(shift_ref, shift_s, w_ref, w_s)

    def pair(i, carry):
        bases = [pl.multiple_of((2 * i + k) * PEER_SLOTS, PEER_SLOTS) for k in range(2)]
        rows = [row_ref.at[0, 0, pl.ds(b_, PEER_SLOTS)] for b_ in bases]
        accs = [[jnp.zeros((VREG_SUBLANES, VREG_LANES), jnp.float32) for _ in range(n_acc)] for _ in range(2)]
        for s in range(PEER_SLOTS):
            for k in range(2):
                val = _unpack_row(tab_ref, rows[k][s], shift_s[pl.ds(bases[k] + s, 1), :])
                accs[k][s % n_acc] = accs[k][s % n_acc] + w_s[pl.ds(bases[k] + s, 1), :] * val
        for k in range(2):
            yt_ref[pl.ds(pl.multiple_of((2 * i + k) * D_TILES, D_TILES), D_TILES), :] = accs[k][0] + accs[k][1]
        return carry

    lax.fori_loop(0, PEER_TB // 2, pair, 0)
    y = jnp.concatenate([yt_ref[_tile_rows(k), :] for k in range(D_TILES)], axis=1)
    o_ref[...] = _layer_norm_rows(DEEPNORM_ALPHA * x_ref[...] + g_ref[0] * y, lng_ref[...], lnb_ref[...])


def peer_experts_ln(x, mods, offs, shifts, gate, u_tab, v_tab, ln_g, ln_b):
    sh, sc, g = mods
    T = x.shape[0]
    nblk = T // PEER_TB
    per_blk = PEER_TB * PEER_SLOTS
    per_mod = T // sc.shape[0] // PEER_TB
    offs = offs.reshape(nblk, 1, per_blk)
    smem_spec = pl.BlockSpec((1, 1, per_blk), lambda i: (i, 0, 0), memory_space=pltpu.SMEM)
    lane_spec = pl.BlockSpec((1, 1, per_blk), lambda i: (i, 0, 0))
    slot_spec = pl.BlockSpec((PEER_TB, PEER_SLOTS), lambda i: (i, 0))
    tok_spec = pl.BlockSpec((PEER_TB, D_MODEL), lambda i: (i, 0))
    mod_spec = pl.BlockSpec((1, 1, D_MODEL), lambda i: (i // per_mod, 0, 0))
    vec_spec = pl.BlockSpec((1, D_MODEL), lambda i: (0, 0))
    tab_spec = pl.BlockSpec(memory_space=pltpu.VMEM)
    params = pltpu.CompilerParams(dimension_semantics=("arbitrary",), vmem_limit_bytes=PEER_VMEM_LIMIT)
    splat_i32 = pltpu.VMEM((per_blk, VREG_LANES), jnp.int32)
    splat_f32 = pltpu.VMEM((per_blk, VREG_LANES), jnp.float32)
    tok_tiles = pltpu.VMEM((PEER_TB * D_TILES, VREG_LANES), jnp.float32)
    w = pl.pallas_call(
        _peer_act_kernel,
        grid=(nblk,),
        in_specs=[smem_spec, slot_spec, tok_spec, mod_spec, mod_spec, lane_spec, tab_spec],
        out_specs=lane_spec,
        out_shape=jax.ShapeDtypeStruct((nblk, 1, per_blk), jnp.float32),
        scratch_shapes=[splat_f32, splat_i32, tok_tiles],
        compiler_params=params,
        name="peer_act",
    )(offs, shifts, x, sc, sh, gate.reshape(nblk, 1, per_blk), _pack_rows_bf16(u_tab))
    return pl.pallas_call(
        _peer_out_kernel,
        grid=(nblk,),
        in_specs=[smem_spec, slot_spec, slot_spec, tab_spec, tok_spec, mod_spec, vec_spec, vec_spec],
        out_specs=tok_spec,
        out_shape=jax.ShapeDtypeStruct((T, D_MODEL), jnp.float32),
        scratch_shapes=[splat_i32, splat_f32, tok_tiles],
        compiler_params=params,
        name="peer_out",
    )(offs, shifts, w.reshape(T, PEER_SLOTS), _pack_rows_bf16(v_tab), x, g,
      ln_g.reshape(1, D_MODEL), ln_b.reshape(1, D_MODEL))


def _top_rows(s, k):
    n = s.shape[0]
    row = lax.broadcasted_iota(jnp.int32, s.shape, 0).astype(jnp.float32)
    vals, idxs = [], []
    for _ in range(k):
        m = jnp.max(s, axis=0, keepdims=True)
        i = jnp.min(jnp.where(s == m, row, float(n)), axis=0, keepdims=True)
        vals.append(m)
        idxs.append(i)
        s = jnp.where(row == i, -jnp.inf, s)
    return jnp.concatenate(vals, axis=0), jnp.concatenate(idxs, axis=0).astype(jnp.int32)


def _pick_rows(table, which):
    out = jnp.zeros(which.shape, table.dtype)
    for a in range(table.shape[0]):
        out = jnp.where(which == a, table[a:a + 1, :], out)
    return out


PAIR_COUNTS = tuple(PEER_TOPK // (a + 1) for a in range(PEER_TOPK))
PAIR_STARTS = tuple(sum(PAIR_COUNTS[:a]) for a in range(PEER_TOPK))
N_PAIRS = sum(PAIR_COUNTS)
N_PAIRS_PADDED = -(-N_PAIRS // VREG_SUBLANES) * VREG_SUBLANES
SEL_TB = 512
PEER_PACKED_ROWS = PEER_NKEYS * PEER_NKEYS // 2


def _peer_select_kernel(x_ref, sc_ref, sh_ref, wq_ref, k1_ref, k2_ref, off_ref, shift_ref, gate_ref,
                        off_s, shift_s, gate_s):
    half = PEER_DK // 2
    h = (x_ref[...] * (1.0 + sc_ref[0]) + sh_ref[0]).astype(jnp.bfloat16)
    q = jnp.dot(h, wq_ref[...], preferred_element_type=jnp.float32).astype(jnp.bfloat16)
    contract_last = (((1,), (1,)), ((), ()))
    s1 = lax.dot_general(k1_ref[0], q[:, :half], contract_last, preferred_element_type=jnp.float32)
    s2 = lax.dot_general(k2_ref[0], q[:, half:], contract_last, preferred_element_type=jnp.float32)
    v1, i1 = _top_rows(s1, PEER_TOPK)
    v2, i2 = _top_rows(s2, PEER_TOPK)
    pairs = [v1[a:a + 1, :] + v2[:PAIR_COUNTS[a], :] for a in range(PEER_TOPK)]
    pairs.append(jnp.full((N_PAIRS_PADDED - N_PAIRS, v1.shape[1]), -jnp.inf, jnp.float32))
    top_s, pos = _top_rows(jnp.concatenate(pairs, axis=0), PEER_TOPK)
    a_sel = jnp.zeros(pos.shape, jnp.int32)
    start = jnp.zeros(pos.shape, jnp.int32)
    for a in range(1, PEER_TOPK):
        later = pos >= PAIR_STARTS[a]
        a_sel = jnp.where(later, a, a_sel)
        start = jnp.where(later, PAIR_STARTS[a], start)
    expert = _pick_rows(i1, a_sel) * PEER_NKEYS + _pick_rows(i2, pos - start)
    e = jnp.exp(top_s - top_s[0:1, :])
    head = pl.program_id(1)
    rows = pl.ds(pl.multiple_of(head * PEER_TOPK, PEER_TOPK), PEER_TOPK)
    gate_s[rows, :] = e / jnp.sum(e, axis=0, keepdims=True)
    off_s[rows, :] = (expert & (PEER_PACKED_ROWS - 1)) * VREG_SUBLANES
    shift_s[rows, :] = jnp.where(expert >= PEER_PACKED_ROWS, 0, 16)

    @pl.when(head == PEER_HEADS - 1)
    def _():
        gate_ref[...] = gate_s[...].T
        off_ref[...] = off_s[...].astype(jnp.float32).T.astype(jnp.int32)
        shift_ref[...] = shift_s[...].astype(jnp.float32).T.astype(jnp.int32)


def peer_select(x, sc, sh, wq, k1, k2):
    T = x.shape[0]
    per_seq = T // sc.shape[0] // SEL_TB
    bf16 = jnp.bfloat16
    tok_spec = pl.BlockSpec((SEL_TB, D_MODEL), lambda i, h: (i, 0))
    mod_spec = pl.BlockSpec((1, 1, D_MODEL), lambda i, h: (i // per_seq, 0, 0))
    wq_spec = pl.BlockSpec((D_MODEL, PEER_DK), lambda i, h: (0, h))
    key_spec = pl.BlockSpec((1, PEER_NKEYS, PEER_DK // 2), lambda i, h: (h, 0, 0))
    out_spec = pl.BlockSpec((SEL_TB, PEER_SLOTS), lambda i, h: (i, 0))
    return pl.pallas_call(
        _peer_select_kernel,
        grid=(T // SEL_TB, PEER_HEADS),
        in_specs=[tok_spec, mod_spec, mod_spec, wq_spec, key_spec, key_spec],
        out_specs=[out_spec, out_spec, out_spec],
        out_shape=[jax.ShapeDtypeStruct((T, PEER_SLOTS), jnp.int32), jax.ShapeDtypeStruct((T, PEER_SLOTS), jnp.int32),
                   jax.ShapeDtypeStruct((T, PEER_SLOTS), jnp.float32)],
        scratch_shapes=[pltpu.VMEM((PEER_SLOTS, SEL_TB), jnp.int32), pltpu.VMEM((PEER_SLOTS, SEL_TB), jnp.int32),
                        pltpu.VMEM((PEER_SLOTS, SEL_TB), jnp.float32)],
        compiler_params=pltpu.CompilerParams(dimension_semantics=("arbitrary", "arbitrary")),
        name="peer_select",
    )(x, sc, sh, wq.astype(bf16), k1.astype(bf16), k2.astype(bf16))


def peer_layer(xt, mods, wq, k1, k2, u_tab, v_tab, ln_g, ln_b):
    sh2, sc2, _ = mods
    off, shift, gate = peer_select(xt, sc2, sh2, wq, k1, k2)
    return peer_experts_ln(xt, mods, off, shift, gate, u_tab, v_tab, ln_g, ln_b)


def kernel(x, c, ctx, c_ctx, ada_w, ada_b, ln_g, ln_b,
           m_w_in, m_conv_w, m_conv_b, m_a_log, m_dt_bias, m_d, m_norm_w, m_w_out,
           h_w_in, h_conv_w, h_conv_b, h_f_w1, h_f_b1, h_f_w2, h_f_b2, h_f_w3, h_freq, h_bias, h_w_out,
           p_wq, p_k1, p_k2, p_u, p_v):
    n_batch, seq, _ = x.shape
    rows = seq // GRID_W
    s_lat = jax.nn.silu(c)
    s_ctx = jax.nn.silu(c_ctx)
    xt = x.reshape(n_batch * seq, D_MODEL)
    ctx_t = ctx.reshape(-1, D_MODEL)
    for i in range(DEPTH):
        kind, k = i % N_MIXERS, i // N_MIXERS
        sh1, sc1, g1, sh2, sc2, g2 = jnp.split((s_lat @ ada_w[i] + ada_b[i])[:, None, :], 6, axis=-1)
        if kind == 0:
            csh1, csc1 = jnp.split((s_ctx @ ada_w[i] + ada_b[i])[None, None, :], 6, axis=-1)[:2]
            xt = mamba_layer(xt, ctx_t, (sh1, sc1, g1), (csh1, csc1), n_batch, rows, m_w_in[k], m_conv_w[k],
                             m_conv_b[k], m_a_log[k], m_dt_bias[k], m_d[k], m_norm_w[k], m_w_out[k],
                             ln_g[i, 0], ln_b[i, 0])
        else:
            xt = hyena_layer(xt, (sh1, sc1, g1), n_batch, h_w_in[k], h_conv_w[k], h_conv_b[k], h_f_w1[k], h_f_b1[k],
                             h_f_w2[k], h_f_b2[k], h_f_w3[k], h_freq[k], h_bias[k], h_w_out[k], ln_g[i, 0], ln_b[i, 0])
        xt = peer_layer(xt, (sh2, sc2, g2), p_wq[i], p_k1[i], p_k2[i], p_u[i], p_v[i], ln_g[i, 1], ln_b[i, 1])
    return xt.reshape(x.shape)
```

```python
import functools
import math

import jax
import jax.numpy as jnp
from jax import lax
from jax.experimental import pallas as pl
from jax.experimental.pallas import tpu as pltpu

D_MODEL = 1024
DEPTH = 2
GRID_W = 64
N_MIXERS = 2
DEEPNORM_ALPHA = (2 * DEPTH) ** 0.25
LN_EPS = 1e-5
RMS_EPS = 1e-5

SSM_INNER = 2 * D_MODEL
SSM_HEAD_DIM = 64
SSM_HEADS = SSM_INNER // SSM_HEAD_DIM
SSM_GROUPS = 4
SSM_STATE = 128
SSM_CHUNK = 128
SSM_CONV = 3
SSM_CONV_DIM = SSM_INNER + 2 * SSM_GROUPS * SSM_STATE
SSM_GROUP_HEADS = SSM_HEADS // SSM_GROUPS
SSM_GROUP_DIM = SSM_GROUP_HEADS * SSM_HEAD_DIM

HYENA_ORDER = 2
HYENA_EMB = 33
HYENA_INNER_MLPS = 2
HYENA_STEEP_DECAY = 0.3
HYENA_GENTLE_DECAY = 1.5
HYENA_TARGET = 1e-2
HYENA_NORM_EPS = 1e-6

PEER_HEADS = 8
PEER_NKEYS = 128
PEER_DK = 256
PEER_TOPK = 16

VREG_SUBLANES = 8
VREG_LANES = 128
DENSE_VMEM_LIMIT = 48 * 1024 * 1024


def _mod_matmul_kernel(x_ref, sc_ref, sh_ref, w_ref, o_ref):
    h = (x_ref[...] * (1.0 + sc_ref[0]) + sh_ref[0]).astype(jnp.bfloat16)
    o_ref[...] = jnp.dot(h, w_ref[...], preferred_element_type=jnp.float32)


def mod_matmul(x, sc, sh, w, tb, tn):
    T, D = x.shape
    N = w.shape[1]
    assert T % tb == 0 and N % tn == 0 and (T // sc.shape[0]) % tb == 0
    per_mod = T // sc.shape[0] // tb
    mod_spec = pl.BlockSpec((1, 1, D), lambda i, j: (i // per_mod, 0, 0))
    return pl.pallas_call(
        _mod_matmul_kernel,
        grid=(T // tb, N // tn),
        in_specs=[pl.BlockSpec((tb, D), lambda i, j: (i, 0)), mod_spec, mod_spec,
                  pl.BlockSpec((D, tn), lambda i, j: (0, j))],
        out_specs=pl.BlockSpec((tb, tn), lambda i, j: (i, j)),
        out_shape=jax.ShapeDtypeStruct((T, N), jnp.float32),
        compiler_params=pltpu.CompilerParams(dimension_semantics=("arbitrary", "arbitrary"),
                                             vmem_limit_bytes=DENSE_VMEM_LIMIT),
        name="mod_matmul",
    )(x, sc, sh, w.astype(jnp.bfloat16))


def _layer_norm_rows(v, g, b):
    mu = jnp.mean(v, axis=-1, keepdims=True)
    vc = v - mu
    var = jnp.mean(vc * vc, axis=-1, keepdims=True)
    return vc * lax.rsqrt(var + LN_EPS) * g + b


def _proj_ln_kernel(a_ref, w_ref, x_ref, gate_ref, g_ref, b_ref, o_ref):
    y = jnp.dot(a_ref[...].astype(jnp.bfloat16), w_ref[...], preferred_element_type=jnp.float32)
    o_ref[...] = _layer_norm_rows(DEEPNORM_ALPHA * x_ref[...] + gate_ref[0] * y, g_ref[...], b_ref[...])


def _row_specs(tb, per_mod):
    tok = pl.BlockSpec((tb, D_MODEL), lambda i: (i, 0))
    mod = pl.BlockSpec((1, 1, D_MODEL), lambda i: (i // per_mod, 0, 0))
    vec = pl.BlockSpec((1, D_MODEL), lambda i: (0, 0))
    return tok, mod, vec


def proj_ln(a, w, x, gate, ln_g, ln_b, tb=512):
    T, K = a.shape
    per_mod = T // gate.shape[0] // tb
    tok, mod, vec = _row_specs(tb, per_mod)
    return pl.pallas_call(
        _proj_ln_kernel,
        grid=(T // tb,),
        in_specs=[pl.BlockSpec((tb, K), lambda i: (i, 0)), pl.BlockSpec((K, D_MODEL), lambda i: (0, 0)),
                  tok, mod, vec, vec],
        out_specs=tok,
        out_shape=jax.ShapeDtypeStruct((T, D_MODEL), jnp.float32),
        compiler_params=pltpu.CompilerParams(dimension_semantics=("arbitrary",), vmem_limit_bytes=DENSE_VMEM_LIMIT),
        name="proj_ln",
    )(a, w.astype(jnp.bfloat16), x, gate, ln_g.reshape(1, D_MODEL), ln_b.reshape(1, D_MODEL))


def _residual_ln_kernel(y_ref, x_ref, gate_ref, g_ref, b_ref, o_ref):
    o_ref[...] = _layer_norm_rows(DEEPNORM_ALPHA * x_ref[...] + gate_ref[0] * y_ref[...], g_ref[...], b_ref[...])


def residual_ln(y, x, gate, ln_g, ln_b, tb=512):
    T = x.shape[0]
    tok, mod, vec = _row_specs(tb, T // gate.shape[0] // tb)
    return pl.pallas_call(
        _residual_ln_kernel,
        grid=(T // tb,),
        in_specs=[tok, tok, mod, vec, vec],
        out_specs=tok,
        out_shape=jax.ShapeDtypeStruct((T, D_MODEL), jnp.float32),
        compiler_params=pltpu.CompilerParams(dimension_semantics=("arbitrary",)),
        name="residual_ln",
    )(y, x, gate, ln_g.reshape(1, D_MODEL), ln_b.reshape(1, D_MODEL))


def _mamba_out_kernel(y_ref, xs_ref, z_ref, dsk_ref, nw_ref, w_ref, x_ref, gate_ref, g_ref, b_ref, o_ref):
    z = z_ref[...]
    u = (y_ref[0] + y_ref[1] + xs_ref[...] * dsk_ref[...]) * (z * jax.nn.sigmoid(z))
    gw = SSM_INNER // SSM_GROUPS
    parts = []
    for g in range(SSM_GROUPS):
        ug = u[:, g * gw:(g + 1) * gw]
        parts.append(ug * lax.rsqrt(jnp.mean(ug * ug, axis=-1, keepdims=True) + RMS_EPS))
    a = (jnp.concatenate(parts, axis=1) * nw_ref[...]).astype(jnp.bfloat16)
    y = jnp.dot(a, w_ref[...], preferred_element_type=jnp.float32)
    o_ref[...] = _layer_norm_rows(DEEPNORM_ALPHA * x_ref[...] + gate_ref[0] * y, g_ref[...], b_ref[...])


def mamba_out(y2, xbc, z, d_skip, norm_w, w_out, x, gate, ln_g, ln_b, tb=256):
    T = x.shape[0]
    tok, mod, vec = _row_specs(tb, T // gate.shape[0] // tb)
    inner = pl.BlockSpec((tb, SSM_INNER), lambda i: (i, 0))
    ivec = pl.BlockSpec((1, SSM_INNER), lambda i: (0, 0))
    dsk = jnp.repeat(d_skip, SSM_INNER // d_skip.shape[0]).reshape(1, SSM_INNER)
    return pl.pallas_call(
        _mamba_out_kernel,
        grid=(T // tb,),
        in_specs=[pl.BlockSpec((2, tb, SSM_INNER), lambda i: (0, i, 0)), inner, inner, ivec, ivec,
                  pl.BlockSpec((SSM_INNER, D_MODEL), lambda i: (0, 0)), tok, mod, vec, vec],
        out_specs=tok,
        out_shape=jax.ShapeDtypeStruct((T, D_MODEL), jnp.float32),
        compiler_params=pltpu.CompilerParams(dimension_semantics=("arbitrary",), vmem_limit_bytes=DENSE_VMEM_LIMIT),
        name="mamba_out",
    )(y2, xbc, z, dsk, norm_w.reshape(1, SSM_INNER), w_out.astype(jnp.bfloat16), x, gate,
      ln_g.reshape(1, D_MODEL), ln_b.reshape(1, D_MODEL))


CONV_VMEM_LIMIT = 56 * 1024 * 1024


def _shift_rows(a, k):
    n = a.shape[0]
    return a if k == 0 else pltpu.roll(a, (-k) % n, 0)


def _mod_matmul_conv_kernel(x_ref, sc_ref, sh_ref, w_ref, cw_ref, cb_ref, o_ref, *, rows, cols, silu):
    h = (x_ref[...] * (1.0 + sc_ref[0]) + sh_ref[0]).astype(jnp.bfloat16)
    p = jnp.dot(h, w_ref[...], preferred_element_type=jnp.float32)
    L = p.shape[0]
    t = lax.broadcasted_iota(jnp.int32, p.shape, 0)
    col = t & (cols - 1)
    n_dr = cw_ref.shape[0]
    out = jnp.zeros_like(p)
    for j in range(3):
        dc = j - 1
        acc = None
        for i in range(n_dr):
            dr = i - n_dr // 2
            term = _shift_rows(p, dr * cols)
            if dr < 0:
                term = jnp.where(t >= -dr * cols, term, 0.0)
            elif dr > 0:
                term = jnp.where(t < L - dr * cols, term, 0.0)
            term = term * cw_ref[i, j:j + 1, :]
            acc = term if acc is None else acc + term
        acc = _shift_rows(acc, dc)
        if dc < 0:
            acc = jnp.where(col >= -dc, acc, 0.0)
        elif dc > 0:
            acc = jnp.where(col < cols - dc, acc, 0.0)
        out = out + acc
    out = out + cb_ref[...]
    o_ref[...] = out * jax.nn.sigmoid(out) if silu else out


def mod_matmul_conv(x, sc, sh, w, conv_w, conv_b, n_seq, rows, cols, silu, tn=256):
    T, D = x.shape
    N = w.shape[1]
    L = rows * cols
    assert T == n_seq * L and N % tn == 0 and cols & (cols - 1) == 0 and conv_w.shape[0] in (1, 3)
    per_mod = n_seq // sc.shape[0]
    mod_spec = pl.BlockSpec((1, 1, D), lambda b, j: (b // per_mod, 0, 0))
    return pl.pallas_call(
        functools.partial(_mod_matmul_conv_kernel, rows=rows, cols=cols, silu=silu),
        grid=(n_seq, N // tn),
        in_specs=[pl.BlockSpec((L, D), lambda b, j: (b, 0), pipeline_mode=pl.Buffered(1)), mod_spec, mod_spec,
                  pl.BlockSpec((D, tn), lambda b, j: (0, j)),
                  pl.BlockSpec((conv_w.shape[0], 3, tn), lambda b, j: (0, 0, j)),
                  pl.BlockSpec((1, tn), lambda b, j: (0, j))],
        out_specs=pl.BlockSpec((L, tn), lambda b, j: (b, j)),
        out_shape=jax.ShapeDtypeStruct((T, N), jnp.float32),
        compiler_params=pltpu.CompilerParams(dimension_semantics=("arbitrary", "arbitrary"),
                                             vmem_limit_bytes=CONV_VMEM_LIMIT),
        name="mod_matmul_conv",
    )(x, sc, sh, w.astype(jnp.bfloat16), conv_w, conv_b.reshape(1, N))


def _split3_bf16(a):
    f32, bf16 = jnp.float32, jnp.bfloat16
    p0 = a.astype(bf16)
    r = a - p0.astype(f32)
    p1 = r.astype(bf16)
    p2 = (r - p1.astype(f32)).astype(bf16)
    return p0, p1, p2


def _ssd_kernel(x_ref, b_ref, c_ref, dt_ref, an_ref, init_ref, *rest, want_y):
    if want_y:
        y_ref, fin_ref, state_ref = rest
    else:
        fin_ref, state_ref = rest
    f32, bf16 = jnp.float32, jnp.bfloat16
    Q, P, H, G, N = SSM_CHUNK, SSM_HEAD_DIM, SSM_HEADS, SSM_GROUPS, SSM_STATE
    R, GD = SSM_GROUP_HEADS, SSM_GROUP_DIM
    fwd = pl.program_id(1) == 0
    c = pl.program_id(2)

    @pl.when(c == 0)
    def _():
        state_ref[...] = init_ref[0, 0].reshape(SSM_INNER, N)

    contract_rows = (((0,), (0,)), ((), ()))
    contract_lanes = (((1,), (1,)), ((), ()))
    dt = dt_ref[0, 0]
    a = dt * an_ref[0]
    li = lax.broadcasted_iota(jnp.int32, (Q, Q), 0)
    si = lax.broadcasted_iota(jnp.int32, (Q, Q), 1)
    tri = jnp.where(fwd, li - si, si - li) >= 0
    tri_bf = jnp.where(tri, 1.0, 0.0).astype(bf16)
    pieces = _split3_bf16(a)
    a_cum = sum(jnp.dot(tri_bf, p, preferred_element_type=f32) for p in pieces)
    a_cum_t = sum(lax.dot_general(p, tri_bf, (((0,), (1,)), ((), ())), preferred_element_type=f32)
                  for p in pieces)
    total = jnp.sum(a, axis=0, keepdims=True)
    spread = (lax.broadcasted_iota(jnp.int32, (H, SSM_INNER), 1) // P
              == lax.broadcasted_iota(jnp.int32, (H, SSM_INNER), 0)).astype(bf16)

    def per_lane(v):
        return sum(jnp.dot(p, spread, preferred_element_type=f32) for p in _split3_bf16(v))

    xdt = x_ref[0] * per_lane(dt)
    xw = (xdt * per_lane(jnp.exp(total - a_cum))).astype(bf16)
    tot_col = jnp.where(fwd, a_cum_t[:, Q - 1:Q], a_cum_t[:, 0:1])
    keep = jnp.broadcast_to(jnp.exp(tot_col), (H, N))
    keep_rows = sum(lax.dot_general(spread, p, contract_rows, preferred_element_type=f32)
                    for p in _split3_bf16(keep))
    if want_y:
        xdt_bf = xdt.astype(bf16)
        out_scale = per_lane(jnp.exp(a_cum))
        lane = lax.broadcasted_iota(jnp.int32, (Q, 2 * P), 1)
    ys = []
    for g in range(G):
        cols = slice(g * GD, (g + 1) * GD)
        bm = b_ref[0, :, g * N:(g + 1) * N].astype(bf16)
        cm = c_ref[0, :, g * N:(g + 1) * N].astype(bf16)
        state = state_ref[cols, :]
        if want_y:
            cb = lax.dot_general(cm, bm, contract_lanes, preferred_element_type=f32)
            for j in range(R // 2):
                ms = []
                for h in (g * R + 2 * j, g * R + 2 * j + 1):
                    decay = jnp.where(tri, jnp.exp(a_cum[:, h:h + 1] - a_cum_t[h:h + 1, :]), 0.0)
                    ms.append((cb * decay).astype(bf16))
                xp = xdt_bf[:, g * GD + 2 * j * P:g * GD + 2 * (j + 1) * P]
                zero = jnp.zeros_like(xp)
                rhs = jnp.concatenate([jnp.where(lane < P, xp, zero), jnp.where(lane >= P, xp, zero)], axis=0)
                ys.append(jnp.dot(jnp.concatenate(ms, axis=1), rhs, preferred_element_type=f32))
            y_off = lax.dot_general(cm, state.astype(bf16), contract_lanes, preferred_element_type=f32)
            y_ref[0, 0, :, cols] = jnp.concatenate(ys[-(R // 2):], axis=1) + y_off * out_scale[:, cols]
        upd = lax.dot_general(xw[:, cols], bm, contract_rows, preferred_element_type=f32)
        state_ref[cols, :] = state * keep_rows[cols, :] + upd
    fin_ref[0, 0] = state_ref[...].reshape(H, P, N)


def ssd_scan_both(xbc, dt, a_neg, init, want_y):
    B, L, _ = xbc.shape
    Q, G, H, P, N = SSM_CHUNK, SSM_GROUPS, SSM_HEADS, SSM_HEAD_DIM, SSM_STATE
    nc = L // Q
    dt_d = dt.transpose(0, 2, 1, 3)
    an = a_neg.reshape(2, 1, H)
    bc_cols = G * N

    def chunk(d, c):
        return c + d * (nc - 1 - 2 * c)

    x_spec = pl.BlockSpec((1, Q, SSM_INNER), lambda b, d, c: (b, chunk(d, c), 0))
    b_spec = pl.BlockSpec((1, Q, bc_cols), lambda b, d, c: (b, chunk(d, c), SSM_INNER // bc_cols))
    c_spec = pl.BlockSpec((1, Q, bc_cols), lambda b, d, c: (b, chunk(d, c), SSM_INNER // bc_cols + 1))
    dt_spec = pl.BlockSpec((1, 1, Q, H), lambda b, d, c: (b, d, chunk(d, c), 0))
    an_spec = pl.BlockSpec((1, 1, H), lambda b, d, c: (d, 0, 0))
    st_spec = pl.BlockSpec((1, 1, H, P, N), lambda b, d, c: (b, d, 0, 0, 0))
    st_shape = jax.ShapeDtypeStruct((B, 2, H, P, N), jnp.float32)
    out_specs, out_shape = [st_spec], [st_shape]
    if want_y:
        out_specs = [pl.BlockSpec((1, 1, Q, SSM_INNER), lambda b, d, c: (d, b, chunk(d, c), 0))] + out_specs
        out_shape = [jax.ShapeDtypeStruct((2, B, L, SSM_INNER), jnp.float32)] + out_shape
    outs = pl.pallas_call(
        functools.partial(_ssd_kernel, want_y=want_y),
        grid=(B, 2, nc),
        in_specs=[x_spec, b_spec, c_spec, dt_spec, an_spec, st_spec],
        out_specs=out_specs,
        out_shape=out_shape,
        scratch_shapes=[pltpu.VMEM((SSM_INNER, N), jnp.float32)],
        compiler_params=pltpu.CompilerParams(dimension_semantics=("arbitrary",) * 3),
        name="ssd_scan_y" if want_y else "ssd_scan_state",
    )(xbc, xbc, xbc, dt_d, an, init)
    if want_y:
        return outs[0], outs[1]
    return None, outs[0]


def mamba_layer(xt, ctx_t, mods, cmods, n_batch, rows, w_in, conv_w, conv_b, a_log, dt_bias, d_skip, norm_w, w_out,
                ln_g, ln_b):
    f32 = jnp.float32
    sh1, sc1, g1 = mods
    csh1, csc1 = cmods
    a_neg = -jnp.exp(a_log.astype(f32))
    w_xbc = w_in[:, SSM_INNER:SSM_INNER + SSM_CONV_DIM]
    w_dt = w_in[:, SSM_INNER + SSM_CONV_DIM:]

    def softplus_dt(dt_raw, length):
        return jax.nn.softplus(dt_raw.reshape(n_batch, length, 2, SSM_HEADS) + dt_bias.astype(f32))

    lc = ctx_t.shape[0] // n_batch
    xbc_c = mod_matmul_conv(ctx_t, csc1, csh1, w_xbc, conv_w[1:2], conv_b, n_batch, 1, lc, True)
    dt_c = softplus_dt(mod_matmul(ctx_t, csc1, csh1, w_dt, 512, 2 * SSM_HEADS), lc)
    zeros = jnp.zeros((n_batch, 2, SSM_HEADS, SSM_HEAD_DIM, SSM_STATE), f32)
    _, states = ssd_scan_both(xbc_c.reshape(n_batch, lc, SSM_CONV_DIM), dt_c, a_neg, zeros, False)
    T = xt.shape[0]
    L = T // n_batch
    z = mod_matmul(xt, sc1, sh1, w_in[:, :SSM_INNER], 1024, 1024)
    xbc = mod_matmul_conv(xt, sc1, sh1, w_xbc, conv_w, conv_b, n_batch, rows, GRID_W, True)
    dt = softplus_dt(mod_matmul(xt, sc1, sh1, w_dt, 1024, 2 * SSM_HEADS), L)
    y2, _ = ssd_scan_both(xbc.reshape(n_batch, L, SSM_CONV_DIM), dt, a_neg, states, True)
    return mamba_out(y2.reshape(2, T, SSM_INNER), xbc, z, d_skip, norm_w, w_out, xt, g1, ln_g, ln_b)


def hyena_filters(L, w1, b1, w2, b2, w3, freq):
    f32 = jnp.float32
    t = jnp.linspace(0.0, 1.0, L, dtype=f32)[:, None]
    bands = (HYENA_EMB - 1) // 2
    ang = 2.0 * math.pi * jnp.arange(L, dtype=f32)[:, None] / L
    fb = jnp.linspace(1e-4, bands - 1, bands, dtype=f32)[None, :]
    z = jnp.concatenate([t, jnp.cos(fb * ang), -jnp.sin(fb * ang)], axis=-1)
    fr = freq.astype(f32)
    hid = jnp.sin(fr * (z @ w1.astype(f32) + b1.astype(f32)))
    for j in range(HYENA_INNER_MLPS):
        hid = jnp.sin(fr * (hid @ w2[j].astype(f32) + b2[j].astype(f32)))
    h = (hid @ w3.astype(f32)).reshape(L, HYENA_ORDER, 2, D_MODEL)
    max_decay = math.log(HYENA_TARGET) / HYENA_STEEP_DECAY
    min_decay = math.log(HYENA_TARGET) / HYENA_GENTLE_DECAY
    deltas = jnp.linspace(min_decay, max_decay, D_MODEL, dtype=f32)
    h = h * jnp.exp(-t * jnp.abs(deltas))[:, None, None, :]
    h_fwd, h_bwd = h[:, :, 0], h[:, :, 1]
    taps_pos = h_fwd.at[0].add(h_bwd[0]).reshape(L, HYENA_ORDER * D_MODEL)
    taps_neg = h_bwd.at[0].set(0.0).reshape(L, HYENA_ORDER * D_MODEL)
    return taps_pos, taps_neg


FFT_N1 = 128
FFT_N2 = 64
FFT_N = FFT_N1 * FFT_N2
CONV_L = FFT_N // 2
CONV_T1 = CONV_L // FFT_N2
FFT_VMEM_LIMIT = 56 * 1024 * 1024
FFT_UNROLL = 4


def _dft_tables():
    def cos_sin(phase, n):
        ang = (2.0 * math.pi / n) * (phase % n).astype(jnp.float32)
        return jnp.cos(ang), jnp.sin(ang)
    t2 = jnp.arange(FFT_N2, dtype=jnp.int32)[:, None, None]
    k1 = jnp.arange(FFT_N1, dtype=jnp.int32)[None, :, None]
    t1 = jnp.arange(CONV_T1, dtype=jnp.int32)[None, None, :]
    c, s = cos_sin(FFT_N2 * t1 * k1 + t2 * k1, FFT_N)
    fwd = jnp.concatenate([c, -s], axis=1)
    inv = jnp.concatenate([c, s], axis=1).reshape(FFT_N2, 2, FFT_N1, CONV_T1)
    inv = inv.transpose(0, 1, 3, 2).reshape(FFT_N2, 2 * CONV_T1, FFT_N1) / FFT_N
    k2 = jnp.arange(FFT_N2, dtype=jnp.int32)
    c2, s2 = cos_sin(k2[:, None] * k2[None, :], FFT_N2)
    mid = jnp.concatenate([c2, -s2], axis=0)
    return fwd, mid, inv


def _mm(w, x):
    return jnp.dot(w, x.astype(jnp.bfloat16), preferred_element_type=jnp.float32)


def _long_conv_kernel(x_ref, g_ref, hr_ref, hi_ref, b_ref, wf_ref, wm_ref, wi_ref, o_ref, sr_ref, si_ref):
    def rows_t2(t2):
        return pl.ds(t2, CONV_T1, stride=FFT_N2)

    def stage_a(t2, carry):
        w = wf_ref[t2]
        p = _mm(w, x_ref.at[0][rows_t2(t2), :])
        q = _mm(w, x_ref.at[1][rows_t2(t2), :])
        dst = pl.ds(pl.multiple_of(t2 * FFT_N1, FFT_N1), FFT_N1)
        sr_ref[dst, :] = p[:FFT_N1] - q[FFT_N1:]
        si_ref[dst, :] = p[FFT_N1:] + q[:FFT_N1]
        return carry

    lax.fori_loop(0, FFT_N2, stage_a, 0, unroll=FFT_UNROLL)
    wm = wm_ref[...]

    def per_k1(k1, carry):
        sel = pl.ds(k1, FFT_N2, stride=FFT_N1)
        p = _mm(wm, sr_ref[sel, :])
        q = _mm(wm, si_ref[sel, :])
        xr = p[:FFT_N2] - q[FFT_N2:]
        xi = p[FFT_N2:] + q[:FFT_N2]
        hsel = pl.ds(pl.multiple_of(k1 * FFT_N2, FFT_N2), FFT_N2)
        hr = hr_ref[hsel, :]
        hi = hi_ref[hsel, :]
        p = _mm(wm, xr * hr - xi * hi)
        q = _mm(wm, xr * hi + xi * hr)
        sr_ref[sel, :] = p[:FFT_N2] + q[FFT_N2:]
        si_ref[sel, :] = q[:FFT_N2] - p[FFT_N2:]
        return carry

    lax.fori_loop(0, FFT_N1, per_k1, 0, unroll=FFT_UNROLL)
    bias = b_ref[...]

    def stage_c(t2, carry):
        w = wi_ref[t2]
        src = pl.ds(pl.multiple_of(t2 * FFT_N1, FFT_N1), FFT_N1)
        p = _mm(w, sr_ref[src, :])
        q = _mm(w, si_ref[src, :])
        rows = rows_t2(t2)
        o_ref.at[0][rows, :] = g_ref.at[0][rows, :] * (p[:CONV_T1] - q[CONV_T1:] + x_ref.at[0][rows, :] * bias)
        o_ref.at[1][rows, :] = g_ref.at[1][rows, :] * (q[:CONV_T1] + p[CONV_T1:] + x_ref.at[1][rows, :] * bias)
        return carry

    lax.fori_loop(0, FFT_N2, stage_c, 0, unroll=FFT_UNROLL)


def _split_bf16(w):
    hi = w.astype(jnp.bfloat16)
    return hi, (w - hi.astype(jnp.float32)).astype(jnp.bfloat16)


def _mm3(w_hi, w_lo, x):
    x_hi, x_lo = _split_bf16(x)
    f32 = jnp.float32
    return (jnp.dot(w_hi, x_hi, preferred_element_type=f32) + jnp.dot(w_lo, x_hi, preferred_element_type=f32)
            + jnp.dot(w_hi, x_lo, preferred_element_type=f32))


def _filter_spectrum_kernel(a_ref, b_ref, wfh_ref, wfl_ref, wmh_ref, wml_ref, hr_ref, hi_ref,
                            ar_ref, ai_ref, br_ref, bi_ref):
    def rows_t2(t2):
        return pl.ds(t2, CONV_T1, stride=FFT_N2)

    def stage_a(t2, carry):
        wh, wl = wfh_ref[t2], wfl_ref[t2]
        p = _mm3(wh, wl, a_ref[rows_t2(t2), :])
        q = _mm3(wh, wl, b_ref[rows_t2(t2), :])
        dst = pl.ds(pl.multiple_of(t2 * FFT_N1, FFT_N1), FFT_N1)
        ar_ref[dst, :] = p[:FFT_N1]
        ai_ref[dst, :] = p[FFT_N1:]
        br_ref[dst, :] = q[:FFT_N1]
        bi_ref[dst, :] = q[FFT_N1:]
        return carry

    lax.fori_loop(0, FFT_N2, stage_a, 0, unroll=FFT_UNROLL)
    wmh, wml = wmh_ref[...], wml_ref[...]
    a = a_ref[...]
    b = b_ref[...]
    energy = jnp.sum(a * a, axis=0, keepdims=True) + jnp.sum(b * b, axis=0, keepdims=True)
    scale = lax.rsqrt(energy + HYENA_NORM_EPS)

    def per_k1(k1, carry):
        sel = pl.ds(k1, FFT_N2, stride=FFT_N1)
        pa, qa = _mm3(wmh, wml, ar_ref[sel, :]), _mm3(wmh, wml, ai_ref[sel, :])
        pb, qb = _mm3(wmh, wml, br_ref[sel, :]), _mm3(wmh, wml, bi_ref[sel, :])
        dst = pl.ds(pl.multiple_of(k1 * FFT_N2, FFT_N2), FFT_N2)
        a_re, a_im = pa[:FFT_N2] - qa[FFT_N2:], pa[FFT_N2:] + qa[:FFT_N2]
        b_re, b_im = pb[:FFT_N2] - qb[FFT_N2:], pb[FFT_N2:] + qb[:FFT_N2]
        hr_ref[dst, :] = (a_re + b_re) * scale
        hi_ref[dst, :] = (a_im - b_im) * scale
        return carry

    lax.fori_loop(0, FFT_N1, per_k1, 0, unroll=FFT_UNROLL)


def filter_spectrum(a, b, tables):
    L, C = a.shape
    wf, wm, _ = tables
    sig = pl.BlockSpec((L, VREG_LANES), lambda c: (0, c))
    out = pl.BlockSpec((FFT_N, VREG_LANES), lambda c: (0, c))
    once = pl.Buffered(1)
    plane = pltpu.VMEM((FFT_N, VREG_LANES), jnp.float32)
    return pl.pallas_call(
        _filter_spectrum_kernel,
        grid=(C // VREG_LANES,),
        in_specs=[sig, sig] + [pl.BlockSpec(wf.shape, lambda c: (0, 0, 0), pipeline_mode=once)] * 2
        + [pl.BlockSpec(wm.shape, lambda c: (0, 0), pipeline_mode=once)] * 2,
        out_specs=[out, out],
        out_shape=[jax.ShapeDtypeStruct((FFT_N, C), jnp.float32)] * 2,
        scratch_shapes=[plane] * 4,
        compiler_params=pltpu.CompilerParams(dimension_semantics=("arbitrary",), vmem_limit_bytes=FFT_VMEM_LIMIT),
        name="filter_spectrum",
    )(a, b, *_split_bf16(wf), *_split_bf16(wm))


def long_conv_gated(u, u_col, gate, gate_col, h_re, h_im, h_col, bias, tables):
    B, L, _ = u.shape
    C = bias.shape[0]
    assert L == CONV_L and B % 2 == 0 and C % VREG_LANES == 0
    assert u_col % VREG_LANES == 0 and gate_col % VREG_LANES == 0 and h_col % VREG_LANES == 0
    consts = [t.astype(jnp.bfloat16) for t in tables]
    sig_spec = pl.BlockSpec((2, L, VREG_LANES), lambda c, b: (b, 0, c))
    u_spec = pl.BlockSpec((2, L, VREG_LANES), lambda c, b: (b, 0, c + u_col // VREG_LANES))
    gate_spec = pl.BlockSpec((2, L, VREG_LANES), lambda c, b: (b, 0, c + gate_col // VREG_LANES))
    once = pl.Buffered(1)
    spec_h = pl.BlockSpec((FFT_N, VREG_LANES), lambda c, b: (0, c + h_col // VREG_LANES), pipeline_mode=once)
    const_specs = [pl.BlockSpec(t.shape, functools.partial(lambda nd, c, b: (0,) * nd, t.ndim), pipeline_mode=once)
                   for t in consts]
    return pl.pallas_call(
        _long_conv_kernel,
        grid=(C // VREG_LANES, B // 2),
        in_specs=[u_spec, gate_spec, spec_h, spec_h, pl.BlockSpec((1, VREG_LANES), lambda c, b: (0, c))] + const_specs,
        out_specs=sig_spec,
        out_shape=jax.ShapeDtypeStruct((B, L, C), jnp.float32),
        scratch_shapes=[pltpu.VMEM((FFT_N, VREG_LANES), jnp.float32), pltpu.VMEM((FFT_N, VREG_LANES), jnp.float32)],
        compiler_params=pltpu.CompilerParams(dimension_semantics=("arbitrary", "arbitrary"),
                                             vmem_limit_bytes=FFT_VMEM_LIMIT),
        name="long_conv",
    )(u, gate, h_re, h_im, bias.reshape(1, C), *consts)


def hyena_layer(xt, mods, n_batch, w_in, conv_w, conv_b, w1, b1, w2, b2, w3, freq, fbias, w_out, ln_g, ln_b):
    sh1, sc1, g1 = mods
    T = xt.shape[0]
    L = T // n_batch
    tables = _dft_tables()
    h_re, h_im = filter_spectrum(*hyena_filters(L, w1, b1, w2, b2, w3, freq), tables)
    u = mod_matmul_conv(xt, sc1, sh1, w_in, conv_w[None], conv_b, n_batch, 1, L, False).reshape(n_batch, L, 3 * D_MODEL)
    zz = long_conv_gated(u, 2 * D_MODEL, u, 0, h_re, h_im, 0, fbias[0], tables)
    zz = long_conv_gated(zz, 0, u, D_MODEL, h_re, h_im, D_MODEL, fbias[1], tables)
    return proj_ln(zz.reshape(T, D_MODEL), w_out, xt, g1, ln_g, ln_b)


D_TILES = D_MODEL // VREG_LANES
PEER_SLOTS = PEER_HEADS * PEER_TOPK
PEER_TB = 256
PEER_SUB = 64
PEER_VMEM_LIMIT = 52 * 1024 * 1024
HI_MASK = -65536


def _pack_rows_bf16(tab):
    n, d = tab.shape
    bits = lax.bitcast_convert_type(tab.astype(jnp.bfloat16), jnp.uint16).astype(jnp.uint32)
    packed = (bits[n // 2:] << 16) | bits[:n // 2]
    return lax.bitcast_convert_type(packed, jnp.int32).reshape(n // 2 * (d // VREG_LANES), VREG_LANES)


def _unpack_row(tab_ref, off, shift):
    word = tab_ref[pl.ds(pl.multiple_of(off, VREG_SUBLANES), VREG_SUBLANES), :]
    return pltpu.bitcast((word << shift) & HI_MASK, jnp.float32)


def _sublane_sums(tiles, sub):
    for h in (1, 2, 4):
        keep = (sub & h) == 0
        nxt = []
        for a, b in zip(tiles[0::2], tiles[1::2]):
            nxt.append(jnp.where(keep, a, b) + pltpu.roll(jnp.where(keep, b, a), h, 0))
        tiles = nxt
    return tiles[0]


def _splat_tile(row):
    return jnp.broadcast_to(row, (PEER_SLOTS, VREG_LANES)).T


def _splat_small_ints_tile(row, eye, ones):
    diag = jnp.where(eye, row.astype(jnp.float32), 0.0).astype(jnp.bfloat16)
    return jnp.dot(diag, ones, preferred_element_type=jnp.float32).astype(jnp.int32)


def _splat_consts():
    eye = (lax.broadcasted_iota(jnp.int32, (PEER_SLOTS, PEER_SLOTS), 0)
           == lax.broadcasted_iota(jnp.int32, (PEER_SLOTS, PEER_SLOTS), 1))
    return eye, jnp.ones((PEER_SLOTS, VREG_LANES), jnp.bfloat16)


def _splat_block(t0, shift_ref, shift_s, w_ref=None, w_s=None):
    eye, ones = _splat_consts()

    def token(t, carry):
        dst = pl.ds(pl.multiple_of(t * PEER_SLOTS, PEER_SLOTS), PEER_SLOTS)
        shift_s[dst, :] = _splat_small_ints_tile(shift_ref[pl.ds(t0 + t, 1), :], eye, ones)
        if w_ref is not None:
            w_s[dst, :] = _splat_tile(w_ref[pl.ds(t0 + t, 1), :])
        return carry
    lax.fori_loop(0, PEER_SUB, token, 0, unroll=8)


def _tile_rows(k):
    return pl.ds(k, PEER_SUB, stride=D_TILES)


def _peer_act_kernel(row_ref, shift_ref, x_ref, sc_ref, sh_ref, gate_ref, tab_ref, w_ref, part_ref, shift_s, xt_ref):
    sub = lax.broadcasted_iota(jnp.int32, (VREG_SUBLANES, VREG_LANES), 0)
    contract_lanes = (((1,), (1,)), ((), ()))
    ones = jnp.ones((VREG_SUBLANES, VREG_LANES), jnp.bfloat16)
    for t0 in range(0, PEER_TB, PEER_SUB):
        _splat_block(t0, shift_ref, shift_s)
        h = x_ref[t0:t0 + PEER_SUB, :] * (1.0 + sc_ref[0]) + sh_ref[0]
        for k in range(D_TILES):
            xt_ref[_tile_rows(k), :] = h[:, k * VREG_LANES:(k + 1) * VREG_LANES]

        def token(t, carry):
            xt = xt_ref[pl.ds(pl.multiple_of(t * D_TILES, D_TILES), D_TILES), :]
            base = pl.multiple_of(t * PEER_SLOTS, PEER_SLOTS)
            tok_rows = row_ref.at[0, 0, pl.ds(t0 * PEER_SLOTS + base, PEER_SLOTS)]
            for g in range(PEER_SLOTS // VREG_SUBLANES):
                prods = []
                for k in range(VREG_SUBLANES):
                    s = g * VREG_SUBLANES + k
                    prods.append(_unpack_row(tab_ref, tok_rows[s], shift_s[pl.ds(base + s, 1), :]) * xt)
                part_ref[pl.ds(base + g * VREG_SUBLANES, VREG_SUBLANES), :] = _sublane_sums(prods, sub)
            return carry

        lax.fori_loop(0, PEER_SUB, token, 0)
        part = part_ref[...]
        part_hi = part.astype(jnp.bfloat16)
        part_lo = (part - part_hi.astype(jnp.float32)).astype(jnp.bfloat16)
        act = (lax.dot_general(ones, part_hi, contract_lanes, preferred_element_type=jnp.float32)
               + lax.dot_general(ones, part_lo, contract_lanes, preferred_element_type=jnp.float32))[0:1]
        gelu = 0.5 * act * (1.0 + lax.erf(act * (2.0 ** -0.5)))
        lanes = slice(t0 * PEER_SLOTS, (t0 + PEER_SUB) * PEER_SLOTS)
        w_ref[0, :, lanes] = gate_ref[0, :, lanes] * gelu


def _peer_out_kernel(row_ref, shift_ref, w_ref, tab_ref, x_ref, g_ref, lng_ref, lnb_ref, o_ref,
                     shift_s, w_s, yt_ref):
    n_acc = 2
    for t0 in range(0, PEER_TB, PEER_SUB):
        _splat_block(t0, shift_ref, shift_s, w_ref, w_s)

        def pair(i, carry):
            bases = [pl.multiple_of((2 * i + k) * PEER_SLOTS, PEER_SLOTS) for k in range(2)]
            rows = [row_ref.at[0, 0, pl.ds(t0 * PEER_SLOTS + b_, PEER_SLOTS)] for b_ in bases]
            accs = [[jnp.zeros((VREG_SUBLANES, VREG_LANES), jnp.float32) for _ in range(n_acc)] for _ in range(2)]
            for s in range(PEER_SLOTS):
                for k in range(2):
                    val = _unpack_row(tab_ref, rows[k][s], shift_s[pl.ds(bases[k] + s, 1), :])
                    accs[k][s % n_acc] = accs[k][s % n_acc] + w_s[pl.ds(bases[k] + s, 1), :] * val
            for k in range(2):
                yt_ref[pl.ds(pl.multiple_of((2 * i + k) * D_TILES, D_TILES), D_TILES), :] = accs[k][0] + accs[k][1]
            return carry

        lax.fori_loop(0, PEER_SUB // 2, pair, 0)
        y = jnp.concatenate([yt_ref[_tile_rows(k), :] for k in range(D_TILES)], axis=1)
        rows_out = slice(t0, t0 + PEER_SUB)
        o_ref[rows_out, :] = _layer_norm_rows(DEEPNORM_ALPHA * x_ref[rows_out, :] + g_ref[0] * y,
                                              lng_ref[...], lnb_ref[...])


def peer_experts_ln(x, mods, offs, shifts, gate, u_tab, v_tab, ln_g, ln_b):
    sh, sc, g = mods
    T = x.shape[0]
    nblk = T // PEER_TB
    per_blk = PEER_TB * PEER_SLOTS
    per_mod = T // sc.shape[0] // PEER_TB
    offs = offs.reshape(nblk, 1, per_blk)
    smem_spec = pl.BlockSpec((1, 1, per_blk), lambda i: (i, 0, 0), memory_space=pltpu.SMEM)
    lane_spec = pl.BlockSpec((1, 1, per_blk), lambda i: (i, 0, 0))
    slot_spec = pl.BlockSpec((PEER_TB, PEER_SLOTS), lambda i: (i, 0))
    tok_spec = pl.BlockSpec((PEER_TB, D_MODEL), lambda i: (i, 0))
    mod_spec = pl.BlockSpec((1, 1, D_MODEL), lambda i: (i // per_mod, 0, 0))
    vec_spec = pl.BlockSpec((1, D_MODEL), lambda i: (0, 0))
    tab_spec = pl.BlockSpec(memory_space=pltpu.VMEM)
    params = pltpu.CompilerParams(dimension_semantics=("arbitrary",), vmem_limit_bytes=PEER_VMEM_LIMIT)
    splat_i32 = pltpu.VMEM((PEER_SUB * PEER_SLOTS, VREG_LANES), jnp.int32)
    splat_f32 = pltpu.VMEM((PEER_SUB * PEER_SLOTS, VREG_LANES), jnp.float32)
    tok_tiles = pltpu.VMEM((PEER_SUB * D_TILES, VREG_LANES), jnp.float32)
    w = pl.pallas_call(
        _peer_act_kernel,
        grid=(nblk,),
        in_specs=[smem_spec, slot_spec, tok_spec, mod_spec, mod_spec, lane_spec, tab_spec],
        out_specs=lane_spec,
        out_shape=jax.ShapeDtypeStruct((nblk, 1, per_blk), jnp.float32),
        scratch_shapes=[splat_f32, splat_i32, tok_tiles],
        compiler_params=params,
        name="peer_act",
    )(offs, shifts, x, sc, sh, gate.reshape(nblk, 1, per_blk), _pack_rows_bf16(u_tab))
    return pl.pallas_call(
        _peer_out_kernel,
        grid=(nblk,),
        in_specs=[smem_spec, slot_spec, slot_spec, tab_spec, tok_spec, mod_spec, vec_spec, vec_spec],
        out_specs=tok_spec,
        out_shape=jax.ShapeDtypeStruct((T, D_MODEL), jnp.float32),
        scratch_shapes=[splat_i32, splat_f32, tok_tiles],
        compiler_params=params,
        name="peer_out",
    )(offs, shifts, w.reshape(T, PEER_SLOTS), _pack_rows_bf16(v_tab), x, g,
      ln_g.reshape(1, D_MODEL), ln_b.reshape(1, D_MODEL))


def _top_rows(s, k):
    n = s.shape[0]
    row = lax.broadcasted_iota(jnp.int32, s.shape, 0).astype(jnp.float32)
    vals, idxs = [], []
    for _ in range(k):
        m = jnp.max(s, axis=0, keepdims=True)
        i = jnp.min(jnp.where(s == m, row, float(n)), axis=0, keepdims=True)
        vals.append(m)
        idxs.append(i)
        s = jnp.where(row == i, -jnp.inf, s)
    return jnp.concatenate(vals, axis=0), jnp.concatenate(idxs, axis=0).astype(jnp.int32)


def _pick_rows(table, which):
    out = jnp.zeros(which.shape, table.dtype)
    for a in range(table.shape[0]):
        out = jnp.where(which == a, table[a:a + 1, :], out)
    return out


PAIR_COUNTS = tuple(PEER_TOPK // (a + 1) for a in range(PEER_TOPK))
PAIR_STARTS = tuple(sum(PAIR_COUNTS[:a]) for a in range(PEER_TOPK))
N_PAIRS = sum(PAIR_COUNTS)
N_PAIRS_PADDED = -(-N_PAIRS // VREG_SUBLANES) * VREG_SUBLANES
SEL_TB = 512
PEER_PACKED_ROWS = PEER_NKEYS * PEER_NKEYS // 2


def _peer_select_kernel(x_ref, sc_ref, sh_ref, wq_ref, k1_ref, k2_ref, off_ref, shift_ref, gate_ref,
                        off_s, shift_s, gate_s):
    half = PEER_DK // 2
    h = (x_ref[...] * (1.0 + sc_ref[0]) + sh_ref[0]).astype(jnp.bfloat16)
    q = jnp.dot(h, wq_ref[...], preferred_element_type=jnp.float32).astype(jnp.bfloat16)
    contract_last = (((1,), (1,)), ((), ()))
    s1 = lax.dot_general(k1_ref[0], q[:, :half], contract_last, preferred_element_type=jnp.float32)
    s2 = lax.dot_general(k2_ref[0], q[:, half:], contract_last, preferred_element_type=jnp.float32)
    v1, i1 = _top_rows(s1, PEER_TOPK)
    v2, i2 = _top_rows(s2, PEER_TOPK)
    pairs = [v1[a:a + 1, :] + v2[:PAIR_COUNTS[a], :] for a in range(PEER_TOPK)]
    pairs.append(jnp.full((N_PAIRS_PADDED - N_PAIRS, v1.shape[1]), -jnp.inf, jnp.float32))
    top_s, pos = _top_rows(jnp.concatenate(pairs, axis=0), PEER_TOPK)
    a_sel = jnp.zeros(pos.shape, jnp.int32)
    start = jnp.zeros(pos.shape, jnp.int32)
    for a in range(1, PEER_TOPK):
        later = pos >= PAIR_STARTS[a]
        a_sel = jnp.where(later, a, a_sel)
        start = jnp.where(later, PAIR_STARTS[a], start)
    expert = _pick_rows(i1, a_sel) * PEER_NKEYS + _pick_rows(i2, pos - start)
    e = jnp.exp(top_s - top_s[0:1, :])
    head = pl.program_id(1)
    rows = pl.ds(pl.multiple_of(head * PEER_TOPK, PEER_TOPK), PEER_TOPK)
    gate_s[rows, :] = e / jnp.sum(e, axis=0, keepdims=True)
    off_s[rows, :] = (expert & (PEER_PACKED_ROWS - 1)) * VREG_SUBLANES
    shift_s[rows, :] = jnp.where(expert >= PEER_PACKED_ROWS, 0, 16)

    @pl.when(head == PEER_HEADS - 1)
    def _():
        gate_ref[...] = gate_s[...].T
        off_ref[...] = off_s[...].astype(jnp.float32).T.astype(jnp.int32)
        shift_ref[...] = shift_s[...].astype(jnp.float32).T.astype(jnp.int32)


def peer_select(x, sc, sh, wq, k1, k2):
    T = x.shape[0]
    per_seq = T // sc.shape[0] // SEL_TB
    bf16 = jnp.bfloat16
    tok_spec = pl.BlockSpec((SEL_TB, D_MODEL), lambda i, h: (i, 0))
    mod_spec = pl.BlockSpec((1, 1, D_MODEL), lambda i, h: (i // per_seq, 0, 0))
    wq_spec = pl.BlockSpec((D_MODEL, PEER_DK), lambda i, h: (0, h))
    key_spec = pl.BlockSpec((1, PEER_NKEYS, PEER_DK // 2), lambda i, h: (h, 0, 0))
    out_spec = pl.BlockSpec((SEL_TB, PEER_SLOTS), lambda i, h: (i, 0))
    return pl.pallas_call(
        _peer_select_kernel,
        grid=(T // SEL_TB, PEER_HEADS),
        in_specs=[tok_spec, mod_spec, mod_spec, wq_spec, key_spec, key_spec],
        out_specs=[out_spec, out_spec, out_spec],
        out_shape=[jax.ShapeDtypeStruct((T, PEER_SLOTS), jnp.int32), jax.ShapeDtypeStruct((T, PEER_SLOTS), jnp.int32),
                   jax.ShapeDtypeStruct((T, PEER_SLOTS), jnp.float32)],
        scratch_shapes=[pltpu.VMEM((PEER_SLOTS, SEL_TB), jnp.int32), pltpu.VMEM((PEER_SLOTS, SEL_TB), jnp.int32),
                        pltpu.VMEM((PEER_SLOTS, SEL_TB), jnp.float32)],
        compiler_params=pltpu.CompilerParams(dimension_semantics=("arbitrary", "arbitrary")),
        name="peer_select",
    )(x, sc, sh, wq.astype(bf16), k1.astype(bf16), k2.astype(bf16))


def peer_layer(xt, mods, wq, k1, k2, u_tab, v_tab, ln_g, ln_b):
    sh2, sc2, _ = mods
    off, shift, gate = peer_select(xt, sc2, sh2, wq, k1, k2)
    return peer_experts_ln(xt, mods, off, shift, gate, u_tab, v_tab, ln_g, ln_b)


def kernel(x, c, ctx, c_ctx, ada_w, ada_b, ln_g, ln_b,
           m_w_in, m_conv_w, m_conv_b, m_a_log, m_dt_bias, m_d, m_norm_w, m_w_out,
           h_w_in, h_conv_w, h_conv_b, h_f_w1, h_f_b1, h_f_w2, h_f_b2, h_f_w3, h_freq, h_bias, h_w_out,
           p_wq, p_k1, p_k2, p_u, p_v):
    n_batch, seq, _ = x.shape
    rows = seq // GRID_W
    s_lat = jax.nn.silu(c)
    s_ctx = jax.nn.silu(c_ctx)
    xt = x.reshape(n_batch * seq, D_MODEL)
    ctx_t = ctx.reshape(-1, D_MODEL)
    for i in range(DEPTH):
        kind, k = i % N_MIXERS, i // N_MIXERS
        sh1, sc1, g1, sh2, sc2, g2 = jnp.split((s_lat @ ada_w[i] + ada_b[i])[:, None, :], 6, axis=-1)
        if kind == 0:
            csh1, csc1 = jnp.split((s_ctx @ ada_w[i] + ada_b[i])[None, None, :], 6, axis=-1)[:2]
            xt = mamba_layer(xt, ctx_t, (sh1, sc1, g1), (csh1, csc1), n_batch, rows, m_w_in[k], m_conv_w[k],
                             m_conv_b[k], m_a_log[k], m_dt_bias[k], m_d[k], m_norm_w[k], m_w_out[k],
                             ln_g[i, 0], ln_b[i, 0])
        else:
            xt = hyena_layer(xt, (sh1, sc1, g1), n_batch, h_w_in[k], h_conv_w[k], h_conv_b[k], h_f_w1[k], h_f_b1[k],
                             h_f_w2[k], h_f_b2[k], h_f_w3[k], h_freq[k], h_bias[k], h_w_out[k], ln_g[i, 0], ln_b[i, 0])
        xt = peer_layer(xt, (sh2, sc2, g2), p_wq[i], p_k1[i], p_k2[i], p_u[i], p_v[i], ln_g[i, 1], ln_b[i, 1])
    return xt.reshape(x.shape)
```

```python
import functools
import math

import jax
import jax.numpy as jnp
from jax import lax
from jax.experimental import pallas as pl
from jax.experimental.pallas import tpu as pltpu

D_MODEL = 1024
DEPTH = 2
GRID_W = 64
N_MIXERS = 2
DEEPNORM_ALPHA = (2 * DEPTH) ** 0.25
LN_EPS = 1e-5
RMS_EPS = 1e-5

SSM_INNER = 2 * D_MODEL
SSM_HEAD_DIM = 64
SSM_HEADS = SSM_INNER // SSM_HEAD_DIM
SSM_GROUPS = 4
SSM_STATE = 128
SSM_CHUNK = 128
SSM_CONV = 3
SSM_CONV_DIM = SSM_INNER + 2 * SSM_GROUPS * SSM_STATE
SSM_GROUP_HEADS = SSM_HEADS // SSM_GROUPS
SSM_GROUP_DIM = SSM_GROUP_HEADS * SSM_HEAD_DIM

HYENA_ORDER = 2
HYENA_EMB = 33
HYENA_INNER_MLPS = 2
HYENA_STEEP_DECAY = 0.3
HYENA_GENTLE_DECAY = 1.5
HYENA_TARGET = 1e-2
HYENA_NORM_EPS = 1e-6

PEER_HEADS = 8
PEER_NKEYS = 128
PEER_DK = 256
PEER_TOPK = 16

VREG_SUBLANES = 8
VREG_LANES = 128
DENSE_VMEM_LIMIT = 48 * 1024 * 1024


def _mod_matmul_kernel(x_ref, sc_ref, sh_ref, w_ref, o_ref):
    h = (x_ref[...] * (1.0 + sc_ref[0]) + sh_ref[0]).astype(jnp.bfloat16)
    o_ref[...] = jnp.dot(h, w_ref[...], preferred_element_type=jnp.float32)


def mod_matmul(x, sc, sh, w, tb, tn):
    T, D = x.shape
    N = w.shape[1]
    assert T % tb == 0 and N % tn == 0 and (T // sc.shape[0]) % tb == 0
    per_mod = T // sc.shape[0] // tb
    mod_spec = pl.BlockSpec((1, 1, D), lambda i, j: (i // per_mod, 0, 0))
    return pl.pallas_call(
        _mod_matmul_kernel,
        grid=(T // tb, N // tn),
        in_specs=[pl.BlockSpec((tb, D), lambda i, j: (i, 0)), mod_spec, mod_spec,
                  pl.BlockSpec((D, tn), lambda i, j: (0, j))],
        out_specs=pl.BlockSpec((tb, tn), lambda i, j: (i, j)),
        out_shape=jax.ShapeDtypeStruct((T, N), jnp.float32),
        compiler_params=pltpu.CompilerParams(dimension_semantics=("arbitrary", "arbitrary"),
                                             vmem_limit_bytes=DENSE_VMEM_LIMIT),
        name="mod_matmul",
    )(x, sc, sh, w.astype(jnp.bfloat16))


def _layer_norm_rows(v, g, b):
    mu = jnp.mean(v, axis=-1, keepdims=True)
    vc = v - mu
    var = jnp.mean(vc * vc, axis=-1, keepdims=True)
    return vc * lax.rsqrt(var + LN_EPS) * g + b


def _proj_ln_kernel(a_ref, w_ref, x_ref, gate_ref, g_ref, b_ref, o_ref):
    y = jnp.dot(a_ref[...].astype(jnp.bfloat16), w_ref[...], preferred_element_type=jnp.float32)
    o_ref[...] = _layer_norm_rows(DEEPNORM_ALPHA * x_ref[...] + gate_ref[0] * y, g_ref[...], b_ref[...])


def _row_specs(tb, per_mod):
    tok = pl.BlockSpec((tb, D_MODEL), lambda i: (i, 0))
    mod = pl.BlockSpec((1, 1, D_MODEL), lambda i: (i // per_mod, 0, 0))
    vec = pl.BlockSpec((1, D_MODEL), lambda i: (0, 0))
    return tok, mod, vec


def proj_ln(a, w, x, gate, ln_g, ln_b, tb=512):
    T, K = a.shape
    per_mod = T // gate.shape[0] // tb
    tok, mod, vec = _row_specs(tb, per_mod)
    return pl.pallas_call(
        _proj_ln_kernel,
        grid=(T // tb,),
        in_specs=[pl.BlockSpec((tb, K), lambda i: (i, 0)), pl.BlockSpec((K, D_MODEL), lambda i: (0, 0)),
                  tok, mod, vec, vec],
        out_specs=tok,
        out_shape=jax.ShapeDtypeStruct((T, D_MODEL), jnp.float32),
        compiler_params=pltpu.CompilerParams(dimension_semantics=("arbitrary",), vmem_limit_bytes=DENSE_VMEM_LIMIT),
        name="proj_ln",
    )(a, w.astype(jnp.bfloat16), x, gate, ln_g.reshape(1, D_MODEL), ln_b.reshape(1, D_MODEL))


def _residual_ln_kernel(y_ref, x_ref, gate_ref, g_ref, b_ref, o_ref):
    o_ref[...] = _layer_norm_rows(DEEPNORM_ALPHA * x_ref[...] + gate_ref[0] * y_ref[...], g_ref[...], b_ref[...])


def residual_ln(y, x, gate, ln_g, ln_b, tb=512):
    T = x.shape[0]
    tok, mod, vec = _row_specs(tb, T // gate.shape[0] // tb)
    return pl.pallas_call(
        _residual_ln_kernel,
        grid=(T // tb,),
        in_specs=[tok, tok, mod, vec, vec],
        out_specs=tok,
        out_shape=jax.ShapeDtypeStruct((T, D_MODEL), jnp.float32),
        compiler_params=pltpu.CompilerParams(dimension_semantics=("arbitrary",)),
        name="residual_ln",
    )(y, x, gate, ln_g.reshape(1, D_MODEL), ln_b.reshape(1, D_MODEL))


def _mamba_out_kernel(y_ref, xs_ref, z_ref, dsk_ref, nw_ref, w_ref, x_ref, gate_ref, g_ref, b_ref, o_ref):
    z = z_ref[...]
    u = (y_ref[0] + y_ref[1] + xs_ref[...] * dsk_ref[...]) * (z * jax.nn.sigmoid(z))
    gw = SSM_INNER // SSM_GROUPS
    parts = []
    for g in range(SSM_GROUPS):
        ug = u[:, g * gw:(g + 1) * gw]
        parts.append(ug * lax.rsqrt(jnp.mean(ug * ug, axis=-1, keepdims=True) + RMS_EPS))
    a = (jnp.concatenate(parts, axis=1) * nw_ref[...]).astype(jnp.bfloat16)
    y = jnp.dot(a, w_ref[...], preferred_element_type=jnp.float32)
    o_ref[...] = _layer_norm_rows(DEEPNORM_ALPHA * x_ref[...] + gate_ref[0] * y, g_ref[...], b_ref[...])


def mamba_out(y2, xbc, z, d_skip, norm_w, w_out, x, gate, ln_g, ln_b, tb=256):
    T = x.shape[0]
    tok, mod, vec = _row_specs(tb, T // gate.shape[0] // tb)
    inner = pl.BlockSpec((tb, SSM_INNER), lambda i: (i, 0))
    ivec = pl.BlockSpec((1, SSM_INNER), lambda i: (0, 0))
    dsk = jnp.repeat(d_skip, SSM_INNER // d_skip.shape[0]).reshape(1, SSM_INNER)
    return pl.pallas_call(
        _mamba_out_kernel,
        grid=(T // tb,),
        in_specs=[pl.BlockSpec((2, tb, SSM_INNER), lambda i: (0, i, 0)), inner, inner, ivec, ivec,
                  pl.BlockSpec((SSM_INNER, D_MODEL), lambda i: (0, 0)), tok, mod, vec, vec],
        out_specs=tok,
        out_shape=jax.ShapeDtypeStruct((T, D_MODEL), jnp.float32),
        compiler_params=pltpu.CompilerParams(dimension_semantics=("arbitrary",), vmem_limit_bytes=DENSE_VMEM_LIMIT),
        name="mamba_out",
    )(y2, xbc, z, dsk, norm_w.reshape(1, SSM_INNER), w_out.astype(jnp.bfloat16), x, gate,
      ln_g.reshape(1, D_MODEL), ln_b.reshape(1, D_MODEL))


CONV_VMEM_LIMIT = 56 * 1024 * 1024


def _shift_rows(a, k):
    n = a.shape[0]
    return a if k == 0 else pltpu.roll(a, (-k) % n, 0)


def _mod_matmul_conv_kernel(x_ref, sc_ref, sh_ref, w_ref, cw_ref, cb_ref, o_ref, *, rows, cols, silu):
    h = (x_ref[...] * (1.0 + sc_ref[0]) + sh_ref[0]).astype(jnp.bfloat16)
    p = jnp.dot(h, w_ref[...], preferred_element_type=jnp.float32)
    L = p.shape[0]
    t = lax.broadcasted_iota(jnp.int32, p.shape, 0)
    col = t & (cols - 1)
    n_dr = cw_ref.shape[0]
    out = jnp.zeros_like(p)
    for j in range(3):
        dc = j - 1
        acc = None
        for i in range(n_dr):
            dr = i - n_dr // 2
            term = _shift_rows(p, dr * cols)
            if dr < 0:
                term = jnp.where(t >= -dr * cols, term, 0.0)
            elif dr > 0:
                term = jnp.where(t < L - dr * cols, term, 0.0)
            term = term * cw_ref[i, j:j + 1, :]
            acc = term if acc is None else acc + term
        acc = _shift_rows(acc, dc)
        if dc < 0:
            acc = jnp.where(col >= -dc, acc, 0.0)
        elif dc > 0:
            acc = jnp.where(col < cols - dc, acc, 0.0)
        out = out + acc
    out = out + cb_ref[...]
    o_ref[...] = out * jax.nn.sigmoid(out) if silu else out


def mod_matmul_conv(x, sc, sh, w, conv_w, conv_b, n_seq, rows, cols, silu, tn=256):
    T, D = x.shape
    N = w.shape[1]
    L = rows * cols
    assert T == n_seq * L and N % tn == 0 and cols & (cols - 1) == 0 and conv_w.shape[0] in (1, 3)
    per_mod = n_seq // sc.shape[0]
    mod_spec = pl.BlockSpec((1, 1, D), lambda b, j: (b // per_mod, 0, 0))
    return pl.pallas_call(
        functools.partial(_mod_matmul_conv_kernel, rows=rows, cols=cols, silu=silu),
        grid=(n_seq, N // tn),
        in_specs=[pl.BlockSpec((L, D), lambda b, j: (b, 0), pipeline_mode=pl.Buffered(1)), mod_spec, mod_spec,
                  pl.BlockSpec((D, tn), lambda b, j: (0, j)),
                  pl.BlockSpec((conv_w.shape[0], 3, tn), lambda b, j: (0, 0, j)),
                  pl.BlockSpec((1, tn), lambda b, j: (0, j))],
        out_specs=pl.BlockSpec((L, tn), lambda b, j: (b, j)),
        out_shape=jax.ShapeDtypeStruct((T, N), jnp.float32),
        compiler_params=pltpu.CompilerParams(dimension_semantics=("arbitrary", "arbitrary"),
                                             vmem_limit_bytes=CONV_VMEM_LIMIT),
        name="mod_matmul_conv",
    )(x, sc, sh, w.astype(jnp.bfloat16), conv_w, conv_b.reshape(1, N))


def _split3_bf16(a):
    f32, bf16 = jnp.float32, jnp.bfloat16
    p0 = a.astype(bf16)
    r = a - p0.astype(f32)
    p1 = r.astype(bf16)
    p2 = (r - p1.astype(f32)).astype(bf16)
    return p0, p1, p2


def _ssd_kernel(x_ref, b_ref, c_ref, dt_ref, an_ref, init_ref, *rest, want_y):
    if want_y:
        y_ref, fin_ref, state_ref = rest
    else:
        fin_ref, state_ref = rest
    f32, bf16 = jnp.float32, jnp.bfloat16
    Q, P, H, G, N = SSM_CHUNK, SSM_HEAD_DIM, SSM_HEADS, SSM_GROUPS, SSM_STATE
    R, GD = SSM_GROUP_HEADS, SSM_GROUP_DIM
    fwd = pl.program_id(1) == 0
    c = pl.program_id(2)

    @pl.when(c == 0)
    def _():
        state_ref[...] = init_ref[0, 0].reshape(SSM_INNER, N)

    contract_rows = (((0,), (0,)), ((), ()))
    contract_lanes = (((1,), (1,)), ((), ()))
    dt = dt_ref[0, 0]
    a = dt * an_ref[0]
    li = lax.broadcasted_iota(jnp.int32, (Q, Q), 0)
    si = lax.broadcasted_iota(jnp.int32, (Q, Q), 1)
    tri = jnp.where(fwd, li - si, si - li) >= 0
    tri_bf = jnp.where(tri, 1.0, 0.0).astype(bf16)
    pieces = _split3_bf16(a)
    a_cum = sum(jnp.dot(tri_bf, p, preferred_element_type=f32) for p in pieces)
    a_cum_t = sum(lax.dot_general(p, tri_bf, (((0,), (1,)), ((), ())), preferred_element_type=f32)
                  for p in pieces)
    total = jnp.sum(a, axis=0, keepdims=True)
    spread = (lax.broadcasted_iota(jnp.int32, (H, SSM_INNER), 1) // P
              == lax.broadcasted_iota(jnp.int32, (H, SSM_INNER), 0)).astype(bf16)

    def per_lane(v):
        return sum(jnp.dot(p, spread, preferred_element_type=f32) for p in _split3_bf16(v))

    xdt = x_ref[0] * per_lane(dt)
    xw = (xdt * per_lane(jnp.exp(total - a_cum))).astype(bf16)
    tot_col = jnp.where(fwd, a_cum_t[:, Q - 1:Q], a_cum_t[:, 0:1])
    keep = jnp.broadcast_to(jnp.exp(tot_col), (H, N))
    keep_rows = sum(lax.dot_general(spread, p, contract_rows, preferred_element_type=f32)
                    for p in _split3_bf16(keep))
    if want_y:
        xdt_bf = xdt.astype(bf16)
        out_scale = per_lane(jnp.exp(a_cum))
        lane = lax.broadcasted_iota(jnp.int32, (Q, 2 * P), 1)
    ys = []
    for g in range(G):
        cols = slice(g * GD, (g + 1) * GD)
        bm = b_ref[0, :, g * N:(g + 1) * N].astype(bf16)
        cm = c_ref[0, :, g * N:(g + 1) * N].astype(bf16)
        state = state_ref[cols, :]
        if want_y:
            cb = lax.dot_general(cm, bm, contract_lanes, preferred_element_type=f32)
            for j in range(R // 2):
                ms = []
                for h in (g * R + 2 * j, g * R + 2 * j + 1):
                    decay = jnp.where(tri, jnp.exp(a_cum[:, h:h + 1] - a_cum_t[h:h + 1, :]), 0.0)
                    ms.append((cb * decay).astype(bf16))
                xp = xdt_bf[:, g * GD + 2 * j * P:g * GD + 2 * (j + 1) * P]
                zero = jnp.zeros_like(xp)
                rhs = jnp.concatenate([jnp.where(lane < P, xp, zero), jnp.where(lane >= P, xp, zero)], axis=0)
                ys.append(jnp.dot(jnp.concatenate(ms, axis=1), rhs, preferred_element_type=f32))
            y_off = lax.dot_general(cm, state.astype(bf16), contract_lanes, preferred_element_type=f32)
            y_ref[0, 0, :, cols] = jnp.concatenate(ys[-(R // 2):], axis=1) + y_off * out_scale[:, cols]
        upd = lax.dot_general(xw[:, cols], bm, contract_rows, preferred_element_type=f32)
        state_ref[cols, :] = state * keep_rows[cols, :] + upd
    fin_ref[0, 0] = state_ref[...].reshape(H, P, N)


def ssd_scan_both(xbc, dt, a_neg, init, want_y):
    B, L, _ = xbc.shape
    Q, G, H, P, N = SSM_CHUNK, SSM_GROUPS, SSM_HEADS, SSM_HEAD_DIM, SSM_STATE
    nc = L // Q
    dt_d = dt.transpose(0, 2, 1, 3)
    an = a_neg.reshape(2, 1, H)
    bc_cols = G * N

    def chunk(d, c):
        return c + d * (nc - 1 - 2 * c)

    x_spec = pl.BlockSpec((1, Q, SSM_INNER), lambda b, d, c: (b, chunk(d, c), 0))
    b_spec = pl.BlockSpec((1, Q, bc_cols), lambda b, d, c: (b, chunk(d, c), SSM_INNER // bc_cols))
    c_spec = pl.BlockSpec((1, Q, bc_cols), lambda b, d, c: (b, chunk(d, c), SSM_INNER // bc_cols + 1))
    dt_spec = pl.BlockSpec((1, 1, Q, H), lambda b, d, c: (b, d, chunk(d, c), 0))
    an_spec = pl.BlockSpec((1, 1, H), lambda b, d, c: (d, 0, 0))
    st_spec = pl.BlockSpec((1, 1, H, P, N), lambda b, d, c: (b, d, 0, 0, 0))
    st_shape = jax.ShapeDtypeStruct((B, 2, H, P, N), jnp.float32)
    out_specs, out_shape = [st_spec], [st_shape]
    if want_y:
        out_specs = [pl.BlockSpec((1, 1, Q, SSM_INNER), lambda b, d, c: (d, b, chunk(d, c), 0))] + out_specs
        out_shape = [jax.ShapeDtypeStruct((2, B, L, SSM_INNER), jnp.float32)] + out_shape
    outs = pl.pallas_call(
        functools.partial(_ssd_kernel, want_y=want_y),
        grid=(B, 2, nc),
        in_specs=[x_spec, b_spec, c_spec, dt_spec, an_spec, st_spec],
        out_specs=out_specs,
        out_shape=out_shape,
        scratch_shapes=[pltpu.VMEM((SSM_INNER, N), jnp.float32)],
        compiler_params=pltpu.CompilerParams(dimension_semantics=("arbitrary",) * 3),
        name="ssd_scan_y" if want_y else "ssd_scan_state",
    )(xbc, xbc, xbc, dt_d, an, init)
    if want_y:
        return outs[0], outs[1]
    return None, outs[0]


def mamba_layer(xt, ctx_t, mods, cmods, n_batch, rows, w_in, conv_w, conv_b, a_log, dt_bias, d_skip, norm_w, w_out,
                ln_g, ln_b):
    f32 = jnp.float32
    sh1, sc1, g1 = mods
    csh1, csc1 = cmods
    a_neg = -jnp.exp(a_log.astype(f32))
    w_xbc = w_in[:, SSM_INNER:SSM_INNER + SSM_CONV_DIM]
    w_dt = w_in[:, SSM_INNER + SSM_CONV_DIM:]

    def softplus_dt(dt_raw, length):
        return jax.nn.softplus(dt_raw.reshape(n_batch, length, 2, SSM_HEADS) + dt_bias.astype(f32))

    lc = ctx_t.shape[0] // n_batch
    xbc_c = mod_matmul_conv(ctx_t, csc1, csh1, w_xbc, conv_w[1:2], conv_b, n_batch, 1, lc, True)
    dt_c = softplus_dt(mod_matmul(ctx_t, csc1, csh1, w_dt, 512, 2 * SSM_HEADS), lc)
    zeros = jnp.zeros((n_batch, 2, SSM_HEADS, SSM_HEAD_DIM, SSM_STATE), f32)
    _, states = ssd_scan_both(xbc_c.reshape(n_batch, lc, SSM_CONV_DIM), dt_c, a_neg, zeros, False)
    T = xt.shape[0]
    L = T // n_batch
    z = mod_matmul(xt, sc1, sh1, w_in[:, :SSM_INNER], 1024, 1024)
    xbc = mod_matmul_conv(xt, sc1, sh1, w_xbc, conv_w, conv_b, n_batch, rows, GRID_W, True)
    dt = softplus_dt(mod_matmul(xt, sc1, sh1, w_dt, 1024, 2 * SSM_HEADS), L)
    y2, _ = ssd_scan_both(xbc.reshape(n_batch, L, SSM_CONV_DIM), dt, a_neg, states, True)
    return mamba_out(y2.reshape(2, T, SSM_INNER), xbc, z, d_skip, norm_w, w_out, xt, g1, ln_g, ln_b)


def hyena_filters(L, w1, b1, w2, b2, w3, freq):
    f32 = jnp.float32
    t = jnp.linspace(0.0, 1.0, L, dtype=f32)[:, None]
    bands = (HYENA_EMB - 1) // 2
    ang = 2.0 * math.pi * jnp.arange(L, dtype=f32)[:, None] / L
    fb = jnp.linspace(1e-4, bands - 1, bands, dtype=f32)[None, :]
    z = jnp.concatenate([t, jnp.cos(fb * ang), -jnp.sin(fb * ang)], axis=-1)
    fr = freq.astype(f32)
    hid = jnp.sin(fr * (z @ w1.astype(f32) + b1.astype(f32)))
    for j in range(HYENA_INNER_MLPS):
        hid = jnp.sin(fr * (hid @ w2[j].astype(f32) + b2[j].astype(f32)))
    h = (hid @ w3.astype(f32)).reshape(L, HYENA_ORDER, 2, D_MODEL)
    max_decay = math.log(HYENA_TARGET) / HYENA_STEEP_DECAY
    min_decay = math.log(HYENA_TARGET) / HYENA_GENTLE_DECAY
    deltas = jnp.linspace(min_decay, max_decay, D_MODEL, dtype=f32)
    h = h * jnp.exp(-t * jnp.abs(deltas))[:, None, None, :]
    h_fwd, h_bwd = h[:, :, 0], h[:, :, 1]
    taps_pos = h_fwd.at[0].add(h_bwd[0]).reshape(L, HYENA_ORDER * D_MODEL)
    taps_neg = h_bwd.at[0].set(0.0).reshape(L, HYENA_ORDER * D_MODEL)
    return taps_pos, taps_neg


FFT_N1 = 128
FFT_N2 = 64
FFT_N = FFT_N1 * FFT_N2
CONV_L = FFT_N // 2
CONV_T1 = CONV_L // FFT_N2
FFT_VMEM_LIMIT = 56 * 1024 * 1024
FFT_UNROLL = 4


def _dft_tables():
    def cos_sin(phase, n):
        ang = (2.0 * math.pi / n) * (phase % n).astype(jnp.float32)
        return jnp.cos(ang), jnp.sin(ang)
    t2 = jnp.arange(FFT_N2, dtype=jnp.int32)[:, None, None]
    k1 = jnp.arange(FFT_N1, dtype=jnp.int32)[None, :, None]
    t1 = jnp.arange(CONV_T1, dtype=jnp.int32)[None, None, :]
    c, s = cos_sin(FFT_N2 * t1 * k1 + t2 * k1, FFT_N)
    fwd = jnp.concatenate([c, -s], axis=1)
    inv = jnp.concatenate([c, s], axis=1).reshape(FFT_N2, 2, FFT_N1, CONV_T1)
    inv = inv.transpose(0, 1, 3, 2).reshape(FFT_N2, 2 * CONV_T1, FFT_N1) / FFT_N
    k2 = jnp.arange(FFT_N2, dtype=jnp.int32)
    c2, s2 = cos_sin(k2[:, None] * k2[None, :], FFT_N2)
    mid = jnp.concatenate([c2, -s2], axis=0)
    return fwd, mid, inv


def _mm(w, x):
    return jnp.dot(w, x.astype(jnp.bfloat16), preferred_element_type=jnp.float32)


def _long_conv_kernel(x_ref, g_ref, hr_ref, hi_ref, b_ref, wf_ref, wm_ref, wi_ref, o_ref, sr_ref, si_ref):
    def rows_t2(t2):
        return pl.ds(t2, CONV_T1, stride=FFT_N2)

    def stage_a(t2, carry):
        w = wf_ref[t2]
        p = _mm(w, x_ref.at[0][rows_t2(t2), :])
        q = _mm(w, x_ref.at[1][rows_t2(t2), :])
        dst = pl.ds(pl.multiple_of(t2 * FFT_N1, FFT_N1), FFT_N1)
        sr_ref[dst, :] = p[:FFT_N1] - q[FFT_N1:]
        si_ref[dst, :] = p[FFT_N1:] + q[:FFT_N1]
        return carry

    lax.fori_loop(0, FFT_N2, stage_a, 0, unroll=FFT_UNROLL)
    wm = wm_ref[...]

    def per_k1(k1, carry):
        sel = pl.ds(k1, FFT_N2, stride=FFT_N1)
        p = _mm(wm, sr_ref[sel, :])
        q = _mm(wm, si_ref[sel, :])
        xr = p[:FFT_N2] - q[FFT_N2:]
        xi = p[FFT_N2:] + q[:FFT_N2]
        hsel = pl.ds(pl.multiple_of(k1 * FFT_N2, FFT_N2), FFT_N2)
        hr = hr_ref[hsel, :]
        hi = hi_ref[hsel, :]
        p = _mm(wm, xr * hr - xi * hi)
        q = _mm(wm, xr * hi + xi * hr)
        sr_ref[sel, :] = p[:FFT_N2] + q[FFT_N2:]
        si_ref[sel, :] = q[:FFT_N2] - p[FFT_N2:]
        return carry

    lax.fori_loop(0, FFT_N1, per_k1, 0, unroll=FFT_UNROLL)
    bias = b_ref[...]

    def stage_c(t2, carry):
        w = wi_ref[t2]
        src = pl.ds(pl.multiple_of(t2 * FFT_N1, FFT_N1), FFT_N1)
        p = _mm(w, sr_ref[src, :])
        q = _mm(w, si_ref[src, :])
        rows = rows_t2(t2)
        o_ref.at[0][rows, :] = g_ref.at[0][rows, :] * (p[:CONV_T1] - q[CONV_T1:] + x_ref.at[0][rows, :] * bias)
        o_ref.at[1][rows, :] = g_ref.at[1][rows, :] * (q[:CONV_T1] + p[CONV_T1:] + x_ref.at[1][rows, :] * bias)
        return carry

    lax.fori_loop(0, FFT_N2, stage_c, 0, unroll=FFT_UNROLL)


def _split_bf16(w):
    hi = w.astype(jnp.bfloat16)
    return hi, (w - hi.astype(jnp.float32)).astype(jnp.bfloat16)


def _mm3(w_hi, w_lo, x):
    x_hi, x_lo = _split_bf16(x)
    f32 = jnp.float32
    return (jnp.dot(w_hi, x_hi, preferred_element_type=f32) + jnp.dot(w_lo, x_hi, preferred_element_type=f32)
            + jnp.dot(w_hi, x_lo, preferred_element_type=f32))


def _filter_spectrum_kernel(a_ref, b_ref, wfh_ref, wfl_ref, wmh_ref, wml_ref, hr_ref, hi_ref,
                            ar_ref, ai_ref, br_ref, bi_ref):
    def rows_t2(t2):
        return pl.ds(t2, CONV_T1, stride=FFT_N2)

    def stage_a(t2, carry):
        wh, wl = wfh_ref[t2], wfl_ref[t2]
        p = _mm3(wh, wl, a_ref[rows_t2(t2), :])
        q = _mm3(wh, wl, b_ref[rows_t2(t2), :])
        dst = pl.ds(pl.multiple_of(t2 * FFT_N1, FFT_N1), FFT_N1)
        ar_ref[dst, :] = p[:FFT_N1]
        ai_ref[dst, :] = p[FFT_N1:]
        br_ref[dst, :] = q[:FFT_N1]
        bi_ref[dst, :] = q[FFT_N1:]
        return carry

    lax.fori_loop(0, FFT_N2, stage_a, 0, unroll=FFT_UNROLL)
    wmh, wml = wmh_ref[...], wml_ref[...]
    a = a_ref[...]
    b = b_ref[...]
    energy = jnp.sum(a * a, axis=0, keepdims=True) + jnp.sum(b * b, axis=0, keepdims=True)
    scale = lax.rsqrt(energy + HYENA_NORM_EPS)

    def per_k1(k1, carry):
        sel = pl.ds(k1, FFT_N2, stride=FFT_N1)
        pa, qa = _mm3(wmh, wml, ar_ref[sel, :]), _mm3(wmh, wml, ai_ref[sel, :])
        pb, qb = _mm3(wmh, wml, br_ref[sel, :]), _mm3(wmh, wml, bi_ref[sel, :])
        dst = pl.ds(pl.multiple_of(k1 * FFT_N2, FFT_N2), FFT_N2)
        a_re, a_im = pa[:FFT_N2] - qa[FFT_N2:], pa[FFT_N2:] + qa[:FFT_N2]
        b_re, b_im = pb[:FFT_N2] - qb[FFT_N2:], pb[FFT_N2:] + qb[:FFT_N2]
        hr_ref[dst, :] = (a_re + b_re) * scale
        hi_ref[dst, :] = (a_im - b_im) * scale
        return carry

    lax.fori_loop(0, FFT_N1, per_k1, 0, unroll=FFT_UNROLL)


def filter_spectrum(a, b, tables):
    L, C = a.shape
    wf, wm, _ = tables
    sig = pl.BlockSpec((L, VREG_LANES), lambda c: (0, c))
    out = pl.BlockSpec((FFT_N, VREG_LANES), lambda c: (0, c))
    once = pl.Buffered(1)
    plane = pltpu.VMEM((FFT_N, VREG_LANES), jnp.float32)
    return pl.pallas_call(
        _filter_spectrum_kernel,
        grid=(C // VREG_LANES,),
        in_specs=[sig, sig] + [pl.BlockSpec(wf.shape, lambda c: (0, 0, 0), pipeline_mode=once)] * 2
        + [pl.BlockSpec(wm.shape, lambda c: (0, 0), pipeline_mode=once)] * 2,
        out_specs=[out, out],
        out_shape=[jax.ShapeDtypeStruct((FFT_N, C), jnp.float32)] * 2,
        scratch_shapes=[plane] * 4,
        compiler_params=pltpu.CompilerParams(dimension_semantics=("arbitrary",), vmem_limit_bytes=FFT_VMEM_LIMIT),
        name="filter_spectrum",
    )(a, b, *_split_bf16(wf), *_split_bf16(wm))


def long_conv_gated(u, u_col, gate, gate_col, h_re, h_im, h_col, bias, tables):
    B, L, _ = u.shape
    C = bias.shape[0]
    assert L == CONV_L and B % 2 == 0 and C % VREG_LANES == 0
    assert u_col % VREG_LANES == 0 and gate_col % VREG_LANES == 0 and h_col % VREG_LANES == 0
    consts = [t.astype(jnp.bfloat16) for t in tables]
    sig_spec = pl.BlockSpec((2, L, VREG_LANES), lambda c, b: (b, 0, c))
    u_spec = pl.BlockSpec((2, L, VREG_LANES), lambda c, b: (b, 0, c + u_col // VREG_LANES))
    gate_spec = pl.BlockSpec((2, L, VREG_LANES), lambda c, b: (b, 0, c + gate_col // VREG_LANES))
    once = pl.Buffered(1)
    spec_h = pl.BlockSpec((FFT_N, VREG_LANES), lambda c, b: (0, c + h_col // VREG_LANES), pipeline_mode=once)
    const_specs = [pl.BlockSpec(t.shape, functools.partial(lambda nd, c, b: (0,) * nd, t.ndim), pipeline_mode=once)
                   for t in consts]
    return pl.pallas_call(
        _long_conv_kernel,
        grid=(C // VREG_LANES, B // 2),
        in_specs=[u_spec, gate_spec, spec_h, spec_h, pl.BlockSpec((1, VREG_LANES), lambda c, b: (0, c))] + const_specs,
        out_specs=sig_spec,
        out_shape=jax.ShapeDtypeStruct((B, L, C), jnp.float32),
        scratch_shapes=[pltpu.VMEM((FFT_N, VREG_LANES), jnp.float32), pltpu.VMEM((FFT_N, VREG_LANES), jnp.float32)],
        compiler_params=pltpu.CompilerParams(dimension_semantics=("arbitrary", "arbitrary"),
                                             vmem_limit_bytes=FFT_VMEM_LIMIT),
        name="long_conv",
    )(u, gate, h_re, h_im, bias.reshape(1, C), *consts)


def hyena_layer(xt, mods, n_batch, w_in, conv_w, conv_b, w1, b1, w2, b2, w3, freq, fbias, w_out, ln_g, ln_b):
    sh1, sc1, g1 = mods
    T = xt.shape[0]
    L = T // n_batch
    tables = _dft_tables()
    h_re, h_im = filter_spectrum(*hyena_filters(L, w1, b1, w2, b2, w3, freq), tables)
    u = mod_matmul_conv(xt, sc1, sh1, w_in, conv_w[None], conv_b, n_batch, 1, L, False).reshape(n_batch, L, 3 * D_MODEL)
    zz = long_conv_gated(u, 2 * D_MODEL, u, 0, h_re, h_im, 0, fbias[0], tables)
    zz = long_conv_gated(zz, 0, u, D_MODEL, h_re, h_im, D_MODEL, fbias[1], tables)
    return proj_ln(zz.reshape(T, D_MODEL), w_out, xt, g1, ln_g, ln_b)


D_TILES = D_MODEL // VREG_LANES
PEER_SLOTS = PEER_HEADS * PEER_TOPK
PEER_TB = 256
PEER_SUB = 64
PEER_VMEM_LIMIT = 58 * 1024 * 1024
HI_MASK = -65536


def _pack_rows_bf16(tab):
    n, d = tab.shape
    bits = lax.bitcast_convert_type(tab.astype(jnp.bfloat16), jnp.uint16).astype(jnp.uint32)
    packed = (bits[n // 2:] << 16) | bits[:n // 2]
    return lax.bitcast_convert_type(packed, jnp.int32).reshape(n // 2 * (d // VREG_LANES), VREG_LANES)


def _unpack_row(tab_ref, off, shift):
    word = tab_ref[pl.ds(pl.multiple_of(off, VREG_SUBLANES), VREG_SUBLANES), :]
    return pltpu.bitcast((word << shift) & HI_MASK, jnp.float32)


def _sublane_sums(tiles, sub):
    for h in (1, 2, 4):
        keep = (sub & h) == 0
        nxt = []
        for a, b in zip(tiles[0::2], tiles[1::2]):
            nxt.append(jnp.where(keep, a, b) + pltpu.roll(jnp.where(keep, b, a), h, 0))
        tiles = nxt
    return tiles[0]


def _splat_tile(row):
    return jnp.broadcast_to(row, (PEER_SLOTS, VREG_LANES)).T


def _splat_small_ints_tile(row, eye, ones):
    diag = jnp.where(eye, row.astype(jnp.float32), 0.0).astype(jnp.bfloat16)
    return jnp.dot(diag, ones, preferred_element_type=jnp.float32).astype(jnp.int32)


def _splat_consts():
    eye = (lax.broadcasted_iota(jnp.int32, (PEER_SLOTS, PEER_SLOTS), 0)
           == lax.broadcasted_iota(jnp.int32, (PEER_SLOTS, PEER_SLOTS), 1))
    return eye, jnp.ones((PEER_SLOTS, VREG_LANES), jnp.bfloat16)


def _splat_token(t0, t, shift_ref, shift_s, w_ref, w_s, eye, ones):
    dst = pl.ds(pl.multiple_of(t * PEER_SLOTS, PEER_SLOTS), PEER_SLOTS)
    shift_s[dst, :] = _splat_small_ints_tile(shift_ref[pl.ds(t0 + t, 1), :], eye, ones)
    if w_ref is not None:
        w_s[dst, :] = _splat_tile(w_ref[pl.ds(t0 + t, 1), :])


def _splat_block(t0, shift_ref, shift_s, w_ref=None, w_s=None):
    eye, ones = _splat_consts()

    def token(t, carry):
        _splat_token(t0, t, shift_ref, shift_s, w_ref, w_s, eye, ones)
        return carry
    lax.fori_loop(0, PEER_SUB, token, 0, unroll=8)


def _tile_rows(k):
    return pl.ds(k, PEER_SUB, stride=D_TILES)


def _peer_act_kernel(row_ref, shift_ref, x_ref, sc_ref, sh_ref, gate_ref, tab_ref, w_ref,
                     part_ref, shift_a, shift_b, xt_ref):
    sub = lax.broadcasted_iota(jnp.int32, (VREG_SUBLANES, VREG_LANES), 0)
    contract_lanes = (((1,), (1,)), ((), ()))
    ones = jnp.ones((VREG_SUBLANES, VREG_LANES), jnp.bfloat16)
    eye, splat_ones = _splat_consts()
    bufs = (shift_a, shift_b)
    starts = list(range(0, PEER_TB, PEER_SUB))
    _splat_block(0, shift_ref, bufs[0])
    for j, t0 in enumerate(starts):
        shift_s, shift_next = bufs[j % 2], bufs[(j + 1) % 2]
        t_next = starts[j + 1] if j + 1 < len(starts) else None
        h = x_ref[t0:t0 + PEER_SUB, :] * (1.0 + sc_ref[0]) + sh_ref[0]
        for k in range(D_TILES):
            xt_ref[_tile_rows(k), :] = h[:, k * VREG_LANES:(k + 1) * VREG_LANES]

        def token(t, carry, t0=t0, shift_s=shift_s, shift_next=shift_next, t_next=t_next):
            if t_next is not None:
                _splat_token(t_next, t, shift_ref, shift_next, None, None, eye, splat_ones)
            xt = xt_ref[pl.ds(pl.multiple_of(t * D_TILES, D_TILES), D_TILES), :]
            base = pl.multiple_of(t * PEER_SLOTS, PEER_SLOTS)
            tok_rows = row_ref.at[0, 0, pl.ds(t0 * PEER_SLOTS + base, PEER_SLOTS)]
            for g in range(PEER_SLOTS // VREG_SUBLANES):
                prods = []
                for k in range(VREG_SUBLANES):
                    s = g * VREG_SUBLANES + k
                    prods.append(_unpack_row(tab_ref, tok_rows[s], shift_s[pl.ds(base + s, 1), :]) * xt)
                part_ref[pl.ds(base + g * VREG_SUBLANES, VREG_SUBLANES), :] = _sublane_sums(prods, sub)
            return carry

        lax.fori_loop(0, PEER_SUB, token, 0)
        part = part_ref[...]
        part_hi = part.astype(jnp.bfloat16)
        part_lo = (part - part_hi.astype(jnp.float32)).astype(jnp.bfloat16)
        act = (lax.dot_general(ones, part_hi, contract_lanes, preferred_element_type=jnp.float32)
               + lax.dot_general(ones, part_lo, contract_lanes, preferred_element_type=jnp.float32))[0:1]
        gelu = 0.5 * act * (1.0 + lax.erf(act * (2.0 ** -0.5)))
        lanes = slice(t0 * PEER_SLOTS, (t0 + PEER_SUB) * PEER_SLOTS)
        w_ref[0, :, lanes] = gate_ref[0, :, lanes] * gelu


def _peer_out_kernel(row_ref, shift_ref, w_ref, tab_ref, x_ref, g_ref, lng_ref, lnb_ref, o_ref,
                     shift_a, shift_b, w_a, w_b, yt_ref):
    n_acc = 2
    eye, splat_ones = _splat_consts()
    shift_bufs, w_bufs = (shift_a, shift_b), (w_a, w_b)
    starts = list(range(0, PEER_TB, PEER_SUB))
    _splat_block(0, shift_ref, shift_bufs[0], w_ref, w_bufs[0])
    for j, t0 in enumerate(starts):
        shift_s, w_s = shift_bufs[j % 2], w_bufs[j % 2]
        shift_next, w_next = shift_bufs[(j + 1) % 2], w_bufs[(j + 1) % 2]
        t_next = starts[j + 1] if j + 1 < len(starts) else None

        def pair(i, carry, t0=t0, shift_s=shift_s, w_s=w_s, shift_next=shift_next, w_next=w_next, t_next=t_next):
            if t_next is not None:
                for k in range(2):
                    _splat_token(t_next, 2 * i + k, shift_ref, shift_next, w_ref, w_next, eye, splat_ones)
            bases = [pl.multiple_of((2 * i + k) * PEER_SLOTS, PEER_SLOTS) for k in range(2)]
            rows = [row_ref.at[0, 0, pl.ds(t0 * PEER_SLOTS + b_, PEER_SLOTS)] for b_ in bases]
            accs = [[jnp.zeros((VREG_SUBLANES, VREG_LANES), jnp.float32) for _ in range(n_acc)] for _ in range(2)]
            for s in range(PEER_SLOTS):
                for k in range(2):
                    val = _unpack_row(tab_ref, rows[k][s], shift_s[pl.ds(bases[k] + s, 1), :])
                    accs[k][s % n_acc] = accs[k][s % n_acc] + w_s[pl.ds(bases[k] + s, 1), :] * val
            for k in range(2):
                yt_ref[pl.ds(pl.multiple_of((2 * i + k) * D_TILES, D_TILES), D_TILES), :] = accs[k][0] + accs[k][1]
            return carry

        lax.fori_loop(0, PEER_SUB // 2, pair, 0)
        y = jnp.concatenate([yt_ref[_tile_rows(k), :] for k in range(D_TILES)], axis=1)
        rows_out = slice(t0, t0 + PEER_SUB)
        o_ref[rows_out, :] = _layer_norm_rows(DEEPNORM_ALPHA * x_ref[rows_out, :] + g_ref[0] * y,
                                              lng_ref[...], lnb_ref[...])


def peer_experts_ln(x, mods, offs, shifts, gate, u_tab, v_tab, ln_g, ln_b):
    sh, sc, g = mods
    T = x.shape[0]
    nblk = T // PEER_TB
    per_blk = PEER_TB * PEER_SLOTS
    per_mod = T // sc.shape[0] // PEER_TB
    offs = offs.reshape(nblk, 1, per_blk)
    smem_spec = pl.BlockSpec((1, 1, per_blk), lambda i: (i, 0, 0), memory_space=pltpu.SMEM)
    lane_spec = pl.BlockSpec((1, 1, per_blk), lambda i: (i, 0, 0))
    slot_spec = pl.BlockSpec((PEER_TB, PEER_SLOTS), lambda i: (i, 0))
    tok_spec = pl.BlockSpec((PEER_TB, D_MODEL), lambda i: (i, 0))
    mod_spec = pl.BlockSpec((1, 1, D_MODEL), lambda i: (i // per_mod, 0, 0))
    vec_spec = pl.BlockSpec((1, D_MODEL), lambda i: (0, 0))
    tab_spec = pl.BlockSpec(memory_space=pltpu.VMEM)
    params = pltpu.CompilerParams(dimension_semantics=("arbitrary",), vmem_limit_bytes=PEER_VMEM_LIMIT)
    splat_i32 = pltpu.VMEM((PEER_SUB * PEER_SLOTS, VREG_LANES), jnp.int32)
    splat_f32 = pltpu.VMEM((PEER_SUB * PEER_SLOTS, VREG_LANES), jnp.float32)
    tok_tiles = pltpu.VMEM((PEER_SUB * D_TILES, VREG_LANES), jnp.float32)
    w = pl.pallas_call(
        _peer_act_kernel,
        grid=(nblk,),
        in_specs=[smem_spec, slot_spec, tok_spec, mod_spec, mod_spec, lane_spec, tab_spec],
        out_specs=lane_spec,
        out_shape=jax.ShapeDtypeStruct((nblk, 1, per_blk), jnp.float32),
        scratch_shapes=[splat_f32, splat_i32, splat_i32, tok_tiles],
        compiler_params=params,
        name="peer_act",
    )(offs, shifts, x, sc, sh, gate.reshape(nblk, 1, per_blk), _pack_rows_bf16(u_tab))
    return pl.pallas_call(
        _peer_out_kernel,
        grid=(nblk,),
        in_specs=[smem_spec, slot_spec, slot_spec, tab_spec, tok_spec, mod_spec, vec_spec, vec_spec],
        out_specs=tok_spec,
        out_shape=jax.ShapeDtypeStruct((T, D_MODEL), jnp.float32),
        scratch_shapes=[splat_i32, splat_i32, splat_f32, splat_f32, tok_tiles],
        compiler_params=params,
        name="peer_out",
    )(offs, shifts, w.reshape(T, PEER_SLOTS), _pack_rows_bf16(v_tab), x, g,
      ln_g.reshape(1, D_MODEL), ln_b.reshape(1, D_MODEL))


def _top_rows(s, k):
    n = s.shape[0]
    row = lax.broadcasted_iota(jnp.int32, s.shape, 0).astype(jnp.float32)
    vals, idxs = [], []
    for _ in range(k):
        m = jnp.max(s, axis=0, keepdims=True)
        i = jnp.min(jnp.where(s == m, row, float(n)), axis=0, keepdims=True)
        vals.append(m)
        idxs.append(i)
        s = jnp.where(row == i, -jnp.inf, s)
    return jnp.concatenate(vals, axis=0), jnp.concatenate(idxs, axis=0).astype(jnp.int32)


def _pick_rows(table, which):
    out = jnp.zeros(which.shape, table.dtype)
    for a in range(table.shape[0]):
        out = jnp.where(which == a, table[a:a + 1, :], out)
    return out


PAIR_COUNTS = tuple(PEER_TOPK // (a + 1) for a in range(PEER_TOPK))
PAIR_STARTS = tuple(sum(PAIR_COUNTS[:a]) for a in range(PEER_TOPK))
N_PAIRS = sum(PAIR_COUNTS)
N_PAIRS_PADDED = -(-N_PAIRS // VREG_SUBLANES) * VREG_SUBLANES
SEL_TB = 512
PEER_PACKED_ROWS = PEER_NKEYS * PEER_NKEYS // 2


def _peer_select_kernel(x_ref, sc_ref, sh_ref, wq_ref, k1_ref, k2_ref, off_ref, shift_ref, gate_ref,
                        off_s, shift_s, gate_s):
    half = PEER_DK // 2
    h = (x_ref[...] * (1.0 + sc_ref[0]) + sh_ref[0]).astype(jnp.bfloat16)
    q = jnp.dot(h, wq_ref[...], preferred_element_type=jnp.float32).astype(jnp.bfloat16)
    contract_last = (((1,), (1,)), ((), ()))
    s1 = lax.dot_general(k1_ref[0], q[:, :half], contract_last, preferred_element_type=jnp.float32)
    s2 = lax.dot_general(k2_ref[0], q[:, half:], contract_last, preferred_element_type=jnp.float32)
    v1, i1 = _top_rows(s1, PEER_TOPK)
    v2, i2 = _top_rows(s2, PEER_TOPK)
    pairs = [v1[a:a + 1, :] + v2[:PAIR_COUNTS[a], :] for a in range(PEER_TOPK)]
    pairs.append(jnp.full((N_PAIRS_PADDED - N_PAIRS, v1.shape[1]), -jnp.inf, jnp.float32))
    top_s, pos = _top_rows(jnp.concatenate(pairs, axis=0), PEER_TOPK)
    a_sel = jnp.zeros(pos.shape, jnp.int32)
    start = jnp.zeros(pos.shape, jnp.int32)
    for a in range(1, PEER_TOPK):
        later = pos >= PAIR_STARTS[a]
        a_sel = jnp.where(later, a, a_sel)
        start = jnp.where(later, PAIR_STARTS[a], start)
    expert = _pick_rows(i1, a_sel) * PEER_NKEYS + _pick_rows(i2, pos - start)
    e = jnp.exp(top_s - top_s[0:1, :])
    head = pl.program_id(1)
    rows = pl.ds(pl.multiple_of(head * PEER_TOPK, PEER_TOPK), PEER_TOPK)
    gate_s[rows, :] = e / jnp.sum(e, axis=0, keepdims=True)
    off_s[rows, :] = (expert & (PEER_PACKED_ROWS - 1)) * VREG_SUBLANES
    shift_s[rows, :] = jnp.where(expert >= PEER_PACKED_ROWS, 0, 16)

    @pl.when(head == PEER_HEADS - 1)
    def _():
        gate_ref[...] = gate_s[...].T
        off_ref[...] = off_s[...].astype(jnp.float32).T.astype(jnp.int32)
        shift_ref[...] = shift_s[...].astype(jnp.float32).T.astype(jnp.int32)


def peer_select(x, sc, sh, wq, k1, k2):
    T = x.shape[0]
    per_seq = T // sc.shape[0] // SEL_TB
    bf16 = jnp.bfloat16
    tok_spec = pl.BlockSpec((SEL_TB, D_MODEL), lambda i, h: (i, 0))
    mod_spec = pl.BlockSpec((1, 1, D_MODEL), lambda i, h: (i // per_seq, 0, 0))
    wq_spec = pl.BlockSpec((D_MODEL, PEER_DK), lambda i, h: (0, h))
    key_spec = pl.BlockSpec((1, PEER_NKEYS, PEER_DK // 2), lambda i, h: (h, 0, 0))
    out_spec = pl.BlockSpec((SEL_TB, PEER_SLOTS), lambda i, h: (i, 0))
    return pl.pallas_call(
        _peer_select_kernel,
        grid=(T // SEL_TB, PEER_HEADS),
        in_specs=[tok_spec, mod_spec, mod_spec, wq_spec, key_spec, key_spec],
        out_specs=[out_spec, out_spec, out_spec],
        out_shape=[jax.ShapeDtypeStruct((T, PEER_SLOTS), jnp.int32), jax.ShapeDtypeStruct((T, PEER_SLOTS), jnp.int32),
                   jax.ShapeDtypeStruct((T, PEER_SLOTS), jnp.float32)],
        scratch_shapes=[pltpu.VMEM((PEER_SLOTS, SEL_TB), jnp.int32), pltpu.VMEM((PEER_SLOTS, SEL_TB), jnp.int32),
                        pltpu.VMEM((PEER_SLOTS, SEL_TB), jnp.float32)],
        compiler_params=pltpu.CompilerParams(dimension_semantics=("arbitrary", "arbitrary")),
        name="peer_select",
    )(x, sc, sh, wq.astype(bf16), k1.astype(bf16), k2.astype(bf16))


def peer_layer(xt, mods, wq, k1, k2, u_tab, v_tab, ln_g, ln_b):
    sh2, sc2, _ = mods
    off, shift, gate = peer_select(xt, sc2, sh2, wq, k1, k2)
    return peer_experts_ln(xt, mods, off, shift, gate, u_tab, v_tab, ln_g, ln_b)


def kernel(x, c, ctx, c_ctx, ada_w, ada_b, ln_g, ln_b,
           m_w_in, m_conv_w, m_conv_b, m_a_log, m_dt_bias, m_d, m_norm_w, m_w_out,
           h_w_in, h_conv_w, h_conv_b, h_f_w1, h_f_b1, h_f_w2, h_f_b2, h_f_w3, h_freq, h_bias, h_w_out,
           p_wq, p_k1, p_k2, p_u, p_v):
    n_batch, seq, _ = x.shape
    rows = seq // GRID_W
    s_lat = jax.nn.silu(c)
    s_ctx = jax.nn.silu(c_ctx)
    xt = x.reshape(n_batch * seq, D_MODEL)
    ctx_t = ctx.reshape(-1, D_MODEL)
    for i in range(DEPTH):
        kind, k = i % N_MIXERS, i // N_MIXERS
        sh1, sc1, g1, sh2, sc2, g2 = jnp.split((s_lat @ ada_w[i] + ada_b[i])[:, None, :], 6, axis=-1)
        if kind == 0:
            csh1, csc1 = jnp.split((s_ctx @ ada_w[i] + ada_b[i])[None, None, :], 6, axis=-1)[:2]
            xt = mamba_layer(xt, ctx_t, (sh1, sc1, g1), (csh1, csc1), n_batch, rows, m_w_in[k], m_conv_w[k],
                             m_conv_b[k], m_a_log[k], m_dt_bias[k], m_d[k], m_norm_w[k], m_w_out[k],
                             ln_g[i, 0], ln_b[i, 0])
        else:
            xt = hyena_layer(xt, (sh1, sc1, g1), n_batch, h_w_in[k], h_conv_w[k], h_conv_b[k], h_f_w1[k], h_f_b1[k],
                             h_f_w2[k], h_f_b2[k], h_f_w3[k], h_freq[k], h_bias[k], h_w_out[k], ln_g[i, 0], ln_b[i, 0])
        xt = peer_layer(xt, (sh2, sc2, g2), p_wq[i], p_k1[i], p_k2[i], p_u[i], p_v[i], ln_g[i, 1], ln_b[i, 1])
    return xt.reshape(x.shape)
```

```python
import functools
import math

import jax
import jax.numpy as jnp
from jax import lax
from jax.experimental import pallas as pl
from jax.experimental.pallas import tpu as pltpu

D_MODEL = 1024
DEPTH = 2
GRID_W = 64
N_MIXERS = 2
DEEPNORM_ALPHA = (2 * DEPTH) ** 0.25
LN_EPS = 1e-5
RMS_EPS = 1e-5

SSM_INNER = 2 * D_MODEL
SSM_HEAD_DIM = 64
SSM_HEADS = SSM_INNER // SSM_HEAD_DIM
SSM_GROUPS = 4
SSM_STATE = 128
SSM_CHUNK = 128
SSM_CONV = 3
SSM_CONV_DIM = SSM_INNER + 2 * SSM_GROUPS * SSM_STATE
SSM_GROUP_HEADS = SSM_HEADS // SSM_GROUPS
SSM_GROUP_DIM = SSM_GROUP_HEADS * SSM_HEAD_DIM

HYENA_ORDER = 2
HYENA_EMB = 33
HYENA_INNER_MLPS = 2
HYENA_STEEP_DECAY = 0.3
HYENA_GENTLE_DECAY = 1.5
HYENA_TARGET = 1e-2
HYENA_NORM_EPS = 1e-6

PEER_HEADS = 8
PEER_NKEYS = 128
PEER_DK = 256
PEER_TOPK = 16

VREG_SUBLANES = 8
VREG_LANES = 128
DENSE_VMEM_LIMIT = 48 * 1024 * 1024


def _mod_matmul_kernel(x_ref, sc_ref, sh_ref, w_ref, o_ref):
    h = (x_ref[...] * (1.0 + sc_ref[0]) + sh_ref[0]).astype(jnp.bfloat16)
    o_ref[...] = jnp.dot(h, w_ref[...], preferred_element_type=jnp.float32)


def mod_matmul(x, sc, sh, w, tb, tn):
    T, D = x.shape
    N = w.shape[1]
    assert T % tb == 0 and N % tn == 0 and (T // sc.shape[0]) % tb == 0
    per_mod = T // sc.shape[0] // tb
    mod_spec = pl.BlockSpec((1, 1, D), lambda i, j: (i // per_mod, 0, 0))
    return pl.pallas_call(
        _mod_matmul_kernel,
        grid=(T // tb, N // tn),
        in_specs=[pl.BlockSpec((tb, D), lambda i, j: (i, 0)), mod_spec, mod_spec,
                  pl.BlockSpec((D, tn), lambda i, j: (0, j))],
        out_specs=pl.BlockSpec((tb, tn), lambda i, j: (i, j)),
        out_shape=jax.ShapeDtypeStruct((T, N), jnp.float32),
        compiler_params=pltpu.CompilerParams(dimension_semantics=("arbitrary", "arbitrary"),
                                             vmem_limit_bytes=DENSE_VMEM_LIMIT),
        name="mod_matmul",
    )(x, sc, sh, w.astype(jnp.bfloat16))


def _layer_norm_rows(v, g, b):
    mu = jnp.mean(v, axis=-1, keepdims=True)
    vc = v - mu
    var = jnp.mean(vc * vc, axis=-1, keepdims=True)
    return vc * lax.rsqrt(var + LN_EPS) * g + b


def _proj_ln_kernel(a_ref, w_ref, x_ref, gate_ref, g_ref, b_ref, o_ref):
    y = jnp.dot(a_ref[...].astype(jnp.bfloat16), w_ref[...], preferred_element_type=jnp.float32)
    o_ref[...] = _layer_norm_rows(DEEPNORM_ALPHA * x_ref[...] + gate_ref[0] * y, g_ref[...], b_ref[...])


def _row_specs(tb, per_mod):
    tok = pl.BlockSpec((tb, D_MODEL), lambda i: (i, 0))
    mod = pl.BlockSpec((1, 1, D_MODEL), lambda i: (i // per_mod, 0, 0))
    vec = pl.BlockSpec((1, D_MODEL), lambda i: (0, 0))
    return tok, mod, vec


def proj_ln(a, w, x, gate, ln_g, ln_b, tb=512):
    T, K = a.shape
    per_mod = T // gate.shape[0] // tb
    tok, mod, vec = _row_specs(tb, per_mod)
    return pl.pallas_call(
        _proj_ln_kernel,
        grid=(T // tb,),
        in_specs=[pl.BlockSpec((tb, K), lambda i: (i, 0)), pl.BlockSpec((K, D_MODEL), lambda i: (0, 0)),
                  tok, mod, vec, vec],
        out_specs=tok,
        out_shape=jax.ShapeDtypeStruct((T, D_MODEL), jnp.float32),
        compiler_params=pltpu.CompilerParams(dimension_semantics=("arbitrary",), vmem_limit_bytes=DENSE_VMEM_LIMIT),
        name="proj_ln",
    )(a, w.astype(jnp.bfloat16), x, gate, ln_g.reshape(1, D_MODEL), ln_b.reshape(1, D_MODEL))


def _mamba_out_kernel(y_ref, xs_ref, z_ref, dsk_ref, nw_ref, w_ref, x_ref, gate_ref, g_ref, b_ref, o_ref):
    z = z_ref[...]
    u = (y_ref[0] + y_ref[1] + xs_ref[...] * dsk_ref[...]) * (z * jax.nn.sigmoid(z))
    gw = SSM_INNER // SSM_GROUPS
    parts = []
    for g in range(SSM_GROUPS):
        ug = u[:, g * gw:(g + 1) * gw]
        parts.append(ug * lax.rsqrt(jnp.mean(ug * ug, axis=-1, keepdims=True) + RMS_EPS))
    a = (jnp.concatenate(parts, axis=1) * nw_ref[...]).astype(jnp.bfloat16)
    y = jnp.dot(a, w_ref[...], preferred_element_type=jnp.float32)
    o_ref[...] = _layer_norm_rows(DEEPNORM_ALPHA * x_ref[...] + gate_ref[0] * y, g_ref[...], b_ref[...])


def mamba_out(y2, xbc, z, d_skip, norm_w, w_out, x, gate, ln_g, ln_b, tb=256):
    T = x.shape[0]
    tok, mod, vec = _row_specs(tb, T // gate.shape[0] // tb)
    inner = pl.BlockSpec((tb, SSM_INNER), lambda i: (i, 0))
    ivec = pl.BlockSpec((1, SSM_INNER), lambda i: (0, 0))
    dsk = jnp.repeat(d_skip, SSM_INNER // d_skip.shape[0]).reshape(1, SSM_INNER)
    return pl.pallas_call(
        _mamba_out_kernel,
        grid=(T // tb,),
        in_specs=[pl.BlockSpec((2, tb, SSM_INNER), lambda i: (0, i, 0)), inner, inner, ivec, ivec,
                  pl.BlockSpec((SSM_INNER, D_MODEL), lambda i: (0, 0)), tok, mod, vec, vec],
        out_specs=tok,
        out_shape=jax.ShapeDtypeStruct((T, D_MODEL), jnp.float32),
        compiler_params=pltpu.CompilerParams(dimension_semantics=("arbitrary",), vmem_limit_bytes=DENSE_VMEM_LIMIT),
        name="mamba_out",
    )(y2, xbc, z, dsk, norm_w.reshape(1, SSM_INNER), w_out.astype(jnp.bfloat16), x, gate,
      ln_g.reshape(1, D_MODEL), ln_b.reshape(1, D_MODEL))


CONV_VMEM_LIMIT = 56 * 1024 * 1024


def _shift_rows(a, k):
    n = a.shape[0]
    return a if k == 0 else pltpu.roll(a, (-k) % n, 0)


def _mod_matmul_conv_kernel(x_ref, sc_ref, sh_ref, w_ref, cw_ref, cb_ref, o_ref, *, rows, cols, silu):
    h = (x_ref[...] * (1.0 + sc_ref[0]) + sh_ref[0]).astype(jnp.bfloat16)
    p = jnp.dot(h, w_ref[...], preferred_element_type=jnp.float32)
    L = p.shape[0]
    t = lax.broadcasted_iota(jnp.int32, p.shape, 0)
    col = t & (cols - 1)
    n_dr = cw_ref.shape[0]
    out = jnp.zeros_like(p)
    for j in range(3):
        dc = j - 1
        acc = None
        for i in range(n_dr):
            dr = i - n_dr // 2
            term = _shift_rows(p, dr * cols)
            if dr < 0:
                term = jnp.where(t >= -dr * cols, term, 0.0)
            elif dr > 0:
                term = jnp.where(t < L - dr * cols, term, 0.0)
            term = term * cw_ref[i, j:j + 1, :]
            acc = term if acc is None else acc + term
        acc = _shift_rows(acc, dc)
        if dc < 0:
            acc = jnp.where(col >= -dc, acc, 0.0)
        elif dc > 0:
            acc = jnp.where(col < cols - dc, acc, 0.0)
        out = out + acc
    out = out + cb_ref[...]
    o_ref[...] = out * jax.nn.sigmoid(out) if silu else out


def mod_matmul_conv(x, sc, sh, w, conv_w, conv_b, n_seq, rows, cols, silu, tn=256):
    T, D = x.shape
    N = w.shape[1]
    L = rows * cols
    assert T == n_seq * L and N % tn == 0 and cols & (cols - 1) == 0 and conv_w.shape[0] in (1, 3)
    per_mod = n_seq // sc.shape[0]
    mod_spec = pl.BlockSpec((1, 1, D), lambda b, j: (b // per_mod, 0, 0))
    return pl.pallas_call(
        functools.partial(_mod_matmul_conv_kernel, rows=rows, cols=cols, silu=silu),
        grid=(n_seq, N // tn),
        in_specs=[pl.BlockSpec((L, D), lambda b, j: (b, 0), pipeline_mode=pl.Buffered(1)), mod_spec, mod_spec,
                  pl.BlockSpec((D, tn), lambda b, j: (0, j)),
                  pl.BlockSpec((conv_w.shape[0], 3, tn), lambda b, j: (0, 0, j)),
                  pl.BlockSpec((1, tn), lambda b, j: (0, j))],
        out_specs=pl.BlockSpec((L, tn), lambda b, j: (b, j)),
        out_shape=jax.ShapeDtypeStruct((T, N), jnp.float32),
        compiler_params=pltpu.CompilerParams(dimension_semantics=("arbitrary", "arbitrary"),
                                             vmem_limit_bytes=CONV_VMEM_LIMIT),
        name="mod_matmul_conv",
    )(x, sc, sh, w.astype(jnp.bfloat16), conv_w, conv_b.reshape(1, N))


def _split3_bf16(a):
    f32, bf16 = jnp.float32, jnp.bfloat16
    p0 = a.astype(bf16)
    r = a - p0.astype(f32)
    p1 = r.astype(bf16)
    p2 = (r - p1.astype(f32)).astype(bf16)
    return p0, p1, p2


def _ssd_kernel(x_ref, b_ref, c_ref, dt_ref, an_ref, init_ref, *rest, want_y):
    if want_y:
        y_ref, fin_ref, state_ref = rest
    else:
        fin_ref, state_ref = rest
    f32, bf16 = jnp.float32, jnp.bfloat16
    Q, P, H, G, N = SSM_CHUNK, SSM_HEAD_DIM, SSM_HEADS, SSM_GROUPS, SSM_STATE
    R, GD = SSM_GROUP_HEADS, SSM_GROUP_DIM
    fwd = pl.program_id(1) == 0
    c = pl.program_id(2)

    @pl.when(c == 0)
    def _():
        state_ref[...] = init_ref[0, 0].reshape(SSM_INNER, N)

    contract_rows = (((0,), (0,)), ((), ()))
    contract_lanes = (((1,), (1,)), ((), ()))
    dt = dt_ref[0, 0]
    a = dt * an_ref[0]
    li = lax.broadcasted_iota(jnp.int32, (Q, Q), 0)
    si = lax.broadcasted_iota(jnp.int32, (Q, Q), 1)
    tri = jnp.where(fwd, li - si, si - li) >= 0
    tri_bf = jnp.where(tri, 1.0, 0.0).astype(bf16)
    pieces = _split3_bf16(a)
    a_cum = sum(jnp.dot(tri_bf, p, preferred_element_type=f32) for p in pieces)
    a_cum_t = sum(lax.dot_general(p, tri_bf, (((0,), (1,)), ((), ())), preferred_element_type=f32)
                  for p in pieces)
    total = jnp.sum(a, axis=0, keepdims=True)
    spread = (lax.broadcasted_iota(jnp.int32, (H, SSM_INNER), 1) // P
              == lax.broadcasted_iota(jnp.int32, (H, SSM_INNER), 0)).astype(bf16)

    def per_lane(v):
        return sum(jnp.dot(p, spread, preferred_element_type=f32) for p in _split3_bf16(v))

    xdt = x_ref[0] * per_lane(dt)
    xw = (xdt * per_lane(jnp.exp(total - a_cum))).astype(bf16)
    tot_col = jnp.where(fwd, a_cum_t[:, Q - 1:Q], a_cum_t[:, 0:1])
    keep = jnp.broadcast_to(jnp.exp(tot_col), (H, N))
    keep_rows = sum(lax.dot_general(spread, p, contract_rows, preferred_element_type=f32)
                    for p in _split3_bf16(keep))
    if want_y:
        xdt_bf = xdt.astype(bf16)
        out_scale = per_lane(jnp.exp(a_cum))
        lane = lax.broadcasted_iota(jnp.int32, (Q, 2 * P), 1)
    ys = []
    for g in range(G):
        cols = slice(g * GD, (g + 1) * GD)
        bm = b_ref[0, :, g * N:(g + 1) * N].astype(bf16)
        cm = c_ref[0, :, g * N:(g + 1) * N].astype(bf16)
        state = state_ref[cols, :]
        if want_y:
            cb = lax.dot_general(cm, bm, contract_lanes, preferred_element_type=f32)
            for j in range(R // 2):
                ms = []
                for h in (g * R + 2 * j, g * R + 2 * j + 1):
                    decay = jnp.where(tri, jnp.exp(a_cum[:, h:h + 1] - a_cum_t[h:h + 1, :]), 0.0)
                    ms.append((cb * decay).astype(bf16))
                xp = xdt_bf[:, g * GD + 2 * j * P:g * GD + 2 * (j + 1) * P]
                zero = jnp.zeros_like(xp)
                rhs = jnp.concatenate([jnp.where(lane < P, xp, zero), jnp.where(lane >= P, xp, zero)], axis=0)
                ys.append(jnp.dot(jnp.concatenate(ms, axis=1), rhs, preferred_element_type=f32))
            y_off = lax.dot_general(cm, state.astype(bf16), contract_lanes, preferred_element_type=f32)
            y_ref[0, 0, :, cols] = jnp.concatenate(ys[-(R // 2):], axis=1) + y_off * out_scale[:, cols]
        upd = lax.dot_general(xw[:, cols], bm, contract_rows, preferred_element_type=f32)
        state_ref[cols, :] = state * keep_rows[cols, :] + upd
    fin_ref[0, 0] = state_ref[...].reshape(H, P, N)


def ssd_scan_both(xbc, dt, a_neg, init, want_y):
    B, L, _ = xbc.shape
    Q, G, H, P, N = SSM_CHUNK, SSM_GROUPS, SSM_HEADS, SSM_HEAD_DIM, SSM_STATE
    nc = L // Q
    dt_d = dt.transpose(0, 2, 1, 3)
    an = a_neg.reshape(2, 1, H)
    bc_cols = G * N

    def chunk(d, c):
        return c + d * (nc - 1 - 2 * c)

    x_spec = pl.BlockSpec((1, Q, SSM_INNER), lambda b, d, c: (b, chunk(d, c), 0))
    b_spec = pl.BlockSpec((1, Q, bc_cols), lambda b, d, c: (b, chunk(d, c), SSM_INNER // bc_cols))
    c_spec = pl.BlockSpec((1, Q, bc_cols), lambda b, d, c: (b, chunk(d, c), SSM_INNER // bc_cols + 1))
    dt_spec = pl.BlockSpec((1, 1, Q, H), lambda b, d, c: (b, d, chunk(d, c), 0))
    an_spec = pl.BlockSpec((1, 1, H), lambda b, d, c: (d, 0, 0))
    st_spec = pl.BlockSpec((1, 1, H, P, N), lambda b, d, c: (b, d, 0, 0, 0))
    st_shape = jax.ShapeDtypeStruct((B, 2, H, P, N), jnp.float32)
    out_specs, out_shape = [st_spec], [st_shape]
    if want_y:
        out_specs = [pl.BlockSpec((1, 1, Q, SSM_INNER), lambda b, d, c: (d, b, chunk(d, c), 0))] + out_specs
        out_shape = [jax.ShapeDtypeStruct((2, B, L, SSM_INNER), jnp.float32)] + out_shape
    outs = pl.pallas_call(
        functools.partial(_ssd_kernel, want_y=want_y),
        grid=(B, 2, nc),
        in_specs=[x_spec, b_spec, c_spec, dt_spec, an_spec, st_spec],
        out_specs=out_specs,
        out_shape=out_shape,
        scratch_shapes=[pltpu.VMEM((SSM_INNER, N), jnp.float32)],
        compiler_params=pltpu.CompilerParams(dimension_semantics=("arbitrary",) * 3),
        name="ssd_scan_y" if want_y else "ssd_scan_state",
    )(xbc, xbc, xbc, dt_d, an, init)
    if want_y:
        return outs[0], outs[1]
    return None, outs[0]


def mamba_layer(xt, ctx_t, mods, cmods, n_batch, rows, w_in, conv_w, conv_b, a_log, dt_bias, d_skip, norm_w, w_out,
                ln_g, ln_b):
    f32 = jnp.float32
    sh1, sc1, g1 = mods
    csh1, csc1 = cmods
    a_neg = -jnp.exp(a_log.astype(f32))
    w_xbc = w_in[:, SSM_INNER:SSM_INNER + SSM_CONV_DIM]
    w_dt = w_in[:, SSM_INNER + SSM_CONV_DIM:]

    def softplus_dt(dt_raw, length):
        return jax.nn.softplus(dt_raw.reshape(n_batch, length, 2, SSM_HEADS) + dt_bias.astype(f32))

    lc = ctx_t.shape[0] // n_batch
    xbc_c = mod_matmul_conv(ctx_t, csc1, csh1, w_xbc, conv_w[1:2], conv_b, n_batch, 1, lc, True)
    dt_c = softplus_dt(mod_matmul(ctx_t, csc1, csh1, w_dt, 512, 2 * SSM_HEADS), lc)
    zeros = jnp.zeros((n_batch, 2, SSM_HEADS, SSM_HEAD_DIM, SSM_STATE), f32)
    _, states = ssd_scan_both(xbc_c.reshape(n_batch, lc, SSM_CONV_DIM), dt_c, a_neg, zeros, False)
    T = xt.shape[0]
    L = T // n_batch
    z = mod_matmul(xt, sc1, sh1, w_in[:, :SSM_INNER], 1024, 1024)
    xbc = mod_matmul_conv(xt, sc1, sh1, w_xbc, conv_w, conv_b, n_batch, rows, GRID_W, True)
    dt = softplus_dt(mod_matmul(xt, sc1, sh1, w_dt, 1024, 2 * SSM_HEADS), L)
    y2, _ = ssd_scan_both(xbc.reshape(n_batch, L, SSM_CONV_DIM), dt, a_neg, states, True)
    return mamba_out(y2.reshape(2, T, SSM_INNER), xbc, z, d_skip, norm_w, w_out, xt, g1, ln_g, ln_b)


def hyena_filters(L, w1, b1, w2, b2, w3, freq):
    f32 = jnp.float32
    t = jnp.linspace(0.0, 1.0, L, dtype=f32)[:, None]
    bands = (HYENA_EMB - 1) // 2
    ang = 2.0 * math.pi * jnp.arange(L, dtype=f32)[:, None] / L
    fb = jnp.linspace(1e-4, bands - 1, bands, dtype=f32)[None, :]
    z = jnp.concatenate([t, jnp.cos(fb * ang), -jnp.sin(fb * ang)], axis=-1)
    fr = freq.astype(f32)
    hid = jnp.sin(fr * (z @ w1.astype(f32) + b1.astype(f32)))
    for j in range(HYENA_INNER_MLPS):
        hid = jnp.sin(fr * (hid @ w2[j].astype(f32) + b2[j].astype(f32)))
    h = (hid @ w3.astype(f32)).reshape(L, HYENA_ORDER, 2, D_MODEL)
    max_decay = math.log(HYENA_TARGET) / HYENA_STEEP_DECAY
    min_decay = math.log(HYENA_TARGET) / HYENA_GENTLE_DECAY
    deltas = jnp.linspace(min_decay, max_decay, D_MODEL, dtype=f32)
    h = h * jnp.exp(-t * jnp.abs(deltas))[:, None, None, :]
    h_fwd, h_bwd = h[:, :, 0], h[:, :, 1]
    taps_pos = h_fwd.at[0].add(h_bwd[0]).reshape(L, HYENA_ORDER * D_MODEL)
    taps_neg = h_bwd.at[0].set(0.0).reshape(L, HYENA_ORDER * D_MODEL)
    return taps_pos, taps_neg


FFT_N1 = 128
FFT_N2 = 64
FFT_N = FFT_N1 * FFT_N2
CONV_L = FFT_N // 2
CONV_T1 = CONV_L // FFT_N2
FFT_VMEM_LIMIT = 56 * 1024 * 1024
FFT_UNROLL = 8
FFT_MID_UNROLL = 16


def _dft_tables():
    def cos_sin(phase, n):
        ang = (2.0 * math.pi / n) * (phase % n).astype(jnp.float32)
        return jnp.cos(ang), jnp.sin(ang)
    t2 = jnp.arange(FFT_N2, dtype=jnp.int32)[:, None, None]
    k1 = jnp.arange(FFT_N1, dtype=jnp.int32)[None, :, None]
    t1 = jnp.arange(CONV_T1, dtype=jnp.int32)[None, None, :]
    c, s = cos_sin(FFT_N2 * t1 * k1 + t2 * k1, FFT_N)
    fwd = jnp.concatenate([c, -s], axis=1)
    inv = jnp.concatenate([c, s], axis=1).reshape(FFT_N2, 2, FFT_N1, CONV_T1)
    inv = inv.transpose(0, 1, 3, 2).reshape(FFT_N2, 2 * CONV_T1, FFT_N1) / FFT_N
    k2 = jnp.arange(FFT_N2, dtype=jnp.int32)
    c2, s2 = cos_sin(k2[:, None] * k2[None, :], FFT_N2)
    mid = jnp.concatenate([c2, -s2], axis=0)
    return fwd, mid, inv


def _mm(w, x):
    return jnp.dot(w, x.astype(jnp.bfloat16), preferred_element_type=jnp.float32)


def _long_conv_kernel(x_ref, g_ref, hr_ref, hi_ref, b_ref, wf_ref, wm_ref, wi_ref, o_ref, sr_ref, si_ref):
    def rows_t2(t2):
        return pl.ds(t2, CONV_T1, stride=FFT_N2)

    def stage_a(t2, carry):
        w = wf_ref[t2]
        p = _mm(w, x_ref.at[0][rows_t2(t2), :])
        q = _mm(w, x_ref.at[1][rows_t2(t2), :])
        dst = pl.ds(pl.multiple_of(t2 * FFT_N1, FFT_N1), FFT_N1)
        sr_ref[dst, :] = p[:FFT_N1] - q[FFT_N1:]
        si_ref[dst, :] = p[FFT_N1:] + q[:FFT_N1]
        return carry

    lax.fori_loop(0, FFT_N2, stage_a, 0, unroll=FFT_UNROLL)
    wm = wm_ref[...]

    def per_k1(k1, carry):
        sel = pl.ds(k1, FFT_N2, stride=FFT_N1)
        p = _mm(wm, sr_ref[sel, :])
        q = _mm(wm, si_ref[sel, :])
        xr = p[:FFT_N2] - q[FFT_N2:]
        xi = p[FFT_N2:] + q[:FFT_N2]
        hsel = pl.ds(pl.multiple_of(k1 * FFT_N2, FFT_N2), FFT_N2)
        hr = hr_ref[hsel, :]
        hi = hi_ref[hsel, :]
        p = _mm(wm, xr * hr - xi * hi)
        q = _mm(wm, xr * hi + xi * hr)
        sr_ref[sel, :] = p[:FFT_N2] + q[FFT_N2:]
        si_ref[sel, :] = q[:FFT_N2] - p[FFT_N2:]
        return carry

    lax.fori_loop(0, FFT_N1, per_k1, 0, unroll=FFT_MID_UNROLL)
    bias = b_ref[...]

    def stage_c(t2, carry):
        w = wi_ref[t2]
        src = pl.ds(pl.multiple_of(t2 * FFT_N1, FFT_N1), FFT_N1)
        p = _mm(w, sr_ref[src, :])
        q = _mm(w, si_ref[src, :])
        rows = rows_t2(t2)
        o_ref.at[0][rows, :] = g_ref.at[0][rows, :] * (p[:CONV_T1] - q[CONV_T1:] + x_ref.at[0][rows, :] * bias)
        o_ref.at[1][rows, :] = g_ref.at[1][rows, :] * (q[:CONV_T1] + p[CONV_T1:] + x_ref.at[1][rows, :] * bias)
        return carry

    lax.fori_loop(0, FFT_N2, stage_c, 0, unroll=FFT_UNROLL)


def _split_bf16(w):
    hi = w.astype(jnp.bfloat16)
    return hi, (w - hi.astype(jnp.float32)).astype(jnp.bfloat16)


def _mm3(w_hi, w_lo, x):
    x_hi, x_lo = _split_bf16(x)
    f32 = jnp.float32
    return (jnp.dot(w_hi, x_hi, preferred_element_type=f32) + jnp.dot(w_lo, x_hi, preferred_element_type=f32)
            + jnp.dot(w_hi, x_lo, preferred_element_type=f32))


def _filter_spectrum_kernel(a_ref, b_ref, wfh_ref, wfl_ref, wmh_ref, wml_ref, hr_ref, hi_ref,
                            ar_ref, ai_ref, br_ref, bi_ref):
    def rows_t2(t2):
        return pl.ds(t2, CONV_T1, stride=FFT_N2)

    def stage_a(t2, carry):
        wh, wl = wfh_ref[t2], wfl_ref[t2]
        p = _mm3(wh, wl, a_ref[rows_t2(t2), :])
        q = _mm3(wh, wl, b_ref[rows_t2(t2), :])
        dst = pl.ds(pl.multiple_of(t2 * FFT_N1, FFT_N1), FFT_N1)
        ar_ref[dst, :] = p[:FFT_N1]
        ai_ref[dst, :] = p[FFT_N1:]
        br_ref[dst, :] = q[:FFT_N1]
        bi_ref[dst, :] = q[FFT_N1:]
        return carry

    lax.fori_loop(0, FFT_N2, stage_a, 0, unroll=FFT_UNROLL)
    wmh, wml = wmh_ref[...], wml_ref[...]
    a = a_ref[...]
    b = b_ref[...]
    energy = jnp.sum(a * a, axis=0, keepdims=True) + jnp.sum(b * b, axis=0, keepdims=True)
    scale = lax.rsqrt(energy + HYENA_NORM_EPS)

    def per_k1(k1, carry):
        sel = pl.ds(k1, FFT_N2, stride=FFT_N1)
        pa, qa = _mm3(wmh, wml, ar_ref[sel, :]), _mm3(wmh, wml, ai_ref[sel, :])
        pb, qb = _mm3(wmh, wml, br_ref[sel, :]), _mm3(wmh, wml, bi_ref[sel, :])
        dst = pl.ds(pl.multiple_of(k1 * FFT_N2, FFT_N2), FFT_N2)
        a_re, a_im = pa[:FFT_N2] - qa[FFT_N2:], pa[FFT_N2:] + qa[:FFT_N2]
        b_re, b_im = pb[:FFT_N2] - qb[FFT_N2:], pb[FFT_N2:] + qb[:FFT_N2]
        hr_ref[dst, :] = (a_re + b_re) * scale
        hi_ref[dst, :] = (a_im - b_im) * scale
        return carry

    lax.fori_loop(0, FFT_N1, per_k1, 0, unroll=FFT_UNROLL)


def filter_spectrum(a, b, tables):
    L, C = a.shape
    wf, wm, _ = tables
    sig = pl.BlockSpec((L, VREG_LANES), lambda c: (0, c))
    out = pl.BlockSpec((FFT_N, VREG_LANES), lambda c: (0, c))
    once = pl.Buffered(1)
    plane = pltpu.VMEM((FFT_N, VREG_LANES), jnp.float32)
    return pl.pallas_call(
        _filter_spectrum_kernel,
        grid=(C // VREG_LANES,),
        in_specs=[sig, sig] + [pl.BlockSpec(wf.shape, lambda c: (0, 0, 0), pipeline_mode=once)] * 2
        + [pl.BlockSpec(wm.shape, lambda c: (0, 0), pipeline_mode=once)] * 2,
        out_specs=[out, out],
        out_shape=[jax.ShapeDtypeStruct((FFT_N, C), jnp.float32)] * 2,
        scratch_shapes=[plane] * 4,
        compiler_params=pltpu.CompilerParams(dimension_semantics=("arbitrary",), vmem_limit_bytes=FFT_VMEM_LIMIT),
        name="filter_spectrum",
    )(a, b, *_split_bf16(wf), *_split_bf16(wm))


def long_conv_gated(u, u_col, gate, gate_col, h_re, h_im, h_col, bias, tables):
    B, L, _ = u.shape
    C = bias.shape[0]
    assert L == CONV_L and B % 2 == 0 and C % VREG_LANES == 0
    assert u_col % VREG_LANES == 0 and gate_col % VREG_LANES == 0 and h_col % VREG_LANES == 0
    consts = [t.astype(jnp.bfloat16) for t in tables]
    sig_spec = pl.BlockSpec((2, L, VREG_LANES), lambda c, b: (b, 0, c))
    u_spec = pl.BlockSpec((2, L, VREG_LANES), lambda c, b: (b, 0, c + u_col // VREG_LANES))
    gate_spec = pl.BlockSpec((2, L, VREG_LANES), lambda c, b: (b, 0, c + gate_col // VREG_LANES))
    once = pl.Buffered(1)
    spec_h = pl.BlockSpec((FFT_N, VREG_LANES), lambda c, b: (0, c + h_col // VREG_LANES), pipeline_mode=once)
    const_specs = [pl.BlockSpec(t.shape, functools.partial(lambda nd, c, b: (0,) * nd, t.ndim), pipeline_mode=once)
                   for t in consts]
    return pl.pallas_call(
        _long_conv_kernel,
        grid=(C // VREG_LANES, B // 2),
        in_specs=[u_spec, gate_spec, spec_h, spec_h, pl.BlockSpec((1, VREG_LANES), lambda c, b: (0, c))] + const_specs,
        out_specs=sig_spec,
        out_shape=jax.ShapeDtypeStruct((B, L, C), jnp.float32),
        scratch_shapes=[pltpu.VMEM((FFT_N, VREG_LANES), jnp.float32), pltpu.VMEM((FFT_N, VREG_LANES), jnp.float32)],
        compiler_params=pltpu.CompilerParams(dimension_semantics=("arbitrary", "arbitrary"),
                                             vmem_limit_bytes=FFT_VMEM_LIMIT),
        name="long_conv",
    )(u, gate, h_re, h_im, bias.reshape(1, C), *consts)


def hyena_layer(xt, mods, n_batch, w_in, conv_w, conv_b, w1, b1, w2, b2, w3, freq, fbias, w_out, ln_g, ln_b):
    sh1, sc1, g1 = mods
    T = xt.shape[0]
    L = T // n_batch
    tables = _dft_tables()
    h_re, h_im = filter_spectrum(*hyena_filters(L, w1, b1, w2, b2, w3, freq), tables)
    u = mod_matmul_conv(xt, sc1, sh1, w_in, conv_w[None], conv_b, n_batch, 1, L, False).reshape(n_batch, L, 3 * D_MODEL)
    zz = long_conv_gated(u, 2 * D_MODEL, u, 0, h_re, h_im, 0, fbias[0], tables)
    zz = long_conv_gated(zz, 0, u, D_MODEL, h_re, h_im, D_MODEL, fbias[1], tables)
    return proj_ln(zz.reshape(T, D_MODEL), w_out, xt, g1, ln_g, ln_b)


D_TILES = D_MODEL // VREG_LANES
PEER_SLOTS = PEER_HEADS * PEER_TOPK
PEER_TB = 256
PEER_SUB = 64
PEER_VMEM_LIMIT = 58 * 1024 * 1024
HI_MASK = -65536


PACK_TB = 256


def _pack_rows_kernel(lo_ref, hi_ref, o_ref):
    def bf16_bits(v):
        return pltpu.bitcast(v.astype(jnp.bfloat16).astype(jnp.float32), jnp.int32)
    word = (bf16_bits(hi_ref[...]) & HI_MASK) | lax.shift_right_logical(bf16_bits(lo_ref[...]), 16)
    for k in range(D_TILES):
        o_ref[pl.ds(k, PACK_TB, stride=D_TILES), :] = word[:, k * VREG_LANES:(k + 1) * VREG_LANES]


def _pack_rows_bf16(tab):
    n, d = tab.shape
    half = n // 2
    nblk = half // PACK_TB
    return pl.pallas_call(
        _pack_rows_kernel,
        grid=(nblk,),
        in_specs=[pl.BlockSpec((PACK_TB, d), lambda i: (i, 0)), pl.BlockSpec((PACK_TB, d), lambda i: (i + nblk, 0))],
        out_specs=pl.BlockSpec((PACK_TB * D_TILES, VREG_LANES), lambda i: (i, 0)),
        out_shape=jax.ShapeDtypeStruct((half * D_TILES, VREG_LANES), jnp.int32),
        compiler_params=pltpu.CompilerParams(dimension_semantics=("arbitrary",)),
        name="pack_rows",
    )(tab, tab)


def _unpack_row(tab_ref, off, shift):
    word = tab_ref[pl.ds(pl.multiple_of(off, VREG_SUBLANES), VREG_SUBLANES), :]
    return pltpu.bitcast((word << shift) & HI_MASK, jnp.float32)


def _sublane_sums(tiles, sub):
    for h in (1, 2, 4):
        keep = (sub & h) == 0
        nxt = []
        for a, b in zip(tiles[0::2], tiles[1::2]):
            nxt.append(jnp.where(keep, a, b) + pltpu.roll(jnp.where(keep, b, a), h, 0))
        tiles = nxt
    return tiles[0]


def _splat_tile(row):
    return jnp.broadcast_to(row, (PEER_SLOTS, VREG_LANES)).T


def _splat_small_ints_tile(row, eye, ones):
    diag = jnp.where(eye, row.astype(jnp.float32), 0.0).astype(jnp.bfloat16)
    return jnp.dot(diag, ones, preferred_element_type=jnp.float32).astype(jnp.int32)


def _splat_consts():
    eye = (lax.broadcasted_iota(jnp.int32, (PEER_SLOTS, PEER_SLOTS), 0)
           == lax.broadcasted_iota(jnp.int32, (PEER_SLOTS, PEER_SLOTS), 1))
    return eye, jnp.ones((PEER_SLOTS, VREG_LANES), jnp.bfloat16)


def _splat_token(t0, t, shift_ref, shift_s, w_ref, w_s, eye, ones):
    dst = pl.ds(pl.multiple_of(t * PEER_SLOTS, PEER_SLOTS), PEER_SLOTS)
    shift_s[dst, :] = _splat_small_ints_tile(shift_ref[pl.ds(t0 + t, 1), :], eye, ones)
    if w_ref is not None:
        w_s[dst, :] = _splat_tile(w_ref[pl.ds(t0 + t, 1), :])


def _splat_block(t0, shift_ref, shift_s, w_ref=None, w_s=None):
    eye, ones = _splat_consts()

    def token(t, carry):
        _splat_token(t0, t, shift_ref, shift_s, w_ref, w_s, eye, ones)
        return carry
    lax.fori_loop(0, PEER_SUB, token, 0, unroll=8)


def _tile_rows(k):
    return pl.ds(k, PEER_SUB, stride=D_TILES)


def _peer_act_kernel(row_ref, shift_ref, x_ref, sc_ref, sh_ref, gate_ref, tab_ref, w_ref,
                     part_ref, shift_a, shift_b, xt_ref):
    sub = lax.broadcasted_iota(jnp.int32, (VREG_SUBLANES, VREG_LANES), 0)
    contract_lanes = (((1,), (1,)), ((), ()))
    ones = jnp.ones((VREG_SUBLANES, VREG_LANES), jnp.bfloat16)
    eye, splat_ones = _splat_consts()
    bufs = (shift_a, shift_b)
    starts = list(range(0, PEER_TB, PEER_SUB))
    _splat_block(0, shift_ref, bufs[0])
    for j, t0 in enumerate(starts):
        shift_s, shift_next = bufs[j % 2], bufs[(j + 1) % 2]
        t_next = starts[j + 1] if j + 1 < len(starts) else None
        h = x_ref[t0:t0 + PEER_SUB, :] * (1.0 + sc_ref[0]) + sh_ref[0]
        for k in range(D_TILES):
            xt_ref[_tile_rows(k), :] = h[:, k * VREG_LANES:(k + 1) * VREG_LANES]

        def token(t, carry, t0=t0, shift_s=shift_s, shift_next=shift_next, t_next=t_next):
            if t_next is not None:
                _splat_token(t_next, t, shift_ref, shift_next, None, None, eye, splat_ones)
            xt = xt_ref[pl.ds(pl.multiple_of(t * D_TILES, D_TILES), D_TILES), :]
            base = pl.multiple_of(t * PEER_SLOTS, PEER_SLOTS)
            tok_rows = row_ref.at[0, 0, pl.ds(t0 * PEER_SLOTS + base, PEER_SLOTS)]
            for g in range(PEER_SLOTS // VREG_SUBLANES):
                prods = []
                for k in range(VREG_SUBLANES):
                    s = g * VREG_SUBLANES + k
                    prods.append(_unpack_row(tab_ref, tok_rows[s], shift_s[pl.ds(base + s, 1), :]) * xt)
                part_ref[pl.ds(base + g * VREG_SUBLANES, VREG_SUBLANES), :] = _sublane_sums(prods, sub)
            return carry

        lax.fori_loop(0, PEER_SUB, token, 0)
        part = part_ref[...]
        part_hi = part.astype(jnp.bfloat16)
        part_lo = (part - part_hi.astype(jnp.float32)).astype(jnp.bfloat16)
        act = (lax.dot_general(ones, part_hi, contract_lanes, preferred_element_type=jnp.float32)
               + lax.dot_general(ones, part_lo, contract_lanes, preferred_element_type=jnp.float32))[0:1]
        gelu = 0.5 * act * (1.0 + lax.erf(act * (2.0 ** -0.5)))
        lanes = slice(t0 * PEER_SLOTS, (t0 + PEER_SUB) * PEER_SLOTS)
        w_ref[0, :, lanes] = gate_ref[0, :, lanes] * gelu


def _peer_out_kernel(row_ref, shift_ref, w_ref, tab_ref, x_ref, g_ref, lng_ref, lnb_ref, o_ref,
                     shift_a, shift_b, w_a, w_b, yt_ref):
    n_acc = 2
    eye, splat_ones = _splat_consts()
    shift_bufs, w_bufs = (shift_a, shift_b), (w_a, w_b)
    starts = list(range(0, PEER_TB, PEER_SUB))
    _splat_block(0, shift_ref, shift_bufs[0], w_ref, w_bufs[0])
    for j, t0 in enumerate(starts):
        shift_s, w_s = shift_bufs[j % 2], w_bufs[j % 2]
        shift_next, w_next = shift_bufs[(j + 1) % 2], w_bufs[(j + 1) % 2]
        t_next = starts[j + 1] if j + 1 < len(starts) else None

        def pair(i, carry, t0=t0, shift_s=shift_s, w_s=w_s, shift_next=shift_next, w_next=w_next, t_next=t_next):
            if t_next is not None:
                for k in range(2):
                    _splat_token(t_next, 2 * i + k, shift_ref, shift_next, w_ref, w_next, eye, splat_ones)
            bases = [pl.multiple_of((2 * i + k) * PEER_SLOTS, PEER_SLOTS) for k in range(2)]
            rows = [row_ref.at[0, 0, pl.ds(t0 * PEER_SLOTS + b_, PEER_SLOTS)] for b_ in bases]
            accs = [[jnp.zeros((VREG_SUBLANES, VREG_LANES), jnp.float32) for _ in range(n_acc)] for _ in range(2)]
            for s in range(PEER_SLOTS):
                for k in range(2):
                    val = _unpack_row(tab_ref, rows[k][s], shift_s[pl.ds(bases[k] + s, 1), :])
                    accs[k][s % n_acc] = accs[k][s % n_acc] + w_s[pl.ds(bases[k] + s, 1), :] * val
            for k in range(2):
                yt_ref[pl.ds(pl.multiple_of((2 * i + k) * D_TILES, D_TILES), D_TILES), :] = accs[k][0] + accs[k][1]
            return carry

        lax.fori_loop(0, PEER_SUB // 2, pair, 0)
        y = jnp.concatenate([yt_ref[_tile_rows(k), :] for k in range(D_TILES)], axis=1)
        rows_out = slice(t0, t0 + PEER_SUB)
        o_ref[rows_out, :] = _layer_norm_rows(DEEPNORM_ALPHA * x_ref[rows_out, :] + g_ref[0] * y,
                                              lng_ref[...], lnb_ref[...])


def peer_experts_ln(x, mods, offs, shifts, gate, u_tab, v_tab, ln_g, ln_b):
    sh, sc, g = mods
    T = x.shape[0]
    nblk = T // PEER_TB
    per_blk = PEER_TB * PEER_SLOTS
    per_mod = T // sc.shape[0] // PEER_TB
    offs = offs.reshape(nblk, 1, per_blk)
    smem_spec = pl.BlockSpec((1, 1, per_blk), lambda i: (i, 0, 0), memory_space=pltpu.SMEM)
    lane_spec = pl.BlockSpec((1, 1, per_blk), lambda i: (i, 0, 0))
    slot_spec = pl.BlockSpec((PEER_TB, PEER_SLOTS), lambda i: (i, 0))
    tok_spec = pl.BlockSpec((PEER_TB, D_MODEL), lambda i: (i, 0))
    mod_spec = pl.BlockSpec((1, 1, D_MODEL), lambda i: (i // per_mod, 0, 0))
    vec_spec = pl.BlockSpec((1, D_MODEL), lambda i: (0, 0))
    tab_spec = pl.BlockSpec(memory_space=pltpu.VMEM)
    params = pltpu.CompilerParams(dimension_semantics=("arbitrary",), vmem_limit_bytes=PEER_VMEM_LIMIT)
    splat_i32 = pltpu.VMEM((PEER_SUB * PEER_SLOTS, VREG_LANES), jnp.int32)
    splat_f32 = pltpu.VMEM((PEER_SUB * PEER_SLOTS, VREG_LANES), jnp.float32)
    tok_tiles = pltpu.VMEM((PEER_SUB * D_TILES, VREG_LANES), jnp.float32)
    w = pl.pallas_call(
        _peer_act_kernel,
        grid=(nblk,),
        in_specs=[smem_spec, slot_spec, tok_spec, mod_spec, mod_spec, lane_spec, tab_spec],
        out_specs=lane_spec,
        out_shape=jax.ShapeDtypeStruct((nblk, 1, per_blk), jnp.float32),
        scratch_shapes=[splat_f32, splat_i32, splat_i32, tok_tiles],
        compiler_params=params,
        name="peer_act",
    )(offs, shifts, x, sc, sh, gate.reshape(nblk, 1, per_blk), _pack_rows_bf16(u_tab))
    return pl.pallas_call(
        _peer_out_kernel,
        grid=(nblk,),
        in_specs=[smem_spec, slot_spec, slot_spec, tab_spec, tok_spec, mod_spec, vec_spec, vec_spec],
        out_specs=tok_spec,
        out_shape=jax.ShapeDtypeStruct((T, D_MODEL), jnp.float32),
        scratch_shapes=[splat_i32, splat_i32, splat_f32, splat_f32, tok_tiles],
        compiler_params=params,
        name="peer_out",
    )(offs, shifts, w.reshape(T, PEER_SLOTS), _pack_rows_bf16(v_tab), x, g,
      ln_g.reshape(1, D_MODEL), ln_b.reshape(1, D_MODEL))


def _top_rows(s, k):
    n = s.shape[0]
    row = lax.broadcasted_iota(jnp.int32, s.shape, 0).astype(jnp.float32)
    vals, idxs = [], []
    for _ in range(k):
        m = jnp.max(s, axis=0, keepdims=True)
        i = jnp.min(jnp.where(s == m, row, float(n)), axis=0, keepdims=True)
        vals.append(m)
        idxs.append(i)
        s = jnp.where(row == i, -jnp.inf, s)
    return jnp.concatenate(vals, axis=0), jnp.concatenate(idxs, axis=0).astype(jnp.int32)


def _pick_rows(table, which):
    out = jnp.zeros(which.shape, table.dtype)
    for a in range(table.shape[0]):
        out = jnp.where(which == a, table[a:a + 1, :], out)
    return out


PAIR_COUNTS = tuple(PEER_TOPK // (a + 1) for a in range(PEER_TOPK))
PAIR_STARTS = tuple(sum(PAIR_COUNTS[:a]) for a in range(PEER_TOPK))
N_PAIRS = sum(PAIR_COUNTS)
N_PAIRS_PADDED = -(-N_PAIRS // VREG_SUBLANES) * VREG_SUBLANES
SEL_TB = 512
PEER_PACKED_ROWS = PEER_NKEYS * PEER_NKEYS // 2


def _peer_select_kernel(x_ref, sc_ref, sh_ref, wq_ref, k1_ref, k2_ref, off_ref, shift_ref, gate_ref,
                        off_s, shift_s, gate_s):
    half = PEER_DK // 2
    h = (x_ref[...] * (1.0 + sc_ref[0]) + sh_ref[0]).astype(jnp.bfloat16)
    q = jnp.dot(h, wq_ref[...], preferred_element_type=jnp.float32).astype(jnp.bfloat16)
    contract_last = (((1,), (1,)), ((), ()))
    s1 = lax.dot_general(k1_ref[0], q[:, :half], contract_last, preferred_element_type=jnp.float32)
    s2 = lax.dot_general(k2_ref[0], q[:, half:], contract_last, preferred_element_type=jnp.float32)
    v1, i1 = _top_rows(s1, PEER_TOPK)
    v2, i2 = _top_rows(s2, PEER_TOPK)
    pairs = [v1[a:a + 1, :] + v2[:PAIR_COUNTS[a], :] for a in range(PEER_TOPK)]
    pairs.append(jnp.full((N_PAIRS_PADDED - N_PAIRS, v1.shape[1]), -jnp.inf, jnp.float32))
    top_s, pos = _top_rows(jnp.concatenate(pairs, axis=0), PEER_TOPK)
    a_sel = jnp.zeros(pos.shape, jnp.int32)
    start = jnp.zeros(pos.shape, jnp.int32)
    for a in range(1, PEER_TOPK):
        later = pos >= PAIR_STARTS[a]
        a_sel = jnp.where(later, a, a_sel)
        start = jnp.where(later, PAIR_STARTS[a], start)
    expert = _pick_rows(i1, a_sel) * PEER_NKEYS + _pick_rows(i2, pos - start)
    e = jnp.exp(top_s - top_s[0:1, :])
    head = pl.program_id(1)
    rows = pl.ds(pl.multiple_of(head * PEER_TOPK, PEER_TOPK), PEER_TOPK)
    gate_s[rows, :] = e / jnp.sum(e, axis=0, keepdims=True)
    off_s[rows, :] = (expert & (PEER_PACKED_ROWS - 1)) * VREG_SUBLANES
    shift_s[rows, :] = jnp.where(expert >= PEER_PACKED_ROWS, 0, 16)

    @pl.when(head == PEER_HEADS - 1)
    def _():
        gate_ref[...] = gate_s[...].T
        off_ref[...] = off_s[...].astype(jnp.float32).T.astype(jnp.int32)
        shift_ref[...] = shift_s[...].astype(jnp.float32).T.astype(jnp.int32)


def peer_select(x, sc, sh, wq, k1, k2):
    T = x.shape[0]
    per_seq = T // sc.shape[0] // SEL_TB
    bf16 = jnp.bfloat16
    tok_spec = pl.BlockSpec((SEL_TB, D_MODEL), lambda i, h: (i, 0))
    mod_spec = pl.BlockSpec((1, 1, D_MODEL), lambda i, h: (i // per_seq, 0, 0))
    wq_spec = pl.BlockSpec((D_MODEL, PEER_DK), lambda i, h: (0, h))
    key_spec = pl.BlockSpec((1, PEER_NKEYS, PEER_DK // 2), lambda i, h: (h, 0, 0))
    out_spec = pl.BlockSpec((SEL_TB, PEER_SLOTS), lambda i, h: (i, 0))
    return pl.pallas_call(
        _peer_select_kernel,
        grid=(T // SEL_TB, PEER_HEADS),
        in_specs=[tok_spec, mod_spec, mod_spec, wq_spec, key_spec, key_spec],
        out_specs=[out_spec, out_spec, out_spec],
        out_shape=[jax.ShapeDtypeStruct((T, PEER_SLOTS), jnp.int32), jax.ShapeDtypeStruct((T, PEER_SLOTS), jnp.int32),
                   jax.ShapeDtypeStruct((T, PEER_SLOTS), jnp.float32)],
        scratch_shapes=[pltpu.VMEM((PEER_SLOTS, SEL_TB), jnp.int32), pltpu.VMEM((PEER_SLOTS, SEL_TB), jnp.int32),
                        pltpu.VMEM((PEER_SLOTS, SEL_TB), jnp.float32)],
        compiler_params=pltpu.CompilerParams(dimension_semantics=("arbitrary", "arbitrary")),
        name="peer_select",
    )(x, sc, sh, wq.astype(bf16), k1.astype(bf16), k2.astype(bf16))


def peer_layer(xt, mods, wq, k1, k2, u_tab, v_tab, ln_g, ln_b):
    sh2, sc2, _ = mods
    off, shift, gate = peer_select(xt, sc2, sh2, wq, k1, k2)
    return peer_experts_ln(xt, mods, off, shift, gate, u_tab, v_tab, ln_g, ln_b)


def kernel(x, c, ctx, c_ctx, ada_w, ada_b, ln_g, ln_b,
           m_w_in, m_conv_w, m_conv_b, m_a_log, m_dt_bias, m_d, m_norm_w, m_w_out,
           h_w_in, h_conv_w, h_conv_b, h_f_w1, h_f_b1, h_f_w2, h_f_b2, h_f_w3, h_freq, h_bias, h_w_out,
           p_wq, p_k1, p_k2, p_u, p_v):
    n_batch, seq, _ = x.shape
    rows = seq // GRID_W
    s_lat = jax.nn.silu(c)
    s_ctx = jax.nn.silu(c_ctx)
    xt = x.reshape(n_batch * seq, D_MODEL)
    ctx_t = ctx.reshape(-1, D_MODEL)
    for i in range(DEPTH):
        kind, k = i % N_MIXERS, i // N_MIXERS
        sh1, sc1, g1, sh2, sc2, g2 = jnp.split((s_lat @ ada_w[i] + ada_b[i])[:, None, :], 6, axis=-1)
        if kind == 0:
            csh1, csc1 = jnp.split((s_ctx @ ada_w[i] + ada_b[i])[None, None, :], 6, axis=-1)[:2]
            xt = mamba_layer(xt, ctx_t, (sh1, sc1, g1), (csh1, csc1), n_batch, rows, m_w_in[k], m_conv_w[k],
                             m_conv_b[k], m_a_log[k], m_dt_bias[k], m_d[k], m_norm_w[k], m_w_out[k],
                             ln_g[i, 0], ln_b[i, 0])
        else:
            xt = hyena_layer(xt, (sh1, sc1, g1), n_batch, h_w_in[k], h_conv_w[k], h_conv_b[k], h_f_w1[k], h_f_b1[k],
                             h_f_w2[k], h_f_b2[k], h_f_w3[k], h_freq[k], h_bias[k], h_w_out[k], ln_g[i, 0], ln_b[i, 0])
        xt = peer_layer(xt, (sh2, sc2, g2), p_wq[i], p_k1[i], p_k2[i], p_u[i], p_v[i], ln_g[i, 1], ln_b[i, 1])
    return xt.reshape(x.shape)
```

```python
import functools
import math

import jax
import jax.numpy as jnp
from jax import lax
from jax.experimental import pallas as pl
from jax.experimental.pallas import tpu as pltpu

D_MODEL = 1024
DEPTH = 2
GRID_W = 64
N_MIXERS = 2
DEEPNORM_ALPHA = (2 * DEPTH) ** 0.25
LN_EPS = 1e-5
RMS_EPS = 1e-5

SSM_INNER = 2 * D_MODEL
SSM_HEAD_DIM = 64
SSM_HEADS = SSM_INNER // SSM_HEAD_DIM
SSM_GROUPS = 4
SSM_STATE = 128
SSM_CHUNK = 128
SSM_CONV = 3
SSM_CONV_DIM = SSM_INNER + 2 * SSM_GROUPS * SSM_STATE
SSM_GROUP_HEADS = SSM_HEADS // SSM_GROUPS
SSM_GROUP_DIM = SSM_GROUP_HEADS * SSM_HEAD_DIM

HYENA_ORDER = 2
HYENA_EMB = 33
HYENA_INNER_MLPS = 2
HYENA_STEEP_DECAY = 0.3
HYENA_GENTLE_DECAY = 1.5
HYENA_TARGET = 1e-2
HYENA_NORM_EPS = 1e-6

PEER_HEADS = 8
PEER_NKEYS = 128
PEER_DK = 256
PEER_TOPK = 16

VREG_SUBLANES = 8
VREG_LANES = 128
DENSE_VMEM_LIMIT = 48 * 1024 * 1024


def _mod_matmul_kernel(x_ref, sc_ref, sh_ref, w_ref, o_ref):
    h = (x_ref[...] * (1.0 + sc_ref[0]) + sh_ref[0]).astype(jnp.bfloat16)
    o_ref[...] = jnp.dot(h, w_ref[...], preferred_element_type=jnp.float32)


def mod_matmul(x, sc, sh, w, tb, tn):
    T, D = x.shape
    N = w.shape[1]
    assert T % tb == 0 and N % tn == 0 and (T // sc.shape[0]) % tb == 0
    per_mod = T // sc.shape[0] // tb
    mod_spec = pl.BlockSpec((1, 1, D), lambda i, j: (i // per_mod, 0, 0))
    return pl.pallas_call(
        _mod_matmul_kernel,
        grid=(T // tb, N // tn),
        in_specs=[pl.BlockSpec((tb, D), lambda i, j: (i, 0)), mod_spec, mod_spec,
                  pl.BlockSpec((D, tn), lambda i, j: (0, j))],
        out_specs=pl.BlockSpec((tb, tn), lambda i, j: (i, j)),
        out_shape=jax.ShapeDtypeStruct((T, N), jnp.float32),
        compiler_params=pltpu.CompilerParams(dimension_semantics=("arbitrary", "arbitrary"),
                                             vmem_limit_bytes=DENSE_VMEM_LIMIT),
        name="mod_matmul",
    )(x, sc, sh, w.astype(jnp.bfloat16))


def _layer_norm_rows(v, g, b):
    mu = jnp.mean(v, axis=-1, keepdims=True)
    vc = v - mu
    var = jnp.mean(vc * vc, axis=-1, keepdims=True)
    return vc * lax.rsqrt(var + LN_EPS) * g + b


def _proj_ln_kernel(a_ref, w_ref, x_ref, gate_ref, g_ref, b_ref, o_ref):
    y = jnp.dot(a_ref[...].astype(jnp.bfloat16), w_ref[...], preferred_element_type=jnp.float32)
    o_ref[...] = _layer_norm_rows(DEEPNORM_ALPHA * x_ref[...] + gate_ref[0] * y, g_ref[...], b_ref[...])


def _row_specs(tb, per_mod):
    tok = pl.BlockSpec((tb, D_MODEL), lambda i: (i, 0))
    mod = pl.BlockSpec((1, 1, D_MODEL), lambda i: (i // per_mod, 0, 0))
    vec = pl.BlockSpec((1, D_MODEL), lambda i: (0, 0))
    return tok, mod, vec


def proj_ln(a, w, x, gate, ln_g, ln_b, tb=512):
    T, K = a.shape
    per_mod = T // gate.shape[0] // tb
    tok, mod, vec = _row_specs(tb, per_mod)
    return pl.pallas_call(
        _proj_ln_kernel,
        grid=(T // tb,),
        in_specs=[pl.BlockSpec((tb, K), lambda i: (i, 0)), pl.BlockSpec((K, D_MODEL), lambda i: (0, 0)),
                  tok, mod, vec, vec],
        out_specs=tok,
        out_shape=jax.ShapeDtypeStruct((T, D_MODEL), jnp.float32),
        compiler_params=pltpu.CompilerParams(dimension_semantics=("arbitrary",), vmem_limit_bytes=DENSE_VMEM_LIMIT),
        name="proj_ln",
    )(a, w.astype(jnp.bfloat16), x, gate, ln_g.reshape(1, D_MODEL), ln_b.reshape(1, D_MODEL))


def _mamba_out_kernel(y_ref, xs_ref, z_ref, dsk_ref, nw_ref, w_ref, x_ref, gate_ref, g_ref, b_ref, o_ref):
    z = z_ref[...]
    u = (y_ref[0] + y_ref[1] + xs_ref[...] * dsk_ref[...]) * (z * jax.nn.sigmoid(z))
    gw = SSM_INNER // SSM_GROUPS
    parts = []
    for g in range(SSM_GROUPS):
        ug = u[:, g * gw:(g + 1) * gw]
        parts.append(ug * lax.rsqrt(jnp.mean(ug * ug, axis=-1, keepdims=True) + RMS_EPS))
    a = (jnp.concatenate(parts, axis=1) * nw_ref[...]).astype(jnp.bfloat16)
    y = jnp.dot(a, w_ref[...], preferred_element_type=jnp.float32)
    o_ref[...] = _layer_norm_rows(DEEPNORM_ALPHA * x_ref[...] + gate_ref[0] * y, g_ref[...], b_ref[...])


def mamba_out(y2, xbc, z, d_skip, norm_w, w_out, x, gate, ln_g, ln_b, tb=256):
    T = x.shape[0]
    tok, mod, vec = _row_specs(tb, T // gate.shape[0] // tb)
    inner = pl.BlockSpec((tb, SSM_INNER), lambda i: (i, 0))
    ivec = pl.BlockSpec((1, SSM_INNER), lambda i: (0, 0))
    dsk = jnp.repeat(d_skip, SSM_INNER // d_skip.shape[0]).reshape(1, SSM_INNER)
    return pl.pallas_call(
        _mamba_out_kernel,
        grid=(T // tb,),
        in_specs=[pl.BlockSpec((2, tb, SSM_INNER), lambda i: (0, i, 0)), inner, inner, ivec, ivec,
                  pl.BlockSpec((SSM_INNER, D_MODEL), lambda i: (0, 0)), tok, mod, vec, vec],
        out_specs=tok,
        out_shape=jax.ShapeDtypeStruct((T, D_MODEL), jnp.float32),
        compiler_params=pltpu.CompilerParams(dimension_semantics=("arbitrary",), vmem_limit_bytes=DENSE_VMEM_LIMIT),
        name="mamba_out",
    )(y2, xbc, z, dsk, norm_w.reshape(1, SSM_INNER), w_out.astype(jnp.bfloat16), x, gate,
      ln_g.reshape(1, D_MODEL), ln_b.reshape(1, D_MODEL))


CONV_VMEM_LIMIT = 56 * 1024 * 1024


def _shift_rows(a, k):
    n = a.shape[0]
    return a if k == 0 else pltpu.roll(a, (-k) % n, 0)


def _mod_matmul_conv_kernel(x_ref, sc_ref, sh_ref, w_ref, cw_ref, cb_ref, o_ref, *, rows, cols, silu):
    h = (x_ref[...] * (1.0 + sc_ref[0]) + sh_ref[0]).astype(jnp.bfloat16)
    p = jnp.dot(h, w_ref[...], preferred_element_type=jnp.float32)
    L = p.shape[0]
    t = lax.broadcasted_iota(jnp.int32, p.shape, 0)
    col = t & (cols - 1)
    n_dr = cw_ref.shape[0]
    out = jnp.zeros_like(p)
    for j in range(3):
        dc = j - 1
        acc = None
        for i in range(n_dr):
            dr = i - n_dr // 2
            term = _shift_rows(p, dr * cols)
            if dr < 0:
                term = jnp.where(t >= -dr * cols, term, 0.0)
            elif dr > 0:
                term = jnp.where(t < L - dr * cols, term, 0.0)
            term = term * cw_ref[i, j:j + 1, :]
            acc = term if acc is None else acc + term
        acc = _shift_rows(acc, dc)
        if dc < 0:
            acc = jnp.where(col >= -dc, acc, 0.0)
        elif dc > 0:
            acc = jnp.where(col < cols - dc, acc, 0.0)
        out = out + acc
    out = out + cb_ref[...]
    o_ref[...] = out * jax.nn.sigmoid(out) if silu else out


def mod_matmul_conv(x, sc, sh, w, conv_w, conv_b, n_seq, rows, cols, silu, tn=256):
    T, D = x.shape
    N = w.shape[1]
    L = rows * cols
    assert T == n_seq * L and N % tn == 0 and cols & (cols - 1) == 0 and conv_w.shape[0] in (1, 3)
    per_mod = n_seq // sc.shape[0]
    mod_spec = pl.BlockSpec((1, 1, D), lambda b, j: (b // per_mod, 0, 0))
    return pl.pallas_call(
        functools.partial(_mod_matmul_conv_kernel, rows=rows, cols=cols, silu=silu),
        grid=(n_seq, N // tn),
        in_specs=[pl.BlockSpec((L, D), lambda b, j: (b, 0), pipeline_mode=pl.Buffered(1)), mod_spec, mod_spec,
                  pl.BlockSpec((D, tn), lambda b, j: (0, j)),
                  pl.BlockSpec((conv_w.shape[0], 3, tn), lambda b, j: (0, 0, j)),
                  pl.BlockSpec((1, tn), lambda b, j: (0, j))],
        out_specs=pl.BlockSpec((L, tn), lambda b, j: (b, j)),
        out_shape=jax.ShapeDtypeStruct((T, N), jnp.float32),
        compiler_params=pltpu.CompilerParams(dimension_semantics=("arbitrary", "arbitrary"),
                                             vmem_limit_bytes=CONV_VMEM_LIMIT),
        name="mod_matmul_conv",
    )(x, sc, sh, w.astype(jnp.bfloat16), conv_w, conv_b.reshape(1, N))


def _split3_bf16(a):
    f32, bf16 = jnp.float32, jnp.bfloat16
    p0 = a.astype(bf16)
    r = a - p0.astype(f32)
    p1 = r.astype(bf16)
    p2 = (r - p1.astype(f32)).astype(bf16)
    return p0, p1, p2


def _ssd_kernel(x_ref, b_ref, c_ref, dt_ref, an_ref, init_ref, *rest, want_y):
    if want_y:
        y_ref, fin_ref, state_ref = rest
    else:
        fin_ref, state_ref = rest
    f32, bf16 = jnp.float32, jnp.bfloat16
    Q, P, H, G, N = SSM_CHUNK, SSM_HEAD_DIM, SSM_HEADS, SSM_GROUPS, SSM_STATE
    R, GD = SSM_GROUP_HEADS, SSM_GROUP_DIM
    fwd = pl.program_id(1) == 0
    c = pl.program_id(2)

    @pl.when(c == 0)
    def _():
        state_ref[...] = init_ref[0, 0].reshape(SSM_INNER, N)

    contract_rows = (((0,), (0,)), ((), ()))
    contract_lanes = (((1,), (1,)), ((), ()))
    dt = dt_ref[0, 0]
    a = dt * an_ref[0]
    li = lax.broadcasted_iota(jnp.int32, (Q, Q), 0)
    si = lax.broadcasted_iota(jnp.int32, (Q, Q), 1)
    tri = jnp.where(fwd, li - si, si - li) >= 0
    tri_bf = jnp.where(tri, 1.0, 0.0).astype(bf16)
    pieces = _split3_bf16(a)
    a_cum = sum(jnp.dot(tri_bf, p, preferred_element_type=f32) for p in pieces)
    a_cum_t = sum(lax.dot_general(p, tri_bf, (((0,), (1,)), ((), ())), preferred_element_type=f32)
                  for p in pieces)
    total = jnp.sum(a, axis=0, keepdims=True)
    spread = (lax.broadcasted_iota(jnp.int32, (H, SSM_INNER), 1) // P
              == lax.broadcasted_iota(jnp.int32, (H, SSM_INNER), 0)).astype(bf16)

    def per_lane(v):
        return sum(jnp.dot(p, spread, preferred_element_type=f32) for p in _split3_bf16(v))

    xdt = x_ref[0] * per_lane(dt)
    xw = (xdt * per_lane(jnp.exp(total - a_cum))).astype(bf16)
    tot_col = jnp.where(fwd, a_cum_t[:, Q - 1:Q], a_cum_t[:, 0:1])
    keep = jnp.broadcast_to(jnp.exp(tot_col), (H, N))
    keep_rows = sum(lax.dot_general(spread, p, contract_rows, preferred_element_type=f32)
                    for p in _split3_bf16(keep))
    if want_y:
        xdt_bf = xdt.astype(bf16)
        out_scale = per_lane(jnp.exp(a_cum))
        lane = lax.broadcasted_iota(jnp.int32, (Q, 2 * P), 1)
    ys = []
    for g in range(G):
        cols = slice(g * GD, (g + 1) * GD)
        bm = b_ref[0, :, g * N:(g + 1) * N].astype(bf16)
        cm = c_ref[0, :, g * N:(g + 1) * N].astype(bf16)
        state = state_ref[cols, :]
        if want_y:
            cb = lax.dot_general(cm, bm, contract_lanes, preferred_element_type=f32)
            for j in range(R // 2):
                ms = []
                for h in (g * R + 2 * j, g * R + 2 * j + 1):
                    decay = jnp.where(tri, jnp.exp(a_cum[:, h:h + 1] - a_cum_t[h:h + 1, :]), 0.0)
                    ms.append((cb * decay).astype(bf16))
                xp = xdt_bf[:, g * GD + 2 * j * P:g * GD + 2 * (j + 1) * P]
                zero = jnp.zeros_like(xp)
                rhs = jnp.concatenate([jnp.where(lane < P, xp, zero), jnp.where(lane >= P, xp, zero)], axis=0)
                ys.append(jnp.dot(jnp.concatenate(ms, axis=1), rhs, preferred_element_type=f32))
            y_off = lax.dot_general(cm, state.astype(bf16), contract_lanes, preferred_element_type=f32)
            y_ref[0, 0, :, cols] = jnp.concatenate(ys[-(R // 2):], axis=1) + y_off * out_scale[:, cols]
        upd = lax.dot_general(xw[:, cols], bm, contract_rows, preferred_element_type=f32)
        state_ref[cols, :] = state * keep_rows[cols, :] + upd
    fin_ref[0, 0] = state_ref[...].reshape(H, P, N)


def ssd_scan_both(xbc, dt, a_neg, init, want_y):
    B, L, _ = xbc.shape
    Q, G, H, P, N = SSM_CHUNK, SSM_GROUPS, SSM_HEADS, SSM_HEAD_DIM, SSM_STATE
    nc = L // Q
    dt_d = dt.transpose(0, 2, 1, 3)
    an = a_neg.reshape(2, 1, H)
    bc_cols = G * N

    def chunk(d, c):
        return c + d * (nc - 1 - 2 * c)

    x_spec = pl.BlockSpec((1, Q, SSM_INNER), lambda b, d, c: (b, chunk(d, c), 0))
    b_spec = pl.BlockSpec((1, Q, bc_cols), lambda b, d, c: (b, chunk(d, c), SSM_INNER // bc_cols))
    c_spec = pl.BlockSpec((1, Q, bc_cols), lambda b, d, c: (b, chunk(d, c), SSM_INNER // bc_cols + 1))
    dt_spec = pl.BlockSpec((1, 1, Q, H), lambda b, d, c: (b, d, chunk(d, c), 0))
    an_spec = pl.BlockSpec((1, 1, H), lambda b, d, c: (d, 0, 0))
    st_spec = pl.BlockSpec((1, 1, H, P, N), lambda b, d, c: (b, d, 0, 0, 0))
    st_shape = jax.ShapeDtypeStruct((B, 2, H, P, N), jnp.float32)
    out_specs, out_shape = [st_spec], [st_shape]
    if want_y:
        out_specs = [pl.BlockSpec((1, 1, Q, SSM_INNER), lambda b, d, c: (d, b, chunk(d, c), 0))] + out_specs
        out_shape = [jax.ShapeDtypeStruct((2, B, L, SSM_INNER), jnp.float32)] + out_shape
    outs = pl.pallas_call(
        functools.partial(_ssd_kernel, want_y=want_y),
        grid=(B, 2, nc),
        in_specs=[x_spec, b_spec, c_spec, dt_spec, an_spec, st_spec],
        out_specs=out_specs,
        out_shape=out_shape,
        scratch_shapes=[pltpu.VMEM((SSM_INNER, N), jnp.float32)],
        compiler_params=pltpu.CompilerParams(dimension_semantics=("arbitrary",) * 3),
        name="ssd_scan_y" if want_y else "ssd_scan_state",
    )(xbc, xbc, xbc, dt_d, an, init)
    if want_y:
        return outs[0], outs[1]
    return None, outs[0]


def mamba_layer(xt, ctx_t, mods, cmods, n_batch, rows, w_in, conv_w, conv_b, a_log, dt_bias, d_skip, norm_w, w_out,
                ln_g, ln_b):
    f32 = jnp.float32
    sh1, sc1, g1 = mods
    csh1, csc1 = cmods
    a_neg = -jnp.exp(a_log.astype(f32))
    w_xbc = w_in[:, SSM_INNER:SSM_INNER + SSM_CONV_DIM]
    w_dt = w_in[:, SSM_INNER + SSM_CONV_DIM:]

    def softplus_dt(dt_raw, length):
        return jax.nn.softplus(dt_raw.reshape(n_batch, length, 2, SSM_HEADS) + dt_bias.astype(f32))

    lc = ctx_t.shape[0] // n_batch
    xbc_c = mod_matmul_conv(ctx_t, csc1, csh1, w_xbc, conv_w[1:2], conv_b, n_batch, 1, lc, True)
    dt_c = softplus_dt(mod_matmul(ctx_t, csc1, csh1, w_dt, 512, 2 * SSM_HEADS), lc)
    zeros = jnp.zeros((n_batch, 2, SSM_HEADS, SSM_HEAD_DIM, SSM_STATE), f32)
    _, states = ssd_scan_both(xbc_c.reshape(n_batch, lc, SSM_CONV_DIM), dt_c, a_neg, zeros, False)
    T = xt.shape[0]
    L = T // n_batch
    z = mod_matmul(xt, sc1, sh1, w_in[:, :SSM_INNER], 1024, 1024)
    xbc = mod_matmul_conv(xt, sc1, sh1, w_xbc, conv_w, conv_b, n_batch, rows, GRID_W, True)
    dt = softplus_dt(mod_matmul(xt, sc1, sh1, w_dt, 1024, 2 * SSM_HEADS), L)
    y2, _ = ssd_scan_both(xbc.reshape(n_batch, L, SSM_CONV_DIM), dt, a_neg, states, True)
    return mamba_out(y2.reshape(2, T, SSM_INNER), xbc, z, d_skip, norm_w, w_out, xt, g1, ln_g, ln_b)


def hyena_filters(L, w1, b1, w2, b2, w3, freq):
    f32 = jnp.float32
    t = jnp.linspace(0.0, 1.0, L, dtype=f32)[:, None]
    bands = (HYENA_EMB - 1) // 2
    ang = 2.0 * math.pi * jnp.arange(L, dtype=f32)[:, None] / L
    fb = jnp.linspace(1e-4, bands - 1, bands, dtype=f32)[None, :]
    z = jnp.concatenate([t, jnp.cos(fb * ang), -jnp.sin(fb * ang)], axis=-1)
    fr = freq.astype(f32)
    hid = jnp.sin(fr * (z @ w1.astype(f32) + b1.astype(f32)))
    for j in range(HYENA_INNER_MLPS):
        hid = jnp.sin(fr * (hid @ w2[j].astype(f32) + b2[j].astype(f32)))
    h = (hid @ w3.astype(f32)).reshape(L, HYENA_ORDER, 2, D_MODEL)
    max_decay = math.log(HYENA_TARGET) / HYENA_STEEP_DECAY
    min_decay = math.log(HYENA_TARGET) / HYENA_GENTLE_DECAY
    deltas = jnp.linspace(min_decay, max_decay, D_MODEL, dtype=f32)
    h = h * jnp.exp(-t * jnp.abs(deltas))[:, None, None, :]
    h_fwd, h_bwd = h[:, :, 0], h[:, :, 1]
    taps_pos = h_fwd.at[0].add(h_bwd[0]).reshape(L, HYENA_ORDER * D_MODEL)
    taps_neg = h_bwd.at[0].set(0.0).reshape(L, HYENA_ORDER * D_MODEL)
    return taps_pos, taps_neg


FFT_N1 = 128
FFT_N2 = 64
FFT_N = FFT_N1 * FFT_N2
CONV_L = FFT_N // 2
CONV_T1 = CONV_L // FFT_N2
FFT_VMEM_LIMIT = 56 * 1024 * 1024
FFT_UNROLL = 8
FFT_MID_UNROLL = 16


def _dft_tables():
    def cos_sin(phase, n):
        ang = (2.0 * math.pi / n) * (phase % n).astype(jnp.float32)
        return jnp.cos(ang), jnp.sin(ang)
    t2 = jnp.arange(FFT_N2, dtype=jnp.int32)[:, None, None]
    k1 = jnp.arange(FFT_N1, dtype=jnp.int32)[None, :, None]
    t1 = jnp.arange(CONV_T1, dtype=jnp.int32)[None, None, :]
    c, s = cos_sin(FFT_N2 * t1 * k1 + t2 * k1, FFT_N)
    fwd = jnp.concatenate([c, -s], axis=1)
    inv = jnp.concatenate([c, s], axis=1).reshape(FFT_N2, 2, FFT_N1, CONV_T1)
    inv = inv.transpose(0, 1, 3, 2).reshape(FFT_N2, 2 * CONV_T1, FFT_N1) / FFT_N
    k2 = jnp.arange(FFT_N2, dtype=jnp.int32)
    c2, s2 = cos_sin(k2[:, None] * k2[None, :], FFT_N2)
    mid = jnp.concatenate([c2, -s2], axis=0)
    return fwd, mid, inv


def _mm(w, x):
    return jnp.dot(w, x.astype(jnp.bfloat16), preferred_element_type=jnp.float32)


def _long_conv_kernel(x_ref, g_ref, hr_ref, hi_ref, b_ref, wf_ref, wm_ref, wi_ref, o_ref, sr_ref, si_ref):
    def rows_t2(t2):
        return pl.ds(t2, CONV_T1, stride=FFT_N2)

    def stage_a(t2, carry):
        w = wf_ref[t2]
        p = _mm(w, x_ref.at[0][rows_t2(t2), :])
        q = _mm(w, x_ref.at[1][rows_t2(t2), :])
        dst = pl.ds(pl.multiple_of(t2 * FFT_N1, FFT_N1), FFT_N1)
        sr_ref[dst, :] = p[:FFT_N1] - q[FFT_N1:]
        si_ref[dst, :] = p[FFT_N1:] + q[:FFT_N1]
        return carry

    lax.fori_loop(0, FFT_N2, stage_a, 0, unroll=FFT_UNROLL)
    wm = wm_ref[...]

    def per_k1(k1, carry):
        sel = pl.ds(k1, FFT_N2, stride=FFT_N1)
        p = _mm(wm, sr_ref[sel, :])
        q = _mm(wm, si_ref[sel, :])
        xr = p[:FFT_N2] - q[FFT_N2:]
        xi = p[FFT_N2:] + q[:FFT_N2]
        hsel = pl.ds(pl.multiple_of(k1 * FFT_N2, FFT_N2), FFT_N2)
        hr = hr_ref[hsel, :]
        hi = hi_ref[hsel, :]
        p = _mm(wm, xr * hr - xi * hi)
        q = _mm(wm, xr * hi + xi * hr)
        sr_ref[sel, :] = p[:FFT_N2] + q[FFT_N2:]
        si_ref[sel, :] = q[:FFT_N2] - p[FFT_N2:]
        return carry

    lax.fori_loop(0, FFT_N1, per_k1, 0, unroll=FFT_MID_UNROLL)
    bias = b_ref[...]

    def stage_c(t2, carry):
        w = wi_ref[t2]
        src = pl.ds(pl.multiple_of(t2 * FFT_N1, FFT_N1), FFT_N1)
        p = _mm(w, sr_ref[src, :])
        q = _mm(w, si_ref[src, :])
        rows = rows_t2(t2)
        o_ref.at[0][rows, :] = g_ref.at[0][rows, :] * (p[:CONV_T1] - q[CONV_T1:] + x_ref.at[0][rows, :] * bias)
        o_ref.at[1][rows, :] = g_ref.at[1][rows, :] * (q[:CONV_T1] + p[CONV_T1:] + x_ref.at[1][rows, :] * bias)
        return carry

    lax.fori_loop(0, FFT_N2, stage_c, 0, unroll=FFT_UNROLL)


def _split_bf16(w):
    hi = w.astype(jnp.bfloat16)
    return hi, (w - hi.astype(jnp.float32)).astype(jnp.bfloat16)


def _mm3(w_hi, w_lo, x):
    x_hi, x_lo = _split_bf16(x)
    f32 = jnp.float32
    return (jnp.dot(w_hi, x_hi, preferred_element_type=f32) + jnp.dot(w_lo, x_hi, preferred_element_type=f32)
            + jnp.dot(w_hi, x_lo, preferred_element_type=f32))


def _filter_spectrum_kernel(a_ref, b_ref, wfh_ref, wfl_ref, wmh_ref, wml_ref, hr_ref, hi_ref,
                            ar_ref, ai_ref, br_ref, bi_ref):
    def rows_t2(t2):
        return pl.ds(t2, CONV_T1, stride=FFT_N2)

    def stage_a(t2, carry):
        wh, wl = wfh_ref[t2], wfl_ref[t2]
        p = _mm3(wh, wl, a_ref[rows_t2(t2), :])
        q = _mm3(wh, wl, b_ref[rows_t2(t2), :])
        dst = pl.ds(pl.multiple_of(t2 * FFT_N1, FFT_N1), FFT_N1)
        ar_ref[dst, :] = p[:FFT_N1]
        ai_ref[dst, :] = p[FFT_N1:]
        br_ref[dst, :] = q[:FFT_N1]
        bi_ref[dst, :] = q[FFT_N1:]
        return carry

    lax.fori_loop(0, FFT_N2, stage_a, 0, unroll=FFT_UNROLL)
    wmh, wml = wmh_ref[...], wml_ref[...]
    a = a_ref[...]
    b = b_ref[...]
    energy = jnp.sum(a * a, axis=0, keepdims=True) + jnp.sum(b * b, axis=0, keepdims=True)
    scale = lax.rsqrt(energy + HYENA_NORM_EPS)

    def per_k1(k1, carry):
        sel = pl.ds(k1, FFT_N2, stride=FFT_N1)
        pa, qa = _mm3(wmh, wml, ar_ref[sel, :]), _mm3(wmh, wml, ai_ref[sel, :])
        pb, qb = _mm3(wmh, wml, br_ref[sel, :]), _mm3(wmh, wml, bi_ref[sel, :])
        dst = pl.ds(pl.multiple_of(k1 * FFT_N2, FFT_N2), FFT_N2)
        a_re, a_im = pa[:FFT_N2] - qa[FFT_N2:], pa[FFT_N2:] + qa[:FFT_N2]
        b_re, b_im = pb[:FFT_N2] - qb[FFT_N2:], pb[FFT_N2:] + qb[:FFT_N2]
        hr_ref[dst, :] = (a_re + b_re) * scale
        hi_ref[dst, :] = (a_im - b_im) * scale
        return carry

    lax.fori_loop(0, FFT_N1, per_k1, 0, unroll=FFT_UNROLL)


def filter_spectrum(a, b, tables):
    L, C = a.shape
    wf, wm, _ = tables
    sig = pl.BlockSpec((L, VREG_LANES), lambda c: (0, c))
    out = pl.BlockSpec((FFT_N, VREG_LANES), lambda c: (0, c))
    once = pl.Buffered(1)
    plane = pltpu.VMEM((FFT_N, VREG_LANES), jnp.float32)
    return pl.pallas_call(
        _filter_spectrum_kernel,
        grid=(C // VREG_LANES,),
        in_specs=[sig, sig] + [pl.BlockSpec(wf.shape, lambda c: (0, 0, 0), pipeline_mode=once)] * 2
        + [pl.BlockSpec(wm.shape, lambda c: (0, 0), pipeline_mode=once)] * 2,
        out_specs=[out, out],
        out_shape=[jax.ShapeDtypeStruct((FFT_N, C), jnp.float32)] * 2,
        scratch_shapes=[plane] * 4,
        compiler_params=pltpu.CompilerParams(dimension_semantics=("arbitrary",), vmem_limit_bytes=FFT_VMEM_LIMIT),
        name="filter_spectrum",
    )(a, b, *_split_bf16(wf), *_split_bf16(wm))


def long_conv_gated(u, u_col, gate, gate_col, h_re, h_im, h_col, bias, tables):
    B, L, _ = u.shape
    C = bias.shape[0]
    assert L == CONV_L and B % 2 == 0 and C % VREG_LANES == 0
    assert u_col % VREG_LANES == 0 and gate_col % VREG_LANES == 0 and h_col % VREG_LANES == 0
    consts = [t.astype(jnp.bfloat16) for t in tables]
    sig_spec = pl.BlockSpec((2, L, VREG_LANES), lambda c, b: (b, 0, c))
    u_spec = pl.BlockSpec((2, L, VREG_LANES), lambda c, b: (b, 0, c + u_col // VREG_LANES))
    gate_spec = pl.BlockSpec((2, L, VREG_LANES), lambda c, b: (b, 0, c + gate_col // VREG_LANES))
    once = pl.Buffered(1)
    spec_h = pl.BlockSpec((FFT_N, VREG_LANES), lambda c, b: (0, c + h_col // VREG_LANES), pipeline_mode=once)
    const_specs = [pl.BlockSpec(t.shape, functools.partial(lambda nd, c, b: (0,) * nd, t.ndim), pipeline_mode=once)
                   for t in consts]
    return pl.pallas_call(
        _long_conv_kernel,
        grid=(C // VREG_LANES, B // 2),
        in_specs=[u_spec, gate_spec, spec_h, spec_h, pl.BlockSpec((1, VREG_LANES), lambda c, b: (0, c))] + const_specs,
        out_specs=sig_spec,
        out_shape=jax.ShapeDtypeStruct((B, L, C), jnp.float32),
        scratch_shapes=[pltpu.VMEM((FFT_N, VREG_LANES), jnp.float32), pltpu.VMEM((FFT_N, VREG_LANES), jnp.float32)],
        compiler_params=pltpu.CompilerParams(dimension_semantics=("arbitrary", "arbitrary"),
                                             vmem_limit_bytes=FFT_VMEM_LIMIT),
        name="long_conv",
    )(u, gate, h_re, h_im, bias.reshape(1, C), *consts)


def hyena_layer(xt, mods, n_batch, w_in, conv_w, conv_b, w1, b1, w2, b2, w3, freq, fbias, w_out, ln_g, ln_b):
    sh1, sc1, g1 = mods
    T = xt.shape[0]
    L = T // n_batch
    tables = _dft_tables()
    h_re, h_im = filter_spectrum(*hyena_filters(L, w1, b1, w2, b2, w3, freq), tables)
    u = mod_matmul_conv(xt, sc1, sh1, w_in, conv_w[None], conv_b, n_batch, 1, L, False).reshape(n_batch, L, 3 * D_MODEL)
    zz = long_conv_gated(u, 2 * D_MODEL, u, 0, h_re, h_im, 0, fbias[0], tables)
    zz = long_conv_gated(zz, 0, u, D_MODEL, h_re, h_im, D_MODEL, fbias[1], tables)
    return proj_ln(zz.reshape(T, D_MODEL), w_out, xt, g1, ln_g, ln_b)


D_TILES = D_MODEL // VREG_LANES
PEER_SLOTS = PEER_HEADS * PEER_TOPK
PEER_TB = 256
PEER_SUB = 64
PEER_VMEM_LIMIT = 58 * 1024 * 1024
HI_MASK = -65536


PACK_TB = 256


def _pack_rows_kernel(lo_ref, hi_ref, o_ref):
    def bf16_bits(v):
        return pltpu.bitcast(v.astype(jnp.bfloat16).astype(jnp.float32), jnp.int32)
    word = (bf16_bits(hi_ref[0]) & HI_MASK) | lax.shift_right_logical(bf16_bits(lo_ref[0]), 16)
    for k in range(D_TILES):
        o_ref[pl.ds(k, PACK_TB, stride=D_TILES), :] = word[:, k * VREG_LANES:(k + 1) * VREG_LANES]


def _pack_rows_bf16(tabs, layer):
    _, n, d = tabs.shape
    half = n // 2
    nblk = half // PACK_TB
    return pl.pallas_call(
        _pack_rows_kernel,
        grid=(nblk,),
        in_specs=[pl.BlockSpec((1, PACK_TB, d), lambda i: (layer, i, 0)),
                  pl.BlockSpec((1, PACK_TB, d), lambda i: (layer, i + nblk, 0))],
        out_specs=pl.BlockSpec((PACK_TB * D_TILES, VREG_LANES), lambda i: (i, 0)),
        out_shape=jax.ShapeDtypeStruct((half * D_TILES, VREG_LANES), jnp.int32),
        compiler_params=pltpu.CompilerParams(dimension_semantics=("arbitrary",)),
        name="pack_rows",
    )(tabs, tabs)


def _unpack_row(tab_ref, off, shift):
    word = tab_ref[pl.ds(pl.multiple_of(off, VREG_SUBLANES), VREG_SUBLANES), :]
    return pltpu.bitcast((word << shift) & HI_MASK, jnp.float32)


def _sublane_sums(tiles, sub):
    for h in (1, 2, 4):
        keep = (sub & h) == 0
        nxt = []
        for a, b in zip(tiles[0::2], tiles[1::2]):
            nxt.append(jnp.where(keep, a, b) + pltpu.roll(jnp.where(keep, b, a), h, 0))
        tiles = nxt
    return tiles[0]


def _splat_tile(row):
    return jnp.broadcast_to(row, (PEER_SLOTS, VREG_LANES)).T


def _splat_small_ints_tile(row, eye, ones):
    diag = jnp.where(eye, row.astype(jnp.float32), 0.0).astype(jnp.bfloat16)
    return jnp.dot(diag, ones, preferred_element_type=jnp.float32).astype(jnp.int32)


def _splat_consts():
    eye = (lax.broadcasted_iota(jnp.int32, (PEER_SLOTS, PEER_SLOTS), 0)
           == lax.broadcasted_iota(jnp.int32, (PEER_SLOTS, PEER_SLOTS), 1))
    return eye, jnp.ones((PEER_SLOTS, VREG_LANES), jnp.bfloat16)


def _splat_token(t0, t, shift_ref, shift_s, w_ref, w_s, eye, ones):
    dst = pl.ds(pl.multiple_of(t * PEER_SLOTS, PEER_SLOTS), PEER_SLOTS)
    shift_s[dst, :] = _splat_small_ints_tile(shift_ref[pl.ds(t0 + t, 1), :], eye, ones)
    if w_ref is not None:
        w_s[dst, :] = _splat_tile(w_ref[pl.ds(t0 + t, 1), :])


def _splat_block(t0, shift_ref, shift_s, w_ref=None, w_s=None):
    eye, ones = _splat_consts()

    def token(t, carry):
        _splat_token(t0, t, shift_ref, shift_s, w_ref, w_s, eye, ones)
        return carry
    lax.fori_loop(0, PEER_SUB, token, 0, unroll=8)


def _tile_rows(k):
    return pl.ds(k, PEER_SUB, stride=D_TILES)


def _peer_act_kernel(row_ref, shift_ref, x_ref, sc_ref, sh_ref, gate_ref, tab_ref, w_ref,
                     part_ref, shift_a, shift_b, xt_ref):
    sub = lax.broadcasted_iota(jnp.int32, (VREG_SUBLANES, VREG_LANES), 0)
    contract_lanes = (((1,), (1,)), ((), ()))
    ones = jnp.ones((VREG_SUBLANES, VREG_LANES), jnp.bfloat16)
    eye, splat_ones = _splat_consts()
    bufs = (shift_a, shift_b)
    starts = list(range(0, PEER_TB, PEER_SUB))
    _splat_block(0, shift_ref, bufs[0])
    for j, t0 in enumerate(starts):
        shift_s, shift_next = bufs[j % 2], bufs[(j + 1) % 2]
        t_next = starts[j + 1] if j + 1 < len(starts) else None
        h = x_ref[t0:t0 + PEER_SUB, :] * (1.0 + sc_ref[0]) + sh_ref[0]
        for k in range(D_TILES):
            xt_ref[_tile_rows(k), :] = h[:, k * VREG_LANES:(k + 1) * VREG_LANES]

        def token(t, carry, t0=t0, shift_s=shift_s, shift_next=shift_next, t_next=t_next):
            if t_next is not None:
                _splat_token(t_next, t, shift_ref, shift_next, None, None, eye, splat_ones)
            xt = xt_ref[pl.ds(pl.multiple_of(t * D_TILES, D_TILES), D_TILES), :]
            base = pl.multiple_of(t * PEER_SLOTS, PEER_SLOTS)
            tok_rows = row_ref.at[0, 0, pl.ds(t0 * PEER_SLOTS + base, PEER_SLOTS)]
            for g in range(PEER_SLOTS // VREG_SUBLANES):
                prods = []
                for k in range(VREG_SUBLANES):
                    s = g * VREG_SUBLANES + k
                    prods.append(_unpack_row(tab_ref, tok_rows[s], shift_s[pl.ds(base + s, 1), :]) * xt)
                part_ref[pl.ds(base + g * VREG_SUBLANES, VREG_SUBLANES), :] = _sublane_sums(prods, sub)
            return carry

        lax.fori_loop(0, PEER_SUB, token, 0)
        act = lax.dot_general(ones, part_ref[...].astype(jnp.bfloat16), contract_lanes,
                              preferred_element_type=jnp.float32)[0:1]
        gelu = 0.5 * act * (1.0 + lax.erf(act * (2.0 ** -0.5)))
        lanes = slice(t0 * PEER_SLOTS, (t0 + PEER_SUB) * PEER_SLOTS)
        w_ref[0, :, lanes] = gate_ref[0, :, lanes] * gelu


def _peer_out_kernel(row_ref, shift_ref, w_ref, tab_ref, x_ref, g_ref, lng_ref, lnb_ref, o_ref,
                     shift_a, shift_b, w_a, w_b, yt_ref):
    n_acc = 2
    eye, splat_ones = _splat_consts()
    shift_bufs, w_bufs = (shift_a, shift_b), (w_a, w_b)
    starts = list(range(0, PEER_TB, PEER_SUB))
    _splat_block(0, shift_ref, shift_bufs[0], w_ref, w_bufs[0])
    for j, t0 in enumerate(starts):
        shift_s, w_s = shift_bufs[j % 2], w_bufs[j % 2]
        shift_next, w_next = shift_bufs[(j + 1) % 2], w_bufs[(j + 1) % 2]
        t_next = starts[j + 1] if j + 1 < len(starts) else None

        def pair(i, carry, t0=t0, shift_s=shift_s, w_s=w_s, shift_next=shift_next, w_next=w_next, t_next=t_next):
            if t_next is not None:
                for k in range(2):
                    _splat_token(t_next, 2 * i + k, shift_ref, shift_next, w_ref, w_next, eye, splat_ones)
            bases = [pl.multiple_of((2 * i + k) * PEER_SLOTS, PEER_SLOTS) for k in range(2)]
            rows = [row_ref.at[0, 0, pl.ds(t0 * PEER_SLOTS + b_, PEER_SLOTS)] for b_ in bases]
            accs = [[jnp.zeros((VREG_SUBLANES, VREG_LANES), jnp.float32) for _ in range(n_acc)] for _ in range(2)]
            for s in range(PEER_SLOTS):
                for k in range(2):
                    val = _unpack_row(tab_ref, rows[k][s], shift_s[pl.ds(bases[k] + s, 1), :])
                    accs[k][s % n_acc] = accs[k][s % n_acc] + w_s[pl.ds(bases[k] + s, 1), :] * val
            for k in range(2):
                yt_ref[pl.ds(pl.multiple_of((2 * i + k) * D_TILES, D_TILES), D_TILES), :] = accs[k][0] + accs[k][1]
            return carry

        lax.fori_loop(0, PEER_SUB // 2, pair, 0)
        y = jnp.concatenate([yt_ref[_tile_rows(k), :] for k in range(D_TILES)], axis=1)
        rows_out = slice(t0, t0 + PEER_SUB)
        o_ref[rows_out, :] = _layer_norm_rows(DEEPNORM_ALPHA * x_ref[rows_out, :] + g_ref[0] * y,
                                              lng_ref[...], lnb_ref[...])


def peer_experts_ln(x, mods, offs, shifts, gate, u_tabs, v_tabs, layer, ln_g, ln_b):
    sh, sc, g = mods
    T = x.shape[0]
    nblk = T // PEER_TB
    per_blk = PEER_TB * PEER_SLOTS
    per_mod = T // sc.shape[0] // PEER_TB
    offs = offs.reshape(nblk, 1, per_blk)
    smem_spec = pl.BlockSpec((1, 1, per_blk), lambda i: (i, 0, 0), memory_space=pltpu.SMEM)
    lane_spec = pl.BlockSpec((1, 1, per_blk), lambda i: (i, 0, 0))
    slot_spec = pl.BlockSpec((PEER_TB, PEER_SLOTS), lambda i: (i, 0))
    tok_spec = pl.BlockSpec((PEER_TB, D_MODEL), lambda i: (i, 0))
    mod_spec = pl.BlockSpec((1, 1, D_MODEL), lambda i: (i // per_mod, 0, 0))
    vec_spec = pl.BlockSpec((1, D_MODEL), lambda i: (0, 0))
    tab_spec = pl.BlockSpec(memory_space=pltpu.VMEM)
    params = pltpu.CompilerParams(dimension_semantics=("arbitrary",), vmem_limit_bytes=PEER_VMEM_LIMIT)
    splat_i32 = pltpu.VMEM((PEER_SUB * PEER_SLOTS, VREG_LANES), jnp.int32)
    splat_f32 = pltpu.VMEM((PEER_SUB * PEER_SLOTS, VREG_LANES), jnp.float32)
    tok_tiles = pltpu.VMEM((PEER_SUB * D_TILES, VREG_LANES), jnp.float32)
    w = pl.pallas_call(
        _peer_act_kernel,
        grid=(nblk,),
        in_specs=[smem_spec, slot_spec, tok_spec, mod_spec, mod_spec, lane_spec, tab_spec],
        out_specs=lane_spec,
        out_shape=jax.ShapeDtypeStruct((nblk, 1, per_blk), jnp.float32),
        scratch_shapes=[splat_f32, splat_i32, splat_i32, tok_tiles],
        compiler_params=params,
        name="peer_act",
    )(offs, shifts, x, sc, sh, gate.reshape(nblk, 1, per_blk), _pack_rows_bf16(u_tabs, layer))
    return pl.pallas_call(
        _peer_out_kernel,
        grid=(nblk,),
        in_specs=[smem_spec, slot_spec, slot_spec, tab_spec, tok_spec, mod_spec, vec_spec, vec_spec],
        out_specs=tok_spec,
        out_shape=jax.ShapeDtypeStruct((T, D_MODEL), jnp.float32),
        scratch_shapes=[splat_i32, splat_i32, splat_f32, splat_f32, tok_tiles],
        compiler_params=params,
        name="peer_out",
    )(offs, shifts, w.reshape(T, PEER_SLOTS), _pack_rows_bf16(v_tabs, layer), x, g,
      ln_g.reshape(1, D_MODEL), ln_b.reshape(1, D_MODEL))


def _top_rows(s, k):
    n = s.shape[0]
    row = lax.broadcasted_iota(jnp.int32, s.shape, 0).astype(jnp.float32)
    vals, idxs = [], []
    for _ in range(k):
        m = jnp.max(s, axis=0, keepdims=True)
        i = jnp.min(jnp.where(s == m, row, float(n)), axis=0, keepdims=True)
        vals.append(m)
        idxs.append(i)
        s = jnp.where(row == i, -jnp.inf, s)
    return jnp.concatenate(vals, axis=0), jnp.concatenate(idxs, axis=0).astype(jnp.int32)


def _pick_rows(table, which):
    out = jnp.zeros(which.shape, table.dtype)
    for a in range(table.shape[0]):
        out = jnp.where(which == a, table[a:a + 1, :], out)
    return out


PAIR_COUNTS = tuple(PEER_TOPK // (a + 1) for a in range(PEER_TOPK))
PAIR_STARTS = tuple(sum(PAIR_COUNTS[:a]) for a in range(PEER_TOPK))
N_PAIRS = sum(PAIR_COUNTS)
N_PAIRS_PADDED = -(-N_PAIRS // VREG_SUBLANES) * VREG_SUBLANES
SEL_TB = 512
PEER_PACKED_ROWS = PEER_NKEYS * PEER_NKEYS // 2


def _peer_select_kernel(x_ref, sc_ref, sh_ref, wq_ref, k1_ref, k2_ref, off_ref, shift_ref, gate_ref,
                        off_s, shift_s, gate_s):
    half = PEER_DK // 2
    h = (x_ref[...] * (1.0 + sc_ref[0]) + sh_ref[0]).astype(jnp.bfloat16)
    q = jnp.dot(h, wq_ref[...], preferred_element_type=jnp.float32).astype(jnp.bfloat16)
    contract_last = (((1,), (1,)), ((), ()))
    s1 = lax.dot_general(k1_ref[0], q[:, :half], contract_last, preferred_element_type=jnp.float32)
    s2 = lax.dot_general(k2_ref[0], q[:, half:], contract_last, preferred_element_type=jnp.float32)
    v1, i1 = _top_rows(s1, PEER_TOPK)
    v2, i2 = _top_rows(s2, PEER_TOPK)
    pairs = [v1[a:a + 1, :] + v2[:PAIR_COUNTS[a], :] for a in range(PEER_TOPK)]
    pairs.append(jnp.full((N_PAIRS_PADDED - N_PAIRS, v1.shape[1]), -jnp.inf, jnp.float32))
    top_s, pos = _top_rows(jnp.concatenate(pairs, axis=0), PEER_TOPK)
    a_sel = jnp.zeros(pos.shape, jnp.int32)
    start = jnp.zeros(pos.shape, jnp.int32)
    for a in range(1, PEER_TOPK):
        later = pos >= PAIR_STARTS[a]
        a_sel = jnp.where(later, a, a_sel)
        start = jnp.where(later, PAIR_STARTS[a], start)
    expert = _pick_rows(i1, a_sel) * PEER_NKEYS + _pick_rows(i2, pos - start)
    e = jnp.exp(top_s - top_s[0:1, :])
    head = pl.program_id(1)
    rows = pl.ds(pl.multiple_of(head * PEER_TOPK, PEER_TOPK), PEER_TOPK)
    gate_s[rows, :] = e / jnp.sum(e, axis=0, keepdims=True)
    off_s[rows, :] = (expert & (PEER_PACKED_ROWS - 1)) * VREG_SUBLANES
    shift_s[rows, :] = jnp.where(expert >= PEER_PACKED_ROWS, 0, 16)

    @pl.when(head == PEER_HEADS - 1)
    def _():
        gate_ref[...] = gate_s[...].T
        off_ref[...] = off_s[...].astype(jnp.float32).T.astype(jnp.int32)
        shift_ref[...] = shift_s[...].astype(jnp.float32).T.astype(jnp.int32)


def peer_select(x, sc, sh, wq, k1, k2):
    T = x.shape[0]
    per_seq = T // sc.shape[0] // SEL_TB
    bf16 = jnp.bfloat16
    tok_spec = pl.BlockSpec((SEL_TB, D_MODEL), lambda i, h: (i, 0))
    mod_spec = pl.BlockSpec((1, 1, D_MODEL), lambda i, h: (i // per_seq, 0, 0))
    wq_spec = pl.BlockSpec((D_MODEL, PEER_DK), lambda i, h: (0, h))
    key_spec = pl.BlockSpec((1, PEER_NKEYS, PEER_DK // 2), lambda i, h: (h, 0, 0))
    out_spec = pl.BlockSpec((SEL_TB, PEER_SLOTS), lambda i, h: (i, 0))
    return pl.pallas_call(
        _peer_select_kernel,
        grid=(T // SEL_TB, PEER_HEADS),
        in_specs=[tok_spec, mod_spec, mod_spec, wq_spec, key_spec, key_spec],
        out_specs=[out_spec, out_spec, out_spec],
        out_shape=[jax.ShapeDtypeStruct((T, PEER_SLOTS), jnp.int32), jax.ShapeDtypeStruct((T, PEER_SLOTS), jnp.int32),
                   jax.ShapeDtypeStruct((T, PEER_SLOTS), jnp.float32)],
        scratch_shapes=[pltpu.VMEM((PEER_SLOTS, SEL_TB), jnp.int32), pltpu.VMEM((PEER_SLOTS, SEL_TB), jnp.int32),
                        pltpu.VMEM((PEER_SLOTS, SEL_TB), jnp.float32)],
        compiler_params=pltpu.CompilerParams(dimension_semantics=("arbitrary", "arbitrary")),
        name="peer_select",
    )(x, sc, sh, wq.astype(bf16), k1.astype(bf16), k2.astype(bf16))


def peer_layer(xt, mods, wq, k1, k2, u_tabs, v_tabs, layer, ln_g, ln_b):
    sh2, sc2, _ = mods
    off, shift, gate = peer_select(xt, sc2, sh2, wq, k1, k2)
    return peer_experts_ln(xt, mods, off, shift, gate, u_tabs, v_tabs, layer, ln_g, ln_b)


def kernel(x, c, ctx, c_ctx, ada_w, ada_b, ln_g, ln_b,
           m_w_in, m_conv_w, m_conv_b, m_a_log, m_dt_bias, m_d, m_norm_w, m_w_out,
           h_w_in, h_conv_w, h_conv_b, h_f_w1, h_f_b1, h_f_w2, h_f_b2, h_f_w3, h_freq, h_bias, h_w_out,
           p_wq, p_k1, p_k2, p_u, p_v):
    n_batch, seq, _ = x.shape
    rows = seq // GRID_W
    s_lat = jax.nn.silu(c)
    s_ctx = jax.nn.silu(c_ctx)
    xt = x.reshape(n_batch * seq, D_MODEL)
    ctx_t = ctx.reshape(-1, D_MODEL)
    for i in range(DEPTH):
        kind, k = i % N_MIXERS, i // N_MIXERS
        sh1, sc1, g1, sh2, sc2, g2 = jnp.split((s_lat @ ada_w[i] + ada_b[i])[:, None, :], 6, axis=-1)
        if kind == 0:
            csh1, csc1 = jnp.split((s_ctx @ ada_w[i] + ada_b[i])[None, None, :], 6, axis=-1)[:2]
            xt = mamba_layer(xt, ctx_t, (sh1, sc1, g1), (csh1, csc1), n_batch, rows, m_w_in[k], m_conv_w[k],
                             m_conv_b[k], m_a_log[k], m_dt_bias[k], m_d[k], m_norm_w[k], m_w_out[k],
                             ln_g[i, 0], ln_b[i, 0])
        else:
            xt = hyena_layer(xt, (sh1, sc1, g1), n_batch, h_w_in[k], h_conv_w[k], h_conv_b[k], h_f_w1[k], h_f_b1[k],
                             h_f_w2[k], h_f_b2[k], h_f_w3[k], h_freq[k], h_bias[k], h_w_out[k], ln_g[i, 0], ln_b[i, 0])
        xt = peer_layer(xt, (sh2, sc2, g2), p_wq[i], p_k1[i], p_k2[i], p_u, p_v, i, ln_g[i, 1], ln_b[i, 1])
    return xt.reshape(x.shape)
```

```python
import functools
import math

import jax
import jax.numpy as jnp
from jax import lax
from jax.experimental import pallas as pl
from jax.experimental.pallas import tpu as pltpu

D_MODEL = 1024
DEPTH = 2
GRID_W = 64
N_MIXERS = 2
DEEPNORM_ALPHA = (2 * DEPTH) ** 0.25
LN_EPS = 1e-5
RMS_EPS = 1e-5

SSM_INNER = 2 * D_MODEL
SSM_HEAD_DIM = 64
SSM_HEADS = SSM_INNER // SSM_HEAD_DIM
SSM_GROUPS = 4
SSM_STATE = 128
SSM_CHUNK = 128
SSM_CONV = 3
SSM_CONV_DIM = SSM_INNER + 2 * SSM_GROUPS * SSM_STATE
SSM_GROUP_HEADS = SSM_HEADS // SSM_GROUPS
SSM_GROUP_DIM = SSM_GROUP_HEADS * SSM_HEAD_DIM

HYENA_ORDER = 2
HYENA_EMB = 33
HYENA_INNER_MLPS = 2
HYENA_STEEP_DECAY = 0.3
HYENA_GENTLE_DECAY = 1.5
HYENA_TARGET = 1e-2
HYENA_NORM_EPS = 1e-6

PEER_HEADS = 8
PEER_NKEYS = 128
PEER_DK = 256
PEER_TOPK = 16

VREG_SUBLANES = 8
VREG_LANES = 128
DENSE_VMEM_LIMIT = 48 * 1024 * 1024


def _mod_matmul_kernel(x_ref, sc_ref, sh_ref, w_ref, o_ref):
    h = (x_ref[...] * (1.0 + sc_ref[0]) + sh_ref[0]).astype(jnp.bfloat16)
    o_ref[...] = jnp.dot(h, w_ref[...], preferred_element_type=jnp.float32)


def mod_matmul(x, sc, sh, w, tb, tn):
    T, D = x.shape
    N = w.shape[1]
    assert T % tb == 0 and N % tn == 0 and (T // sc.shape[0]) % tb == 0
    per_mod = T // sc.shape[0] // tb
    mod_spec = pl.BlockSpec((1, 1, D), lambda i, j: (i // per_mod, 0, 0))
    return pl.pallas_call(
        _mod_matmul_kernel,
        grid=(T // tb, N // tn),
        in_specs=[pl.BlockSpec((tb, D), lambda i, j: (i, 0)), mod_spec, mod_spec,
                  pl.BlockSpec((D, tn), lambda i, j: (0, j))],
        out_specs=pl.BlockSpec((tb, tn), lambda i, j: (i, j)),
        out_shape=jax.ShapeDtypeStruct((T, N), jnp.float32),
        compiler_params=pltpu.CompilerParams(dimension_semantics=("arbitrary", "arbitrary"),
                                             vmem_limit_bytes=DENSE_VMEM_LIMIT),
        name="mod_matmul",
    )(x, sc, sh, w.astype(jnp.bfloat16))


def _layer_norm_rows(v, g, b):
    mu = jnp.mean(v, axis=-1, keepdims=True)
    vc = v - mu
    var = jnp.mean(vc * vc, axis=-1, keepdims=True)
    return vc * lax.rsqrt(var + LN_EPS) * g + b


def _proj_ln_kernel(a_ref, w_ref, x_ref, gate_ref, g_ref, b_ref, o_ref):
    y = jnp.dot(a_ref[...].astype(jnp.bfloat16), w_ref[...], preferred_element_type=jnp.float32)
    o_ref[...] = _layer_norm_rows(DEEPNORM_ALPHA * x_ref[...] + gate_ref[0] * y, g_ref[...], b_ref[...])


def _row_specs(tb, per_mod):
    tok = pl.BlockSpec((tb, D_MODEL), lambda i: (i, 0))
    mod = pl.BlockSpec((1, 1, D_MODEL), lambda i: (i // per_mod, 0, 0))
    vec = pl.BlockSpec((1, D_MODEL), lambda i: (0, 0))
    return tok, mod, vec


def proj_ln(a, w, x, gate, ln_g, ln_b, tb=512):
    T, K = a.shape
    per_mod = T // gate.shape[0] // tb
    tok, mod, vec = _row_specs(tb, per_mod)
    return pl.pallas_call(
        _proj_ln_kernel,
        grid=(T // tb,),
        in_specs=[pl.BlockSpec((tb, K), lambda i: (i, 0)), pl.BlockSpec((K, D_MODEL), lambda i: (0, 0)),
                  tok, mod, vec, vec],
        out_specs=tok,
        out_shape=jax.ShapeDtypeStruct((T, D_MODEL), jnp.float32),
        compiler_params=pltpu.CompilerParams(dimension_semantics=("arbitrary",), vmem_limit_bytes=DENSE_VMEM_LIMIT),
        name="proj_ln",
    )(a, w.astype(jnp.bfloat16), x, gate, ln_g.reshape(1, D_MODEL), ln_b.reshape(1, D_MODEL))


def _mamba_out_kernel(y_ref, xs_ref, z_ref, dsk_ref, nw_ref, w_ref, x_ref, gate_ref, g_ref, b_ref, o_ref):
    z = z_ref[...]
    u = (y_ref[0] + y_ref[1] + xs_ref[...] * dsk_ref[...]) * (z * jax.nn.sigmoid(z))
    gw = SSM_INNER // SSM_GROUPS
    parts = []
    for g in range(SSM_GROUPS):
        ug = u[:, g * gw:(g + 1) * gw]
        parts.append(ug * lax.rsqrt(jnp.mean(ug * ug, axis=-1, keepdims=True) + RMS_EPS))
    a = (jnp.concatenate(parts, axis=1) * nw_ref[...]).astype(jnp.bfloat16)
    y = jnp.dot(a, w_ref[...], preferred_element_type=jnp.float32)
    o_ref[...] = _layer_norm_rows(DEEPNORM_ALPHA * x_ref[...] + gate_ref[0] * y, g_ref[...], b_ref[...])


def mamba_out(y2, xbc, z, d_skip, norm_w, w_out, x, gate, ln_g, ln_b, tb=256):
    T = x.shape[0]
    tok, mod, vec = _row_specs(tb, T // gate.shape[0] // tb)
    inner = pl.BlockSpec((tb, SSM_INNER), lambda i: (i, 0))
    ivec = pl.BlockSpec((1, SSM_INNER), lambda i: (0, 0))
    dsk = jnp.repeat(d_skip, SSM_INNER // d_skip.shape[0]).reshape(1, SSM_INNER)
    return pl.pallas_call(
        _mamba_out_kernel,
        grid=(T // tb,),
        in_specs=[pl.BlockSpec((2, tb, SSM_INNER), lambda i: (0, i, 0)), inner, inner, ivec, ivec,
                  pl.BlockSpec((SSM_INNER, D_MODEL), lambda i: (0, 0)), tok, mod, vec, vec],
        out_specs=tok,
        out_shape=jax.ShapeDtypeStruct((T, D_MODEL), jnp.float32),
        compiler_params=pltpu.CompilerParams(dimension_semantics=("arbitrary",), vmem_limit_bytes=DENSE_VMEM_LIMIT),
        name="mamba_out",
    )(y2, xbc, z, dsk, norm_w.reshape(1, SSM_INNER), w_out.astype(jnp.bfloat16), x, gate,
      ln_g.reshape(1, D_MODEL), ln_b.reshape(1, D_MODEL))


CONV_VMEM_LIMIT = 56 * 1024 * 1024


def _shift_rows(a, k):
    n = a.shape[0]
    return a if k == 0 else pltpu.roll(a, (-k) % n, 0)


def _mod_matmul_conv_kernel(x_ref, sc_ref, sh_ref, w_ref, cw_ref, cb_ref, o_ref, *, rows, cols, silu):
    h = (x_ref[...] * (1.0 + sc_ref[0]) + sh_ref[0]).astype(jnp.bfloat16)
    p = jnp.dot(h, w_ref[...], preferred_element_type=jnp.float32)
    L = p.shape[0]
    t = lax.broadcasted_iota(jnp.int32, p.shape, 0)
    col = t & (cols - 1)
    n_dr = cw_ref.shape[0]
    out = jnp.zeros_like(p)
    for j in range(3):
        dc = j - 1
        acc = None
        for i in range(n_dr):
            dr = i - n_dr // 2
            term = _shift_rows(p, dr * cols)
            if dr < 0:
                term = jnp.where(t >= -dr * cols, term, 0.0)
            elif dr > 0:
                term = jnp.where(t < L - dr * cols, term, 0.0)
            term = term * cw_ref[i, j:j + 1, :]
            acc = term if acc is None else acc + term
        acc = _shift_rows(acc, dc)
        if dc < 0:
            acc = jnp.where(col >= -dc, acc, 0.0)
        elif dc > 0:
            acc = jnp.where(col < cols - dc, acc, 0.0)
        out = out + acc
    out = out + cb_ref[...]
    o_ref[...] = out * jax.nn.sigmoid(out) if silu else out


def mod_matmul_conv(x, sc, sh, w, conv_w, conv_b, n_seq, rows, cols, silu, tn=256):
    T, D = x.shape
    N = w.shape[1]
    L = rows * cols
    assert T == n_seq * L and N % tn == 0 and cols & (cols - 1) == 0 and conv_w.shape[0] in (1, 3)
    per_mod = n_seq // sc.shape[0]
    mod_spec = pl.BlockSpec((1, 1, D), lambda b, j: (b // per_mod, 0, 0))
    return pl.pallas_call(
        functools.partial(_mod_matmul_conv_kernel, rows=rows, cols=cols, silu=silu),
        grid=(n_seq, N // tn),
        in_specs=[pl.BlockSpec((L, D), lambda b, j: (b, 0), pipeline_mode=pl.Buffered(1)), mod_spec, mod_spec,
                  pl.BlockSpec((D, tn), lambda b, j: (0, j)),
                  pl.BlockSpec((conv_w.shape[0], 3, tn), lambda b, j: (0, 0, j)),
                  pl.BlockSpec((1, tn), lambda b, j: (0, j))],
        out_specs=pl.BlockSpec((L, tn), lambda b, j: (b, j)),
        out_shape=jax.ShapeDtypeStruct((T, N), jnp.float32),
        compiler_params=pltpu.CompilerParams(dimension_semantics=("arbitrary", "arbitrary"),
                                             vmem_limit_bytes=CONV_VMEM_LIMIT),
        name="mod_matmul_conv",
    )(x, sc, sh, w.astype(jnp.bfloat16), conv_w, conv_b.reshape(1, N))


def _split3_bf16(a):
    f32, bf16 = jnp.float32, jnp.bfloat16
    p0 = a.astype(bf16)
    r = a - p0.astype(f32)
    p1 = r.astype(bf16)
    p2 = (r - p1.astype(f32)).astype(bf16)
    return p0, p1, p2


def _ssd_kernel(x_ref, b_ref, c_ref, dt_ref, an_ref, init_ref, *rest, want_y):
    if want_y:
        y_ref, fin_ref, state_ref = rest
    else:
        fin_ref, state_ref = rest
    f32, bf16 = jnp.float32, jnp.bfloat16
    Q, P, H, G, N = SSM_CHUNK, SSM_HEAD_DIM, SSM_HEADS, SSM_GROUPS, SSM_STATE
    R, GD = SSM_GROUP_HEADS, SSM_GROUP_DIM
    fwd = pl.program_id(1) == 0
    c = pl.program_id(2)

    @pl.when(c == 0)
    def _():
        state_ref[...] = init_ref[0, 0].reshape(SSM_INNER, N)

    contract_rows = (((0,), (0,)), ((), ()))
    contract_lanes = (((1,), (1,)), ((), ()))
    dt = dt_ref[0, 0]
    a = dt * an_ref[0]
    li = lax.broadcasted_iota(jnp.int32, (Q, Q), 0)
    si = lax.broadcasted_iota(jnp.int32, (Q, Q), 1)
    tri = jnp.where(fwd, li - si, si - li) >= 0
    tri_bf = jnp.where(tri, 1.0, 0.0).astype(bf16)
    pieces = _split3_bf16(a)
    a_cum = sum(jnp.dot(tri_bf, p, preferred_element_type=f32) for p in pieces)
    a_cum_t = sum(lax.dot_general(p, tri_bf, (((0,), (1,)), ((), ())), preferred_element_type=f32)
                  for p in pieces)
    total = jnp.sum(a, axis=0, keepdims=True)
    spread = (lax.broadcasted_iota(jnp.int32, (H, SSM_INNER), 1) // P
              == lax.broadcasted_iota(jnp.int32, (H, SSM_INNER), 0)).astype(bf16)

    def per_lane(v):
        return sum(jnp.dot(p, spread, preferred_element_type=f32) for p in _split3_bf16(v))

    xdt = x_ref[0] * per_lane(dt)
    xw = (xdt * per_lane(jnp.exp(total - a_cum))).astype(bf16)
    tot_col = jnp.where(fwd, a_cum_t[:, Q - 1:Q], a_cum_t[:, 0:1])
    keep = jnp.broadcast_to(jnp.exp(tot_col), (H, N))
    keep_rows = sum(lax.dot_general(spread, p, contract_rows, preferred_element_type=f32)
                    for p in _split3_bf16(keep))
    if want_y:
        xdt_bf = xdt.astype(bf16)
        out_scale = per_lane(jnp.exp(a_cum))
        lane = lax.broadcasted_iota(jnp.int32, (Q, 2 * P), 1)
    ys = []
    for g in range(G):
        cols = slice(g * GD, (g + 1) * GD)
        bm = b_ref[0, :, g * N:(g + 1) * N].astype(bf16)
        cm = c_ref[0, :, g * N:(g + 1) * N].astype(bf16)
        state = state_ref[cols, :]
        if want_y:
            cb = lax.dot_general(cm, bm, contract_lanes, preferred_element_type=f32)
            for j in range(R // 2):
                ms = []
                for h in (g * R + 2 * j, g * R + 2 * j + 1):
                    decay = jnp.where(tri, jnp.exp(a_cum[:, h:h + 1] - a_cum_t[h:h + 1, :]), 0.0)
                    ms.append((cb * decay).astype(bf16))
                xp = xdt_bf[:, g * GD + 2 * j * P:g * GD + 2 * (j + 1) * P]
                zero = jnp.zeros_like(xp)
                rhs = jnp.concatenate([jnp.where(lane < P, xp, zero), jnp.where(lane >= P, xp, zero)], axis=0)
                ys.append(jnp.dot(jnp.concatenate(ms, axis=1), rhs, preferred_element_type=f32))
            y_off = lax.dot_general(cm, state.astype(bf16), contract_lanes, preferred_element_type=f32)
            y_ref[0, 0, :, cols] = jnp.concatenate(ys[-(R // 2):], axis=1) + y_off * out_scale[:, cols]
        upd = lax.dot_general(xw[:, cols], bm, contract_rows, preferred_element_type=f32)
        state_ref[cols, :] = state * keep_rows[cols, :] + upd
    fin_ref[0, 0] = state_ref[...].reshape(H, P, N)


def ssd_scan_both(xbc, dt, a_neg, init, want_y):
    B, L, _ = xbc.shape
    Q, G, H, P, N = SSM_CHUNK, SSM_GROUPS, SSM_HEADS, SSM_HEAD_DIM, SSM_STATE
    nc = L // Q
    dt_d = dt.transpose(0, 2, 1, 3)
    an = a_neg.reshape(2, 1, H)
    bc_cols = G * N

    def chunk(d, c):
        return c + d * (nc - 1 - 2 * c)

    x_spec = pl.BlockSpec((1, Q, SSM_INNER), lambda b, d, c: (b, chunk(d, c), 0))
    b_spec = pl.BlockSpec((1, Q, bc_cols), lambda b, d, c: (b, chunk(d, c), SSM_INNER // bc_cols))
    c_spec = pl.BlockSpec((1, Q, bc_cols), lambda b, d, c: (b, chunk(d, c), SSM_INNER // bc_cols + 1))
    dt_spec = pl.BlockSpec((1, 1, Q, H), lambda b, d, c: (b, d, chunk(d, c), 0))
    an_spec = pl.BlockSpec((1, 1, H), lambda b, d, c: (d, 0, 0))
    st_spec = pl.BlockSpec((1, 1, H, P, N), lambda b, d, c: (b, d, 0, 0, 0))
    st_shape = jax.ShapeDtypeStruct((B, 2, H, P, N), jnp.float32)
    out_specs, out_shape = [st_spec], [st_shape]
    if want_y:
        out_specs = [pl.BlockSpec((1, 1, Q, SSM_INNER), lambda b, d, c: (d, b, chunk(d, c), 0))] + out_specs
        out_shape = [jax.ShapeDtypeStruct((2, B, L, SSM_INNER), jnp.float32)] + out_shape
    outs = pl.pallas_call(
        functools.partial(_ssd_kernel, want_y=want_y),
        grid=(B, 2, nc),
        in_specs=[x_spec, b_spec, c_spec, dt_spec, an_spec, st_spec],
        out_specs=out_specs,
        out_shape=out_shape,
        scratch_shapes=[pltpu.VMEM((SSM_INNER, N), jnp.float32)],
        compiler_params=pltpu.CompilerParams(dimension_semantics=("arbitrary",) * 3),
        name="ssd_scan_y" if want_y else "ssd_scan_state",
    )(xbc, xbc, xbc, dt_d, an, init)
    if want_y:
        return outs[0], outs[1]
    return None, outs[0]


def mamba_layer(xt, ctx_t, mods, cmods, n_batch, rows, w_in, conv_w, conv_b, a_log, dt_bias, d_skip, norm_w, w_out,
                ln_g, ln_b):
    f32 = jnp.float32
    sh1, sc1, g1 = mods
    csh1, csc1 = cmods
    a_neg = -jnp.exp(a_log.astype(f32))
    w_xbc = w_in[:, SSM_INNER:SSM_INNER + SSM_CONV_DIM]
    w_dt = w_in[:, SSM_INNER + SSM_CONV_DIM:]

    def softplus_dt(dt_raw, length):
        return jax.nn.softplus(dt_raw.reshape(n_batch, length, 2, SSM_HEADS) + dt_bias.astype(f32))

    lc = ctx_t.shape[0] // n_batch
    xbc_c = mod_matmul_conv(ctx_t, csc1, csh1, w_xbc, conv_w[1:2], conv_b, n_batch, 1, lc, True)
    dt_c = softplus_dt(mod_matmul(ctx_t, csc1, csh1, w_dt, 512, 2 * SSM_HEADS), lc)
    zeros = jnp.zeros((n_batch, 2, SSM_HEADS, SSM_HEAD_DIM, SSM_STATE), f32)
    _, states = ssd_scan_both(xbc_c.reshape(n_batch, lc, SSM_CONV_DIM), dt_c, a_neg, zeros, False)
    T = xt.shape[0]
    L = T // n_batch
    z = mod_matmul(xt, sc1, sh1, w_in[:, :SSM_INNER], 1024, 1024)
    xbc = mod_matmul_conv(xt, sc1, sh1, w_xbc, conv_w, conv_b, n_batch, rows, GRID_W, True)
    dt = softplus_dt(mod_matmul(xt, sc1, sh1, w_dt, 1024, 2 * SSM_HEADS), L)
    y2, _ = ssd_scan_both(xbc.reshape(n_batch, L, SSM_CONV_DIM), dt, a_neg, states, True)
    return mamba_out(y2.reshape(2, T, SSM_INNER), xbc, z, d_skip, norm_w, w_out, xt, g1, ln_g, ln_b)


def hyena_filters(L, w1, b1, w2, b2, w3, freq):
    f32 = jnp.float32
    t = jnp.linspace(0.0, 1.0, L, dtype=f32)[:, None]
    bands = (HYENA_EMB - 1) // 2
    ang = 2.0 * math.pi * jnp.arange(L, dtype=f32)[:, None] / L
    fb = jnp.linspace(1e-4, bands - 1, bands, dtype=f32)[None, :]
    z = jnp.concatenate([t, jnp.cos(fb * ang), -jnp.sin(fb * ang)], axis=-1)
    fr = freq.astype(f32)
    hid = jnp.sin(fr * (z @ w1.astype(f32) + b1.astype(f32)))
    for j in range(HYENA_INNER_MLPS):
        hid = jnp.sin(fr * (hid @ w2[j].astype(f32) + b2[j].astype(f32)))
    h = (hid @ w3.astype(f32)).reshape(L, HYENA_ORDER, 2, D_MODEL)
    max_decay = math.log(HYENA_TARGET) / HYENA_STEEP_DECAY
    min_decay = math.log(HYENA_TARGET) / HYENA_GENTLE_DECAY
    deltas = jnp.linspace(min_decay, max_decay, D_MODEL, dtype=f32)
    h = h * jnp.exp(-t * jnp.abs(deltas))[:, None, None, :]
    h_fwd, h_bwd = h[:, :, 0], h[:, :, 1]
    taps_pos = h_fwd.at[0].add(h_bwd[0]).reshape(L, HYENA_ORDER * D_MODEL)
    taps_neg = h_bwd.at[0].set(0.0).reshape(L, HYENA_ORDER * D_MODEL)
    return taps_pos, taps_neg


FFT_N1 = 128
FFT_N2 = 64
FFT_N = FFT_N1 * FFT_N2
CONV_L = FFT_N // 2
CONV_T1 = CONV_L // FFT_N2
FFT_VMEM_LIMIT = 56 * 1024 * 1024
FFT_UNROLL = 8
FFT_MID_UNROLL = 16


def _dft_tables():
    def cos_sin(phase, n):
        ang = (2.0 * math.pi / n) * (phase % n).astype(jnp.float32)
        return jnp.cos(ang), jnp.sin(ang)
    t2 = jnp.arange(FFT_N2, dtype=jnp.int32)[:, None, None]
    k1 = jnp.arange(FFT_N1, dtype=jnp.int32)[None, :, None]
    t1 = jnp.arange(CONV_T1, dtype=jnp.int32)[None, None, :]
    c, s = cos_sin(FFT_N2 * t1 * k1 + t2 * k1, FFT_N)
    fwd = jnp.concatenate([c, -s], axis=1)
    inv = jnp.concatenate([c, s], axis=1).reshape(FFT_N2, 2, FFT_N1, CONV_T1)
    inv = inv.transpose(0, 1, 3, 2).reshape(FFT_N2, 2 * CONV_T1, FFT_N1) / FFT_N
    k2 = jnp.arange(FFT_N2, dtype=jnp.int32)
    c2, s2 = cos_sin(k2[:, None] * k2[None, :], FFT_N2)
    mid = jnp.concatenate([c2, -s2], axis=0)
    return fwd, mid, inv


def _mm(w, x):
    return jnp.dot(w, x.astype(jnp.bfloat16), preferred_element_type=jnp.float32)


def _long_conv_kernel(x_ref, g_ref, hr_ref, hi_ref, b_ref, wf_ref, wm_ref, wi_ref, o_ref, sr_ref, si_ref):
    def rows_t2(t2):
        return pl.ds(t2, CONV_T1, stride=FFT_N2)

    def stage_a(t2, carry):
        w = wf_ref[t2]
        p = _mm(w, x_ref.at[0][rows_t2(t2), :])
        q = _mm(w, x_ref.at[1][rows_t2(t2), :])
        dst = pl.ds(pl.multiple_of(t2 * FFT_N1, FFT_N1), FFT_N1)
        sr_ref[dst, :] = p[:FFT_N1] - q[FFT_N1:]
        si_ref[dst, :] = p[FFT_N1:] + q[:FFT_N1]
        return carry

    lax.fori_loop(0, FFT_N2, stage_a, 0, unroll=FFT_UNROLL)
    wm = wm_ref[...]

    def per_k1(k1, carry):
        sel = pl.ds(k1, FFT_N2, stride=FFT_N1)
        p = _mm(wm, sr_ref[sel, :])
        q = _mm(wm, si_ref[sel, :])
        xr = p[:FFT_N2] - q[FFT_N2:]
        xi = p[FFT_N2:] + q[:FFT_N2]
        hsel = pl.ds(pl.multiple_of(k1 * FFT_N2, FFT_N2), FFT_N2)
        hr = hr_ref[hsel, :]
        hi = hi_ref[hsel, :]
        p = _mm(wm, xr * hr - xi * hi)
        q = _mm(wm, xr * hi + xi * hr)
        sr_ref[sel, :] = p[:FFT_N2] + q[FFT_N2:]
        si_ref[sel, :] = q[:FFT_N2] - p[FFT_N2:]
        return carry

    lax.fori_loop(0, FFT_N1, per_k1, 0, unroll=FFT_MID_UNROLL)
    bias = b_ref[...]

    def stage_c(t2, carry):
        w = wi_ref[t2]
        src = pl.ds(pl.multiple_of(t2 * FFT_N1, FFT_N1), FFT_N1)
        p = _mm(w, sr_ref[src, :])
        q = _mm(w, si_ref[src, :])
        rows = rows_t2(t2)
        o_ref.at[0][rows, :] = g_ref.at[0][rows, :] * (p[:CONV_T1] - q[CONV_T1:] + x_ref.at[0][rows, :] * bias)
        o_ref.at[1][rows, :] = g_ref.at[1][rows, :] * (q[:CONV_T1] + p[CONV_T1:] + x_ref.at[1][rows, :] * bias)
        return carry

    lax.fori_loop(0, FFT_N2, stage_c, 0, unroll=FFT_UNROLL)


def _split_bf16(w):
    hi = w.astype(jnp.bfloat16)
    return hi, (w - hi.astype(jnp.float32)).astype(jnp.bfloat16)


def _mm3(w_hi, w_lo, x):
    x_hi, x_lo = _split_bf16(x)
    f32 = jnp.float32
    return (jnp.dot(w_hi, x_hi, preferred_element_type=f32) + jnp.dot(w_lo, x_hi, preferred_element_type=f32)
            + jnp.dot(w_hi, x_lo, preferred_element_type=f32))


def _filter_spectrum_kernel(a_ref, b_ref, wfh_ref, wfl_ref, wmh_ref, wml_ref, hr_ref, hi_ref,
                            ar_ref, ai_ref, br_ref, bi_ref):
    def rows_t2(t2):
        return pl.ds(t2, CONV_T1, stride=FFT_N2)

    def stage_a(t2, carry):
        wh, wl = wfh_ref[t2], wfl_ref[t2]
        p = _mm3(wh, wl, a_ref[rows_t2(t2), :])
        q = _mm3(wh, wl, b_ref[rows_t2(t2), :])
        dst = pl.ds(pl.multiple_of(t2 * FFT_N1, FFT_N1), FFT_N1)
        ar_ref[dst, :] = p[:FFT_N1]
        ai_ref[dst, :] = p[FFT_N1:]
        br_ref[dst, :] = q[:FFT_N1]
        bi_ref[dst, :] = q[FFT_N1:]
        return carry

    lax.fori_loop(0, FFT_N2, stage_a, 0, unroll=FFT_UNROLL)
    wmh, wml = wmh_ref[...], wml_ref[...]
    a = a_ref[...]
    b = b_ref[...]
    energy = jnp.sum(a * a, axis=0, keepdims=True) + jnp.sum(b * b, axis=0, keepdims=True)
    scale = lax.rsqrt(energy + HYENA_NORM_EPS)

    def per_k1(k1, carry):
        sel = pl.ds(k1, FFT_N2, stride=FFT_N1)
        pa, qa = _mm3(wmh, wml, ar_ref[sel, :]), _mm3(wmh, wml, ai_ref[sel, :])
        pb, qb = _mm3(wmh, wml, br_ref[sel, :]), _mm3(wmh, wml, bi_ref[sel, :])
        dst = pl.ds(pl.multiple_of(k1 * FFT_N2, FFT_N2), FFT_N2)
        a_re, a_im = pa[:FFT_N2] - qa[FFT_N2:], pa[FFT_N2:] + qa[:FFT_N2]
        b_re, b_im = pb[:FFT_N2] - qb[FFT_N2:], pb[FFT_N2:] + qb[:FFT_N2]
        hr_ref[dst, :] = (a_re + b_re) * scale
        hi_ref[dst, :] = (a_im - b_im) * scale
        return carry

    lax.fori_loop(0, FFT_N1, per_k1, 0, unroll=FFT_UNROLL)


def filter_spectrum(a, b, tables):
    L, C = a.shape
    wf, wm, _ = tables
    sig = pl.BlockSpec((L, VREG_LANES), lambda c: (0, c))
    out = pl.BlockSpec((FFT_N, VREG_LANES), lambda c: (0, c))
    once = pl.Buffered(1)
    plane = pltpu.VMEM((FFT_N, VREG_LANES), jnp.float32)
    return pl.pallas_call(
        _filter_spectrum_kernel,
        grid=(C // VREG_LANES,),
        in_specs=[sig, sig] + [pl.BlockSpec(wf.shape, lambda c: (0, 0, 0), pipeline_mode=once)] * 2
        + [pl.BlockSpec(wm.shape, lambda c: (0, 0), pipeline_mode=once)] * 2,
        out_specs=[out, out],
        out_shape=[jax.ShapeDtypeStruct((FFT_N, C), jnp.float32)] * 2,
        scratch_shapes=[plane] * 4,
        compiler_params=pltpu.CompilerParams(dimension_semantics=("arbitrary",), vmem_limit_bytes=FFT_VMEM_LIMIT),
        name="filter_spectrum",
    )(a, b, *_split_bf16(wf), *_split_bf16(wm))


def long_conv_gated(u, u_col, gate, gate_col, h_re, h_im, h_col, bias, tables):
    B, L, _ = u.shape
    C = bias.shape[0]
    assert L == CONV_L and B % 2 == 0 and C % VREG_LANES == 0
    assert u_col % VREG_LANES == 0 and gate_col % VREG_LANES == 0 and h_col % VREG_LANES == 0
    consts = [t.astype(jnp.bfloat16) for t in tables]
    sig_spec = pl.BlockSpec((2, L, VREG_LANES), lambda c, b: (b, 0, c))
    u_spec = pl.BlockSpec((2, L, VREG_LANES), lambda c, b: (b, 0, c + u_col // VREG_LANES))
    gate_spec = pl.BlockSpec((2, L, VREG_LANES), lambda c, b: (b, 0, c + gate_col // VREG_LANES))
    once = pl.Buffered(1)
    spec_h = pl.BlockSpec((FFT_N, VREG_LANES), lambda c, b: (0, c + h_col // VREG_LANES), pipeline_mode=once)
    const_specs = [pl.BlockSpec(t.shape, functools.partial(lambda nd, c, b: (0,) * nd, t.ndim), pipeline_mode=once)
                   for t in consts]
    return pl.pallas_call(
        _long_conv_kernel,
        grid=(C // VREG_LANES, B // 2),
        in_specs=[u_spec, gate_spec, spec_h, spec_h, pl.BlockSpec((1, VREG_LANES), lambda c, b: (0, c))] + const_specs,
        out_specs=sig_spec,
        out_shape=jax.ShapeDtypeStruct((B, L, C), jnp.float32),
        scratch_shapes=[pltpu.VMEM((FFT_N, VREG_LANES), jnp.float32), pltpu.VMEM((FFT_N, VREG_LANES), jnp.float32)],
        compiler_params=pltpu.CompilerParams(dimension_semantics=("arbitrary", "arbitrary"),
                                             vmem_limit_bytes=FFT_VMEM_LIMIT),
        name="long_conv",
    )(u, gate, h_re, h_im, bias.reshape(1, C), *consts)


def hyena_layer(xt, mods, n_batch, w_in, conv_w, conv_b, w1, b1, w2, b2, w3, freq, fbias, w_out, ln_g, ln_b):
    sh1, sc1, g1 = mods
    T = xt.shape[0]
    L = T // n_batch
    tables = _dft_tables()
    h_re, h_im = filter_spectrum(*hyena_filters(L, w1, b1, w2, b2, w3, freq), tables)
    u = mod_matmul_conv(xt, sc1, sh1, w_in, conv_w[None], conv_b, n_batch, 1, L, False).reshape(n_batch, L, 3 * D_MODEL)
    zz = long_conv_gated(u, 2 * D_MODEL, u, 0, h_re, h_im, 0, fbias[0], tables)
    zz = long_conv_gated(zz, 0, u, D_MODEL, h_re, h_im, D_MODEL, fbias[1], tables)
    return proj_ln(zz.reshape(T, D_MODEL), w_out, xt, g1, ln_g, ln_b)


D_TILES = D_MODEL // VREG_LANES
PEER_SLOTS = PEER_HEADS * PEER_TOPK
PEER_TB = 256
PEER_SUB = 64
PEER_VMEM_LIMIT = 58 * 1024 * 1024
HI_MASK = -65536


PACK_TB = 256


def _pack_rows_kernel(lo_ref, hi_ref, o_ref):
    def bf16_bits(v):
        return pltpu.bitcast(v.astype(jnp.bfloat16).astype(jnp.float32), jnp.int32)
    word = (bf16_bits(hi_ref[0]) & HI_MASK) | lax.shift_right_logical(bf16_bits(lo_ref[0]), 16)
    for k in range(D_TILES):
        o_ref[pl.ds(k, PACK_TB, stride=D_TILES), :] = word[:, k * VREG_LANES:(k + 1) * VREG_LANES]


def _pack_rows_bf16(tabs, layer):
    _, n, d = tabs.shape
    half = n // 2
    nblk = half // PACK_TB
    return pl.pallas_call(
        _pack_rows_kernel,
        grid=(nblk,),
        in_specs=[pl.BlockSpec((1, PACK_TB, d), lambda i: (layer, i, 0)),
                  pl.BlockSpec((1, PACK_TB, d), lambda i: (layer, i + nblk, 0))],
        out_specs=pl.BlockSpec((PACK_TB * D_TILES, VREG_LANES), lambda i: (i, 0)),
        out_shape=jax.ShapeDtypeStruct((half * D_TILES, VREG_LANES), jnp.int32),
        compiler_params=pltpu.CompilerParams(dimension_semantics=("arbitrary",)),
        name="pack_rows",
    )(tabs, tabs)


def _unpack_row(tab_ref, off, shift):
    word = tab_ref[pl.ds(pl.multiple_of(off, VREG_SUBLANES), VREG_SUBLANES), :]
    return pltpu.bitcast((word << shift) & HI_MASK, jnp.float32)


def _sublane_sums(tiles, sub):
    for h in (1, 2, 4):
        keep = (sub & h) == 0
        nxt = []
        for a, b in zip(tiles[0::2], tiles[1::2]):
            nxt.append(jnp.where(keep, a, b) + pltpu.roll(jnp.where(keep, b, a), h, 0))
        tiles = nxt
    return tiles[0]


def _splat_tile(row):
    return jnp.broadcast_to(row, (PEER_SLOTS, VREG_LANES)).T


def _splat_small_ints_tile(row, eye, ones):
    diag = jnp.where(eye, row.astype(jnp.float32), 0.0).astype(jnp.bfloat16)
    return jnp.dot(diag, ones, preferred_element_type=jnp.float32).astype(jnp.int32)


def _splat_consts():
    eye = (lax.broadcasted_iota(jnp.int32, (PEER_SLOTS, PEER_SLOTS), 0)
           == lax.broadcasted_iota(jnp.int32, (PEER_SLOTS, PEER_SLOTS), 1))
    return eye, jnp.ones((PEER_SLOTS, VREG_LANES), jnp.bfloat16)


def _splat_token(t0, t, shift_ref, shift_s, w_ref, w_s, eye, ones):
    dst = pl.ds(pl.multiple_of(t * PEER_SLOTS, PEER_SLOTS), PEER_SLOTS)
    shift_s[dst, :] = _splat_small_ints_tile(shift_ref[pl.ds(t0 + t, 1), :], eye, ones)
    if w_ref is not None:
        w_s[dst, :] = _splat_tile(w_ref[pl.ds(t0 + t, 1), :])


def _splat_block(t0, shift_ref, shift_s, w_ref=None, w_s=None):
    eye, ones = _splat_consts()

    def token(t, carry):
        _splat_token(t0, t, shift_ref, shift_s, w_ref, w_s, eye, ones)
        return carry
    lax.fori_loop(0, PEER_SUB, token, 0, unroll=8)


def _tile_rows(k):
    return pl.ds(k, PEER_SUB, stride=D_TILES)


def _peer_act_kernel(row_ref, shift_ref, x_ref, sc_ref, sh_ref, gate_ref, tab_ref, w_ref,
                     part_ref, shift_a, shift_b, xt_ref):
    sub = lax.broadcasted_iota(jnp.int32, (VREG_SUBLANES, VREG_LANES), 0)
    contract_lanes = (((1,), (1,)), ((), ()))
    ones = jnp.ones((VREG_SUBLANES, VREG_LANES), jnp.bfloat16)
    eye, splat_ones = _splat_consts()
    bufs = (shift_a, shift_b)
    starts = list(range(0, PEER_TB, PEER_SUB))
    _splat_block(0, shift_ref, bufs[0])
    for j, t0 in enumerate(starts):
        shift_s, shift_next = bufs[j % 2], bufs[(j + 1) % 2]
        t_next = starts[j + 1] if j + 1 < len(starts) else None
        h = x_ref[t0:t0 + PEER_SUB, :] * (1.0 + sc_ref[0]) + sh_ref[0]
        for k in range(D_TILES):
            xt_ref[_tile_rows(k), :] = h[:, k * VREG_LANES:(k + 1) * VREG_LANES]

        def token(t, carry, t0=t0, shift_s=shift_s, shift_next=shift_next, t_next=t_next):
            if t_next is not None:
                _splat_token(t_next, t, shift_ref, shift_next, None, None, eye, splat_ones)
            xt = xt_ref[pl.ds(pl.multiple_of(t * D_TILES, D_TILES), D_TILES), :]
            base = pl.multiple_of(t * PEER_SLOTS, PEER_SLOTS)
            tok_rows = row_ref.at[0, 0, pl.ds(t0 * PEER_SLOTS + base, PEER_SLOTS)]
            for g in range(PEER_SLOTS // VREG_SUBLANES):
                prods = []
                for k in range(VREG_SUBLANES):
                    s = g * VREG_SUBLANES + k
                    prods.append(_unpack_row(tab_ref, tok_rows[s], shift_s[pl.ds(base + s, 1), :]) * xt)
                part_ref[pl.ds(base + g * VREG_SUBLANES, VREG_SUBLANES), :] = _sublane_sums(prods, sub)
            return carry

        lax.fori_loop(0, PEER_SUB, token, 0)
        act = lax.dot_general(ones, part_ref[...].astype(jnp.bfloat16), contract_lanes,
                              preferred_element_type=jnp.float32)[0:1]
        gelu = 0.5 * act * (1.0 + lax.erf(act * (2.0 ** -0.5)))
        lanes = slice(t0 * PEER_SLOTS, (t0 + PEER_SUB) * PEER_SLOTS)
        w_ref[0, :, lanes] = gate_ref[0, :, lanes] * gelu


def _peer_out_kernel(row_ref, shift_ref, w_ref, tab_ref, x_ref, g_ref, lng_ref, lnb_ref, o_ref,
                     shift_a, shift_b, w_a, w_b, yt_ref):
    n_acc = 2
    eye, splat_ones = _splat_consts()
    shift_bufs, w_bufs = (shift_a, shift_b), (w_a, w_b)
    starts = list(range(0, PEER_TB, PEER_SUB))
    _splat_block(0, shift_ref, shift_bufs[0], w_ref, w_bufs[0])
    for j, t0 in enumerate(starts):
        shift_s, w_s = shift_bufs[j % 2], w_bufs[j % 2]
        shift_next, w_next = shift_bufs[(j + 1) % 2], w_bufs[(j + 1) % 2]
        t_next = starts[j + 1] if j + 1 < len(starts) else None

        def pair(i, carry, t0=t0, shift_s=shift_s, w_s=w_s, shift_next=shift_next, w_next=w_next, t_next=t_next):
            if t_next is not None:
                for k in range(2):
                    _splat_token(t_next, 2 * i + k, shift_ref, shift_next, w_ref, w_next, eye, splat_ones)
            bases = [pl.multiple_of((2 * i + k) * PEER_SLOTS, PEER_SLOTS) for k in range(2)]
            rows = [row_ref.at[0, 0, pl.ds(t0 * PEER_SLOTS + b_, PEER_SLOTS)] for b_ in bases]
            accs = [[jnp.zeros((VREG_SUBLANES, VREG_LANES), jnp.float32) for _ in range(n_acc)] for _ in range(2)]
            for s in range(PEER_SLOTS):
                for k in range(2):
                    val = _unpack_row(tab_ref, rows[k][s], shift_s[pl.ds(bases[k] + s, 1), :])
                    accs[k][s % n_acc] = accs[k][s % n_acc] + w_s[pl.ds(bases[k] + s, 1), :] * val
            for k in range(2):
                yt_ref[pl.ds(pl.multiple_of((2 * i + k) * D_TILES, D_TILES), D_TILES), :] = accs[k][0] + accs[k][1]
            return carry

        lax.fori_loop(0, PEER_SUB // 2, pair, 0)
        y = jnp.concatenate([yt_ref[_tile_rows(k), :] for k in range(D_TILES)], axis=1)
        rows_out = slice(t0, t0 + PEER_SUB)
        o_ref[rows_out, :] = _layer_norm_rows(DEEPNORM_ALPHA * x_ref[rows_out, :] + g_ref[0] * y,
                                              lng_ref[...], lnb_ref[...])


def peer_experts_ln(x, mods, offs, shifts, gate, u_tabs, v_tabs, layer, ln_g, ln_b):
    sh, sc, g = mods
    T = x.shape[0]
    nblk = T // PEER_TB
    per_blk = PEER_TB * PEER_SLOTS
    per_mod = T // sc.shape[0] // PEER_TB
    offs = offs.reshape(nblk, 1, per_blk)
    smem_spec = pl.BlockSpec((1, 1, per_blk), lambda i: (i, 0, 0), memory_space=pltpu.SMEM)
    lane_spec = pl.BlockSpec((1, 1, per_blk), lambda i: (i, 0, 0))
    slot_spec = pl.BlockSpec((PEER_TB, PEER_SLOTS), lambda i: (i, 0))
    tok_spec = pl.BlockSpec((PEER_TB, D_MODEL), lambda i: (i, 0))
    mod_spec = pl.BlockSpec((1, 1, D_MODEL), lambda i: (i // per_mod, 0, 0))
    vec_spec = pl.BlockSpec((1, D_MODEL), lambda i: (0, 0))
    tab_spec = pl.BlockSpec(memory_space=pltpu.VMEM)
    params = pltpu.CompilerParams(dimension_semantics=("arbitrary",), vmem_limit_bytes=PEER_VMEM_LIMIT)
    splat_i32 = pltpu.VMEM((PEER_SUB * PEER_SLOTS, VREG_LANES), jnp.int32)
    splat_f32 = pltpu.VMEM((PEER_SUB * PEER_SLOTS, VREG_LANES), jnp.float32)
    tok_tiles = pltpu.VMEM((PEER_SUB * D_TILES, VREG_LANES), jnp.float32)
    w = pl.pallas_call(
        _peer_act_kernel,
        grid=(nblk,),
        in_specs=[smem_spec, slot_spec, tok_spec, mod_spec, mod_spec, lane_spec, tab_spec],
        out_specs=lane_spec,
        out_shape=jax.ShapeDtypeStruct((nblk, 1, per_blk), jnp.float32),
        scratch_shapes=[splat_f32, splat_i32, splat_i32, tok_tiles],
        compiler_params=params,
        name="peer_act",
    )(offs, shifts, x, sc, sh, gate.reshape(nblk, 1, per_blk), _pack_rows_bf16(u_tabs, layer))
    return pl.pallas_call(
        _peer_out_kernel,
        grid=(nblk,),
        in_specs=[smem_spec, slot_spec, slot_spec, tab_spec, tok_spec, mod_spec, vec_spec, vec_spec],
        out_specs=tok_spec,
        out_shape=jax.ShapeDtypeStruct((T, D_MODEL), jnp.float32),
        scratch_shapes=[splat_i32, splat_i32, splat_f32, splat_f32, tok_tiles],
        compiler_params=params,
        name="peer_out",
    )(offs, shifts, w.reshape(T, PEER_SLOTS), _pack_rows_bf16(v_tabs, layer), x, g,
      ln_g.reshape(1, D_MODEL), ln_b.reshape(1, D_MODEL))


def _top_rows(s, k):
    n = s.shape[0]
    row = lax.broadcasted_iota(jnp.int32, s.shape, 0).astype(jnp.float32)
    vals, idxs = [], []
    for _ in range(k):
        m = jnp.max(s, axis=0, keepdims=True)
        i = jnp.min(jnp.where(s == m, row, float(n)), axis=0, keepdims=True)
        vals.append(m)
        idxs.append(i)
        s = jnp.where(row == i, -jnp.inf, s)
    return jnp.concatenate(vals, axis=0), jnp.concatenate(idxs, axis=0).astype(jnp.int32)


def _pick_rows(table, which):
    out = jnp.zeros(which.shape, table.dtype)
    for a in range(table.shape[0]):
        out = jnp.where(which == a, table[a:a + 1, :], out)
    return out


PAIR_COUNTS = tuple(PEER_TOPK // (a + 1) for a in range(PEER_TOPK))
PAIR_STARTS = tuple(sum(PAIR_COUNTS[:a]) for a in range(PEER_TOPK))
N_PAIRS = sum(PAIR_COUNTS)
N_PAIRS_PADDED = -(-N_PAIRS // VREG_SUBLANES) * VREG_SUBLANES
SEL_TB = 2048
PEER_PACKED_ROWS = PEER_NKEYS * PEER_NKEYS // 2


def _peer_select_kernel(x_ref, sc_ref, sh_ref, wq_ref, k1_ref, k2_ref, off_ref, shift_ref, gate_ref,
                        off_s, shift_s, gate_s):
    half = PEER_DK // 2
    h = (x_ref[...] * (1.0 + sc_ref[0]) + sh_ref[0]).astype(jnp.bfloat16)
    q = jnp.dot(h, wq_ref[...], preferred_element_type=jnp.float32).astype(jnp.bfloat16)
    contract_last = (((1,), (1,)), ((), ()))
    s1 = lax.dot_general(k1_ref[0], q[:, :half], contract_last, preferred_element_type=jnp.float32)
    s2 = lax.dot_general(k2_ref[0], q[:, half:], contract_last, preferred_element_type=jnp.float32)
    v1, i1 = _top_rows(s1, PEER_TOPK)
    v2, i2 = _top_rows(s2, PEER_TOPK)
    pairs = [v1[a:a + 1, :] + v2[:PAIR_COUNTS[a], :] for a in range(PEER_TOPK)]
    pairs.append(jnp.full((N_PAIRS_PADDED - N_PAIRS, v1.shape[1]), -jnp.inf, jnp.float32))
    top_s, pos = _top_rows(jnp.concatenate(pairs, axis=0), PEER_TOPK)
    a_sel = jnp.zeros(pos.shape, jnp.int32)
    start = jnp.zeros(pos.shape, jnp.int32)
    for a in range(1, PEER_TOPK):
        later = pos >= PAIR_STARTS[a]
        a_sel = jnp.where(later, a, a_sel)
        start = jnp.where(later, PAIR_STARTS[a], start)
    expert = _pick_rows(i1, a_sel) * PEER_NKEYS + _pick_rows(i2, pos - start)
    e = jnp.exp(top_s - top_s[0:1, :])
    head = pl.program_id(1)
    rows = pl.ds(pl.multiple_of(head * PEER_TOPK, PEER_TOPK), PEER_TOPK)
    gate_s[rows, :] = e / jnp.sum(e, axis=0, keepdims=True)
    off_s[rows, :] = (expert & (PEER_PACKED_ROWS - 1)) * VREG_SUBLANES
    shift_s[rows, :] = jnp.where(expert >= PEER_PACKED_ROWS, 0, 16)

    @pl.when(head == PEER_HEADS - 1)
    def _():
        gate_ref[...] = gate_s[...].T
        off_ref[...] = off_s[...].astype(jnp.float32).T.astype(jnp.int32)
        shift_ref[...] = shift_s[...].astype(jnp.float32).T.astype(jnp.int32)


def peer_select(x, sc, sh, wq, k1, k2):
    T = x.shape[0]
    per_seq = T // sc.shape[0] // SEL_TB
    bf16 = jnp.bfloat16
    tok_spec = pl.BlockSpec((SEL_TB, D_MODEL), lambda i, h: (i, 0))
    mod_spec = pl.BlockSpec((1, 1, D_MODEL), lambda i, h: (i // per_seq, 0, 0))
    wq_spec = pl.BlockSpec((D_MODEL, PEER_DK), lambda i, h: (0, h))
    key_spec = pl.BlockSpec((1, PEER_NKEYS, PEER_DK // 2), lambda i, h: (h, 0, 0))
    out_spec = pl.BlockSpec((SEL_TB, PEER_SLOTS), lambda i, h: (i, 0))
    return pl.pallas_call(
        _peer_select_kernel,
        grid=(T // SEL_TB, PEER_HEADS),
        in_specs=[tok_spec, mod_spec, mod_spec, wq_spec, key_spec, key_spec],
        out_specs=[out_spec, out_spec, out_spec],
        out_shape=[jax.ShapeDtypeStruct((T, PEER_SLOTS), jnp.int32), jax.ShapeDtypeStruct((T, PEER_SLOTS), jnp.int32),
                   jax.ShapeDtypeStruct((T, PEER_SLOTS), jnp.float32)],
        scratch_shapes=[pltpu.VMEM((PEER_SLOTS, SEL_TB), jnp.int32), pltpu.VMEM((PEER_SLOTS, SEL_TB), jnp.int32),
                        pltpu.VMEM((PEER_SLOTS, SEL_TB), jnp.float32)],
        compiler_params=pltpu.CompilerParams(dimension_semantics=("arbitrary", "arbitrary")),
        name="peer_select",
    )(x, sc, sh, wq.astype(bf16), k1.astype(bf16), k2.astype(bf16))


def peer_layer(xt, mods, wq, k1, k2, u_tabs, v_tabs, layer, ln_g, ln_b):
    sh2, sc2, _ = mods
    off, shift, gate = peer_select(xt, sc2, sh2, wq, k1, k2)
    return peer_experts_ln(xt, mods, off, shift, gate, u_tabs, v_tabs, layer, ln_g, ln_b)


def kernel(x, c, ctx, c_ctx, ada_w, ada_b, ln_g, ln_b,
           m_w_in, m_conv_w, m_conv_b, m_a_log, m_dt_bias, m_d, m_norm_w, m_w_out,
           h_w_in, h_conv_w, h_conv_b, h_f_w1, h_f_b1, h_f_w2, h_f_b2, h_f_w3, h_freq, h_bias, h_w_out,
           p_wq, p_k1, p_k2, p_u, p_v):
    n_batch, seq, _ = x.shape
    rows = seq // GRID_W
    s_lat = jax.nn.silu(c)
    s_ctx = jax.nn.silu(c_ctx)
    xt = x.reshape(n_batch * seq, D_MODEL)
    ctx_t = ctx.reshape(-1, D_MODEL)
    for i in range(DEPTH):
        kind, k = i % N_MIXERS, i // N_MIXERS
        sh1, sc1, g1, sh2, sc2, g2 = jnp.split((s_lat @ ada_w[i] + ada_b[i])[:, None, :], 6, axis=-1)
        if kind == 0:
            csh1, csc1 = jnp.split((s_ctx @ ada_w[i] + ada_b[i])[None, None, :], 6, axis=-1)[:2]
            xt = mamba_layer(xt, ctx_t, (sh1, sc1, g1), (csh1, csc1), n_batch, rows, m_w_in[k], m_conv_w[k],
                             m_conv_b[k], m_a_log[k], m_dt_bias[k], m_d[k], m_norm_w[k], m_w_out[k],
                             ln_g[i, 0], ln_b[i, 0])
        else:
            xt = hyena_layer(xt, (sh1, sc1, g1), n_batch, h_w_in[k], h_conv_w[k], h_conv_b[k], h_f_w1[k], h_f_b1[k],
                             h_f_w2[k], h_f_b2[k], h_f_w3[k], h_freq[k], h_bias[k], h_w_out[k], ln_g[i, 0], ln_b[i, 0])
        xt = peer_layer(xt, (sh2, sc2, g2), p_wq[i], p_k1[i], p_k2[i], p_u, p_v, i, ln_g[i, 1], ln_b[i, 1])
    return xt.reshape(x.shape)
```

```python
import functools
import math

import jax
import jax.numpy as jnp
from jax import lax
from jax.experimental import pallas as pl
from jax.experimental.pallas import tpu as pltpu

D_MODEL = 1024
DEPTH = 2
GRID_W = 64
N_MIXERS = 2
DEEPNORM_ALPHA = (2 * DEPTH) ** 0.25
LN_EPS = 1e-5
RMS_EPS = 1e-5

SSM_INNER = 2 * D_MODEL
SSM_HEAD_DIM = 64
SSM_HEADS = SSM_INNER // SSM_HEAD_DIM
SSM_GROUPS = 4
SSM_STATE = 128
SSM_CHUNK = 128
SSM_CONV = 3
SSM_CONV_DIM = SSM_INNER + 2 * SSM_GROUPS * SSM_STATE
SSM_GROUP_HEADS = SSM_HEADS // SSM_GROUPS
SSM_GROUP_DIM = SSM_GROUP_HEADS * SSM_HEAD_DIM

HYENA_ORDER = 2
HYENA_EMB = 33
HYENA_INNER_MLPS = 2
HYENA_STEEP_DECAY = 0.3
HYENA_GENTLE_DECAY = 1.5
HYENA_TARGET = 1e-2
HYENA_NORM_EPS = 1e-6

PEER_HEADS = 8
PEER_NKEYS = 128
PEER_DK = 256
PEER_TOPK = 16

VREG_SUBLANES = 8
VREG_LANES = 128
DENSE_VMEM_LIMIT = 48 * 1024 * 1024


def _mod_matmul_kernel(x_ref, sc_ref, sh_ref, w_ref, o_ref):
    h = (x_ref[...] * (1.0 + sc_ref[0]) + sh_ref[0]).astype(jnp.bfloat16)
    o_ref[...] = jnp.dot(h, w_ref[...], preferred_element_type=jnp.float32)


def mod_matmul(x, sc, sh, w, tb, tn):
    T, D = x.shape
    N = w.shape[1]
    assert T % tb == 0 and N % tn == 0 and (T // sc.shape[0]) % tb == 0
    per_mod = T // sc.shape[0] // tb
    mod_spec = pl.BlockSpec((1, 1, D), lambda i, j: (i // per_mod, 0, 0))
    return pl.pallas_call(
        _mod_matmul_kernel,
        grid=(T // tb, N // tn),
        in_specs=[pl.BlockSpec((tb, D), lambda i, j: (i, 0)), mod_spec, mod_spec,
                  pl.BlockSpec((D, tn), lambda i, j: (0, j))],
        out_specs=pl.BlockSpec((tb, tn), lambda i, j: (i, j)),
        out_shape=jax.ShapeDtypeStruct((T, N), jnp.float32),
        compiler_params=pltpu.CompilerParams(dimension_semantics=("arbitrary", "arbitrary"),
                                             vmem_limit_bytes=DENSE_VMEM_LIMIT),
        name="mod_matmul",
    )(x, sc, sh, w.astype(jnp.bfloat16))


def _layer_norm_rows(v, g, b):
    mu = jnp.mean(v, axis=-1, keepdims=True)
    vc = v - mu
    var = jnp.mean(vc * vc, axis=-1, keepdims=True)
    return vc * lax.rsqrt(var + LN_EPS) * g + b


def _proj_ln_kernel(a_ref, w_ref, x_ref, gate_ref, g_ref, b_ref, o_ref):
    y = jnp.dot(a_ref[...].astype(jnp.bfloat16), w_ref[...], preferred_element_type=jnp.float32)
    o_ref[...] = _layer_norm_rows(DEEPNORM_ALPHA * x_ref[...] + gate_ref[0] * y, g_ref[...], b_ref[...])


def _row_specs(tb, per_mod):
    tok = pl.BlockSpec((tb, D_MODEL), lambda i: (i, 0))
    mod = pl.BlockSpec((1, 1, D_MODEL), lambda i: (i // per_mod, 0, 0))
    vec = pl.BlockSpec((1, D_MODEL), lambda i: (0, 0))
    return tok, mod, vec


def proj_ln(a, w, x, gate, ln_g, ln_b, tb=512):
    T, K = a.shape
    per_mod = T // gate.shape[0] // tb
    tok, mod, vec = _row_specs(tb, per_mod)
    return pl.pallas_call(
        _proj_ln_kernel,
        grid=(T // tb,),
        in_specs=[pl.BlockSpec((tb, K), lambda i: (i, 0)), pl.BlockSpec((K, D_MODEL), lambda i: (0, 0)),
                  tok, mod, vec, vec],
        out_specs=tok,
        out_shape=jax.ShapeDtypeStruct((T, D_MODEL), jnp.float32),
        compiler_params=pltpu.CompilerParams(dimension_semantics=("arbitrary",), vmem_limit_bytes=DENSE_VMEM_LIMIT),
        name="proj_ln",
    )(a, w.astype(jnp.bfloat16), x, gate, ln_g.reshape(1, D_MODEL), ln_b.reshape(1, D_MODEL))


def _mamba_out_kernel(y_ref, xs_ref, z_ref, dsk_ref, nw_ref, w_ref, x_ref, gate_ref, g_ref, b_ref, o_ref):
    z = z_ref[...]
    u = (y_ref[0] + y_ref[1] + xs_ref[...] * dsk_ref[...]) * (z * jax.nn.sigmoid(z))
    gw = SSM_INNER // SSM_GROUPS
    parts = []
    for g in range(SSM_GROUPS):
        ug = u[:, g * gw:(g + 1) * gw]
        parts.append(ug * lax.rsqrt(jnp.mean(ug * ug, axis=-1, keepdims=True) + RMS_EPS))
    a = (jnp.concatenate(parts, axis=1) * nw_ref[...]).astype(jnp.bfloat16)
    y = jnp.dot(a, w_ref[...], preferred_element_type=jnp.float32)
    o_ref[...] = _layer_norm_rows(DEEPNORM_ALPHA * x_ref[...] + gate_ref[0] * y, g_ref[...], b_ref[...])


def mamba_out(y2, xbc, z, d_skip, norm_w, w_out, x, gate, ln_g, ln_b, tb=256):
    T = x.shape[0]
    tok, mod, vec = _row_specs(tb, T // gate.shape[0] // tb)
    inner = pl.BlockSpec((tb, SSM_INNER), lambda i: (i, 0))
    ivec = pl.BlockSpec((1, SSM_INNER), lambda i: (0, 0))
    dsk = jnp.repeat(d_skip, SSM_INNER // d_skip.shape[0]).reshape(1, SSM_INNER)
    return pl.pallas_call(
        _mamba_out_kernel,
        grid=(T // tb,),
        in_specs=[pl.BlockSpec((2, tb, SSM_INNER), lambda i: (0, i, 0)), inner, inner, ivec, ivec,
                  pl.BlockSpec((SSM_INNER, D_MODEL), lambda i: (0, 0)), tok, mod, vec, vec],
        out_specs=tok,
        out_shape=jax.ShapeDtypeStruct((T, D_MODEL), jnp.float32),
        compiler_params=pltpu.CompilerParams(dimension_semantics=("arbitrary",), vmem_limit_bytes=DENSE_VMEM_LIMIT),
        name="mamba_out",
    )(y2, xbc, z, dsk, norm_w.reshape(1, SSM_INNER), w_out.astype(jnp.bfloat16), x, gate,
      ln_g.reshape(1, D_MODEL), ln_b.reshape(1, D_MODEL))


CONV_VMEM_LIMIT = 56 * 1024 * 1024


def _shift_rows(a, k):
    n = a.shape[0]
    return a if k == 0 else pltpu.roll(a, (-k) % n, 0)


def _mod_matmul_conv_kernel(x_ref, sc_ref, sh_ref, w_ref, cw_ref, cb_ref, o_ref, *, rows, cols, silu):
    h = (x_ref[...] * (1.0 + sc_ref[0]) + sh_ref[0]).astype(jnp.bfloat16)
    p = jnp.dot(h, w_ref[...], preferred_element_type=jnp.float32)
    L = p.shape[0]
    t = lax.broadcasted_iota(jnp.int32, p.shape, 0)
    col = t & (cols - 1)
    n_dr = cw_ref.shape[0]
    out = jnp.zeros_like(p)
    for j in range(3):
        dc = j - 1
        acc = None
        for i in range(n_dr):
            dr = i - n_dr // 2
            term = _shift_rows(p, dr * cols)
            if dr < 0:
                term = jnp.where(t >= -dr * cols, term, 0.0)
            elif dr > 0:
                term = jnp.where(t < L - dr * cols, term, 0.0)
            term = term * cw_ref[i, j:j + 1, :]
            acc = term if acc is None else acc + term
        acc = _shift_rows(acc, dc)
        if dc < 0:
            acc = jnp.where(col >= -dc, acc, 0.0)
        elif dc > 0:
            acc = jnp.where(col < cols - dc, acc, 0.0)
        out = out + acc
    out = out + cb_ref[...]
    o_ref[...] = out * jax.nn.sigmoid(out) if silu else out


def mod_matmul_conv(x, sc, sh, w, conv_w, conv_b, n_seq, rows, cols, silu, tn=256):
    T, D = x.shape
    N = w.shape[1]
    L = rows * cols
    assert T == n_seq * L and N % tn == 0 and cols & (cols - 1) == 0 and conv_w.shape[0] in (1, 3)
    per_mod = n_seq // sc.shape[0]
    mod_spec = pl.BlockSpec((1, 1, D), lambda b, j: (b // per_mod, 0, 0))
    return pl.pallas_call(
        functools.partial(_mod_matmul_conv_kernel, rows=rows, cols=cols, silu=silu),
        grid=(n_seq, N // tn),
        in_specs=[pl.BlockSpec((L, D), lambda b, j: (b, 0), pipeline_mode=pl.Buffered(1)), mod_spec, mod_spec,
                  pl.BlockSpec((D, tn), lambda b, j: (0, j)),
                  pl.BlockSpec((conv_w.shape[0], 3, tn), lambda b, j: (0, 0, j)),
                  pl.BlockSpec((1, tn), lambda b, j: (0, j))],
        out_specs=pl.BlockSpec((L, tn), lambda b, j: (b, j)),
        out_shape=jax.ShapeDtypeStruct((T, N), jnp.float32),
        compiler_params=pltpu.CompilerParams(dimension_semantics=("arbitrary", "arbitrary"),
                                             vmem_limit_bytes=CONV_VMEM_LIMIT),
        name="mod_matmul_conv",
    )(x, sc, sh, w.astype(jnp.bfloat16), conv_w, conv_b.reshape(1, N))


def _split3_bf16(a):
    f32, bf16 = jnp.float32, jnp.bfloat16
    p0 = a.astype(bf16)
    r = a - p0.astype(f32)
    p1 = r.astype(bf16)
    p2 = (r - p1.astype(f32)).astype(bf16)
    return p0, p1, p2


def _ssd_kernel(x_ref, b_ref, c_ref, dt_ref, an_ref, init_ref, *rest, want_y):
    if want_y:
        y_ref, fin_ref, state_ref = rest
    else:
        fin_ref, state_ref = rest
    f32, bf16 = jnp.float32, jnp.bfloat16
    Q, P, H, G, N = SSM_CHUNK, SSM_HEAD_DIM, SSM_HEADS, SSM_GROUPS, SSM_STATE
    R, GD = SSM_GROUP_HEADS, SSM_GROUP_DIM
    fwd = pl.program_id(1) == 0
    c = pl.program_id(2)

    @pl.when(c == 0)
    def _():
        state_ref[...] = init_ref[0, 0].reshape(SSM_INNER, N)

    contract_rows = (((0,), (0,)), ((), ()))
    contract_lanes = (((1,), (1,)), ((), ()))
    dt = dt_ref[0, 0]
    a = dt * an_ref[0]
    li = lax.broadcasted_iota(jnp.int32, (Q, Q), 0)
    si = lax.broadcasted_iota(jnp.int32, (Q, Q), 1)
    tri = jnp.where(fwd, li - si, si - li) >= 0
    tri_bf = jnp.where(tri, 1.0, 0.0).astype(bf16)
    pieces = _split3_bf16(a)
    a_cum = sum(jnp.dot(tri_bf, p, preferred_element_type=f32) for p in pieces)
    a_cum_t = sum(lax.dot_general(p, tri_bf, (((0,), (1,)), ((), ())), preferred_element_type=f32)
                  for p in pieces)
    total = jnp.sum(a, axis=0, keepdims=True)
    spread = (lax.broadcasted_iota(jnp.int32, (H, SSM_INNER), 1) // P
              == lax.broadcasted_iota(jnp.int32, (H, SSM_INNER), 0)).astype(bf16)

    def per_lane(v):
        return sum(jnp.dot(p, spread, preferred_element_type=f32) for p in _split3_bf16(v))

    xdt = x_ref[0] * per_lane(dt)
    xw = (xdt * per_lane(jnp.exp(total - a_cum))).astype(bf16)
    tot_col = jnp.where(fwd, a_cum_t[:, Q - 1:Q], a_cum_t[:, 0:1])
    keep = jnp.broadcast_to(jnp.exp(tot_col), (H, N))
    keep_rows = sum(lax.dot_general(spread, p, contract_rows, preferred_element_type=f32)
                    for p in _split3_bf16(keep))
    if want_y:
        xdt_bf = xdt.astype(bf16)
        out_scale = per_lane(jnp.exp(a_cum))
        lane = lax.broadcasted_iota(jnp.int32, (Q, 2 * P), 1)
    ys = []
    for g in range(G):
        cols = slice(g * GD, (g + 1) * GD)
        bm = b_ref[0, :, g * N:(g + 1) * N].astype(bf16)
        cm = c_ref[0, :, g * N:(g + 1) * N].astype(bf16)
        state = state_ref[cols, :]
        if want_y:
            cb = lax.dot_general(cm, bm, contract_lanes, preferred_element_type=f32)
            for j in range(R // 2):
                ms = []
                for h in (g * R + 2 * j, g * R + 2 * j + 1):
                    decay = jnp.where(tri, jnp.exp(a_cum[:, h:h + 1] - a_cum_t[h:h + 1, :]), 0.0)
                    ms.append((cb * decay).astype(bf16))
                xp = xdt_bf[:, g * GD + 2 * j * P:g * GD + 2 * (j + 1) * P]
                zero = jnp.zeros_like(xp)
                rhs = jnp.concatenate([jnp.where(lane < P, xp, zero), jnp.where(lane >= P, xp, zero)], axis=0)
                ys.append(jnp.dot(jnp.concatenate(ms, axis=1), rhs, preferred_element_type=f32))
            y_off = lax.dot_general(cm, state.astype(bf16), contract_lanes, preferred_element_type=f32)
            y_ref[0, 0, :, cols] = jnp.concatenate(ys[-(R // 2):], axis=1) + y_off * out_scale[:, cols]
        upd = lax.dot_general(xw[:, cols], bm, contract_rows, preferred_element_type=f32)
        state_ref[cols, :] = state * keep_rows[cols, :] + upd
    fin_ref[0, 0] = state_ref[...].reshape(H, P, N)


def ssd_scan_both(xbc, dt, a_neg, init, want_y):
    B, L, _ = xbc.shape
    Q, G, H, P, N = SSM_CHUNK, SSM_GROUPS, SSM_HEADS, SSM_HEAD_DIM, SSM_STATE
    nc = L // Q
    dt_d = dt.transpose(0, 2, 1, 3)
    an = a_neg.reshape(2, 1, H)
    bc_cols = G * N

    def chunk(d, c):
        return c + d * (nc - 1 - 2 * c)

    x_spec = pl.BlockSpec((1, Q, SSM_INNER), lambda b, d, c: (b, chunk(d, c), 0))
    b_spec = pl.BlockSpec((1, Q, bc_cols), lambda b, d, c: (b, chunk(d, c), SSM_INNER // bc_cols))
    c_spec = pl.BlockSpec((1, Q, bc_cols), lambda b, d, c: (b, chunk(d, c), SSM_INNER // bc_cols + 1))
    dt_spec = pl.BlockSpec((1, 1, Q, H), lambda b, d, c: (b, d, chunk(d, c), 0))
    an_spec = pl.BlockSpec((1, 1, H), lambda b, d, c: (d, 0, 0))
    st_spec = pl.BlockSpec((1, 1, H, P, N), lambda b, d, c: (b, d, 0, 0, 0))
    st_shape = jax.ShapeDtypeStruct((B, 2, H, P, N), jnp.float32)
    out_specs, out_shape = [st_spec], [st_shape]
    if want_y:
        out_specs = [pl.BlockSpec((1, 1, Q, SSM_INNER), lambda b, d, c: (d, b, chunk(d, c), 0))] + out_specs
        out_shape = [jax.ShapeDtypeStruct((2, B, L, SSM_INNER), jnp.float32)] + out_shape
    outs = pl.pallas_call(
        functools.partial(_ssd_kernel, want_y=want_y),
        grid=(B, 2, nc),
        in_specs=[x_spec, b_spec, c_spec, dt_spec, an_spec, st_spec],
        out_specs=out_specs,
        out_shape=out_shape,
        scratch_shapes=[pltpu.VMEM((SSM_INNER, N), jnp.float32)],
        compiler_params=pltpu.CompilerParams(dimension_semantics=("arbitrary",) * 3),
        name="ssd_scan_y" if want_y else "ssd_scan_state",
    )(xbc, xbc, xbc, dt_d, an, init)
    if want_y:
        return outs[0], outs[1]
    return None, outs[0]


def mamba_layer(xt, ctx_t, mods, cmods, n_batch, rows, w_in, conv_w, conv_b, a_log, dt_bias, d_skip, norm_w, w_out,
                ln_g, ln_b):
    f32 = jnp.float32
    sh1, sc1, g1 = mods
    csh1, csc1 = cmods
    a_neg = -jnp.exp(a_log.astype(f32))
    w_xbc = w_in[:, SSM_INNER:SSM_INNER + SSM_CONV_DIM]
    w_dt = w_in[:, SSM_INNER + SSM_CONV_DIM:]

    def softplus_dt(dt_raw, length):
        return jax.nn.softplus(dt_raw.reshape(n_batch, length, 2, SSM_HEADS) + dt_bias.astype(f32))

    lc = ctx_t.shape[0] // n_batch
    xbc_c = mod_matmul_conv(ctx_t, csc1, csh1, w_xbc, conv_w[1:2], conv_b, n_batch, 1, lc, True)
    dt_c = softplus_dt(mod_matmul(ctx_t, csc1, csh1, w_dt, 512, 2 * SSM_HEADS), lc)
    zeros = jnp.zeros((n_batch, 2, SSM_HEADS, SSM_HEAD_DIM, SSM_STATE), f32)
    _, states = ssd_scan_both(xbc_c.reshape(n_batch, lc, SSM_CONV_DIM), dt_c, a_neg, zeros, False)
    T = xt.shape[0]
    L = T // n_batch
    z = mod_matmul(xt, sc1, sh1, w_in[:, :SSM_INNER], 1024, 1024)
    xbc = mod_matmul_conv(xt, sc1, sh1, w_xbc, conv_w, conv_b, n_batch, rows, GRID_W, True)
    dt = softplus_dt(mod_matmul(xt, sc1, sh1, w_dt, 1024, 2 * SSM_HEADS), L)
    y2, _ = ssd_scan_both(xbc.reshape(n_batch, L, SSM_CONV_DIM), dt, a_neg, states, True)
    return mamba_out(y2.reshape(2, T, SSM_INNER), xbc, z, d_skip, norm_w, w_out, xt, g1, ln_g, ln_b)


def hyena_filters(L, w1, b1, w2, b2, w3, freq):
    f32 = jnp.float32
    t = jnp.linspace(0.0, 1.0, L, dtype=f32)[:, None]
    bands = (HYENA_EMB - 1) // 2
    ang = 2.0 * math.pi * jnp.arange(L, dtype=f32)[:, None] / L
    fb = jnp.linspace(1e-4, bands - 1, bands, dtype=f32)[None, :]
    z = jnp.concatenate([t, jnp.cos(fb * ang), -jnp.sin(fb * ang)], axis=-1)
    fr = freq.astype(f32)
    hid = jnp.sin(fr * (z @ w1.astype(f32) + b1.astype(f32)))
    for j in range(HYENA_INNER_MLPS):
        hid = jnp.sin(fr * (hid @ w2[j].astype(f32) + b2[j].astype(f32)))
    h = (hid @ w3.astype(f32)).reshape(L, HYENA_ORDER, 2, D_MODEL)
    max_decay = math.log(HYENA_TARGET) / HYENA_STEEP_DECAY
    min_decay = math.log(HYENA_TARGET) / HYENA_GENTLE_DECAY
    deltas = jnp.linspace(min_decay, max_decay, D_MODEL, dtype=f32)
    h = h * jnp.exp(-t * jnp.abs(deltas))[:, None, None, :]
    h_fwd, h_bwd = h[:, :, 0], h[:, :, 1]
    taps_pos = h_fwd.at[0].add(h_bwd[0]).reshape(L, HYENA_ORDER * D_MODEL)
    taps_neg = h_bwd.at[0].set(0.0).reshape(L, HYENA_ORDER * D_MODEL)
    return taps_pos, taps_neg


FFT_N1 = 128
FFT_N2 = 64
FFT_N = FFT_N1 * FFT_N2
CONV_L = FFT_N // 2
CONV_T1 = CONV_L // FFT_N2
FFT_VMEM_LIMIT = 56 * 1024 * 1024
FFT_UNROLL = 8
FFT_MID_UNROLL = 16


def _dft_tables():
    def cos_sin(phase, n):
        ang = (2.0 * math.pi / n) * (phase % n).astype(jnp.float32)
        return jnp.cos(ang), jnp.sin(ang)
    t2 = jnp.arange(FFT_N2, dtype=jnp.int32)[:, None, None]
    k1 = jnp.arange(FFT_N1, dtype=jnp.int32)[None, :, None]
    t1 = jnp.arange(CONV_T1, dtype=jnp.int32)[None, None, :]
    c, s = cos_sin(FFT_N2 * t1 * k1 + t2 * k1, FFT_N)
    fwd = jnp.concatenate([c, -s], axis=1)
    inv = jnp.concatenate([c, s], axis=1).reshape(FFT_N2, 2, FFT_N1, CONV_T1)
    inv = inv.transpose(0, 1, 3, 2).reshape(FFT_N2, 2 * CONV_T1, FFT_N1) / FFT_N
    k2 = jnp.arange(FFT_N2, dtype=jnp.int32)
    c2, s2 = cos_sin(k2[:, None] * k2[None, :], FFT_N2)
    mid = jnp.concatenate([c2, -s2], axis=0)
    return fwd, mid, inv


def _mm(w, x):
    return jnp.dot(w, x.astype(jnp.bfloat16), preferred_element_type=jnp.float32)


def _long_conv_kernel(x_ref, g_ref, hr_ref, hi_ref, b_ref, wf_ref, wm_ref, wi_ref, o_ref, sr_ref, si_ref):
    def rows_t2(t2):
        return pl.ds(t2, CONV_T1, stride=FFT_N2)

    def stage_a(t2, carry):
        w = wf_ref[t2]
        p = _mm(w, x_ref.at[0][rows_t2(t2), :])
        q = _mm(w, x_ref.at[1][rows_t2(t2), :])
        dst = pl.ds(pl.multiple_of(t2 * FFT_N1, FFT_N1), FFT_N1)
        sr_ref[dst, :] = p[:FFT_N1] - q[FFT_N1:]
        si_ref[dst, :] = p[FFT_N1:] + q[:FFT_N1]
        return carry

    lax.fori_loop(0, FFT_N2, stage_a, 0, unroll=FFT_UNROLL)
    wm = wm_ref[...]

    def per_k1(k1, carry):
        sel = pl.ds(k1, FFT_N2, stride=FFT_N1)
        p = _mm(wm, sr_ref[sel, :])
        q = _mm(wm, si_ref[sel, :])
        xr = p[:FFT_N2] - q[FFT_N2:]
        xi = p[FFT_N2:] + q[:FFT_N2]
        hsel = pl.ds(pl.multiple_of(k1 * FFT_N2, FFT_N2), FFT_N2)
        hr = hr_ref[hsel, :]
        hi = hi_ref[hsel, :]
        p = _mm(wm, xr * hr - xi * hi)
        q = _mm(wm, xr * hi + xi * hr)
        sr_ref[sel, :] = p[:FFT_N2] + q[FFT_N2:]
        si_ref[sel, :] = q[:FFT_N2] - p[FFT_N2:]
        return carry

    lax.fori_loop(0, FFT_N1, per_k1, 0, unroll=FFT_MID_UNROLL)
    bias = b_ref[...]

    def stage_c(t2, carry):
        w = wi_ref[t2]
        src = pl.ds(pl.multiple_of(t2 * FFT_N1, FFT_N1), FFT_N1)
        p = _mm(w, sr_ref[src, :])
        q = _mm(w, si_ref[src, :])
        rows = rows_t2(t2)
        o_ref.at[0][rows, :] = g_ref.at[0][rows, :] * (p[:CONV_T1] - q[CONV_T1:] + x_ref.at[0][rows, :] * bias)
        o_ref.at[1][rows, :] = g_ref.at[1][rows, :] * (q[:CONV_T1] + p[CONV_T1:] + x_ref.at[1][rows, :] * bias)
        return carry

    lax.fori_loop(0, FFT_N2, stage_c, 0, unroll=FFT_UNROLL)


def _split_bf16(w):
    hi = w.astype(jnp.bfloat16)
    return hi, (w - hi.astype(jnp.float32)).astype(jnp.bfloat16)


def _mm3(w_hi, w_lo, x):
    x_hi, x_lo = _split_bf16(x)
    f32 = jnp.float32
    return (jnp.dot(w_hi, x_hi, preferred_element_type=f32) + jnp.dot(w_lo, x_hi, preferred_element_type=f32)
            + jnp.dot(w_hi, x_lo, preferred_element_type=f32))


def _filter_spectrum_kernel(a_ref, b_ref, wfh_ref, wfl_ref, wmh_ref, wml_ref, hr_ref, hi_ref,
                            ar_ref, ai_ref, br_ref, bi_ref):
    def rows_t2(t2):
        return pl.ds(t2, CONV_T1, stride=FFT_N2)

    def stage_a(t2, carry):
        wh, wl = wfh_ref[t2], wfl_ref[t2]
        p = _mm3(wh, wl, a_ref[rows_t2(t2), :])
        q = _mm3(wh, wl, b_ref[rows_t2(t2), :])
        dst = pl.ds(pl.multiple_of(t2 * FFT_N1, FFT_N1), FFT_N1)
        ar_ref[dst, :] = p[:FFT_N1]
        ai_ref[dst, :] = p[FFT_N1:]
        br_ref[dst, :] = q[:FFT_N1]
        bi_ref[dst, :] = q[FFT_N1:]
        return carry

    lax.fori_loop(0, FFT_N2, stage_a, 0, unroll=FFT_UNROLL)
    wmh, wml = wmh_ref[...], wml_ref[...]
    a = a_ref[...]
    b = b_ref[...]
    energy = jnp.sum(a * a, axis=0, keepdims=True) + jnp.sum(b * b, axis=0, keepdims=True)
    scale = lax.rsqrt(energy + HYENA_NORM_EPS)

    def per_k1(k1, carry):
        sel = pl.ds(k1, FFT_N2, stride=FFT_N1)
        pa, qa = _mm3(wmh, wml, ar_ref[sel, :]), _mm3(wmh, wml, ai_ref[sel, :])
        pb, qb = _mm3(wmh, wml, br_ref[sel, :]), _mm3(wmh, wml, bi_ref[sel, :])
        dst = pl.ds(pl.multiple_of(k1 * FFT_N2, FFT_N2), FFT_N2)
        a_re, a_im = pa[:FFT_N2] - qa[FFT_N2:], pa[FFT_N2:] + qa[:FFT_N2]
        b_re, b_im = pb[:FFT_N2] - qb[FFT_N2:], pb[FFT_N2:] + qb[:FFT_N2]
        hr_ref[dst, :] = (a_re + b_re) * scale
        hi_ref[dst, :] = (a_im - b_im) * scale
        return carry

    lax.fori_loop(0, FFT_N1, per_k1, 0, unroll=FFT_UNROLL)


def filter_spectrum(a, b, tables):
    L, C = a.shape
    wf, wm, _ = tables
    sig = pl.BlockSpec((L, VREG_LANES), lambda c: (0, c))
    out = pl.BlockSpec((FFT_N, VREG_LANES), lambda c: (0, c))
    once = pl.Buffered(1)
    plane = pltpu.VMEM((FFT_N, VREG_LANES), jnp.float32)
    return pl.pallas_call(
        _filter_spectrum_kernel,
        grid=(C // VREG_LANES,),
        in_specs=[sig, sig] + [pl.BlockSpec(wf.shape, lambda c: (0, 0, 0), pipeline_mode=once)] * 2
        + [pl.BlockSpec(wm.shape, lambda c: (0, 0), pipeline_mode=once)] * 2,
        out_specs=[out, out],
        out_shape=[jax.ShapeDtypeStruct((FFT_N, C), jnp.float32)] * 2,
        scratch_shapes=[plane] * 4,
        compiler_params=pltpu.CompilerParams(dimension_semantics=("arbitrary",), vmem_limit_bytes=FFT_VMEM_LIMIT),
        name="filter_spectrum",
    )(a, b, *_split_bf16(wf), *_split_bf16(wm))


def long_conv_gated(u, u_col, gate, gate_col, h_re, h_im, h_col, bias, tables):
    B, L, _ = u.shape
    C = bias.shape[0]
    assert L == CONV_L and B % 2 == 0 and C % VREG_LANES == 0
    assert u_col % VREG_LANES == 0 and gate_col % VREG_LANES == 0 and h_col % VREG_LANES == 0
    consts = [t.astype(jnp.bfloat16) for t in tables]
    sig_spec = pl.BlockSpec((2, L, VREG_LANES), lambda c, b: (b, 0, c))
    u_spec = pl.BlockSpec((2, L, VREG_LANES), lambda c, b: (b, 0, c + u_col // VREG_LANES))
    gate_spec = pl.BlockSpec((2, L, VREG_LANES), lambda c, b: (b, 0, c + gate_col // VREG_LANES))
    once = pl.Buffered(1)
    spec_h = pl.BlockSpec((FFT_N, VREG_LANES), lambda c, b: (0, c + h_col // VREG_LANES), pipeline_mode=once)
    const_specs = [pl.BlockSpec(t.shape, functools.partial(lambda nd, c, b: (0,) * nd, t.ndim), pipeline_mode=once)
                   for t in consts]
    return pl.pallas_call(
        _long_conv_kernel,
        grid=(C // VREG_LANES, B // 2),
        in_specs=[u_spec, gate_spec, spec_h, spec_h, pl.BlockSpec((1, VREG_LANES), lambda c, b: (0, c))] + const_specs,
        out_specs=sig_spec,
        out_shape=jax.ShapeDtypeStruct((B, L, C), jnp.float32),
        scratch_shapes=[pltpu.VMEM((FFT_N, VREG_LANES), jnp.float32), pltpu.VMEM((FFT_N, VREG_LANES), jnp.float32)],
        compiler_params=pltpu.CompilerParams(dimension_semantics=("arbitrary", "arbitrary"),
                                             vmem_limit_bytes=FFT_VMEM_LIMIT),
        name="long_conv",
    )(u, gate, h_re, h_im, bias.reshape(1, C), *consts)


def hyena_layer(xt, mods, n_batch, w_in, conv_w, conv_b, w1, b1, w2, b2, w3, freq, fbias, w_out, ln_g, ln_b):
    sh1, sc1, g1 = mods
    T = xt.shape[0]
    L = T // n_batch
    tables = _dft_tables()
    h_re, h_im = filter_spectrum(*hyena_filters(L, w1, b1, w2, b2, w3, freq), tables)
    u = mod_matmul_conv(xt, sc1, sh1, w_in, conv_w[None], conv_b, n_batch, 1, L, False).reshape(n_batch, L, 3 * D_MODEL)
    zz = long_conv_gated(u, 2 * D_MODEL, u, 0, h_re, h_im, 0, fbias[0], tables)
    zz = long_conv_gated(zz, 0, u, D_MODEL, h_re, h_im, D_MODEL, fbias[1], tables)
    return proj_ln(zz.reshape(T, D_MODEL), w_out, xt, g1, ln_g, ln_b)


D_TILES = D_MODEL // VREG_LANES
PEER_SLOTS = PEER_HEADS * PEER_TOPK
PEER_TB = 512
PEER_SUB = 64
PEER_VMEM_LIMIT = 58 * 1024 * 1024
HI_MASK = -65536


PACK_TB = 256


def _pack_rows_kernel(lo_ref, hi_ref, o_ref):
    def bf16_bits(v):
        return pltpu.bitcast(v.astype(jnp.bfloat16).astype(jnp.float32), jnp.int32)
    word = (bf16_bits(hi_ref[0]) & HI_MASK) | lax.shift_right_logical(bf16_bits(lo_ref[0]), 16)
    for k in range(D_TILES):
        o_ref[pl.ds(k, PACK_TB, stride=D_TILES), :] = word[:, k * VREG_LANES:(k + 1) * VREG_LANES]


def _pack_rows_bf16(tabs, layer):
    _, n, d = tabs.shape
    half = n // 2
    nblk = half // PACK_TB
    return pl.pallas_call(
        _pack_rows_kernel,
        grid=(nblk,),
        in_specs=[pl.BlockSpec((1, PACK_TB, d), lambda i: (layer, i, 0)),
                  pl.BlockSpec((1, PACK_TB, d), lambda i: (layer, i + nblk, 0))],
        out_specs=pl.BlockSpec((PACK_TB * D_TILES, VREG_LANES), lambda i: (i, 0)),
        out_shape=jax.ShapeDtypeStruct((half * D_TILES, VREG_LANES), jnp.int32),
        compiler_params=pltpu.CompilerParams(dimension_semantics=("arbitrary",)),
        name="pack_rows",
    )(tabs, tabs)


def _unpack_row(tab_ref, off, shift):
    word = tab_ref[pl.ds(pl.multiple_of(off, VREG_SUBLANES), VREG_SUBLANES), :]
    return pltpu.bitcast((word << shift) & HI_MASK, jnp.float32)


def _sublane_sums(tiles, sub):
    for h in (1, 2, 4):
        keep = (sub & h) == 0
        nxt = []
        for a, b in zip(tiles[0::2], tiles[1::2]):
            nxt.append(jnp.where(keep, a, b) + pltpu.roll(jnp.where(keep, b, a), h, 0))
        tiles = nxt
    return tiles[0]


def _splat_tile(row):
    return jnp.broadcast_to(row, (PEER_SLOTS, VREG_LANES)).T


def _splat_small_ints_tile(row, eye, ones):
    diag = jnp.where(eye, row.astype(jnp.float32), 0.0).astype(jnp.bfloat16)
    return jnp.dot(diag, ones, preferred_element_type=jnp.float32).astype(jnp.int32)


def _splat_consts():
    eye = (lax.broadcasted_iota(jnp.int32, (PEER_SLOTS, PEER_SLOTS), 0)
           == lax.broadcasted_iota(jnp.int32, (PEER_SLOTS, PEER_SLOTS), 1))
    return eye, jnp.ones((PEER_SLOTS, VREG_LANES), jnp.bfloat16)


def _splat_token(t0, t, shift_ref, shift_s, w_ref, w_s, eye, ones):
    dst = pl.ds(pl.multiple_of(t * PEER_SLOTS, PEER_SLOTS), PEER_SLOTS)
    shift_s[dst, :] = _splat_small_ints_tile(shift_ref[pl.ds(t0 + t, 1), :], eye, ones)
    if w_ref is not None:
        w_s[dst, :] = _splat_tile(w_ref[pl.ds(t0 + t, 1), :])


def _splat_block(t0, shift_ref, shift_s, w_ref=None, w_s=None):
    eye, ones = _splat_consts()

    def token(t, carry):
        _splat_token(t0, t, shift_ref, shift_s, w_ref, w_s, eye, ones)
        return carry
    lax.fori_loop(0, PEER_SUB, token, 0, unroll=8)


def _tile_rows(k):
    return pl.ds(k, PEER_SUB, stride=D_TILES)


def _peer_act_kernel(row_ref, shift_ref, x_ref, sc_ref, sh_ref, gate_ref, tab_ref, w_ref,
                     part_ref, shift_a, shift_b, xt_ref):
    sub = lax.broadcasted_iota(jnp.int32, (VREG_SUBLANES, VREG_LANES), 0)
    contract_lanes = (((1,), (1,)), ((), ()))
    ones = jnp.ones((VREG_SUBLANES, VREG_LANES), jnp.bfloat16)
    eye, splat_ones = _splat_consts()
    bufs = (shift_a, shift_b)
    starts = list(range(0, PEER_TB, PEER_SUB))
    _splat_block(0, shift_ref, bufs[0])
    for j, t0 in enumerate(starts):
        shift_s, shift_next = bufs[j % 2], bufs[(j + 1) % 2]
        t_next = starts[j + 1] if j + 1 < len(starts) else None
        h = x_ref[t0:t0 + PEER_SUB, :] * (1.0 + sc_ref[0]) + sh_ref[0]
        for k in range(D_TILES):
            xt_ref[_tile_rows(k), :] = h[:, k * VREG_LANES:(k + 1) * VREG_LANES]

        def token(t, carry, t0=t0, shift_s=shift_s, shift_next=shift_next, t_next=t_next):
            if t_next is not None:
                _splat_token(t_next, t, shift_ref, shift_next, None, None, eye, splat_ones)
            xt = xt_ref[pl.ds(pl.multiple_of(t * D_TILES, D_TILES), D_TILES), :]
            base = pl.multiple_of(t * PEER_SLOTS, PEER_SLOTS)
            tok_rows = row_ref.at[0, 0, pl.ds(t0 * PEER_SLOTS + base, PEER_SLOTS)]
            for g in range(PEER_SLOTS // VREG_SUBLANES):
                prods = []
                for k in range(VREG_SUBLANES):
                    s = g * VREG_SUBLANES + k
                    prods.append(_unpack_row(tab_ref, tok_rows[s], shift_s[pl.ds(base + s, 1), :]) * xt)
                part_ref[pl.ds(base + g * VREG_SUBLANES, VREG_SUBLANES), :] = _sublane_sums(prods, sub)
            return carry

        lax.fori_loop(0, PEER_SUB, token, 0)
        act = lax.dot_general(ones, part_ref[...].astype(jnp.bfloat16), contract_lanes,
                              preferred_element_type=jnp.float32)[0:1]
        gelu = 0.5 * act * (1.0 + lax.erf(act * (2.0 ** -0.5)))
        lanes = slice(t0 * PEER_SLOTS, (t0 + PEER_SUB) * PEER_SLOTS)
        w_ref[0, :, lanes] = gate_ref[0, :, lanes] * gelu


def _peer_out_kernel(row_ref, shift_ref, w_ref, tab_ref, x_ref, g_ref, lng_ref, lnb_ref, o_ref,
                     shift_a, shift_b, w_a, w_b, yt_ref):
    n_acc = 2
    eye, splat_ones = _splat_consts()
    shift_bufs, w_bufs = (shift_a, shift_b), (w_a, w_b)
    starts = list(range(0, PEER_TB, PEER_SUB))
    _splat_block(0, shift_ref, shift_bufs[0], w_ref, w_bufs[0])
    for j, t0 in enumerate(starts):
        shift_s, w_s = shift_bufs[j % 2], w_bufs[j % 2]
        shift_next, w_next = shift_bufs[(j + 1) % 2], w_bufs[(j + 1) % 2]
        t_next = starts[j + 1] if j + 1 < len(starts) else None

        def pair(i, carry, t0=t0, shift_s=shift_s, w_s=w_s, shift_next=shift_next, w_next=w_next, t_next=t_next):
            if t_next is not None:
                for k in range(2):
                    _splat_token(t_next, 2 * i + k, shift_ref, shift_next, w_ref, w_next, eye, splat_ones)
            bases = [pl.multiple_of((2 * i + k) * PEER_SLOTS, PEER_SLOTS) for k in range(2)]
            rows = [row_ref.at[0, 0, pl.ds(t0 * PEER_SLOTS + b_, PEER_SLOTS)] for b_ in bases]
            accs = [[jnp.zeros((VREG_SUBLANES, VREG_LANES), jnp.float32) for _ in range(n_acc)] for _ in range(2)]
            for s in range(PEER_SLOTS):
                for k in range(2):
                    val = _unpack_row(tab_ref, rows[k][s], shift_s[pl.ds(bases[k] + s, 1), :])
                    accs[k][s % n_acc] = accs[k][s % n_acc] + w_s[pl.ds(bases[k] + s, 1), :] * val
            for k in range(2):
                yt_ref[pl.ds(pl.multiple_of((2 * i + k) * D_TILES, D_TILES), D_TILES), :] = accs[k][0] + accs[k][1]
            return carry

        lax.fori_loop(0, PEER_SUB // 2, pair, 0)
        y = jnp.concatenate([yt_ref[_tile_rows(k), :] for k in range(D_TILES)], axis=1)
        rows_out = slice(t0, t0 + PEER_SUB)
        o_ref[rows_out, :] = _layer_norm_rows(DEEPNORM_ALPHA * x_ref[rows_out, :] + g_ref[0] * y,
                                              lng_ref[...], lnb_ref[...])


def peer_experts_ln(x, mods, offs, shifts, gate, u_tabs, v_tabs, layer, ln_g, ln_b):
    sh, sc, g = mods
    T = x.shape[0]
    nblk = T // PEER_TB
    per_blk = PEER_TB * PEER_SLOTS
    per_mod = T // sc.shape[0] // PEER_TB
    offs = offs.reshape(nblk, 1, per_blk)
    smem_spec = pl.BlockSpec((1, 1, per_blk), lambda i: (i, 0, 0), memory_space=pltpu.SMEM)
    lane_spec = pl.BlockSpec((1, 1, per_blk), lambda i: (i, 0, 0))
    slot_spec = pl.BlockSpec((PEER_TB, PEER_SLOTS), lambda i: (i, 0))
    tok_spec = pl.BlockSpec((PEER_TB, D_MODEL), lambda i: (i, 0))
    mod_spec = pl.BlockSpec((1, 1, D_MODEL), lambda i: (i // per_mod, 0, 0))
    vec_spec = pl.BlockSpec((1, D_MODEL), lambda i: (0, 0))
    tab_spec = pl.BlockSpec(memory_space=pltpu.VMEM)
    params = pltpu.CompilerParams(dimension_semantics=("arbitrary",), vmem_limit_bytes=PEER_VMEM_LIMIT)
    splat_i32 = pltpu.VMEM((PEER_SUB * PEER_SLOTS, VREG_LANES), jnp.int32)
    splat_f32 = pltpu.VMEM((PEER_SUB * PEER_SLOTS, VREG_LANES), jnp.float32)
    tok_tiles = pltpu.VMEM((PEER_SUB * D_TILES, VREG_LANES), jnp.float32)
    w = pl.pallas_call(
        _peer_act_kernel,
        grid=(nblk,),
        in_specs=[smem_spec, slot_spec, tok_spec, mod_spec, mod_spec, lane_spec, tab_spec],
        out_specs=lane_spec,
        out_shape=jax.ShapeDtypeStruct((nblk, 1, per_blk), jnp.float32),
        scratch_shapes=[splat_f32, splat_i32, splat_i32, tok_tiles],
        compiler_params=params,
        name="peer_act",
    )(offs, shifts, x, sc, sh, gate.reshape(nblk, 1, per_blk), _pack_rows_bf16(u_tabs, layer))
    return pl.pallas_call(
        _peer_out_kernel,
        grid=(nblk,),
        in_specs=[smem_spec, slot_spec, slot_spec, tab_spec, tok_spec, mod_spec, vec_spec, vec_spec],
        out_specs=tok_spec,
        out_shape=jax.ShapeDtypeStruct((T, D_MODEL), jnp.float32),
        scratch_shapes=[splat_i32, splat_i32, splat_f32, splat_f32, tok_tiles],
        compiler_params=params,
        name="peer_out",
    )(offs, shifts, w.reshape(T, PEER_SLOTS), _pack_rows_bf16(v_tabs, layer), x, g,
      ln_g.reshape(1, D_MODEL), ln_b.reshape(1, D_MODEL))


def _top_rows(s, k):
    n = s.shape[0]
    row = lax.broadcasted_iota(jnp.int32, s.shape, 0).astype(jnp.float32)
    vals, idxs = [], []
    for _ in range(k):
        m = jnp.max(s, axis=0, keepdims=True)
        i = jnp.min(jnp.where(s == m, row, float(n)), axis=0, keepdims=True)
        vals.append(m)
        idxs.append(i)
        s = jnp.where(row == i, -jnp.inf, s)
    return jnp.concatenate(vals, axis=0), jnp.concatenate(idxs, axis=0).astype(jnp.int32)


def _pick_rows(table, which):
    out = jnp.zeros(which.shape, table.dtype)
    for a in range(table.shape[0]):
        out = jnp.where(which == a, table[a:a + 1, :], out)
    return out


PAIR_COUNTS = tuple(PEER_TOPK // (a + 1) for a in range(PEER_TOPK))
PAIR_STARTS = tuple(sum(PAIR_COUNTS[:a]) for a in range(PEER_TOPK))
N_PAIRS = sum(PAIR_COUNTS)
N_PAIRS_PADDED = -(-N_PAIRS // VREG_SUBLANES) * VREG_SUBLANES
SEL_TB = 2048
PEER_PACKED_ROWS = PEER_NKEYS * PEER_NKEYS // 2


def _peer_select_kernel(x_ref, sc_ref, sh_ref, wq_ref, k1_ref, k2_ref, off_ref, shift_ref, gate_ref,
                        off_s, shift_s, gate_s):
    half = PEER_DK // 2
    h = (x_ref[...] * (1.0 + sc_ref[0]) + sh_ref[0]).astype(jnp.bfloat16)
    q = jnp.dot(h, wq_ref[...], preferred_element_type=jnp.float32).astype(jnp.bfloat16)
    contract_last = (((1,), (1,)), ((), ()))
    s1 = lax.dot_general(k1_ref[0], q[:, :half], contract_last, preferred_element_type=jnp.float32)
    s2 = lax.dot_general(k2_ref[0], q[:, half:], contract_last, preferred_element_type=jnp.float32)
    v1, i1 = _top_rows(s1, PEER_TOPK)
    v2, i2 = _top_rows(s2, PEER_TOPK)
    pairs = [v1[a:a + 1, :] + v2[:PAIR_COUNTS[a], :] for a in range(PEER_TOPK)]
    pairs.append(jnp.full((N_PAIRS_PADDED - N_PAIRS, v1.shape[1]), -jnp.inf, jnp.float32))
    top_s, pos = _top_rows(jnp.concatenate(pairs, axis=0), PEER_TOPK)
    a_sel = jnp.zeros(pos.shape, jnp.int32)
    start = jnp.zeros(pos.shape, jnp.int32)
    for a in range(1, PEER_TOPK):
        later = pos >= PAIR_STARTS[a]
        a_sel = jnp.where(later, a, a_sel)
        start = jnp.where(later, PAIR_STARTS[a], start)
    expert = _pick_rows(i1, a_sel) * PEER_NKEYS + _pick_rows(i2, pos - start)
    e = jnp.exp(top_s - top_s[0:1, :])
    head = pl.program_id(1)
    rows = pl.ds(pl.multiple_of(head * PEER_TOPK, PEER_TOPK), PEER_TOPK)
    gate_s[rows, :] = e / jnp.sum(e, axis=0, keepdims=True)
    off_s[rows, :] = (expert & (PEER_PACKED_ROWS - 1)) * VREG_SUBLANES
    shift_s[rows, :] = jnp.where(expert >= PEER_PACKED_ROWS, 0, 16)

    @pl.when(head == PEER_HEADS - 1)
    def _():
        gate_ref[...] = gate_s[...].T
        off_ref[...] = off_s[...].astype(jnp.float32).T.astype(jnp.int32)
        shift_ref[...] = shift_s[...].astype(jnp.float32).T.astype(jnp.int32)


def peer_select(x, sc, sh, wq, k1, k2):
    T = x.shape[0]
    per_seq = T // sc.shape[0] // SEL_TB
    bf16 = jnp.bfloat16
    tok_spec = pl.BlockSpec((SEL_TB, D_MODEL), lambda i, h: (i, 0))
    mod_spec = pl.BlockSpec((1, 1, D_MODEL), lambda i, h: (i // per_seq, 0, 0))
    wq_spec = pl.BlockSpec((D_MODEL, PEER_DK), lambda i, h: (0, h))
    key_spec = pl.BlockSpec((1, PEER_NKEYS, PEER_DK // 2), lambda i, h: (h, 0, 0))
    out_spec = pl.BlockSpec((SEL_TB, PEER_SLOTS), lambda i, h: (i, 0))
    return pl.pallas_call(
        _peer_select_kernel,
        grid=(T // SEL_TB, PEER_HEADS),
        in_specs=[tok_spec, mod_spec, mod_spec, wq_spec, key_spec, key_spec],
        out_specs=[out_spec, out_spec, out_spec],
        out_shape=[jax.ShapeDtypeStruct((T, PEER_SLOTS), jnp.int32), jax.ShapeDtypeStruct((T, PEER_SLOTS), jnp.int32),
                   jax.ShapeDtypeStruct((T, PEER_SLOTS), jnp.float32)],
        scratch_shapes=[pltpu.VMEM((PEER_SLOTS, SEL_TB), jnp.int32), pltpu.VMEM((PEER_SLOTS, SEL_TB), jnp.int32),
                        pltpu.VMEM((PEER_SLOTS, SEL_TB), jnp.float32)],
        compiler_params=pltpu.CompilerParams(dimension_semantics=("arbitrary", "arbitrary")),
        name="peer_select",
    )(x, sc, sh, wq.astype(bf16), k1.astype(bf16), k2.astype(bf16))


def peer_layer(xt, mods, wq, k1, k2, u_tabs, v_tabs, layer, ln_g, ln_b):
    sh2, sc2, _ = mods
    off, shift, gate = peer_select(xt, sc2, sh2, wq, k1, k2)
    return peer_experts_ln(xt, mods, off, shift, gate, u_tabs, v_tabs, layer, ln_g, ln_b)


def kernel(x, c, ctx, c_ctx, ada_w, ada_b, ln_g, ln_b,
           m_w_in, m_conv_w, m_conv_b, m_a_log, m_dt_bias, m_d, m_norm_w, m_w_out,
           h_w_in, h_conv_w, h_conv_b, h_f_w1, h_f_b1, h_f_w2, h_f_b2, h_f_w3, h_freq, h_bias, h_w_out,
           p_wq, p_k1, p_k2, p_u, p_v):
    n_batch, seq, _ = x.shape
    rows = seq // GRID_W
    s_lat = jax.nn.silu(c)
    s_ctx = jax.nn.silu(c_ctx)
    xt = x.reshape(n_batch * seq, D_MODEL)
    ctx_t = ctx.reshape(-1, D_MODEL)
    for i in range(DEPTH):
        kind, k = i % N_MIXERS, i // N_MIXERS
        sh1, sc1, g1, sh2, sc2, g2 = jnp.split((s_lat @ ada_w[i] + ada_b[i])[:, None, :], 6, axis=-1)
        if kind == 0:
            csh1, csc1 = jnp.split((s_ctx @ ada_w[i] + ada_b[i])[None, None, :], 6, axis=-1)[:2]
            xt = mamba_layer(xt, ctx_t, (sh1, sc1, g1), (csh1, csc1), n_batch, rows, m_w_in[k], m_conv_w[k],
                             m_conv_b[k], m_a_log[k], m_dt_bias[k], m_d[k], m_norm_w[k], m_w_out[k],
                             ln_g[i, 0], ln_b[i, 0])
        else:
            xt = hyena_layer(xt, (sh1, sc1, g1), n_batch, h_w_in[k], h_conv_w[k], h_conv_b[k], h_f_w1[k], h_f_b1[k],
                             h_f_w2[k], h_f_b2[k], h_f_w3[k], h_freq[k], h_bias[k], h_w_out[k], ln_g[i, 0], ln_b[i, 0])
        xt = peer_layer(xt, (sh2, sc2, g2), p_wq[i], p_k1[i], p_k2[i], p_u, p_v, i, ln_g[i, 1], ln_b[i, 1])
    return xt.reshape(x.shape)
```

```python
import functools
import math

import jax
import jax.numpy as jnp
from jax import lax
from jax.experimental import pallas as pl
from jax.experimental.pallas import tpu as pltpu

D_MODEL = 1024
DEPTH = 2
GRID_W = 64
N_MIXERS = 2
DEEPNORM_ALPHA = (2 * DEPTH) ** 0.25
LN_EPS = 1e-5
RMS_EPS = 1e-5

SSM_INNER = 2 * D_MODEL
SSM_HEAD_DIM = 64
SSM_HEADS = SSM_INNER // SSM_HEAD_DIM
SSM_GROUPS = 4
SSM_STATE = 128
SSM_CHUNK = 128
SSM_CONV = 3
SSM_CONV_DIM = SSM_INNER + 2 * SSM_GROUPS * SSM_STATE
SSM_GROUP_HEADS = SSM_HEADS // SSM_GROUPS
SSM_GROUP_DIM = SSM_GROUP_HEADS * SSM_HEAD_DIM

HYENA_ORDER = 2
HYENA_EMB = 33
HYENA_INNER_MLPS = 2
HYENA_STEEP_DECAY = 0.3
HYENA_GENTLE_DECAY = 1.5
HYENA_TARGET = 1e-2
HYENA_NORM_EPS = 1e-6

PEER_HEADS = 8
PEER_NKEYS = 128
PEER_DK = 256
PEER_TOPK = 16

VREG_SUBLANES = 8
VREG_LANES = 128
DENSE_VMEM_LIMIT = 48 * 1024 * 1024


def _mod_matmul_kernel(x_ref, sc_ref, sh_ref, w_ref, o_ref):
    h = (x_ref[...] * (1.0 + sc_ref[0]) + sh_ref[0]).astype(jnp.bfloat16)
    o_ref[...] = jnp.dot(h, w_ref[...], preferred_element_type=jnp.float32)


def mod_matmul(x, sc, sh, w, tb, tn):
    T, D = x.shape
    N = w.shape[1]
    assert T % tb == 0 and N % tn == 0 and (T // sc.shape[0]) % tb == 0
    per_mod = T // sc.shape[0] // tb
    mod_spec = pl.BlockSpec((1, 1, D), lambda i, j: (i // per_mod, 0, 0))
    return pl.pallas_call(
        _mod_matmul_kernel,
        grid=(T // tb, N // tn),
        in_specs=[pl.BlockSpec((tb, D), lambda i, j: (i, 0)), mod_spec, mod_spec,
                  pl.BlockSpec((D, tn), lambda i, j: (0, j))],
        out_specs=pl.BlockSpec((tb, tn), lambda i, j: (i, j)),
        out_shape=jax.ShapeDtypeStruct((T, N), jnp.float32),
        compiler_params=pltpu.CompilerParams(dimension_semantics=("arbitrary", "arbitrary"),
                                             vmem_limit_bytes=DENSE_VMEM_LIMIT),
        name="mod_matmul",
    )(x, sc, sh, w.astype(jnp.bfloat16))


def _layer_norm_rows(v, g, b):
    mu = jnp.mean(v, axis=-1, keepdims=True)
    vc = v - mu
    var = jnp.mean(vc * vc, axis=-1, keepdims=True)
    return vc * lax.rsqrt(var + LN_EPS) * g + b


def _proj_ln_kernel(a_ref, w_ref, x_ref, gate_ref, g_ref, b_ref, o_ref):
    y = jnp.dot(a_ref[...].astype(jnp.bfloat16), w_ref[...], preferred_element_type=jnp.float32)
    o_ref[...] = _layer_norm_rows(DEEPNORM_ALPHA * x_ref[...] + gate_ref[0] * y, g_ref[...], b_ref[...])


def _row_specs(tb, per_mod):
    tok = pl.BlockSpec((tb, D_MODEL), lambda i: (i, 0))
    mod = pl.BlockSpec((1, 1, D_MODEL), lambda i: (i // per_mod, 0, 0))
    vec = pl.BlockSpec((1, D_MODEL), lambda i: (0, 0))
    return tok, mod, vec


def proj_ln(a, w, x, gate, ln_g, ln_b, tb=512):
    T, K = a.shape
    per_mod = T // gate.shape[0] // tb
    tok, mod, vec = _row_specs(tb, per_mod)
    return pl.pallas_call(
        _proj_ln_kernel,
        grid=(T // tb,),
        in_specs=[pl.BlockSpec((tb, K), lambda i: (i, 0)), pl.BlockSpec((K, D_MODEL), lambda i: (0, 0)),
                  tok, mod, vec, vec],
        out_specs=tok,
        out_shape=jax.ShapeDtypeStruct((T, D_MODEL), jnp.float32),
        compiler_params=pltpu.CompilerParams(dimension_semantics=("arbitrary",), vmem_limit_bytes=DENSE_VMEM_LIMIT),
        name="proj_ln",
    )(a, w.astype(jnp.bfloat16), x, gate, ln_g.reshape(1, D_MODEL), ln_b.reshape(1, D_MODEL))


def _mamba_out_kernel(y_ref, xs_ref, z_ref, dsk_ref, nw_ref, w_ref, x_ref, gate_ref, g_ref, b_ref, o_ref):
    z = z_ref[...]
    u = (y_ref[0] + y_ref[1] + xs_ref[...] * dsk_ref[...]) * (z * jax.nn.sigmoid(z))
    gw = SSM_INNER // SSM_GROUPS
    parts = []
    for g in range(SSM_GROUPS):
        ug = u[:, g * gw:(g + 1) * gw]
        parts.append(ug * lax.rsqrt(jnp.mean(ug * ug, axis=-1, keepdims=True) + RMS_EPS))
    a = (jnp.concatenate(parts, axis=1) * nw_ref[...]).astype(jnp.bfloat16)
    y = jnp.dot(a, w_ref[...], preferred_element_type=jnp.float32)
    o_ref[...] = _layer_norm_rows(DEEPNORM_ALPHA * x_ref[...] + gate_ref[0] * y, g_ref[...], b_ref[...])


def mamba_out(y2, xbc, z, d_skip, norm_w, w_out, x, gate, ln_g, ln_b, tb=256):
    T = x.shape[0]
    tok, mod, vec = _row_specs(tb, T // gate.shape[0] // tb)
    inner = pl.BlockSpec((tb, SSM_INNER), lambda i: (i, 0))
    ivec = pl.BlockSpec((1, SSM_INNER), lambda i: (0, 0))
    dsk = jnp.repeat(d_skip, SSM_INNER // d_skip.shape[0]).reshape(1, SSM_INNER)
    return pl.pallas_call(
        _mamba_out_kernel,
        grid=(T // tb,),
        in_specs=[pl.BlockSpec((2, tb, SSM_INNER), lambda i: (0, i, 0)), inner, inner, ivec, ivec,
                  pl.BlockSpec((SSM_INNER, D_MODEL), lambda i: (0, 0)), tok, mod, vec, vec],
        out_specs=tok,
        out_shape=jax.ShapeDtypeStruct((T, D_MODEL), jnp.float32),
        compiler_params=pltpu.CompilerParams(dimension_semantics=("arbitrary",), vmem_limit_bytes=DENSE_VMEM_LIMIT),
        name="mamba_out",
    )(y2, xbc, z, dsk, norm_w.reshape(1, SSM_INNER), w_out.astype(jnp.bfloat16), x, gate,
      ln_g.reshape(1, D_MODEL), ln_b.reshape(1, D_MODEL))


CONV_VMEM_LIMIT = 56 * 1024 * 1024


def _shift_rows(a, k):
    n = a.shape[0]
    return a if k == 0 else pltpu.roll(a, (-k) % n, 0)


def _mod_matmul_conv_kernel(x_ref, sc_ref, sh_ref, w_ref, cw_ref, cb_ref, o_ref, h_ref, *, rows, cols, silu):
    @pl.when(pl.program_id(1) == 0)
    def _():
        h_ref[...] = (x_ref[...] * (1.0 + sc_ref[0]) + sh_ref[0]).astype(jnp.bfloat16)

    p = jnp.dot(h_ref[...], w_ref[...], preferred_element_type=jnp.float32)
    L = p.shape[0]
    t = lax.broadcasted_iota(jnp.int32, p.shape, 0)
    col = t & (cols - 1)
    n_dr = cw_ref.shape[0]
    out = jnp.zeros_like(p)
    for j in range(3):
        dc = j - 1
        acc = None
        for i in range(n_dr):
            dr = i - n_dr // 2
            term = _shift_rows(p, dr * cols)
            if dr < 0:
                term = jnp.where(t >= -dr * cols, term, 0.0)
            elif dr > 0:
                term = jnp.where(t < L - dr * cols, term, 0.0)
            term = term * cw_ref[i, j:j + 1, :]
            acc = term if acc is None else acc + term
        acc = _shift_rows(acc, dc)
        if dc < 0:
            acc = jnp.where(col >= -dc, acc, 0.0)
        elif dc > 0:
            acc = jnp.where(col < cols - dc, acc, 0.0)
        out = out + acc
    out = out + cb_ref[...]
    o_ref[...] = out * jax.nn.sigmoid(out) if silu else out


def mod_matmul_conv(x, sc, sh, w, conv_w, conv_b, n_seq, rows, cols, silu, tn=256):
    T, D = x.shape
    N = w.shape[1]
    L = rows * cols
    assert T == n_seq * L and N % tn == 0 and cols & (cols - 1) == 0 and conv_w.shape[0] in (1, 3)
    per_mod = n_seq // sc.shape[0]
    mod_spec = pl.BlockSpec((1, 1, D), lambda b, j: (b // per_mod, 0, 0))
    return pl.pallas_call(
        functools.partial(_mod_matmul_conv_kernel, rows=rows, cols=cols, silu=silu),
        grid=(n_seq, N // tn),
        in_specs=[pl.BlockSpec((L, D), lambda b, j: (b, 0), pipeline_mode=pl.Buffered(1)), mod_spec, mod_spec,
                  pl.BlockSpec((D, tn), lambda b, j: (0, j)),
                  pl.BlockSpec((conv_w.shape[0], 3, tn), lambda b, j: (0, 0, j)),
                  pl.BlockSpec((1, tn), lambda b, j: (0, j))],
        out_specs=pl.BlockSpec((L, tn), lambda b, j: (b, j)),
        out_shape=jax.ShapeDtypeStruct((T, N), jnp.float32),
        scratch_shapes=[pltpu.VMEM((L, D), jnp.bfloat16)],
        compiler_params=pltpu.CompilerParams(dimension_semantics=("arbitrary", "arbitrary"),
                                             vmem_limit_bytes=CONV_VMEM_LIMIT),
        name="mod_matmul_conv",
    )(x, sc, sh, w.astype(jnp.bfloat16), conv_w, conv_b.reshape(1, N))


def _split3_bf16(a):
    f32, bf16 = jnp.float32, jnp.bfloat16
    p0 = a.astype(bf16)
    r = a - p0.astype(f32)
    p1 = r.astype(bf16)
    p2 = (r - p1.astype(f32)).astype(bf16)
    return p0, p1, p2


def _ssd_kernel(x_ref, b_ref, c_ref, dt_ref, an_ref, init_ref, *rest, want_y):
    if want_y:
        y_ref, fin_ref, state_ref = rest
    else:
        fin_ref, state_ref = rest
    f32, bf16 = jnp.float32, jnp.bfloat16
    Q, P, H, G, N = SSM_CHUNK, SSM_HEAD_DIM, SSM_HEADS, SSM_GROUPS, SSM_STATE
    R, GD = SSM_GROUP_HEADS, SSM_GROUP_DIM
    fwd = pl.program_id(1) == 0
    c = pl.program_id(2)

    @pl.when(c == 0)
    def _():
        state_ref[...] = init_ref[0, 0].reshape(SSM_INNER, N)

    contract_rows = (((0,), (0,)), ((), ()))
    contract_lanes = (((1,), (1,)), ((), ()))
    dt = dt_ref[0, 0]
    a = dt * an_ref[0]
    li = lax.broadcasted_iota(jnp.int32, (Q, Q), 0)
    si = lax.broadcasted_iota(jnp.int32, (Q, Q), 1)
    tri = jnp.where(fwd, li - si, si - li) >= 0
    tri_bf = jnp.where(tri, 1.0, 0.0).astype(bf16)
    pieces = _split3_bf16(a)
    a_cum = sum(jnp.dot(tri_bf, p, preferred_element_type=f32) for p in pieces)
    a_cum_t = sum(lax.dot_general(p, tri_bf, (((0,), (1,)), ((), ())), preferred_element_type=f32)
                  for p in pieces)
    total = jnp.sum(a, axis=0, keepdims=True)
    spread = (lax.broadcasted_iota(jnp.int32, (H, SSM_INNER), 1) // P
              == lax.broadcasted_iota(jnp.int32, (H, SSM_INNER), 0)).astype(bf16)

    def per_lane(v):
        return sum(jnp.dot(p, spread, preferred_element_type=f32) for p in _split3_bf16(v))

    xdt = x_ref[0] * per_lane(dt)
    xw = (xdt * per_lane(jnp.exp(total - a_cum))).astype(bf16)
    tot_col = jnp.where(fwd, a_cum_t[:, Q - 1:Q], a_cum_t[:, 0:1])
    keep = jnp.broadcast_to(jnp.exp(tot_col), (H, N))
    keep_rows = sum(lax.dot_general(spread, p, contract_rows, preferred_element_type=f32)
                    for p in _split3_bf16(keep))
    if want_y:
        xdt_bf = xdt.astype(bf16)
        out_scale = per_lane(jnp.exp(a_cum))
        lane = lax.broadcasted_iota(jnp.int32, (Q, 2 * P), 1)
    ys = []
    for g in range(G):
        cols = slice(g * GD, (g + 1) * GD)
        bm = b_ref[0, :, g * N:(g + 1) * N].astype(bf16)
        cm = c_ref[0, :, g * N:(g + 1) * N].astype(bf16)
        state = state_ref[cols, :]
        if want_y:
            cb = lax.dot_general(cm, bm, contract_lanes, preferred_element_type=f32)
            for j in range(R // 2):
                ms = []
                for h in (g * R + 2 * j, g * R + 2 * j + 1):
                    decay = jnp.where(tri, jnp.exp(a_cum[:, h:h + 1] - a_cum_t[h:h + 1, :]), 0.0)
                    ms.append((cb * decay).astype(bf16))
                xp = xdt_bf[:, g * GD + 2 * j * P:g * GD + 2 * (j + 1) * P]
                zero = jnp.zeros_like(xp)
                rhs = jnp.concatenate([jnp.where(lane < P, xp, zero), jnp.where(lane >= P, xp, zero)], axis=0)
                ys.append(jnp.dot(jnp.concatenate(ms, axis=1), rhs, preferred_element_type=f32))
            y_off = lax.dot_general(cm, state.astype(bf16), contract_lanes, preferred_element_type=f32)
            y_ref[0, 0, :, cols] = jnp.concatenate(ys[-(R // 2):], axis=1) + y_off * out_scale[:, cols]
        upd = lax.dot_general(xw[:, cols], bm, contract_rows, preferred_element_type=f32)
        state_ref[cols, :] = state * keep_rows[cols, :] + upd
    fin_ref[0, 0] = state_ref[...].reshape(H, P, N)


def ssd_scan_both(xbc, dt, a_neg, init, want_y):
    B, L, _ = xbc.shape
    Q, G, H, P, N = SSM_CHUNK, SSM_GROUPS, SSM_HEADS, SSM_HEAD_DIM, SSM_STATE
    nc = L // Q
    dt_d = dt.transpose(0, 2, 1, 3)
    an = a_neg.reshape(2, 1, H)
    bc_cols = G * N

    def chunk(d, c):
        return c + d * (nc - 1 - 2 * c)

    x_spec = pl.BlockSpec((1, Q, SSM_INNER), lambda b, d, c: (b, chunk(d, c), 0))
    b_spec = pl.BlockSpec((1, Q, bc_cols), lambda b, d, c: (b, chunk(d, c), SSM_INNER // bc_cols))
    c_spec = pl.BlockSpec((1, Q, bc_cols), lambda b, d, c: (b, chunk(d, c), SSM_INNER // bc_cols + 1))
    dt_spec = pl.BlockSpec((1, 1, Q, H), lambda b, d, c: (b, d, chunk(d, c), 0))
    an_spec = pl.BlockSpec((1, 1, H), lambda b, d, c: (d, 0, 0))
    st_spec = pl.BlockSpec((1, 1, H, P, N), lambda b, d, c: (b, d, 0, 0, 0))
    st_shape = jax.ShapeDtypeStruct((B, 2, H, P, N), jnp.float32)
    out_specs, out_shape = [st_spec], [st_shape]
    if want_y:
        out_specs = [pl.BlockSpec((1, 1, Q, SSM_INNER), lambda b, d, c: (d, b, chunk(d, c), 0))] + out_specs
        out_shape = [jax.ShapeDtypeStruct((2, B, L, SSM_INNER), jnp.float32)] + out_shape
    outs = pl.pallas_call(
        functools.partial(_ssd_kernel, want_y=want_y),
        grid=(B, 2, nc),
        in_specs=[x_spec, b_spec, c_spec, dt_spec, an_spec, st_spec],
        out_specs=out_specs,
        out_shape=out_shape,
        scratch_shapes=[pltpu.VMEM((SSM_INNER, N), jnp.float32)],
        compiler_params=pltpu.CompilerParams(dimension_semantics=("arbitrary",) * 3),
        name="ssd_scan_y" if want_y else "ssd_scan_state",
    )(xbc, xbc, xbc, dt_d, an, init)
    if want_y:
        return outs[0], outs[1]
    return None, outs[0]


def mamba_layer(xt, ctx_t, mods, cmods, n_batch, rows, w_in, conv_w, conv_b, a_log, dt_bias, d_skip, norm_w, w_out,
                ln_g, ln_b):
    f32 = jnp.float32
    sh1, sc1, g1 = mods
    csh1, csc1 = cmods
    a_neg = -jnp.exp(a_log.astype(f32))
    w_xbc = w_in[:, SSM_INNER:SSM_INNER + SSM_CONV_DIM]
    w_dt = w_in[:, SSM_INNER + SSM_CONV_DIM:]

    def softplus_dt(dt_raw, length):
        return jax.nn.softplus(dt_raw.reshape(n_batch, length, 2, SSM_HEADS) + dt_bias.astype(f32))

    lc = ctx_t.shape[0] // n_batch
    xbc_c = mod_matmul_conv(ctx_t, csc1, csh1, w_xbc, conv_w[1:2], conv_b, n_batch, 1, lc, True)
    dt_c = softplus_dt(mod_matmul(ctx_t, csc1, csh1, w_dt, 512, 2 * SSM_HEADS), lc)
    zeros = jnp.zeros((n_batch, 2, SSM_HEADS, SSM_HEAD_DIM, SSM_STATE), f32)
    _, states = ssd_scan_both(xbc_c.reshape(n_batch, lc, SSM_CONV_DIM), dt_c, a_neg, zeros, False)
    T = xt.shape[0]
    L = T // n_batch
    z = mod_matmul(xt, sc1, sh1, w_in[:, :SSM_INNER], 1024, 1024)
    xbc = mod_matmul_conv(xt, sc1, sh1, w_xbc, conv_w, conv_b, n_batch, rows, GRID_W, True)
    dt = softplus_dt(mod_matmul(xt, sc1, sh1, w_dt, 1024, 2 * SSM_HEADS), L)
    y2, _ = ssd_scan_both(xbc.reshape(n_batch, L, SSM_CONV_DIM), dt, a_neg, states, True)
    return mamba_out(y2.reshape(2, T, SSM_INNER), xbc, z, d_skip, norm_w, w_out, xt, g1, ln_g, ln_b)


def hyena_filters(L, w1, b1, w2, b2, w3, freq):
    f32 = jnp.float32
    t = jnp.linspace(0.0, 1.0, L, dtype=f32)[:, None]
    bands = (HYENA_EMB - 1) // 2
    ang = 2.0 * math.pi * jnp.arange(L, dtype=f32)[:, None] / L
    fb = jnp.linspace(1e-4, bands - 1, bands, dtype=f32)[None, :]
    z = jnp.concatenate([t, jnp.cos(fb * ang), -jnp.sin(fb * ang)], axis=-1)
    fr = freq.astype(f32)
    hid = jnp.sin(fr * (z @ w1.astype(f32) + b1.astype(f32)))
    for j in range(HYENA_INNER_MLPS):
        hid = jnp.sin(fr * (hid @ w2[j].astype(f32) + b2[j].astype(f32)))
    h = (hid @ w3.astype(f32)).reshape(L, HYENA_ORDER, 2, D_MODEL)
    max_decay = math.log(HYENA_TARGET) / HYENA_STEEP_DECAY
    min_decay = math.log(HYENA_TARGET) / HYENA_GENTLE_DECAY
    deltas = jnp.linspace(min_decay, max_decay, D_MODEL, dtype=f32)
    h = h * jnp.exp(-t * jnp.abs(deltas))[:, None, None, :]
    h_fwd, h_bwd = h[:, :, 0], h[:, :, 1]
    taps_pos = h_fwd.at[0].add(h_bwd[0]).reshape(L, HYENA_ORDER * D_MODEL)
    taps_neg = h_bwd.at[0].set(0.0).reshape(L, HYENA_ORDER * D_MODEL)
    return taps_pos, taps_neg


FFT_N1 = 128
FFT_N2 = 64
FFT_N = FFT_N1 * FFT_N2
CONV_L = FFT_N // 2
CONV_T1 = CONV_L // FFT_N2
FFT_VMEM_LIMIT = 56 * 1024 * 1024
FFT_UNROLL = 8
FFT_MID_UNROLL = 16


def _dft_tables():
    def cos_sin(phase, n):
        ang = (2.0 * math.pi / n) * (phase % n).astype(jnp.float32)
        return jnp.cos(ang), jnp.sin(ang)
    t2 = jnp.arange(FFT_N2, dtype=jnp.int32)[:, None, None]
    k1 = jnp.arange(FFT_N1, dtype=jnp.int32)[None, :, None]
    t1 = jnp.arange(CONV_T1, dtype=jnp.int32)[None, None, :]
    c, s = cos_sin(FFT_N2 * t1 * k1 + t2 * k1, FFT_N)
    fwd = jnp.concatenate([c, -s], axis=1)
    inv = jnp.concatenate([c, s], axis=1).reshape(FFT_N2, 2, FFT_N1, CONV_T1)
    inv = inv.transpose(0, 1, 3, 2).reshape(FFT_N2, 2 * CONV_T1, FFT_N1) / FFT_N
    k2 = jnp.arange(FFT_N2, dtype=jnp.int32)
    c2, s2 = cos_sin(k2[:, None] * k2[None, :], FFT_N2)
    mid = jnp.concatenate([c2, -s2], axis=0)
    return fwd, mid, inv


def _mm(w, x):
    return jnp.dot(w, x.astype(jnp.bfloat16), preferred_element_type=jnp.float32)


def _long_conv_kernel(x_ref, g_ref, hr_ref, hi_ref, b_ref, wf_ref, wm_ref, wi_ref, o_ref, sr_ref, si_ref):
    def rows_t2(t2):
        return pl.ds(t2, CONV_T1, stride=FFT_N2)

    def stage_a(t2, carry):
        w = wf_ref[t2]
        p = _mm(w, x_ref.at[0][rows_t2(t2), :])
        q = _mm(w, x_ref.at[1][rows_t2(t2), :])
        dst = pl.ds(pl.multiple_of(t2 * FFT_N1, FFT_N1), FFT_N1)
        sr_ref[dst, :] = p[:FFT_N1] - q[FFT_N1:]
        si_ref[dst, :] = p[FFT_N1:] + q[:FFT_N1]
        return carry

    lax.fori_loop(0, FFT_N2, stage_a, 0, unroll=FFT_UNROLL)
    wm = wm_ref[...]

    def per_k1(k1, carry):
        sel = pl.ds(k1, FFT_N2, stride=FFT_N1)
        p = _mm(wm, sr_ref[sel, :])
        q = _mm(wm, si_ref[sel, :])
        xr = p[:FFT_N2] - q[FFT_N2:]
        xi = p[FFT_N2:] + q[:FFT_N2]
        hsel = pl.ds(pl.multiple_of(k1 * FFT_N2, FFT_N2), FFT_N2)
        hr = hr_ref[hsel, :]
        hi = hi_ref[hsel, :]
        p = _mm(wm, xr * hr - xi * hi)
        q = _mm(wm, xr * hi + xi * hr)
        sr_ref[sel, :] = p[:FFT_N2] + q[FFT_N2:]
        si_ref[sel, :] = q[:FFT_N2] - p[FFT_N2:]
        return carry

    lax.fori_loop(0, FFT_N1, per_k1, 0, unroll=FFT_MID_UNROLL)
    bias = b_ref[...]

    def stage_c(t2, carry):
        w = wi_ref[t2]
        src = pl.ds(pl.multiple_of(t2 * FFT_N1, FFT_N1), FFT_N1)
        p = _mm(w, sr_ref[src, :])
        q = _mm(w, si_ref[src, :])
        rows = rows_t2(t2)
        o_ref.at[0][rows, :] = g_ref.at[0][rows, :] * (p[:CONV_T1] - q[CONV_T1:] + x_ref.at[0][rows, :] * bias)
        o_ref.at[1][rows, :] = g_ref.at[1][rows, :] * (q[:CONV_T1] + p[CONV_T1:] + x_ref.at[1][rows, :] * bias)
        return carry

    lax.fori_loop(0, FFT_N2, stage_c, 0, unroll=FFT_UNROLL)


def _split_bf16(w):
    hi = w.astype(jnp.bfloat16)
    return hi, (w - hi.astype(jnp.float32)).astype(jnp.bfloat16)


def _mm3(w_hi, w_lo, x):
    x_hi, x_lo = _split_bf16(x)
    f32 = jnp.float32
    return (jnp.dot(w_hi, x_hi, preferred_element_type=f32) + jnp.dot(w_lo, x_hi, preferred_element_type=f32)
            + jnp.dot(w_hi, x_lo, preferred_element_type=f32))


def _filter_spectrum_kernel(a_ref, b_ref, wfh_ref, wfl_ref, wmh_ref, wml_ref, hr_ref, hi_ref,
                            ar_ref, ai_ref, br_ref, bi_ref):
    def rows_t2(t2):
        return pl.ds(t2, CONV_T1, stride=FFT_N2)

    def stage_a(t2, carry):
        wh, wl = wfh_ref[t2], wfl_ref[t2]
        p = _mm3(wh, wl, a_ref[rows_t2(t2), :])
        q = _mm3(wh, wl, b_ref[rows_t2(t2), :])
        dst = pl.ds(pl.multiple_of(t2 * FFT_N1, FFT_N1), FFT_N1)
        ar_ref[dst, :] = p[:FFT_N1]
        ai_ref[dst, :] = p[FFT_N1:]
        br_ref[dst, :] = q[:FFT_N1]
        bi_ref[dst, :] = q[FFT_N1:]
        return carry

    lax.fori_loop(0, FFT_N2, stage_a, 0, unroll=FFT_UNROLL)
    wmh, wml = wmh_ref[...], wml_ref[...]
    a = a_ref[...]
    b = b_ref[...]
    energy = jnp.sum(a * a, axis=0, keepdims=True) + jnp.sum(b * b, axis=0, keepdims=True)
    scale = lax.rsqrt(energy + HYENA_NORM_EPS)

    def per_k1(k1, carry):
        sel = pl.ds(k1, FFT_N2, stride=FFT_N1)
        pa, qa = _mm3(wmh, wml, ar_ref[sel, :]), _mm3(wmh, wml, ai_ref[sel, :])
        pb, qb = _mm3(wmh, wml, br_ref[sel, :]), _mm3(wmh, wml, bi_ref[sel, :])
        dst = pl.ds(pl.multiple_of(k1 * FFT_N2, FFT_N2), FFT_N2)
        a_re, a_im = pa[:FFT_N2] - qa[FFT_N2:], pa[FFT_N2:] + qa[:FFT_N2]
        b_re, b_im = pb[:FFT_N2] - qb[FFT_N2:], pb[FFT_N2:] + qb[:FFT_N2]
        hr_ref[dst, :] = (a_re + b_re) * scale
        hi_ref[dst, :] = (a_im - b_im) * scale
        return carry

    lax.fori_loop(0, FFT_N1, per_k1, 0, unroll=FFT_UNROLL)


def filter_spectrum(a, b, tables):
    L, C = a.shape
    wf, wm, _ = tables
    sig = pl.BlockSpec((L, VREG_LANES), lambda c: (0, c))
    out = pl.BlockSpec((FFT_N, VREG_LANES), lambda c: (0, c))
    once = pl.Buffered(1)
    plane = pltpu.VMEM((FFT_N, VREG_LANES), jnp.float32)
    return pl.pallas_call(
        _filter_spectrum_kernel,
        grid=(C // VREG_LANES,),
        in_specs=[sig, sig] + [pl.BlockSpec(wf.shape, lambda c: (0, 0, 0), pipeline_mode=once)] * 2
        + [pl.BlockSpec(wm.shape, lambda c: (0, 0), pipeline_mode=once)] * 2,
        out_specs=[out, out],
        out_shape=[jax.ShapeDtypeStruct((FFT_N, C), jnp.float32)] * 2,
        scratch_shapes=[plane] * 4,
        compiler_params=pltpu.CompilerParams(dimension_semantics=("arbitrary",), vmem_limit_bytes=FFT_VMEM_LIMIT),
        name="filter_spectrum",
    )(a, b, *_split_bf16(wf), *_split_bf16(wm))


def long_conv_gated(u, u_col, gate, gate_col, h_re, h_im, h_col, bias, tables):
    B, L, _ = u.shape
    C = bias.shape[0]
    assert L == CONV_L and B % 2 == 0 and C % VREG_LANES == 0
    assert u_col % VREG_LANES == 0 and gate_col % VREG_LANES == 0 and h_col % VREG_LANES == 0
    consts = [t.astype(jnp.bfloat16) for t in tables]
    sig_spec = pl.BlockSpec((2, L, VREG_LANES), lambda c, b: (b, 0, c))
    u_spec = pl.BlockSpec((2, L, VREG_LANES), lambda c, b: (b, 0, c + u_col // VREG_LANES))
    gate_spec = pl.BlockSpec((2, L, VREG_LANES), lambda c, b: (b, 0, c + gate_col // VREG_LANES))
    once = pl.Buffered(1)
    spec_h = pl.BlockSpec((FFT_N, VREG_LANES), lambda c, b: (0, c + h_col // VREG_LANES), pipeline_mode=once)
    const_specs = [pl.BlockSpec(t.shape, functools.partial(lambda nd, c, b: (0,) * nd, t.ndim), pipeline_mode=once)
                   for t in consts]
    return pl.pallas_call(
        _long_conv_kernel,
        grid=(C // VREG_LANES, B // 2),
        in_specs=[u_spec, gate_spec, spec_h, spec_h, pl.BlockSpec((1, VREG_LANES), lambda c, b: (0, c))] + const_specs,
        out_specs=sig_spec,
        out_shape=jax.ShapeDtypeStruct((B, L, C), jnp.float32),
        scratch_shapes=[pltpu.VMEM((FFT_N, VREG_LANES), jnp.float32), pltpu.VMEM((FFT_N, VREG_LANES), jnp.float32)],
        compiler_params=pltpu.CompilerParams(dimension_semantics=("arbitrary", "arbitrary"),
                                             vmem_limit_bytes=FFT_VMEM_LIMIT),
        name="long_conv",
    )(u, gate, h_re, h_im, bias.reshape(1, C), *consts)


def hyena_layer(xt, mods, n_batch, w_in, conv_w, conv_b, w1, b1, w2, b2, w3, freq, fbias, w_out, ln_g, ln_b):
    sh1, sc1, g1 = mods
    T = xt.shape[0]
    L = T // n_batch
    tables = _dft_tables()
    h_re, h_im = filter_spectrum(*hyena_filters(L, w1, b1, w2, b2, w3, freq), tables)
    u = mod_matmul_conv(xt, sc1, sh1, w_in, conv_w[None], conv_b, n_batch, 1, L, False).reshape(n_batch, L, 3 * D_MODEL)
    zz = long_conv_gated(u, 2 * D_MODEL, u, 0, h_re, h_im, 0, fbias[0], tables)
    zz = long_conv_gated(zz, 0, u, D_MODEL, h_re, h_im, D_MODEL, fbias[1], tables)
    return proj_ln(zz.reshape(T, D_MODEL), w_out, xt, g1, ln_g, ln_b)


D_TILES = D_MODEL // VREG_LANES
PEER_SLOTS = PEER_HEADS * PEER_TOPK
PEER_TB = 512
PEER_SUB = 64
PEER_VMEM_LIMIT = 58 * 1024 * 1024
HI_MASK = -65536


PACK_TB = 256


def _pack_rows_kernel(lo_ref, hi_ref, o_ref):
    def bf16_bits(v):
        return pltpu.bitcast(v.astype(jnp.bfloat16).astype(jnp.float32), jnp.int32)
    word = (bf16_bits(hi_ref[0]) & HI_MASK) | lax.shift_right_logical(bf16_bits(lo_ref[0]), 16)
    for k in range(D_TILES):
        o_ref[pl.ds(k, PACK_TB, stride=D_TILES), :] = word[:, k * VREG_LANES:(k + 1) * VREG_LANES]


def _pack_rows_bf16(tabs, layer):
    _, n, d = tabs.shape
    half = n // 2
    nblk = half // PACK_TB
    return pl.pallas_call(
        _pack_rows_kernel,
        grid=(nblk,),
        in_specs=[pl.BlockSpec((1, PACK_TB, d), lambda i: (layer, i, 0)),
                  pl.BlockSpec((1, PACK_TB, d), lambda i: (layer, i + nblk, 0))],
        out_specs=pl.BlockSpec((PACK_TB * D_TILES, VREG_LANES), lambda i: (i, 0)),
        out_shape=jax.ShapeDtypeStruct((half * D_TILES, VREG_LANES), jnp.int32),
        compiler_params=pltpu.CompilerParams(dimension_semantics=("arbitrary",)),
        name="pack_rows",
    )(tabs, tabs)


def _unpack_row(tab_ref, off, shift):
    word = tab_ref[pl.ds(pl.multiple_of(off, VREG_SUBLANES), VREG_SUBLANES), :]
    return pltpu.bitcast((word << shift) & HI_MASK, jnp.float32)


def _sublane_sums(tiles, sub):
    for h in (1, 2, 4):
        keep = (sub & h) == 0
        nxt = []
        for a, b in zip(tiles[0::2], tiles[1::2]):
            nxt.append(jnp.where(keep, a, b) + pltpu.roll(jnp.where(keep, b, a), h, 0))
        tiles = nxt
    return tiles[0]


def _splat_tile(row):
    return jnp.broadcast_to(row, (PEER_SLOTS, VREG_LANES)).T


def _splat_small_ints_tile(row, eye, ones):
    diag = jnp.where(eye, row.astype(jnp.float32), 0.0).astype(jnp.bfloat16)
    return jnp.dot(diag, ones, preferred_element_type=jnp.float32).astype(jnp.int32)


def _splat_consts():
    eye = (lax.broadcasted_iota(jnp.int32, (PEER_SLOTS, PEER_SLOTS), 0)
           == lax.broadcasted_iota(jnp.int32, (PEER_SLOTS, PEER_SLOTS), 1))
    return eye, jnp.ones((PEER_SLOTS, VREG_LANES), jnp.bfloat16)


def _splat_token(t0, t, shift_ref, shift_s, w_ref, w_s, eye, ones):
    dst = pl.ds(pl.multiple_of(t * PEER_SLOTS, PEER_SLOTS), PEER_SLOTS)
    shift_s[dst, :] = _splat_small_ints_tile(shift_ref[pl.ds(t0 + t, 1), :], eye, ones)
    if w_ref is not None:
        w_s[dst, :] = _splat_tile(w_ref[pl.ds(t0 + t, 1), :])


def _splat_block(t0, shift_ref, shift_s, w_ref=None, w_s=None):
    eye, ones = _splat_consts()

    def token(t, carry):
        _splat_token(t0, t, shift_ref, shift_s, w_ref, w_s, eye, ones)
        return carry
    lax.fori_loop(0, PEER_SUB, token, 0, unroll=8)


def _tile_rows(k):
    return pl.ds(k, PEER_SUB, stride=D_TILES)


def _peer_act_kernel(row_ref, shift_ref, x_ref, sc_ref, sh_ref, gate_ref, tab_ref, w_ref,
                     part_ref, shift_a, shift_b, xt_ref):
    sub = lax.broadcasted_iota(jnp.int32, (VREG_SUBLANES, VREG_LANES), 0)
    contract_lanes = (((1,), (1,)), ((), ()))
    ones = jnp.ones((VREG_SUBLANES, VREG_LANES), jnp.bfloat16)
    eye, splat_ones = _splat_consts()
    bufs = (shift_a, shift_b)
    starts = list(range(0, PEER_TB, PEER_SUB))
    _splat_block(0, shift_ref, bufs[0])
    for j, t0 in enumerate(starts):
        shift_s, shift_next = bufs[j % 2], bufs[(j + 1) % 2]
        t_next = starts[j + 1] if j + 1 < len(starts) else None
        h = x_ref[t0:t0 + PEER_SUB, :] * (1.0 + sc_ref[0]) + sh_ref[0]
        for k in range(D_TILES):
            xt_ref[_tile_rows(k), :] = h[:, k * VREG_LANES:(k + 1) * VREG_LANES]

        def token(t, carry, t0=t0, shift_s=shift_s, shift_next=shift_next, t_next=t_next):
            if t_next is not None:
                _splat_token(t_next, t, shift_ref, shift_next, None, None, eye, splat_ones)
            xt = xt_ref[pl.ds(pl.multiple_of(t * D_TILES, D_TILES), D_TILES), :]
            base = pl.multiple_of(t * PEER_SLOTS, PEER_SLOTS)
            tok_rows = row_ref.at[0, 0, pl.ds(t0 * PEER_SLOTS + base, PEER_SLOTS)]
            for g in range(PEER_SLOTS // VREG_SUBLANES):
                prods = []
                for k in range(VREG_SUBLANES):
                    s = g * VREG_SUBLANES + k
                    prods.append(_unpack_row(tab_ref, tok_rows[s], shift_s[pl.ds(base + s, 1), :]) * xt)
                part_ref[pl.ds(base + g * VREG_SUBLANES, VREG_SUBLANES), :] = _sublane_sums(prods, sub)
            return carry

        lax.fori_loop(0, PEER_SUB, token, 0)
        act = lax.dot_general(ones, part_ref[...].astype(jnp.bfloat16), contract_lanes,
                              preferred_element_type=jnp.float32)[0:1]
        gelu = 0.5 * act * (1.0 + lax.erf(act * (2.0 ** -0.5)))
        lanes = slice(t0 * PEER_SLOTS, (t0 + PEER_SUB) * PEER_SLOTS)
        w_ref[0, :, lanes] = gate_ref[0, :, lanes] * gelu


def _peer_out_kernel(row_ref, shift_ref, w_ref, tab_ref, x_ref, g_ref, lng_ref, lnb_ref, o_ref,
                     shift_a, shift_b, w_a, w_b, yt_ref):
    n_acc = 2
    eye, splat_ones = _splat_consts()
    shift_bufs, w_bufs = (shift_a, shift_b), (w_a, w_b)
    starts = list(range(0, PEER_TB, PEER_SUB))
    _splat_block(0, shift_ref, shift_bufs[0], w_ref, w_bufs[0])
    for j, t0 in enumerate(starts):
        shift_s, w_s = shift_bufs[j % 2], w_bufs[j % 2]
        shift_next, w_next = shift_bufs[(j + 1) % 2], w_bufs[(j + 1) % 2]
        t_next = starts[j + 1] if j + 1 < len(starts) else None

        def pair(i, carry, t0=t0, shift_s=shift_s, w_s=w_s, shift_next=shift_next, w_next=w_next, t_next=t_next):
            if t_next is not None:
                for k in range(2):
                    _splat_token(t_next, 2 * i + k, shift_ref, shift_next, w_ref, w_next, eye, splat_ones)
            bases = [pl.multiple_of((2 * i + k) * PEER_SLOTS, PEER_SLOTS) for k in range(2)]
            rows = [row_ref.at[0, 0, pl.ds(t0 * PEER_SLOTS + b_, PEER_SLOTS)] for b_ in bases]
            accs = [[jnp.zeros((VREG_SUBLANES, VREG_LANES), jnp.float32) for _ in range(n_acc)] for _ in range(2)]
            for s in range(PEER_SLOTS):
                for k in range(2):
                    val = _unpack_row(tab_ref, rows[k][s], shift_s[pl.ds(bases[k] + s, 1), :])
                    accs[k][s % n_acc] = accs[k][s % n_acc] + w_s[pl.ds(bases[k] + s, 1), :] * val
            for k in range(2):
                yt_ref[pl.ds(pl.multiple_of((2 * i + k) * D_TILES, D_TILES), D_TILES), :] = accs[k][0] + accs[k][1]
            return carry

        lax.fori_loop(0, PEER_SUB // 2, pair, 0)
        y = jnp.concatenate([yt_ref[_tile_rows(k), :] for k in range(D_TILES)], axis=1)
        rows_out = slice(t0, t0 + PEER_SUB)
        o_ref[rows_out, :] = _layer_norm_rows(DEEPNORM_ALPHA * x_ref[rows_out, :] + g_ref[0] * y,
                                              lng_ref[...], lnb_ref[...])


def peer_experts_ln(x, mods, offs, shifts, gate, u_tabs, v_tabs, layer, ln_g, ln_b):
    sh, sc, g = mods
    T = x.shape[0]
    nblk = T // PEER_TB
    per_blk = PEER_TB * PEER_SLOTS
    per_mod = T // sc.shape[0] // PEER_TB
    offs = offs.reshape(nblk, 1, per_blk)
    smem_spec = pl.BlockSpec((1, 1, per_blk), lambda i: (i, 0, 0), memory_space=pltpu.SMEM)
    lane_spec = pl.BlockSpec((1, 1, per_blk), lambda i: (i, 0, 0))
    slot_spec = pl.BlockSpec((PEER_TB, PEER_SLOTS), lambda i: (i, 0))
    tok_spec = pl.BlockSpec((PEER_TB, D_MODEL), lambda i: (i, 0))
    mod_spec = pl.BlockSpec((1, 1, D_MODEL), lambda i: (i // per_mod, 0, 0))
    vec_spec = pl.BlockSpec((1, D_MODEL), lambda i: (0, 0))
    tab_spec = pl.BlockSpec(memory_space=pltpu.VMEM)
    params = pltpu.CompilerParams(dimension_semantics=("arbitrary",), vmem_limit_bytes=PEER_VMEM_LIMIT)
    splat_i32 = pltpu.VMEM((PEER_SUB * PEER_SLOTS, VREG_LANES), jnp.int32)
    splat_f32 = pltpu.VMEM((PEER_SUB * PEER_SLOTS, VREG_LANES), jnp.float32)
    tok_tiles = pltpu.VMEM((PEER_SUB * D_TILES, VREG_LANES), jnp.float32)
    w = pl.pallas_call(
        _peer_act_kernel,
        grid=(nblk,),
        in_specs=[smem_spec, slot_spec, tok_spec, mod_spec, mod_spec, lane_spec, tab_spec],
        out_specs=lane_spec,
        out_shape=jax.ShapeDtypeStruct((nblk, 1, per_blk), jnp.float32),
        scratch_shapes=[splat_f32, splat_i32, splat_i32, tok_tiles],
        compiler_params=params,
        name="peer_act",
    )(offs, shifts, x, sc, sh, gate.reshape(nblk, 1, per_blk), _pack_rows_bf16(u_tabs, layer))
    return pl.pallas_call(
        _peer_out_kernel,
        grid=(nblk,),
        in_specs=[smem_spec, slot_spec, slot_spec, tab_spec, tok_spec, mod_spec, vec_spec, vec_spec],
        out_specs=tok_spec,
        out_shape=jax.ShapeDtypeStruct((T, D_MODEL), jnp.float32),
        scratch_shapes=[splat_i32, splat_i32, splat_f32, splat_f32, tok_tiles],
        compiler_params=params,
        name="peer_out",
    )(offs, shifts, w.reshape(T, PEER_SLOTS), _pack_rows_bf16(v_tabs, layer), x, g,
      ln_g.reshape(1, D_MODEL), ln_b.reshape(1, D_MODEL))


def _top_rows(s, k):
    n = s.shape[0]
    row = lax.broadcasted_iota(jnp.int32, s.shape, 0).astype(jnp.float32)
    vals, idxs = [], []
    for _ in range(k):
        m = jnp.max(s, axis=0, keepdims=True)
        i = jnp.min(jnp.where(s == m, row, float(n)), axis=0, keepdims=True)
        vals.append(m)
        idxs.append(i)
        s = jnp.where(row == i, -jnp.inf, s)
    return jnp.concatenate(vals, axis=0), jnp.concatenate(idxs, axis=0).astype(jnp.int32)


def _pick_rows(table, which):
    out = jnp.zeros(which.shape, table.dtype)
    for a in range(table.shape[0]):
        out = jnp.where(which == a, table[a:a + 1, :], out)
    return out


PAIR_COUNTS = tuple(PEER_TOPK // (a + 1) for a in range(PEER_TOPK))
PAIR_STARTS = tuple(sum(PAIR_COUNTS[:a]) for a in range(PEER_TOPK))
N_PAIRS = sum(PAIR_COUNTS)
N_PAIRS_PADDED = -(-N_PAIRS // VREG_SUBLANES) * VREG_SUBLANES
SEL_TB = 2048
PEER_PACKED_ROWS = PEER_NKEYS * PEER_NKEYS // 2


def _peer_select_kernel(x_ref, sc_ref, sh_ref, wq_ref, k1_ref, k2_ref, off_ref, shift_ref, gate_ref,
                        off_s, shift_s, gate_s, h_ref):
    half = PEER_DK // 2

    @pl.when(pl.program_id(1) == 0)
    def _():
        h_ref[...] = (x_ref[...] * (1.0 + sc_ref[0]) + sh_ref[0]).astype(jnp.bfloat16)

    q = jnp.dot(h_ref[...], wq_ref[...], preferred_element_type=jnp.float32).astype(jnp.bfloat16)
    contract_last = (((1,), (1,)), ((), ()))
    s1 = lax.dot_general(k1_ref[0], q[:, :half], contract_last, preferred_element_type=jnp.float32)
    s2 = lax.dot_general(k2_ref[0], q[:, half:], contract_last, preferred_element_type=jnp.float32)
    v1, i1 = _top_rows(s1, PEER_TOPK)
    v2, i2 = _top_rows(s2, PEER_TOPK)
    pairs = [v1[a:a + 1, :] + v2[:PAIR_COUNTS[a], :] for a in range(PEER_TOPK)]
    pairs.append(jnp.full((N_PAIRS_PADDED - N_PAIRS, v1.shape[1]), -jnp.inf, jnp.float32))
    top_s, pos = _top_rows(jnp.concatenate(pairs, axis=0), PEER_TOPK)
    a_sel = jnp.zeros(pos.shape, jnp.int32)
    start = jnp.zeros(pos.shape, jnp.int32)
    for a in range(1, PEER_TOPK):
        later = pos >= PAIR_STARTS[a]
        a_sel = jnp.where(later, a, a_sel)
        start = jnp.where(later, PAIR_STARTS[a], start)
    expert = _pick_rows(i1, a_sel) * PEER_NKEYS + _pick_rows(i2, pos - start)
    e = jnp.exp(top_s - top_s[0:1, :])
    head = pl.program_id(1)
    rows = pl.ds(pl.multiple_of(head * PEER_TOPK, PEER_TOPK), PEER_TOPK)
    gate_s[rows, :] = e / jnp.sum(e, axis=0, keepdims=True)
    off_s[rows, :] = (expert & (PEER_PACKED_ROWS - 1)) * VREG_SUBLANES
    shift_s[rows, :] = jnp.where(expert >= PEER_PACKED_ROWS, 0, 16)

    @pl.when(head == PEER_HEADS - 1)
    def _():
        gate_ref[...] = gate_s[...].T
        off_ref[...] = off_s[...].astype(jnp.float32).T.astype(jnp.int32)
        shift_ref[...] = shift_s[...].astype(jnp.float32).T.astype(jnp.int32)


def peer_select(x, sc, sh, wq, k1, k2):
    T = x.shape[0]
    per_seq = T // sc.shape[0] // SEL_TB
    bf16 = jnp.bfloat16
    tok_spec = pl.BlockSpec((SEL_TB, D_MODEL), lambda i, h: (i, 0))
    mod_spec = pl.BlockSpec((1, 1, D_MODEL), lambda i, h: (i // per_seq, 0, 0))
    wq_spec = pl.BlockSpec((D_MODEL, PEER_DK), lambda i, h: (0, h))
    key_spec = pl.BlockSpec((1, PEER_NKEYS, PEER_DK // 2), lambda i, h: (h, 0, 0))
    out_spec = pl.BlockSpec((SEL_TB, PEER_SLOTS), lambda i, h: (i, 0))
    return pl.pallas_call(
        _peer_select_kernel,
        grid=(T // SEL_TB, PEER_HEADS),
        in_specs=[tok_spec, mod_spec, mod_spec, wq_spec, key_spec, key_spec],
        out_specs=[out_spec, out_spec, out_spec],
        out_shape=[jax.ShapeDtypeStruct((T, PEER_SLOTS), jnp.int32), jax.ShapeDtypeStruct((T, PEER_SLOTS), jnp.int32),
                   jax.ShapeDtypeStruct((T, PEER_SLOTS), jnp.float32)],
        scratch_shapes=[pltpu.VMEM((PEER_SLOTS, SEL_TB), jnp.int32), pltpu.VMEM((PEER_SLOTS, SEL_TB), jnp.int32),
                        pltpu.VMEM((PEER_SLOTS, SEL_TB), jnp.float32), pltpu.VMEM((SEL_TB, D_MODEL), jnp.bfloat16)],
        compiler_params=pltpu.CompilerParams(dimension_semantics=("arbitrary", "arbitrary")),
        name="peer_select",
    )(x, sc, sh, wq.astype(bf16), k1.astype(bf16), k2.astype(bf16))


def peer_layer(xt, mods, wq, k1, k2, u_tabs, v_tabs, layer, ln_g, ln_b):
    sh2, sc2, _ = mods
    off, shift, gate = peer_select(xt, sc2, sh2, wq, k1, k2)
    return peer_experts_ln(xt, mods, off, shift, gate, u_tabs, v_tabs, layer, ln_g, ln_b)


def kernel(x, c, ctx, c_ctx, ada_w, ada_b, ln_g, ln_b,
           m_w_in, m_conv_w, m_conv_b, m_a_log, m_dt_bias, m_d, m_norm_w, m_w_out,
           h_w_in, h_conv_w, h_conv_b, h_f_w1, h_f_b1, h_f_w2, h_f_b2, h_f_w3, h_freq, h_bias, h_w_out,
           p_wq, p_k1, p_k2, p_u, p_v):
    n_batch, seq, _ = x.shape
    rows = seq // GRID_W
    s_lat = jax.nn.silu(c)
    s_ctx = jax.nn.silu(c_ctx)
    xt = x.reshape(n_batch * seq, D_MODEL)
    ctx_t = ctx.reshape(-1, D_MODEL)
    for i in range(DEPTH):
        kind, k = i % N_MIXERS, i // N_MIXERS
        sh1, sc1, g1, sh2, sc2, g2 = jnp.split((s_lat @ ada_w[i] + ada_b[i])[:, None, :], 6, axis=-1)
        if kind == 0:
            csh1, csc1 = jnp.split((s_ctx @ ada_w[i] + ada_b[i])[None, None, :], 6, axis=-1)[:2]
            xt = mamba_layer(xt, ctx_t, (sh1, sc1, g1), (csh1, csc1), n_batch, rows, m_w_in[k], m_conv_w[k],
                             m_conv_b[k], m_a_log[k], m_dt_bias[k], m_d[k], m_norm_w[k], m_w_out[k],
                             ln_g[i, 0], ln_b[i, 0])
        else:
            xt = hyena_layer(xt, (sh1, sc1, g1), n_batch, h_w_in[k], h_conv_w[k], h_conv_b[k], h_f_w1[k], h_f_b1[k],
                             h_f_w2[k], h_f_b2[k], h_f_w3[k], h_freq[k], h_bias[k], h_w_out[k], ln_g[i, 0], ln_b[i, 0])
        xt = peer_layer(xt, (sh2, sc2, g2), p_wq[i], p_k1[i], p_k2[i], p_u, p_v, i, ln_g[i, 1], ln_b[i, 1])
    return xt.reshape(x.shape)
```
